```python
import jax, jax.numpy as jnp
from jax import lax
import numpy as np

D_MODEL = 1024
BATCH = 8
SEQ = 4096
DEPTH = 1
DEC_BATCH = 8
DEC_SEQ = 64
PAST_LEN = 4096

CHUNK = 64
HEAD_DIM = 64
N_HEADS_A = 8
D_A = N_HEADS_A * HEAD_DIM
N_GROUPS_B = 8
D_B = N_GROUPS_B * HEAD_DIM
CHUNK_B = 128
Q_BLOCK = 128
N_EXPERTS = 32
TOP_K = 4
D_FF = D_MODEL
SWIGLU_ALPHA = 1.702
SWIGLU_LIMIT = 7.0
MOE_ROWS = 128
IN_COLS = 3 * D_A + 2 * D_B + 2 * D_MODEL
RMS_EPS = 1e-6

kernel_name = 'stick_breaking_gmlp_moe_streaming_step'


def _rms_norm(x, g):
    xf = x.astype(jnp.float32)
    y = xf * lax.rsqrt(jnp.mean(xf * xf, axis=-1, keepdims=True) + RMS_EPS)
    return (y * g.astype(jnp.float32)).astype(x.dtype)


def _sb_block(q, k, v, q_pos, k_pos):
    z = jnp.einsum('bqhd,bkhd->bhqk', q, k, preferred_element_type=jnp.float32) * (HEAD_DIM ** -0.5)
    causal = k_pos[None, :] < q_pos[:, None]
    log_one_minus = jnp.where(causal, jax.nn.log_sigmoid(-z), 0.0)
    later = lax.cumsum(log_one_minus, axis=3, reverse=True) - log_one_minus
    weight = jnp.where(causal, jnp.exp(jax.nn.log_sigmoid(z) + later), 0.0)
    return jnp.einsum('bhqk,bkhd->bqhd', weight.astype(v.dtype), v)


def _stick_breaking(q, k, v, q_offset):
    B, Tq, H, Dh = q.shape
    blk = min(Q_BLOCK, Tq)
    nblk = Tq // blk
    k_pos = jnp.arange(k.shape[1], dtype=jnp.int32)
    q_blocks = q.reshape(B, nblk, blk, H, Dh).swapaxes(0, 1)
    pos_blocks = (q_offset + jnp.arange(Tq, dtype=jnp.int32)).reshape(nblk, blk)
    out = lax.map(lambda a: _sb_block(a[0], k, v, a[1], k_pos), (q_blocks, pos_blocks))
    return out.swapaxes(0, 1).reshape(B, Tq, H, Dh)


def _chunk_mlp(u, vb, w_s, b_s, g_v):
    B, T, G, Dh = u.shape
    L = min(CHUNK_B, T)
    nc = T // L
    u = jax.nn.gelu(u)
    vn = _rms_norm(jax.nn.gelu(vb), g_v)
    w = jnp.tril(w_s[:, :L, :L])
    vc = vn.reshape(B, nc, L, G, Dh)
    s = jnp.einsum('gij,bcjgd->bcigd', w, vc) + b_s[:, :L].T[None, None, :, :, None]
    return u * s.reshape(B, T, G, Dh), vn


def _expert_ffn(xb, w1, b1, w2, b2):
    hdn = xb @ w1 + b1
    x_glu = jnp.minimum(hdn[..., 0::2], SWIGLU_LIMIT)
    x_lin = jnp.clip(hdn[..., 1::2], -SWIGLU_LIMIT, SWIGLU_LIMIT)
    act = x_glu * jax.nn.sigmoid(SWIGLU_ALPHA * x_glu) * (x_lin + 1.0)
    return act @ w2 + b2


def _moe(h, w_router, b_router, w1, b1, w2, b2):
    B, T, D = h.shape
    n = B * T
    xf = h.reshape(n, D)
    logits = (xf @ w_router + b_router).astype(jnp.float32)
    top_val, top_idx = lax.top_k(logits, TOP_K)
    gates = jax.nn.softmax(top_val, axis=-1).astype(h.dtype)
    nk = n * TOP_K
    e_flat = top_idx.reshape(nk).astype(jnp.int32)
    tok_flat = jnp.arange(nk, dtype=jnp.int32) // TOP_K
    order = jnp.argsort(e_flat, stable=True)
    e_sorted = e_flat[order]
    tok_sorted = tok_flat[order]
    gate_sorted = gates.reshape(nk)[order]
    counts = jnp.bincount(e_flat, length=N_EXPERTS).astype(jnp.int32)
    padded = (counts + MOE_ROWS - 1) // MOE_ROWS * MOE_ROWS
    start = jnp.cumsum(counts) - counts
    pend = jnp.cumsum(padded)
    pstart = pend - padded
    dest = pstart[e_sorted] + jnp.arange(nk, dtype=jnp.int32) - start[e_sorted]
    n_blocks = -(-nk // MOE_ROWS) + N_EXPERTS
    rows = jnp.zeros((n_blocks * MOE_ROWS, D), h.dtype).at[dest].set(xf[tok_sorted])
    block_start = jnp.arange(n_blocks, dtype=jnp.int32) * MOE_ROWS
    block_expert = jnp.minimum(jnp.searchsorted(pend, block_start, side='right'), N_EXPERTS - 1)

    def run(args):
        xb, e = args
        return _expert_ffn(xb, w1[e], b1[e], w2[e], b2[e])

    y_rows = lax.map(run, (rows.reshape(n_blocks, MOE_ROWS, D), block_expert)).reshape(n_blocks * MOE_ROWS, D)
    y = jax.ops.segment_sum(y_rows[dest] * gate_sorted[:, None], tok_sorted, num_segments=n)
    return y.reshape(B, T, D)


def _layer(x, c, cache_k, cache_v, p):
    B, T, _ = x.shape
    mod = jax.nn.silu(c) @ p['w_ada'] + p['b_ada']
    sh1, sc1, g1, sh2, sc2, g2 = jnp.split(mod[:, None, :], 6, axis=-1)
    h = _rms_norm(x, p['norm1_g']) * (1.0 + sc1) + sh1
    proj = h @ p['w_in']
    splits = [D_A, 2 * D_A, 3 * D_A, 3 * D_A + D_B, 3 * D_A + 2 * D_B, 3 * D_A + 2 * D_B + D_MODEL]
    q, k, v, u, vb, ga, gb = jnp.split(proj, splits, axis=-1)
    q = _rms_norm(q.reshape(B, T, N_HEADS_A, HEAD_DIM), p['q_norm_g'])
    k = _rms_norm(k.reshape(B, T, N_HEADS_A, HEAD_DIM), p['k_norm_g'])
    v = v.reshape(B, T, N_HEADS_A, HEAD_DIM)
    if cache_k is None:
        k_all, v_all, q_offset = k, v, 0
    else:
        k_all = jnp.concatenate([cache_k, k], axis=1)
        v_all = jnp.concatenate([cache_v, v], axis=1)
        q_offset = cache_k.shape[1]
    y_a = _stick_breaking(q, k_all, v_all, q_offset).reshape(B, T, D_A)
    y_b, vn = _chunk_mlp(u.reshape(B, T, N_GROUPS_B, HEAD_DIM), vb.reshape(B, T, N_GROUPS_B, HEAD_DIM),
                         p['w_s'], p['b_s'], p['v_norm_g'])
    merged = jax.nn.sigmoid(ga) * (y_a @ p['w_pa']) + jax.nn.sigmoid(gb) * (y_b.reshape(B, T, D_B) @ p['w_pb'])
    x = x + g1 * (merged @ p['w_o'])
    h2 = _rms_norm(x, p['norm2_g']) * (1.0 + sc2) + sh2
    x = x + g2 * _moe(h2, p['w_router'], p['b_router'], p['w1'], p['b1'], p['w2'], p['b2'])
    return x, k, v, vn


def setup_inputs(seed: int = 0) -> dict:
    key = jax.random.key(seed)
    ks = jax.random.split(key, 26)
    D = D_MODEL

    def nrm(k, shape, s):
        return jax.random.normal(k, shape, jnp.float32) * s

    return {
        'x_prompt': nrm(ks[0], (BATCH, SEQ, D), 1.0),
        'x_sample': nrm(ks[1], (DEC_BATCH, DEC_SEQ, D), 1.0),
        'cache_k': nrm(ks[2], (DEPTH, DEC_BATCH, PAST_LEN, N_HEADS_A, HEAD_DIM), 1.0),
        'cache_v': nrm(ks[3], (DEPTH, DEC_BATCH, PAST_LEN, N_HEADS_A, HEAD_DIM), 1.0),
        'c_prompt': nrm(ks[4], (BATCH, D), 1.0),
        'c_sample': nrm(ks[5], (DEC_BATCH, D), 1.0),
        'norm1_g': 1.0 + nrm(ks[6], (DEPTH, D), 0.1),
        'norm2_g': 1.0 + nrm(ks[7], (DEPTH, D), 0.1),
        'w_ada': nrm(ks[8], (DEPTH, D, 6 * D), 0.5 * D ** -0.5),
        'b_ada': nrm(ks[9], (DEPTH, 6 * D), 0.01),
        'w_in': nrm(ks[10], (DEPTH, D, IN_COLS), D ** -0.5),
        'q_norm_g': 1.0 + nrm(ks[11], (DEPTH, HEAD_DIM), 0.1),
        'k_norm_g': 1.0 + nrm(ks[12], (DEPTH, HEAD_DIM), 0.1),
        'v_norm_g': 1.0 + nrm(ks[13], (DEPTH, HEAD_DIM), 0.1),
        'w_s': nrm(ks[14], (DEPTH, N_GROUPS_B, CHUNK_B, CHUNK_B), CHUNK_B ** -0.5),
        'b_s': 1.0 + nrm(ks[15], (DEPTH, N_GROUPS_B, CHUNK_B), 0.1),
        'w_pa': nrm(ks[16], (DEPTH, D_A, D), D_A ** -0.5),
        'w_pb': nrm(ks[17], (DEPTH, D_B, D), D_B ** -0.5),
        'w_o': nrm(ks[18], (DEPTH, D, D), D ** -0.5),
        'w_router': nrm(ks[19], (DEPTH, D, N_EXPERTS), D ** -0.5),
        'b_router': nrm(ks[20], (DEPTH, N_EXPERTS), 0.01),
        'w1': nrm(ks[21], (DEPTH, N_EXPERTS, D, 2 * D_FF), D ** -0.5),
        'b1': nrm(ks[22], (DEPTH, N_EXPERTS, 2 * D_FF), 0.01),
        'w2': nrm(ks[23], (DEPTH, N_EXPERTS, D_FF, D), D_FF ** -0.5),
        'b2': nrm(ks[24], (DEPTH, N_EXPERTS, D), 0.01),
    }


def reference(x_prompt, x_sample, cache_k, cache_v, c_prompt, c_sample, norm1_g, norm2_g, w_ada, b_ada,
              w_in, q_norm_g, k_norm_g, v_norm_g, w_s, b_s, w_pa, w_pb, w_o, w_router, b_router,
              w1, b1, w2, b2):
    xp, xs = x_prompt, x_sample
    kp_list, vp_list, ks_list, vs_list, vbs_list = [], [], [], [], []
    for l in range(DEPTH):
        p = {'norm1_g': norm1_g[l], 'norm2_g': norm2_g[l], 'w_ada': w_ada[l], 'b_ada': b_ada[l],
             'w_in': w_in[l], 'q_norm_g': q_norm_g[l], 'k_norm_g': k_norm_g[l], 'v_norm_g': v_norm_g[l],
             'w_s': w_s[l], 'b_s': b_s[l], 'w_pa': w_pa[l], 'w_pb': w_pb[l], 'w_o': w_o[l],
             'w_router': w_router[l], 'b_router': b_router[l], 'w1': w1[l], 'b1': b1[l],
             'w2': w2[l], 'b2': b2[l]}
        xp, kp, vp, _ = _layer(xp, c_prompt, None, None, p)
        xs, ksm, vsm, vbs = _layer(xs, c_sample, cache_k[l], cache_v[l], p)
        kp_list.append(kp)
        vp_list.append(vp)
        ks_list.append(ksm)
        vs_list.append(vsm)
        vbs_list.append(vbs)
    new_k_prompt = jnp.stack(kp_list)
    new_v_prompt = jnp.stack(vp_list)
    new_k_sample = jnp.stack(ks_list)
    new_v_sample = jnp.stack(vs_list)
    new_vb_sample = jnp.stack(vbs_list)
    return (xp, xs, new_k_prompt, new_v_prompt, new_k_sample, new_v_sample, new_vb_sample)
```

```python
import functools

import jax
import jax.numpy as jnp
from jax import lax
from jax.experimental import pallas as pl
from jax.experimental.pallas import tpu as pltpu

F32 = jnp.float32
BF16 = jnp.bfloat16

LANES = 128
MXU_DIM = 256
VMEM_LIMIT = 56 * 1024 * 1024

HEAD_DIM = 64
N_HEADS = 8
D_HEADS = N_HEADS * HEAD_DIM
PAIRS = D_HEADS // LANES
CHUNK_B = 128
N_EXPERTS = 32
TOP_K = 4
RMS_EPS = 1e-6
SWIGLU_ALPHA = 1.702
SWIGLU_LIMIT = 7.0

TM_IN = 512
TQ = 256
TK = 256
TM_POST = 256
TR = 512
TD = 256
TM_E = 512
TC = 256


def _params(sem, **kw):
    return pltpu.CompilerParams(dimension_semantics=sem, vmem_limit_bytes=VMEM_LIMIT, **kw)


def _dot(a, b):
    return jnp.dot(a, b, preferred_element_type=F32)


def _dot_nt(a, b):
    return lax.dot_general(a, b, (((1,), (1,)), ((), ())), preferred_element_type=F32)


def _gelu(x):
    return jax.nn.gelu(x)


def _sigmoid(x):
    return 1.0 / (1.0 + jnp.exp(-x))


def _ada_kernel(c_ref, w_ref, b_ref, o_ref):
    c = c_ref[...]
    s = c * _sigmoid(c)
    o_ref[...] = jnp.dot(s, w_ref[...], preferred_element_type=F32,
                         precision=lax.Precision.HIGHEST) + b_ref[...]


def _ada(c, w_ada, b_ada):
    nb, d = c.shape
    n = w_ada.shape[1]
    tn = 1024
    return pl.pallas_call(
        _ada_kernel,
        out_shape=jax.ShapeDtypeStruct((nb, n), F32),
        grid=(n // tn,),
        in_specs=[pl.BlockSpec((nb, d), lambda j: (0, 0)),
                  pl.BlockSpec((d, tn), lambda j: (0, j)),
                  pl.BlockSpec((1, tn), lambda j: (0, j))],
        out_specs=pl.BlockSpec((nb, tn), lambda j: (0, j)),
        compiler_params=_params(("parallel",)),
        name="ada",
    )(c, w_ada, b_ada.reshape(1, n))


def _head_mean_sq(p, bd):
    sq = (p * p).astype(BF16)
    return jnp.concatenate([_dot(sq[:, :MXU_DIM], bd), _dot(sq[:, MXU_DIM:], bd)], axis=-1)


def _inproj_kernel(x_ref, sh_ref, sc_ref, g_ref, w_ref, qg_ref, kg_ref, vg_ref, bd_ref,
                   q_ref, k32_ref, kb_ref, v32_ref, vb_ref, u_ref, vnb_ref, sga_ref, sgb_ref, *vn32_ref):
    x = x_ref[...]
    ms = jnp.mean(x * x, axis=-1, keepdims=True)
    xn = x * lax.rsqrt(ms + RMS_EPS) * g_ref[...]
    h = (xn * (1.0 + sc_ref[...]) + sh_ref[...]).astype(BF16)
    bd = bd_ref[...]
    d = x.shape[1]

    def sec(a, n):
        return _dot(h, w_ref[:, a:a + n])

    o = 0
    pq = sec(o, D_HEADS); o += D_HEADS
    q = pq * lax.rsqrt(_head_mean_sq(pq, bd) + RMS_EPS) * qg_ref[...]
    q_ref[...] = (q * (HEAD_DIM ** -0.5)).astype(BF16)
    pk = sec(o, D_HEADS); o += D_HEADS
    k = pk * lax.rsqrt(_head_mean_sq(pk, bd) + RMS_EPS) * kg_ref[...]
    k32_ref[...] = k
    kb_ref[...] = k.astype(BF16)
    pv = sec(o, D_HEADS); o += D_HEADS
    v32_ref[...] = pv
    vb_ref[...] = pv.astype(BF16)
    pu = sec(o, D_HEADS); o += D_HEADS
    u_ref[...] = _gelu(pu).astype(BF16)
    pvb = _gelu(sec(o, D_HEADS)); o += D_HEADS
    vn = pvb * lax.rsqrt(_head_mean_sq(pvb, bd) + RMS_EPS) * vg_ref[...]
    vnb_ref[...] = vn.astype(BF16)
    if vn32_ref:
        vn32_ref[0][...] = vn
    sga_ref[...] = _sigmoid(sec(o, d)).astype(BF16); o += d
    sgb_ref[...] = _sigmoid(sec(o, d)).astype(BF16)


def _mod_spec(tm, arr, rows_per_mod):
    if rows_per_mod is None:
        return pl.BlockSpec((tm, arr.shape[-1]), lambda i: (i, 0))
    return pl.BlockSpec((None, 1, arr.shape[-1]), lambda i: ((i * tm) // rows_per_mod, 0, 0))


def _mod_operand(m, seq, tm):
    if seq % tm == 0:
        return m[:, None, :], seq
    return jnp.repeat(m, seq, axis=0), None


def _inproj(x, sh, sc, seq, g, w_in_b, qg, kg, vg, bd, want_vn32):
    r, d = x.shape
    tm = min(TM_IN, r)
    assert r % tm == 0
    sh_o, rpm = _mod_operand(sh, seq, tm)
    sc_o, _ = _mod_operand(sc, seq, tm)
    ncol = w_in_b.shape[1]
    const = lambda i: (0, 0)
    row = lambda n: pl.BlockSpec((tm, n), lambda i: (i, 0))
    outs = [jax.ShapeDtypeStruct((r, D_HEADS), BF16),
            jax.ShapeDtypeStruct((r, D_HEADS), F32),
            jax.ShapeDtypeStruct((r, D_HEADS), BF16),
            jax.ShapeDtypeStruct((r, D_HEADS), F32),
            jax.ShapeDtypeStruct((r, D_HEADS), BF16),
            jax.ShapeDtypeStruct((r, D_HEADS), BF16),
            jax.ShapeDtypeStruct((r, D_HEADS), BF16),
            jax.ShapeDtypeStruct((r, d), BF16),
            jax.ShapeDtypeStruct((r, d), BF16)]
    out_specs = [row(D_HEADS)] * 7 + [row(d), row(d)]
    if want_vn32:
        outs.append(jax.ShapeDtypeStruct((r, D_HEADS), F32))
        out_specs.append(row(D_HEADS))
    return pl.pallas_call(
        _inproj_kernel,
        out_shape=outs,
        grid=(r // tm,),
        in_specs=[row(d), _mod_spec(tm, sh_o, rpm), _mod_spec(tm, sc_o, rpm),
                  pl.BlockSpec((1, d), const),
                  pl.BlockSpec((d, ncol), const, pipeline_mode=pl.Buffered(1)),
                  pl.BlockSpec((1, D_HEADS), const), pl.BlockSpec((1, D_HEADS), const),
                  pl.BlockSpec((1, D_HEADS), const), pl.BlockSpec((MXU_DIM, MXU_DIM), const)],
        out_specs=out_specs,
        compiler_params=_params(("parallel",)),
        name="inproj",
    )(x, sh_o, sc_o, g, w_in_b, qg, kg, vg, bd)


def _sb_block(qs, ks, vs, u, carries, mask):
    pvs, new_carries = [], []
    for qh, c in zip(qs, carries):
        z = _dot_nt(qh, ks)
        lom = -(jnp.maximum(z, 0.0) + jnp.log1p(jnp.exp(-jnp.abs(z))))
        if mask is not None:
            lom = jnp.where(mask, lom, 0.0)
        hi = lom.astype(BF16)
        lo = (lom - hi.astype(F32)).astype(BF16)
        later = _dot(hi, u) + _dot(lo, u)
        w = jnp.exp(z + lom + later + c)
        if mask is not None:
            w = jnp.where(mask, w, 0.0)
        pvs.append(_dot(w.astype(BF16), vs))
        new_carries.append(c + jnp.sum(lom, axis=-1, keepdims=True))
    return pvs, new_carries


def _pair_select(shape):
    return lax.broadcasted_iota(jnp.int32, shape, 1) < HEAD_DIM


def _split_heads(q):
    first = _pair_select(q.shape)
    zero = jnp.zeros_like(q)
    return [jnp.where(first, q, zero), jnp.where(first, zero, q)]


def _causal_mask(tq, tk):
    return lax.broadcasted_iota(jnp.int32, (tq, tk), 1) < lax.broadcasted_iota(jnp.int32, (tq, tk), 0)


def _attn_prompt_kernel(q_ref, k_ref, v_ref, u_ref, o_ref, acc_ref, c0_ref, c1_ref):
    i = pl.program_id(2)
    tq = q_ref.shape[0]
    qs = _split_heads(q_ref[...])
    u = u_ref[...]
    first = _pair_select((tq, LANES))

    def load(j):
        start = pl.multiple_of(j * TK, TK)
        return k_ref[pl.ds(start, TK), :], v_ref[pl.ds(start, TK), :]

    ks, vs = load(i)
    zero_c = jnp.zeros((tq, 1), F32)
    pvs, cs = _sb_block(qs, ks, vs, u, [zero_c, zero_c], _causal_mask(tq, TK))
    acc_ref[...] = jnp.where(first, pvs[0], pvs[1])
    c0_ref[...] = cs[0]
    c1_ref[...] = cs[1]

    def body(jj, _):
        ks, vs = load(i - 1 - jj)
        pvs, cs = _sb_block(qs, ks, vs, u, [c0_ref[...], c1_ref[...]], None)
        acc_ref[...] += jnp.where(first, pvs[0], pvs[1])
        c0_ref[...] = cs[0]
        c1_ref[...] = cs[1]
        return 0

    lax.fori_loop(0, i, body, 0)
    o_ref[...] = acc_ref[...].astype(o_ref.dtype)


def _upper_ones(n):
    return (lax.broadcasted_iota(jnp.int32, (n, n), 0) > lax.broadcasted_iota(jnp.int32, (n, n), 1)).astype(BF16)


def _attn_prompt(q, kb, vb, batch, seq):
    assert TQ == TK and seq % TQ == 0
    nq = seq // TQ
    return pl.pallas_call(
        _attn_prompt_kernel,
        out_shape=jax.ShapeDtypeStruct(q.shape, BF16),
        grid=(batch, PAIRS, nq),
        in_specs=[pl.BlockSpec((TQ, LANES), lambda b, p, i: (b * nq + i, p)),
                  pl.BlockSpec((seq, LANES), lambda b, p, i: (b, p)),
                  pl.BlockSpec((seq, LANES), lambda b, p, i: (b, p)),
                  pl.BlockSpec((TK, TK), lambda b, p, i: (0, 0))],
        out_specs=pl.BlockSpec((TQ, LANES), lambda b, p, i: (b * nq + i, p)),
        scratch_shapes=[pltpu.VMEM((TQ, LANES), F32), pltpu.VMEM((TQ, 1), F32), pltpu.VMEM((TQ, 1), F32)],
        compiler_params=_params(("parallel", "parallel", "arbitrary")),
        name="attn_prompt",
    )(q, kb, vb, _upper_ones(TK))


def _attn_sample_kernel(q_ref, kn_ref, vn_ref, kc_ref, vc_ref, un_ref, u_ref, o_ref, acc_ref, c0_ref, c1_ref):
    tq = q_ref.shape[0]
    qs = _split_heads(q_ref[...])
    first = _pair_select((tq, LANES))
    zero_c = jnp.zeros((tq, 1), F32)
    pvs, cs = _sb_block(qs, kn_ref[...], vn_ref[...], un_ref[...], [zero_c, zero_c], _causal_mask(tq, tq))
    acc_ref[...] = jnp.where(first, pvs[0], pvs[1])
    c0_ref[...] = cs[0]
    c1_ref[...] = cs[1]
    u = u_ref[...]
    nblk = kc_ref.shape[0] // TK

    def body(jj, _):
        start = pl.multiple_of((nblk - 1 - jj) * TK, TK)
        ks = kc_ref[pl.ds(start, TK), :].astype(BF16)
        vs = vc_ref[pl.ds(start, TK), :].astype(BF16)
        pvs, cs = _sb_block(qs, ks, vs, u, [c0_ref[...], c1_ref[...]], None)
        acc_ref[...] += jnp.where(first, pvs[0], pvs[1])
        c0_ref[...] = cs[0]
        c1_ref[...] = cs[1]
        return 0

    lax.fori_loop(0, nblk, body, 0)
    o_ref[...] = acc_ref[...].astype(o_ref.dtype)


def _attn_sample(q, kb, vb, cache_k, cache_v, batch, seq):
    past = cache_k.shape[1]
    assert past % TK == 0
    return pl.pallas_call(
        _attn_sample_kernel,
        out_shape=jax.ShapeDtypeStruct(q.shape, BF16),
        grid=(batch, PAIRS),
        in_specs=[pl.BlockSpec((seq, LANES), lambda b, p: (b, p)),
                  pl.BlockSpec((seq, LANES), lambda b, p: (b, p)),
                  pl.BlockSpec((seq, LANES), lambda b, p: (b, p)),
                  pl.BlockSpec((None, past, LANES), lambda b, p: (b, 0, p)),
                  pl.BlockSpec((None, past, LANES), lambda b, p: (b, 0, p)),
                  pl.BlockSpec((seq, seq), lambda b, p: (0, 0)),
                  pl.BlockSpec((TK, TK), lambda b, p: (0, 0))],
        out_specs=pl.BlockSpec((seq, LANES), lambda b, p: (b, p)),
        scratch_shapes=[pltpu.VMEM((seq, LANES), F32), pltpu.VMEM((seq, 1), F32), pltpu.VMEM((seq, 1), F32)],
        compiler_params=_params(("parallel", "parallel")),
        name="attn_sample",
    )(q, kb, vb, cache_k, cache_v, _upper_ones(seq), _upper_ones(TK))


def _post_kernel(x_ref, ya_ref, u_ref, vn_ref, sga_ref, sgb_ref, g1_ref, sh2_ref, sc2_ref,
                 ws_ref, bs_ref, wpa_ref, wpb_ref, wo_ref, n2g_ref, wr_ref, br_ref,
                 x1_ref, h2_ref, lg_ref):
    tm = x_ref.shape[0]
    first = _pair_select((CHUNK_B, LANES))
    chunks = []
    for c in range(tm // CHUNK_B):
        rows = slice(c * CHUNK_B, (c + 1) * CHUNK_B)
        parts = []
        for p in range(PAIRS):
            vnp = vn_ref[rows, p * LANES:(p + 1) * LANES]
            parts.append(jnp.where(first, _dot(ws_ref[0, 2 * p], vnp), _dot(ws_ref[0, 2 * p + 1], vnp)))
        s = jnp.concatenate(parts, axis=-1) + bs_ref[0]
        chunks.append((u_ref[rows, :].astype(F32) * s).astype(BF16))
    yb = jnp.concatenate(chunks, axis=0)
    merged = (sga_ref[...].astype(F32) * _dot(ya_ref[...], wpa_ref[...])
              + sgb_ref[...].astype(F32) * _dot(yb, wpb_ref[...]))
    x1 = x_ref[...] + g1_ref[...] * _dot(merged.astype(BF16), wo_ref[...])
    x1_ref[...] = x1
    ms = jnp.mean(x1 * x1, axis=-1, keepdims=True)
    h2 = (x1 * lax.rsqrt(ms + RMS_EPS) * n2g_ref[...]) * (1.0 + sc2_ref[...]) + sh2_ref[...]
    h2_ref[...] = h2
    hi = h2.astype(BF16)
    lo = (h2 - hi.astype(F32)).astype(BF16)
    l2 = _dot(hi, wr_ref[...]) + _dot(lo, wr_ref[...])
    lg_ref[...] = l2[:, :N_EXPERTS] + l2[:, N_EXPERTS:] + br_ref[...]


def _post(x, ya, u, vn, sga, sgb, g1, sh2, sc2, seq, ws_all, bs_all, ws_idx, wpa, wpb, wo, n2g, wr, br):
    r, d = x.shape
    tm = min(TM_POST, r)
    assert r % tm == 0 and tm % CHUNK_B == 0
    g1_o, rpm = _mod_operand(g1, seq, tm)
    sh2_o, _ = _mod_operand(sh2, seq, tm)
    sc2_o, _ = _mod_operand(sc2, seq, tm)
    const = lambda i: (0, 0)
    row = lambda n: pl.BlockSpec((tm, n), lambda i: (i, 0))
    return pl.pallas_call(
        _post_kernel,
        out_shape=[jax.ShapeDtypeStruct((r, d), F32), jax.ShapeDtypeStruct((r, d), F32),
                   jax.ShapeDtypeStruct((r, N_EXPERTS), F32)],
        grid=(r // tm,),
        in_specs=[row(d), row(D_HEADS), row(D_HEADS), row(D_HEADS), row(d), row(d),
                  _mod_spec(tm, g1_o, rpm), _mod_spec(tm, sh2_o, rpm), _mod_spec(tm, sc2_o, rpm),
                  pl.BlockSpec((1, N_HEADS, CHUNK_B, CHUNK_B), lambda i: (ws_idx, 0, 0, 0)),
                  pl.BlockSpec((1, CHUNK_B, D_HEADS), lambda i: (ws_idx, 0, 0)),
                  pl.BlockSpec(wpa.shape, const), pl.BlockSpec(wpb.shape, const), pl.BlockSpec(wo.shape, const),
                  pl.BlockSpec((1, d), const), pl.BlockSpec(wr.shape, const), pl.BlockSpec((1, N_EXPERTS), const)],
        out_specs=[row(d), row(d), row(N_EXPERTS)],
        compiler_params=_params(("parallel",)),
        name="post",
    )(x, ya, u, vn, sga, sgb, g1_o, sh2_o, sc2_o, ws_all, bs_all, wpa, wpb, wo, n2g, wr, br)


_R_IDX, _R_GATE, _R_RANK = 0, TOP_K, 2 * TOP_K


def _route_kernel(lp_ref, ls_ref, tri_ref, o_ref, cnt_ref, carry_ref, *, n_prompt_tiles):
    i = pl.program_id(0)

    @pl.when(i == 0)
    def _():
        carry_ref[...] = jnp.zeros_like(carry_ref)

    l = jnp.where(i < n_prompt_tiles, lp_ref[...], ls_ref[...])
    tr = l.shape[0]
    lane = lax.broadcasted_iota(jnp.int32, l.shape, 1).astype(F32)
    cur = l
    vals, idxs = [], []
    for _ in range(TOP_K):
        m = jnp.max(cur, axis=-1, keepdims=True)
        idx = jnp.min(jnp.where(cur == m, lane, float(N_EXPERTS)), axis=-1, keepdims=True)
        vals.append(m)
        idxs.append(idx)
        cur = jnp.where(lane == idx, -jnp.inf, cur)
    es = [jnp.exp(v - vals[0]) for v in vals]
    den = es[0] + es[1] + es[2] + es[3]
    onehot = jnp.zeros(l.shape, F32)
    for idx in idxs:
        onehot = onehot + (lane == idx).astype(F32)
    before = _dot(tri_ref[...], onehot.astype(BF16)) + carry_ref[...]
    wide = lax.broadcasted_iota(jnp.int32, (tr, LANES), 1)
    out = jnp.zeros((tr, LANES), F32)
    for k in range(TOP_K):
        rank = jnp.sum(jnp.where(lane == idxs[k], before, 0.0), axis=-1, keepdims=True)
        out = jnp.where(wide == _R_IDX + k, idxs[k].astype(F32), out)
        out = jnp.where(wide == _R_GATE + k, es[k] / den, out)
        out = jnp.where(wide == _R_RANK + k, rank, out)
    o_ref[...] = out
    carry_ref[...] += jnp.sum(onehot, axis=0, keepdims=True)
    cnt_ref[...] = carry_ref[...]


def _route(lg_p, lg_s):
    rp, rs = lg_p.shape[0], lg_s.shape[0]
    assert rp % TR == 0 and rs % TR == 0
    npt, nst = rp // TR, rs // TR
    tri = (lax.broadcasted_iota(jnp.int32, (TR, TR), 1) < lax.broadcasted_iota(jnp.int32, (TR, TR), 0)).astype(BF16)
    return pl.pallas_call(
        functools.partial(_route_kernel, n_prompt_tiles=npt),
        out_shape=[jax.ShapeDtypeStruct((rp + rs, LANES), F32), jax.ShapeDtypeStruct((1, N_EXPERTS), F32)],
        grid=(npt + nst,),
        in_specs=[pl.BlockSpec((TR, N_EXPERTS), lambda i: (jnp.minimum(i, npt - 1), 0)),
                  pl.BlockSpec((TR, N_EXPERTS), lambda i: (jnp.maximum(i - npt, 0), 0)),
                  pl.BlockSpec((TR, TR), lambda i: (0, 0))],
        out_specs=[pl.BlockSpec((TR, LANES), lambda i: (i, 0)), pl.BlockSpec((1, N_EXPERTS), lambda i: (0, 0))],
        scratch_shapes=[pltpu.VMEM((1, N_EXPERTS), F32)],
        compiler_params=_params(("arbitrary",)),
        name="route",
    )(lg_p, lg_s, tri)


def _row_copy(src, s, dst, d, sem):
    return pltpu.make_async_copy(src.at[pl.ds(s, 1), :], dst.at[pl.ds(d, 1), :], sem)


def _dispatch_kernel(pad_start_ref, pad_count_ref, tail_ref, dest_ref, hp_ref, hs_ref, xs_ref, zero_ref, sem, psem,
                     *, n_prompt_tiles):
    i = pl.program_id(0)

    def scatter(h_ref):
        def start(t, _):
            for k in range(TOP_K):
                _row_copy(h_ref, t, xs_ref, dest_ref[t * TOP_K + k], sem).start()
            return 0

        def wait(t, _):
            for k in range(TOP_K):
                _row_copy(h_ref, t, xs_ref, dest_ref[t * TOP_K + k], sem).wait()
            return 0

        lax.fori_loop(0, TD, start, 0)
        lax.fori_loop(0, TD, wait, 0)

    @pl.when(i < n_prompt_tiles)
    def _():
        scatter(hp_ref)

    @pl.when(i >= n_prompt_tiles)
    def _():
        scatter(hs_ref)

    @pl.when(i == 0)
    def _():
        zero_ref[...] = jnp.zeros_like(zero_ref)

        def tail(t, _):
            rows = pl.ds(pl.multiple_of(t * TM_E, TM_E), TM_E)
            cp = pltpu.make_async_copy(zero_ref, xs_ref.at[rows, :], psem)
            cp.start()
            cp.wait()
            return 0

        lax.fori_loop(tail_ref[0], tail_ref[1], tail, 0)

        def per_expert(e, _):
            base = pad_start_ref[e]
            n = pad_count_ref[e]

            def start(r, _):
                _row_copy(zero_ref, 0, xs_ref, base + r, psem).start()
                return 0

            def wait(r, _):
                _row_copy(zero_ref, 0, xs_ref, base + r, psem).wait()
                return 0

            lax.fori_loop(0, n, start, 0)
            lax.fori_loop(0, n, wait, 0)
            return 0

        lax.fori_loop(0, N_EXPERTS, per_expert, 0)


def _dispatch(h_p, h_s, dest_flat, pad_start, pad_count, tail_tiles, n_rows):
    rp, d = h_p.shape
    rs = h_s.shape[0]
    assert rp % TD == 0 and rs % TD == 0
    npt, nst = rp // TD, rs // TD
    grid_spec = pltpu.PrefetchScalarGridSpec(
        num_scalar_prefetch=3,
        grid=(npt + nst,),
        in_specs=[pl.BlockSpec((TD * TOP_K,), lambda i, *_: (i,), memory_space=pltpu.SMEM),
                  pl.BlockSpec((TD, d), lambda i, *_: (jnp.minimum(i, npt - 1), 0)),
                  pl.BlockSpec((TD, d), lambda i, *_: (jnp.maximum(i - npt, 0), 0))],
        out_specs=pl.BlockSpec(memory_space=pl.ANY),
        scratch_shapes=[pltpu.VMEM((TM_E, d), F32), pltpu.SemaphoreType.DMA(()), pltpu.SemaphoreType.DMA(())],
    )
    return pl.pallas_call(
        functools.partial(_dispatch_kernel, n_prompt_tiles=npt),
        out_shape=jax.ShapeDtypeStruct((n_rows, d), F32),
        grid_spec=grid_spec,
        compiler_params=_params(("arbitrary",), has_side_effects=True),
        name="dispatch",
    )(pad_start, pad_count, tail_tiles, dest_flat, h_p, h_s)


def _experts_kernel(te_ref, tf_ref, tu_ref, x_ref, w1_ref, b1g_ref, b1l_ref, w2_ref, b2_ref, perm_ref,
                    y_ref, w1s_ref, w2s_ref):
    i = pl.program_id(0)
    dff = w2_ref.shape[1]

    @pl.when(tf_ref[i] == 1)
    def _():
        perm = perm_ref[...]
        half = MXU_DIM // 2
        for c in range(w1_ref.shape[2] // MXU_DIM):
            blk = w1_ref[0, :, c * MXU_DIM:(c + 1) * MXU_DIM].astype(BF16)
            sp = _dot(blk, perm).astype(BF16)
            w1s_ref[:, c * half:(c + 1) * half] = sp[:, :half]
            w1s_ref[:, dff + c * half:dff + (c + 1) * half] = sp[:, half:]
        w2s_ref[...] = w2_ref[0].astype(BF16)

    @pl.when(tu_ref[i] == 1)
    def _():
        x = x_ref[...].astype(BF16)
        hdn = _dot(x, w1s_ref[...])
        glu = jnp.minimum(hdn[:, :dff] + b1g_ref[0], SWIGLU_LIMIT)
        lin = jnp.clip(hdn[:, dff:] + b1l_ref[0], -SWIGLU_LIMIT, SWIGLU_LIMIT)
        act = glu * _sigmoid(SWIGLU_ALPHA * glu) * (lin + 1.0)
        y_ref[...] = _dot(act.astype(BF16), w2s_ref[...]) + b2_ref[0]

    @pl.when(tu_ref[i] == 0)
    def _():
        y_ref[...] = jnp.zeros_like(y_ref)


def _deinterleave_perm():
    j = lax.broadcasted_iota(jnp.int32, (MXU_DIM, MXU_DIM), 0)
    c = lax.broadcasted_iota(jnp.int32, (MXU_DIM, MXU_DIM), 1)
    half = MXU_DIM // 2
    src = jnp.where(c < half, 2 * c, 2 * (c - half) + 1)
    return (j == src).astype(BF16)


def _experts(xs, tile_expert, tile_first, tile_used, w1, b1g, b1l, w2, b2):
    n_rows, d = xs.shape
    n_tiles = n_rows // TM_E
    ne, _, two_f = w1.shape
    dff = two_f // 2
    grid_spec = pltpu.PrefetchScalarGridSpec(
        num_scalar_prefetch=3,
        grid=(n_tiles,),
        in_specs=[pl.BlockSpec((TM_E, d), lambda i, te, tf, tu: (i, 0)),
                  pl.BlockSpec((1, d, two_f), lambda i, te, tf, tu: (te[i], 0, 0)),
                  pl.BlockSpec((1, 1, dff), lambda i, te, tf, tu: (te[i], 0, 0)),
                  pl.BlockSpec((1, 1, dff), lambda i, te, tf, tu: (te[i], 0, 0)),
                  pl.BlockSpec((1, dff, d), lambda i, te, tf, tu: (te[i], 0, 0)),
                  pl.BlockSpec((1, 1, d), lambda i, te, tf, tu: (te[i], 0, 0)),
                  pl.BlockSpec((MXU_DIM, MXU_DIM), lambda i, te, tf, tu: (0, 0))],
        out_specs=pl.BlockSpec((TM_E, d), lambda i, te, tf, tu: (i, 0)),
        scratch_shapes=[pltpu.VMEM((d, two_f), BF16), pltpu.VMEM((dff, d), BF16)],
    )
    return pl.pallas_call(
        _experts_kernel,
        out_shape=jax.ShapeDtypeStruct((n_rows, d), F32),
        grid_spec=grid_spec,
        compiler_params=_params(("arbitrary",)),
        name="experts",
    )(tile_expert, tile_first, tile_used, xs, w1, b1g, b1l, w2, b2, _deinterleave_perm())


def _combine_kernel(dest_ref, x1_ref, rt_ref, g2_ref, y_ref, o_ref, buf_ref, sem):
    tc = x1_ref.shape[0]

    def start(t, _):
        for k in range(TOP_K):
            pltpu.make_async_copy(y_ref.at[pl.ds(dest_ref[t * TOP_K + k], 1), :],
                                  buf_ref.at[k, pl.ds(t, 1), :], sem).start()
        return 0

    def wait(t, _):
        for k in range(TOP_K):
            pltpu.make_async_copy(y_ref.at[pl.ds(dest_ref[t * TOP_K + k], 1), :],
                                  buf_ref.at[k, pl.ds(t, 1), :], sem).wait()
        return 0

    lax.fori_loop(0, tc, start, 0)
    lax.fori_loop(0, tc, wait, 0)
    rt = rt_ref[...]
    moe = jnp.zeros(o_ref.shape, F32)
    for k in range(TOP_K):
        moe = moe + rt[:, _R_GATE + k:_R_GATE + k + 1] * buf_ref[k]
    o_ref[...] = x1_ref[...] + g2_ref[...] * moe


def _combine(x1, route, dest_flat, g2, seq, y_rows, row_offset):
    r, d = x1.shape
    tc = min(TC, r)
    assert r % tc == 0 and row_offset % tc == 0
    off = row_offset // tc
    g2_o, rpm = _mod_operand(g2, seq, tc)
    g2_spec = _mod_spec(tc, g2_o, rpm)
    return pl.pallas_call(
        _combine_kernel,
        out_shape=jax.ShapeDtypeStruct((r, d), F32),
        grid=(r // tc,),
        in_specs=[pl.BlockSpec((tc * TOP_K,), lambda i: (i + off,), memory_space=pltpu.SMEM),
                  pl.BlockSpec((tc, d), lambda i: (i, 0)),
                  pl.BlockSpec((tc, LANES), lambda i: (i + off, 0)),
                  g2_spec,
                  pl.BlockSpec(memory_space=pl.ANY)],
        out_specs=pl.BlockSpec((tc, d), lambda i: (i, 0)),
        scratch_shapes=[pltpu.VMEM((TOP_K, tc, d), F32), pltpu.SemaphoreType.DMA(())],
        compiler_params=_params(("arbitrary",)),
        name="combine",
    )(dest_flat, x1, route, g2_o, y_rows)


def _gmlp_tables(w_s, b_s, seq_s):
    g = w_s.shape[0]
    tril = jnp.tril(w_s)
    short = jnp.tril(w_s[:, :seq_s, :seq_s])
    rep = CHUNK_B // seq_s
    blockdiag = jnp.zeros_like(w_s)
    for r in range(rep):
        blockdiag = blockdiag.at[:, r * seq_s:(r + 1) * seq_s, r * seq_s:(r + 1) * seq_s].set(short)
    ws_all = jnp.stack([tril, blockdiag]).astype(BF16)
    bs_full = jnp.repeat(b_s.T, HEAD_DIM, axis=1)
    bs_short = jnp.tile(jnp.repeat(b_s[:, :seq_s].T, HEAD_DIM, axis=1), (rep, 1))
    return ws_all, jnp.stack([bs_full, bs_short])


def _layer(xp, xs, cache_k, cache_v, cp, cs, p):
    bp, tp, d = xp.shape
    bs_, ts, _ = xs.shape
    assert tp % CHUNK_B == 0 and CHUNK_B % ts == 0
    rp, rs = bp * tp, bs_ * ts
    n_tok = rp + rs

    mod = _ada(jnp.concatenate([cp, cs], axis=0), p['w_ada'], p['b_ada'])
    mods = [mod[:, j * d:(j + 1) * d] for j in range(6)]
    mp = [m[:bp] for m in mods]
    msm = [m[bp:] for m in mods]

    tile_h = lambda g: jnp.tile(g, N_HEADS).reshape(1, D_HEADS)
    bd = jnp.kron(jnp.eye(MXU_DIM // HEAD_DIM, dtype=F32),
                  jnp.full((HEAD_DIM, HEAD_DIM), 1.0 / HEAD_DIM, F32)).astype(BF16)
    w_in_b = p['w_in'].astype(BF16)
    n1g = p['norm1_g'].reshape(1, d)
    common = (n1g, w_in_b, tile_h(p['q_norm_g']), tile_h(p['k_norm_g']), tile_h(p['v_norm_g']), bd)
    q_p, k_p, kb_p, v_p, vb_p, u_p, vn_p, sga_p, sgb_p = _inproj(
        xp.reshape(rp, d), mp[0], mp[1], tp, *common, want_vn32=False)
    q_s, k_s, kb_s, v_s, vb_s, u_s, vn_s, sga_s, sgb_s, vn32_s = _inproj(
        xs.reshape(rs, d), msm[0], msm[1], ts, *common, want_vn32=True)

    ya_p = _attn_prompt(q_p, kb_p, vb_p, bp, tp)
    past = cache_k.shape[1]
    ya_s = _attn_sample(q_s, kb_s, vb_s, cache_k.reshape(bs_, past, D_HEADS), cache_v.reshape(bs_, past, D_HEADS),
                        bs_, ts)

    ws_all, bs_all = _gmlp_tables(p['w_s'], p['b_s'], ts)
    wr = p['w_router']
    wr_hi = wr.astype(BF16)
    wr_lo = (wr - wr_hi.astype(F32)).astype(BF16)
    post_w = (p['w_pa'].astype(BF16), p['w_pb'].astype(BF16), p['w_o'].astype(BF16), p['norm2_g'].reshape(1, d),
              jnp.concatenate([wr_hi, wr_lo], axis=1), p['b_router'].reshape(1, N_EXPERTS))
    x1_p, h2_p, lg_p = _post(xp.reshape(rp, d), ya_p, u_p, vn_p, sga_p, sgb_p, mp[2], mp[3], mp[4], tp,
                             ws_all, bs_all, 0, *post_w)
    x1_s, h2_s, lg_s = _post(xs.reshape(rs, d), ya_s, u_s, vn_s, sga_s, sgb_s, msm[2], msm[3], msm[4], ts,
                             ws_all, bs_all, 1, *post_w)

    route, counts = _route(lg_p, lg_s)

    counts = counts[0].astype(jnp.int32)
    padded = (counts + TM_E - 1) // TM_E * TM_E
    pend = jnp.cumsum(padded)
    pstart = pend - padded
    idx = route[:, _R_IDX:_R_IDX + TOP_K].astype(jnp.int32)
    rank = route[:, _R_RANK:_R_RANK + TOP_K].astype(jnp.int32)
    dest_flat = (pstart[idx] + rank).reshape(n_tok * TOP_K)
    n_tiles = -(-(n_tok * TOP_K) // TM_E) + N_EXPERTS
    tile_start = jnp.arange(n_tiles, dtype=jnp.int32) * TM_E
    tile_used = (tile_start < pend[-1]).astype(jnp.int32)
    last_expert = jnp.searchsorted(pend, pend[-1] - 1, side='right').astype(jnp.int32)
    tile_expert = jnp.minimum(jnp.searchsorted(pend, tile_start, side='right').astype(jnp.int32), last_expert)
    tile_first = jnp.concatenate([jnp.ones((1,), jnp.int32),
                                  (tile_expert[1:] != tile_expert[:-1]).astype(jnp.int32)])

    tail_tiles = jnp.stack([pend[-1] // TM_E, jnp.int32(n_tiles)]).astype(jnp.int32)
    xs_rows = _dispatch(h2_p, h2_s, dest_flat, pstart + counts, padded - counts, tail_tiles, n_tiles * TM_E)
    b1 = p['b1']
    ne = b1.shape[0]
    y_rows = _experts(xs_rows, tile_expert, tile_first, tile_used, p['w1'],
                      b1[:, 0::2].reshape(ne, 1, -1), b1[:, 1::2].reshape(ne, 1, -1),
                      p['w2'], p['b2'].reshape(ne, 1, d))
    out_p = _combine(x1_p, route, dest_flat, mp[5], tp, y_rows, 0)
    out_s = _combine(x1_s, route, dest_flat, msm[5], ts, y_rows, rp)

    shp = lambda a, b, t: a.reshape(b, t, N_HEADS, HEAD_DIM)
    return (out_p.reshape(bp, tp, d), out_s.reshape(bs_, ts, d),
            shp(k_p, bp, tp), shp(v_p, bp, tp), shp(k_s, bs_, ts), shp(v_s, bs_, ts), shp(vn32_s, bs_, ts))


def kernel(x_prompt, x_sample, cache_k, cache_v, c_prompt, c_sample, norm1_g, norm2_g, w_ada, b_ada, w_in,
           q_norm_g, k_norm_g, v_norm_g, w_s, b_s, w_pa, w_pb, w_o, w_router, b_router, w1, b1, w2, b2):
    depth = w_in.shape[0]
    xp, xs = x_prompt, x_sample
    kp, vp, ks, vs, vbs = [], [], [], [], []
    for l in range(depth):
        p = {'norm1_g': norm1_g[l], 'norm2_g': norm2_g[l], 'w_ada': w_ada[l], 'b_ada': b_ada[l],
             'w_in': w_in[l], 'q_norm_g': q_norm_g[l], 'k_norm_g': k_norm_g[l], 'v_norm_g': v_norm_g[l],
             'w_s': w_s[l], 'b_s': b_s[l], 'w_pa': w_pa[l], 'w_pb': w_pb[l], 'w_o': w_o[l],
             'w_router': w_router[l], 'b_router': b_router[l], 'w1': w1[l], 'b1': b1[l],
             'w2': w2[l], 'b2': b2[l]}
        xp, xs, k_p, v_p, k_s, v_s, vn_s = _layer(xp, xs, cache_k[l], cache_v[l], c_prompt, c_sample, p)
        kp.append(k_p); vp.append(v_p); ks.append(k_s); vs.append(v_s); vbs.append(vn_s)
    return (xp, xs, jnp.stack(kp), jnp.stack(vp), jnp.stack(ks), jnp.stack(vs), jnp.stack(vbs))
```

```python
import functools

import jax
import jax.numpy as jnp
from jax import lax
from jax.experimental import pallas as pl
from jax.experimental.pallas import tpu as pltpu

F32 = jnp.float32
BF16 = jnp.bfloat16

LANES = 128
MXU_DIM = 256
VMEM_LIMIT = 56 * 1024 * 1024

HEAD_DIM = 64
N_HEADS = 8
D_HEADS = N_HEADS * HEAD_DIM
PAIRS = D_HEADS // LANES
CHUNK_B = 128
N_EXPERTS = 32
TOP_K = 4
RMS_EPS = 1e-6
SWIGLU_ALPHA = 1.702
SWIGLU_LIMIT = 7.0

TM_IN = 512
TQ = 256
TK = 256
TM_POST = 256
TR = 512
TD = 256
TM_E = 512
TC = 256
ROW_DMA_UNROLL = 8


def _params(sem, **kw):
    return pltpu.CompilerParams(dimension_semantics=sem, vmem_limit_bytes=VMEM_LIMIT, **kw)


def _dot(a, b):
    return jnp.dot(a, b, preferred_element_type=F32)


def _dot_nt(a, b):
    return lax.dot_general(a, b, (((1,), (1,)), ((), ())), preferred_element_type=F32)


def _gelu(x):
    return jax.nn.gelu(x)


def _sigmoid(x):
    return 1.0 / (1.0 + jnp.exp(-x))


def _ada_kernel(c_ref, w_ref, b_ref, o_ref):
    c = c_ref[...]
    s = c * _sigmoid(c)
    o_ref[...] = jnp.dot(s, w_ref[...], preferred_element_type=F32,
                         precision=lax.Precision.HIGHEST) + b_ref[...]


def _ada(c, w_ada, b_ada):
    nb, d = c.shape
    n = w_ada.shape[1]
    tn = 1024
    return pl.pallas_call(
        _ada_kernel,
        out_shape=jax.ShapeDtypeStruct((nb, n), F32),
        grid=(n // tn,),
        in_specs=[pl.BlockSpec((nb, d), lambda j: (0, 0)),
                  pl.BlockSpec((d, tn), lambda j: (0, j)),
                  pl.BlockSpec((1, tn), lambda j: (0, j))],
        out_specs=pl.BlockSpec((nb, tn), lambda j: (0, j)),
        compiler_params=_params(("parallel",)),
        name="ada",
    )(c, w_ada, b_ada.reshape(1, n))


def _head_mean_sq(p, bd):
    sq = (p * p).astype(BF16)
    return jnp.concatenate([_dot(sq[:, :MXU_DIM], bd), _dot(sq[:, MXU_DIM:], bd)], axis=-1)


def _inproj_kernel(x_ref, sh_ref, sc_ref, g_ref, w_ref, qg_ref, kg_ref, vg_ref, bd_ref,
                   q_ref, k32_ref, kb_ref, v32_ref, vb_ref, u_ref, vnb_ref, sga_ref, sgb_ref, *vn32_ref):
    x = x_ref[...]
    ms = jnp.mean(x * x, axis=-1, keepdims=True)
    xn = x * lax.rsqrt(ms + RMS_EPS) * g_ref[...]
    h = (xn * (1.0 + sc_ref[...]) + sh_ref[...]).astype(BF16)
    bd = bd_ref[...]
    d = x.shape[1]

    def sec(a, n):
        return _dot(h, w_ref[:, a:a + n])

    o = 0
    pq = sec(o, D_HEADS); o += D_HEADS
    q = pq * lax.rsqrt(_head_mean_sq(pq, bd) + RMS_EPS) * qg_ref[...]
    q_ref[...] = (q * (HEAD_DIM ** -0.5)).astype(BF16)
    pk = sec(o, D_HEADS); o += D_HEADS
    k = pk * lax.rsqrt(_head_mean_sq(pk, bd) + RMS_EPS) * kg_ref[...]
    k32_ref[...] = k
    kb_ref[...] = k.astype(BF16)
    pv = sec(o, D_HEADS); o += D_HEADS
    v32_ref[...] = pv
    vb_ref[...] = pv.astype(BF16)
    pu = sec(o, D_HEADS); o += D_HEADS
    u_ref[...] = _gelu(pu).astype(BF16)
    pvb = _gelu(sec(o, D_HEADS)); o += D_HEADS
    vn = pvb * lax.rsqrt(_head_mean_sq(pvb, bd) + RMS_EPS) * vg_ref[...]
    vnb_ref[...] = vn.astype(BF16)
    if vn32_ref:
        vn32_ref[0][...] = vn
    sga_ref[...] = _sigmoid(sec(o, d)).astype(BF16); o += d
    sgb_ref[...] = _sigmoid(sec(o, d)).astype(BF16)


def _mod_spec(tm, arr, rows_per_mod):
    if rows_per_mod is None:
        return pl.BlockSpec((tm, arr.shape[-1]), lambda i: (i, 0))
    return pl.BlockSpec((None, 1, arr.shape[-1]), lambda i: ((i * tm) // rows_per_mod, 0, 0))


def _mod_operand(m, seq, tm):
    if seq % tm == 0:
        return m[:, None, :], seq
    return jnp.repeat(m, seq, axis=0), None


def _inproj(x, sh, sc, seq, g, w_in_b, qg, kg, vg, bd, want_vn32):
    r, d = x.shape
    tm = min(TM_IN, r)
    assert r % tm == 0
    sh_o, rpm = _mod_operand(sh, seq, tm)
    sc_o, _ = _mod_operand(sc, seq, tm)
    ncol = w_in_b.shape[1]
    const = lambda i: (0, 0)
    row = lambda n: pl.BlockSpec((tm, n), lambda i: (i, 0))
    outs = [jax.ShapeDtypeStruct((r, D_HEADS), BF16),
            jax.ShapeDtypeStruct((r, D_HEADS), F32),
            jax.ShapeDtypeStruct((r, D_HEADS), BF16),
            jax.ShapeDtypeStruct((r, D_HEADS), F32),
            jax.ShapeDtypeStruct((r, D_HEADS), BF16),
            jax.ShapeDtypeStruct((r, D_HEADS), BF16),
            jax.ShapeDtypeStruct((r, D_HEADS), BF16),
            jax.ShapeDtypeStruct((r, d), BF16),
            jax.ShapeDtypeStruct((r, d), BF16)]
    out_specs = [row(D_HEADS)] * 7 + [row(d), row(d)]
    if want_vn32:
        outs.append(jax.ShapeDtypeStruct((r, D_HEADS), F32))
        out_specs.append(row(D_HEADS))
    return pl.pallas_call(
        _inproj_kernel,
        out_shape=outs,
        grid=(r // tm,),
        in_specs=[row(d), _mod_spec(tm, sh_o, rpm), _mod_spec(tm, sc_o, rpm),
                  pl.BlockSpec((1, d), const),
                  pl.BlockSpec((d, ncol), const, pipeline_mode=pl.Buffered(1)),
                  pl.BlockSpec((1, D_HEADS), const), pl.BlockSpec((1, D_HEADS), const),
                  pl.BlockSpec((1, D_HEADS), const), pl.BlockSpec((MXU_DIM, MXU_DIM), const)],
        out_specs=out_specs,
        compiler_params=_params(("parallel",)),
        name="inproj",
    )(x, sh_o, sc_o, g, w_in_b, qg, kg, vg, bd)


LOG_WEIGHT_CUTOFF = -104.0


def _sb_weights(z, u, c, mask):
    lom = -(jnp.maximum(z, 0.0) + jnp.log(1.0 + jnp.exp(-jnp.abs(z))))
    if mask is not None:
        lom = jnp.where(mask, lom, 0.0)
    hi = lom.astype(BF16)
    lo = (lom - hi.astype(F32)).astype(BF16)
    later = _dot(hi, u) + _dot(lo, u)
    w = jnp.exp(z + lom + later + c)
    if mask is not None:
        w = jnp.where(mask, w, 0.0)
    return w, c + jnp.sum(lom, axis=-1, keepdims=True)


def _pair_select(shape):
    return lax.broadcasted_iota(jnp.int32, shape, 1) < HEAD_DIM


def _causal_mask(m, tq, tk):
    row = lax.broadcasted_iota(jnp.int32, (m, tk), 0) & (tq - 1)
    return lax.broadcasted_iota(jnp.int32, (m, tk), 1) < row


def _attn_prompt_kernel(q_ref, k_ref, v_ref, u_ref, o_ref, acc_ref, c_ref):
    i = pl.program_id(2)
    tq = q_ref.shape[0]
    q = q_ref[...]
    first = _pair_select((tq, LANES))
    zero = jnp.zeros_like(q)
    q2 = jnp.concatenate([jnp.where(first, q, zero), jnp.where(first, zero, q)], axis=0)
    u = u_ref[...]

    def block(j, c, mask):
        start = pl.multiple_of(j * TK, TK)
        ks = k_ref[pl.ds(start, TK), :]
        vs = v_ref[pl.ds(start, TK), :]
        w, c = _sb_weights(_dot_nt(q2, ks), u, c, mask)
        wb = w.astype(BF16)
        return jnp.where(first, _dot(wb[:tq], vs), _dot(wb[tq:], vs)), c

    pv, c = block(i, jnp.zeros((2 * tq, 1), F32), _causal_mask(2 * tq, tq, TK))
    acc_ref[...] = pv
    c_ref[...] = c

    def cond(state):
        j, cmax = state
        return jnp.logical_and(j >= 0, cmax > LOG_WEIGHT_CUTOFF)

    def body(state):
        j, _ = state
        pv, c = block(j, c_ref[...], None)
        acc_ref[...] += pv
        c_ref[...] = c
        return j - 1, jnp.max(c)

    lax.while_loop(cond, body, (i - 1, jnp.max(c)))
    o_ref[...] = acc_ref[...].astype(o_ref.dtype)


def _upper_ones(n):
    return (lax.broadcasted_iota(jnp.int32, (n, n), 0) > lax.broadcasted_iota(jnp.int32, (n, n), 1)).astype(BF16)


def _attn_prompt(q, kb, vb, batch, seq):
    assert TQ == TK and seq % TQ == 0
    nq = seq // TQ
    return pl.pallas_call(
        _attn_prompt_kernel,
        out_shape=jax.ShapeDtypeStruct(q.shape, BF16),
        grid=(batch, PAIRS, nq),
        in_specs=[pl.BlockSpec((TQ, LANES), lambda b, p, i: (b * nq + i, p)),
                  pl.BlockSpec((seq, LANES), lambda b, p, i: (b, p)),
                  pl.BlockSpec((seq, LANES), lambda b, p, i: (b, p)),
                  pl.BlockSpec((TK, TK), lambda b, p, i: (0, 0))],
        out_specs=pl.BlockSpec((TQ, LANES), lambda b, p, i: (b * nq + i, p)),
        scratch_shapes=[pltpu.VMEM((TQ, LANES), F32), pltpu.VMEM((2 * TQ, 1), F32)],
        compiler_params=_params(("parallel", "parallel", "arbitrary")),
        name="attn_prompt",
    )(q, kb, vb, _upper_ones(TK))


def _attn_sample_kernel(q_ref, kn_ref, vn_ref, kc_ref, vc_ref, un_ref, u_ref, o_ref,
                        kbuf_ref, vbuf_ref, sem, acc_ref, c_ref):
    b = pl.program_id(0)
    tq = q_ref.shape[0]
    nblk = kc_ref.shape[3] // TK
    head = lambda a, h: a[:, h * HEAD_DIM:(h + 1) * HEAD_DIM]
    q = q_ref[...]
    qh = [head(q, h) for h in range(N_HEADS)]

    def copies(j, slot):
        rows = pl.ds(pl.multiple_of(j * TK, TK), TK)
        return (pltpu.make_async_copy(kc_ref.at[b, :, :, rows], kbuf_ref.at[slot], sem.at[0, slot]),
                pltpu.make_async_copy(vc_ref.at[b, :, :, rows], vbuf_ref.at[slot], sem.at[1, slot]))

    def start(j, slot):
        for cp in copies(j, slot):
            cp.start()

    def wait(j, slot):
        for cp in copies(j, slot):
            cp.wait()

    def sweep(score, apply, u, c, mask):
        z = jnp.concatenate([score(qh[h], h) for h in range(N_HEADS)], axis=0)
        w, c = _sb_weights(z, u, c, mask)
        wb = w.astype(BF16)
        pv = jnp.concatenate([apply(wb[h * tq:(h + 1) * tq], h) for h in range(N_HEADS)], axis=-1)
        return pv, c

    start(nblk - 1, 0)
    kn = kn_ref[...]
    vn = vn_ref[...]
    pv, c = sweep(lambda qq, h: _dot_nt(qq, head(kn, h)), lambda ww, h: _dot(ww, head(vn, h)), un_ref[...],
                  jnp.zeros((N_HEADS * tq, 1), F32), _causal_mask(N_HEADS * tq, tq, tq))
    acc_ref[...] = pv
    c_ref[...] = c
    u = u_ref[...]

    def cond(state):
        j, _, cmax = state
        return jnp.logical_and(j >= 0, cmax > LOG_WEIGHT_CUTOFF)

    def body(state):
        j, slot, _ = state
        wait(j, slot)

        @pl.when(j > 0)
        def _():
            start(j - 1, 1 - slot)

        pv, c = sweep(lambda qq, h: _dot(qq, kbuf_ref[slot, h].astype(BF16)),
                      lambda ww, h: _dot_nt(ww, vbuf_ref[slot, h].astype(BF16)), u, c_ref[...], None)
        acc_ref[...] += pv
        c_ref[...] = c
        return j - 1, 1 - slot, jnp.max(c)

    j, slot, _ = lax.while_loop(cond, body, (nblk - 1, 0, jnp.max(c)))

    @pl.when(j >= 0)
    def _():
        wait(j, slot)

    o_ref[...] = acc_ref[...].astype(o_ref.dtype)


def _attn_sample(q, kb, vb, cache_k, cache_v, batch, seq):
    past = cache_k.shape[1]
    assert past % TK == 0 and past >= TK
    cache_k = jnp.transpose(cache_k, (0, 2, 3, 1))
    cache_v = jnp.transpose(cache_v, (0, 2, 3, 1))
    row = pl.BlockSpec((seq, D_HEADS), lambda b: (b, 0))
    return pl.pallas_call(
        _attn_sample_kernel,
        out_shape=jax.ShapeDtypeStruct(q.shape, BF16),
        grid=(batch,),
        in_specs=[row, row, row,
                  pl.BlockSpec(memory_space=pl.ANY), pl.BlockSpec(memory_space=pl.ANY),
                  pl.BlockSpec((seq, seq), lambda b: (0, 0)),
                  pl.BlockSpec((TK, TK), lambda b: (0, 0))],
        out_specs=row,
        scratch_shapes=[pltpu.VMEM((2, N_HEADS, HEAD_DIM, TK), F32), pltpu.VMEM((2, N_HEADS, HEAD_DIM, TK), F32),
                        pltpu.SemaphoreType.DMA((2, 2)),
                        pltpu.VMEM((seq, D_HEADS), F32), pltpu.VMEM((N_HEADS * seq, 1), F32)],
        compiler_params=_params(("arbitrary",)),
        name="attn_sample",
    )(q, kb, vb, cache_k, cache_v, _upper_ones(seq), _upper_ones(TK))


def _pack_bf16_pairs(x):
    half = x.shape[1] // 2
    a = pltpu.bitcast(x[:, :half], jnp.uint32)
    b = pltpu.bitcast(x[:, half:], jnp.uint32)
    return a | (b >> 16)


def _unpack_bf16_pairs(p):
    a = pltpu.bitcast(p & jnp.uint32(0xFFFF0000), F32).astype(BF16)
    b = pltpu.bitcast(p << 16, F32).astype(BF16)
    return jnp.concatenate([a, b], axis=1)


def _post_kernel(x_ref, ya_ref, u_ref, vn_ref, sga_ref, sgb_ref, g1_ref, sh2_ref, sc2_ref,
                 ws_ref, bs_ref, wpa_ref, wpb_ref, wo_ref, n2g_ref, wr_ref, br_ref,
                 x1_ref, h2_ref, lg_ref):
    tm = x_ref.shape[0]
    first = _pair_select((CHUNK_B, LANES))
    chunks = []
    for c in range(tm // CHUNK_B):
        rows = slice(c * CHUNK_B, (c + 1) * CHUNK_B)
        parts = []
        for p in range(PAIRS):
            vnp = vn_ref[rows, p * LANES:(p + 1) * LANES]
            parts.append(jnp.where(first, _dot(ws_ref[0, 2 * p], vnp), _dot(ws_ref[0, 2 * p + 1], vnp)))
        s = jnp.concatenate(parts, axis=-1) + bs_ref[0]
        chunks.append((u_ref[rows, :].astype(F32) * s).astype(BF16))
    yb = jnp.concatenate(chunks, axis=0)
    merged = (sga_ref[...].astype(F32) * _dot(ya_ref[...], wpa_ref[...])
              + sgb_ref[...].astype(F32) * _dot(yb, wpb_ref[...]))
    x1 = x_ref[...] + g1_ref[...] * _dot(merged.astype(BF16), wo_ref[...])
    x1_ref[...] = x1
    ms = jnp.mean(x1 * x1, axis=-1, keepdims=True)
    h2 = (x1 * lax.rsqrt(ms + RMS_EPS) * n2g_ref[...]) * (1.0 + sc2_ref[...]) + sh2_ref[...]
    hi = h2.astype(BF16)
    hf = hi.astype(F32)
    lo = (h2 - hf).astype(BF16)
    h2_ref[...] = _pack_bf16_pairs(hf)
    l2 = _dot(hi, wr_ref[...]) + _dot(lo, wr_ref[...])
    lg_ref[...] = l2[:, :N_EXPERTS] + l2[:, N_EXPERTS:] + br_ref[...]


def _post(x, ya, u, vn, sga, sgb, g1, sh2, sc2, seq, ws_all, bs_all, ws_idx, wpa, wpb, wo, n2g, wr, br):
    r, d = x.shape
    tm = min(TM_POST, r)
    assert r % tm == 0 and tm % CHUNK_B == 0
    g1_o, rpm = _mod_operand(g1, seq, tm)
    sh2_o, _ = _mod_operand(sh2, seq, tm)
    sc2_o, _ = _mod_operand(sc2, seq, tm)
    const = lambda i: (0, 0)
    row = lambda n: pl.BlockSpec((tm, n), lambda i: (i, 0))
    return pl.pallas_call(
        _post_kernel,
        out_shape=[jax.ShapeDtypeStruct((r, d), F32), jax.ShapeDtypeStruct((r, d // 2), jnp.uint32),
                   jax.ShapeDtypeStruct((r, N_EXPERTS), F32)],
        grid=(r // tm,),
        in_specs=[row(d), row(D_HEADS), row(D_HEADS), row(D_HEADS), row(d), row(d),
                  _mod_spec(tm, g1_o, rpm), _mod_spec(tm, sh2_o, rpm), _mod_spec(tm, sc2_o, rpm),
                  pl.BlockSpec((1, N_HEADS, CHUNK_B, CHUNK_B), lambda i: (ws_idx, 0, 0, 0)),
                  pl.BlockSpec((1, CHUNK_B, D_HEADS), lambda i: (ws_idx, 0, 0)),
                  pl.BlockSpec(wpa.shape, const), pl.BlockSpec(wpb.shape, const), pl.BlockSpec(wo.shape, const),
                  pl.BlockSpec((1, d), const), pl.BlockSpec(wr.shape, const), pl.BlockSpec((1, N_EXPERTS), const)],
        out_specs=[row(d), row(d // 2), row(N_EXPERTS)],
        compiler_params=_params(("parallel",)),
        name="post",
    )(x, ya, u, vn, sga, sgb, g1_o, sh2_o, sc2_o, ws_all, bs_all, wpa, wpb, wo, n2g, wr, br)


_R_IDX, _R_GATE, _R_RANK = 0, TOP_K, 2 * TOP_K


def _route_kernel(lp_ref, ls_ref, tri_ref, o_ref, cnt_ref, carry_ref, *, n_prompt_tiles):
    i = pl.program_id(0)

    @pl.when(i == 0)
    def _():
        carry_ref[...] = jnp.zeros_like(carry_ref)

    l = jnp.where(i < n_prompt_tiles, lp_ref[...], ls_ref[...])
    tr = l.shape[0]
    lane = lax.broadcasted_iota(jnp.int32, l.shape, 1).astype(F32)
    cur = l
    vals, idxs = [], []
    for _ in range(TOP_K):
        m = jnp.max(cur, axis=-1, keepdims=True)
        idx = jnp.min(jnp.where(cur == m, lane, float(N_EXPERTS)), axis=-1, keepdims=True)
        vals.append(m)
        idxs.append(idx)
        cur = jnp.where(lane == idx, -jnp.inf, cur)
    es = [jnp.exp(v - vals[0]) for v in vals]
    den = es[0] + es[1] + es[2] + es[3]
    onehot = jnp.zeros(l.shape, F32)
    for idx in idxs:
        onehot = onehot + (lane == idx).astype(F32)
    before = _dot(tri_ref[...], onehot.astype(BF16)) + carry_ref[...]
    wide = lax.broadcasted_iota(jnp.int32, (tr, LANES), 1)
    out = jnp.zeros((tr, LANES), F32)
    for k in range(TOP_K):
        rank = jnp.sum(jnp.where(lane == idxs[k], before, 0.0), axis=-1, keepdims=True)
        out = jnp.where(wide == _R_IDX + k, idxs[k].astype(F32), out)
        out = jnp.where(wide == _R_GATE + k, es[k] / den, out)
        out = jnp.where(wide == _R_RANK + k, rank, out)
    o_ref[...] = out
    carry_ref[...] += jnp.sum(onehot, axis=0, keepdims=True)
    cnt_ref[...] = carry_ref[...]


def _route(lg_p, lg_s):
    rp, rs = lg_p.shape[0], lg_s.shape[0]
    assert rp % TR == 0 and rs % TR == 0
    npt, nst = rp // TR, rs // TR
    tri = (lax.broadcasted_iota(jnp.int32, (TR, TR), 1) < lax.broadcasted_iota(jnp.int32, (TR, TR), 0)).astype(BF16)
    return pl.pallas_call(
        functools.partial(_route_kernel, n_prompt_tiles=npt),
        out_shape=[jax.ShapeDtypeStruct((rp + rs, LANES), F32), jax.ShapeDtypeStruct((1, N_EXPERTS), F32)],
        grid=(npt + nst,),
        in_specs=[pl.BlockSpec((TR, N_EXPERTS), lambda i: (jnp.minimum(i, npt - 1), 0)),
                  pl.BlockSpec((TR, N_EXPERTS), lambda i: (jnp.maximum(i - npt, 0), 0)),
                  pl.BlockSpec((TR, TR), lambda i: (0, 0))],
        out_specs=[pl.BlockSpec((TR, LANES), lambda i: (i, 0)), pl.BlockSpec((1, N_EXPERTS), lambda i: (0, 0))],
        scratch_shapes=[pltpu.VMEM((1, N_EXPERTS), F32)],
        compiler_params=_params(("arbitrary",)),
        name="route",
    )(lg_p, lg_s, tri)


def _row_copy(src, s, dst, d, sem):
    return pltpu.make_async_copy(src.at[pl.ds(s, 1), :], dst.at[pl.ds(d, 1), :], sem)


def _dispatch_kernel(pad_start_ref, pad_count_ref, tail_ref, dest_ref, hp_ref, hs_ref, xs_ref, zero_ref, sem, psem,
                     *, n_prompt_tiles):
    i = pl.program_id(0)

    def scatter(h_ref):
        def copies(g):
            for tt in range(ROW_DMA_UNROLL):
                t = g * ROW_DMA_UNROLL + tt
                for k in range(TOP_K):
                    yield k, _row_copy(h_ref, t, xs_ref, dest_ref[t * TOP_K + k], sem)

        def start(g, _):
            for k, cp in copies(g):
                cp.start(priority=k % 2)
            return 0

        def wait(g, _):
            for _, cp in copies(g):
                cp.wait()
            return 0

        lax.fori_loop(0, TD // ROW_DMA_UNROLL, start, 0)
        lax.fori_loop(0, TD // ROW_DMA_UNROLL, wait, 0)

    @pl.when(i < n_prompt_tiles)
    def _():
        scatter(hp_ref)

    @pl.when(i >= n_prompt_tiles)
    def _():
        scatter(hs_ref)

    @pl.when(i == 0)
    def _():
        zero_ref[...] = jnp.zeros_like(zero_ref)

        def tail(t, _):
            rows = pl.ds(pl.multiple_of(t * TM_E, TM_E), TM_E)
            cp = pltpu.make_async_copy(zero_ref, xs_ref.at[rows, :], psem)
            cp.start()
            cp.wait()
            return 0

        lax.fori_loop(tail_ref[0], tail_ref[1], tail, 0)

        def per_expert(e, _):
            base = pad_start_ref[e]
            n = pad_count_ref[e]

            def start(r, _):
                _row_copy(zero_ref, 0, xs_ref, base + r, psem).start()
                return 0

            def wait(r, _):
                _row_copy(zero_ref, 0, xs_ref, base + r, psem).wait()
                return 0

            lax.fori_loop(0, n, start, 0)
            lax.fori_loop(0, n, wait, 0)
            return 0

        lax.fori_loop(0, N_EXPERTS, per_expert, 0)


def _dispatch(h_p, h_s, dest_flat, pad_start, pad_count, tail_tiles, n_rows):
    rp, d = h_p.shape
    rs = h_s.shape[0]
    assert rp % TD == 0 and rs % TD == 0
    npt, nst = rp // TD, rs // TD
    grid_spec = pltpu.PrefetchScalarGridSpec(
        num_scalar_prefetch=3,
        grid=(npt + nst,),
        in_specs=[pl.BlockSpec((TD * TOP_K,), lambda i, *_: (i,), memory_space=pltpu.SMEM),
                  pl.BlockSpec((TD, d), lambda i, *_: (jnp.minimum(i, npt - 1), 0)),
                  pl.BlockSpec((TD, d), lambda i, *_: (jnp.maximum(i - npt, 0), 0))],
        out_specs=pl.BlockSpec(memory_space=pl.ANY),
        scratch_shapes=[pltpu.VMEM((TM_E, d), h_p.dtype), pltpu.SemaphoreType.DMA(()), pltpu.SemaphoreType.DMA(())],
    )
    return pl.pallas_call(
        functools.partial(_dispatch_kernel, n_prompt_tiles=npt),
        out_shape=jax.ShapeDtypeStruct((n_rows, d), h_p.dtype),
        grid_spec=grid_spec,
        compiler_params=_params(("arbitrary",), has_side_effects=True),
        name="dispatch",
    )(pad_start, pad_count, tail_tiles, dest_flat, h_p, h_s)


def _experts_kernel(te_ref, tf_ref, tu_ref, x_ref, w1_ref, b1g_ref, b1l_ref, w2_ref, b2_ref, perm_ref,
                    y_ref, w1s_ref, w2s_ref):
    i = pl.program_id(0)
    dff = w2_ref.shape[1]

    @pl.when(tf_ref[i] == 1)
    def _():
        perm = perm_ref[...]
        half = MXU_DIM // 2
        for c in range(w1_ref.shape[2] // MXU_DIM):
            blk = w1_ref[0, :, c * MXU_DIM:(c + 1) * MXU_DIM].astype(BF16)
            sp = _dot(blk, perm).astype(BF16)
            w1s_ref[:, c * half:(c + 1) * half] = sp[:, :half]
            w1s_ref[:, dff + c * half:dff + (c + 1) * half] = sp[:, half:]
        w2s_ref[...] = w2_ref[0].astype(BF16)

    @pl.when(tu_ref[i] == 1)
    def _():
        x = _unpack_bf16_pairs(x_ref[...])
        hdn = _dot(x, w1s_ref[...])
        glu = jnp.minimum(hdn[:, :dff] + b1g_ref[0], SWIGLU_LIMIT)
        lin = jnp.clip(hdn[:, dff:] + b1l_ref[0], -SWIGLU_LIMIT, SWIGLU_LIMIT)
        act = glu * _sigmoid(SWIGLU_ALPHA * glu) * (lin + 1.0)
        y_ref[...] = _dot(act.astype(BF16), w2s_ref[...]) + b2_ref[0]

    @pl.when(tu_ref[i] == 0)
    def _():
        y_ref[...] = jnp.zeros_like(y_ref)


def _deinterleave_perm():
    j = lax.broadcasted_iota(jnp.int32, (MXU_DIM, MXU_DIM), 0)
    c = lax.broadcasted_iota(jnp.int32, (MXU_DIM, MXU_DIM), 1)
    half = MXU_DIM // 2
    src = jnp.where(c < half, 2 * c, 2 * (c - half) + 1)
    return (j == src).astype(BF16)


def _experts(xs, tile_expert, tile_first, tile_used, w1, b1g, b1l, w2, b2):
    n_rows, d_packed = xs.shape
    n_tiles = n_rows // TM_E
    ne, d, two_f = w1.shape
    dff = two_f // 2
    grid_spec = pltpu.PrefetchScalarGridSpec(
        num_scalar_prefetch=3,
        grid=(n_tiles,),
        in_specs=[pl.BlockSpec((TM_E, d_packed), lambda i, te, tf, tu: (i, 0)),
                  pl.BlockSpec((1, d, two_f), lambda i, te, tf, tu: (te[i], 0, 0)),
                  pl.BlockSpec((1, 1, dff), lambda i, te, tf, tu: (te[i], 0, 0)),
                  pl.BlockSpec((1, 1, dff), lambda i, te, tf, tu: (te[i], 0, 0)),
                  pl.BlockSpec((1, dff, d), lambda i, te, tf, tu: (te[i], 0, 0)),
                  pl.BlockSpec((1, 1, d), lambda i, te, tf, tu: (te[i], 0, 0)),
                  pl.BlockSpec((MXU_DIM, MXU_DIM), lambda i, te, tf, tu: (0, 0))],
        out_specs=pl.BlockSpec((TM_E, d), lambda i, te, tf, tu: (i, 0)),
        scratch_shapes=[pltpu.VMEM((d, two_f), BF16), pltpu.VMEM((dff, d), BF16)],
    )
    return pl.pallas_call(
        _experts_kernel,
        out_shape=jax.ShapeDtypeStruct((n_rows, d), F32),
        grid_spec=grid_spec,
        compiler_params=_params(("arbitrary",)),
        name="experts",
    )(tile_expert, tile_first, tile_used, xs, w1, b1g, b1l, w2, b2, _deinterleave_perm())


def _combine_kernel(dest_ref, x1_ref, rt_ref, g2_ref, y_ref, o_ref, buf_ref, sem):
    tc = x1_ref.shape[0]

    def copies(g):
        for tt in range(ROW_DMA_UNROLL):
            t = g * ROW_DMA_UNROLL + tt
            for k in range(TOP_K):
                yield k, pltpu.make_async_copy(y_ref.at[pl.ds(dest_ref[t * TOP_K + k], 1), :],
                                               buf_ref.at[k, pl.ds(t, 1), :], sem)

    def start(g, _):
        for k, cp in copies(g):
            cp.start(priority=k % 2)
        return 0

    def wait(g, _):
        for _, cp in copies(g):
            cp.wait()
        return 0

    lax.fori_loop(0, tc // ROW_DMA_UNROLL, start, 0)
    lax.fori_loop(0, tc // ROW_DMA_UNROLL, wait, 0)
    rt = rt_ref[...]
    moe = jnp.zeros(o_ref.shape, F32)
    for k in range(TOP_K):
        moe = moe + rt[:, _R_GATE + k:_R_GATE + k + 1] * buf_ref[k]
    o_ref[...] = x1_ref[...] + g2_ref[...] * moe


def _combine(x1, route, dest_flat, g2, seq, y_rows, row_offset):
    r, d = x1.shape
    tc = min(TC, r)
    assert r % tc == 0 and row_offset % tc == 0
    off = row_offset // tc
    g2_o, rpm = _mod_operand(g2, seq, tc)
    g2_spec = _mod_spec(tc, g2_o, rpm)
    return pl.pallas_call(
        _combine_kernel,
        out_shape=jax.ShapeDtypeStruct((r, d), F32),
        grid=(r // tc,),
        in_specs=[pl.BlockSpec((tc * TOP_K,), lambda i: (i + off,), memory_space=pltpu.SMEM),
                  pl.BlockSpec((tc, d), lambda i: (i, 0)),
                  pl.BlockSpec((tc, LANES), lambda i: (i + off, 0)),
                  g2_spec,
                  pl.BlockSpec(memory_space=pl.ANY)],
        out_specs=pl.BlockSpec((tc, d), lambda i: (i, 0)),
        scratch_shapes=[pltpu.VMEM((TOP_K, tc, d), F32), pltpu.SemaphoreType.DMA(())],
        compiler_params=_params(("arbitrary",)),
        name="combine",
    )(dest_flat, x1, route, g2_o, y_rows)


def _gmlp_tables(w_s, b_s, seq_s):
    g = w_s.shape[0]
    tril = jnp.tril(w_s)
    short = jnp.tril(w_s[:, :seq_s, :seq_s])
    rep = CHUNK_B // seq_s
    blockdiag = jnp.zeros_like(w_s)
    for r in range(rep):
        blockdiag = blockdiag.at[:, r * seq_s:(r + 1) * seq_s, r * seq_s:(r + 1) * seq_s].set(short)
    ws_all = jnp.stack([tril, blockdiag]).astype(BF16)
    bs_full = jnp.repeat(b_s.T, HEAD_DIM, axis=1)
    bs_short = jnp.tile(jnp.repeat(b_s[:, :seq_s].T, HEAD_DIM, axis=1), (rep, 1))
    return ws_all, jnp.stack([bs_full, bs_short])


def _layer(xp, xs, cache_k, cache_v, cp, cs, p):
    bp, tp, d = xp.shape
    bs_, ts, _ = xs.shape
    assert tp % CHUNK_B == 0 and CHUNK_B % ts == 0
    rp, rs = bp * tp, bs_ * ts
    n_tok = rp + rs

    mod = _ada(jnp.concatenate([cp, cs], axis=0), p['w_ada'], p['b_ada'])
    mods = [mod[:, j * d:(j + 1) * d] for j in range(6)]
    mp = [m[:bp] for m in mods]
    msm = [m[bp:] for m in mods]

    tile_h = lambda g: jnp.tile(g, N_HEADS).reshape(1, D_HEADS)
    bd = jnp.kron(jnp.eye(MXU_DIM // HEAD_DIM, dtype=F32),
                  jnp.full((HEAD_DIM, HEAD_DIM), 1.0 / HEAD_DIM, F32)).astype(BF16)
    w_in_b = p['w_in'].astype(BF16)
    n1g = p['norm1_g'].reshape(1, d)
    common = (n1g, w_in_b, tile_h(p['q_norm_g']), tile_h(p['k_norm_g']), tile_h(p['v_norm_g']), bd)
    q_p, k_p, kb_p, v_p, vb_p, u_p, vn_p, sga_p, sgb_p = _inproj(
        xp.reshape(rp, d), mp[0], mp[1], tp, *common, want_vn32=False)
    q_s, k_s, kb_s, v_s, vb_s, u_s, vn_s, sga_s, sgb_s, vn32_s = _inproj(
        xs.reshape(rs, d), msm[0], msm[1], ts, *common, want_vn32=True)

    ya_p = _attn_prompt(q_p, kb_p, vb_p, bp, tp)
    ya_s = _attn_sample(q_s, kb_s, vb_s, cache_k, cache_v, bs_, ts)

    ws_all, bs_all = _gmlp_tables(p['w_s'], p['b_s'], ts)
    wr = p['w_router']
    wr_hi = wr.astype(BF16)
    wr_lo = (wr - wr_hi.astype(F32)).astype(BF16)
    post_w = (p['w_pa'].astype(BF16), p['w_pb'].astype(BF16), p['w_o'].astype(BF16), p['norm2_g'].reshape(1, d),
              jnp.concatenate([wr_hi, wr_lo], axis=1), p['b_router'].reshape(1, N_EXPERTS))
    x1_p, h2_p, lg_p = _post(xp.reshape(rp, d), ya_p, u_p, vn_p, sga_p, sgb_p, mp[2], mp[3], mp[4], tp,
                             ws_all, bs_all, 0, *post_w)
    x1_s, h2_s, lg_s = _post(xs.reshape(rs, d), ya_s, u_s, vn_s, sga_s, sgb_s, msm[2], msm[3], msm[4], ts,
                             ws_all, bs_all, 1, *post_w)

    route, counts = _route(lg_p, lg_s)

    counts = counts[0].astype(jnp.int32)
    padded = (counts + TM_E - 1) // TM_E * TM_E
    pend = jnp.cumsum(padded)
    pstart = pend - padded
    idx = route[:, _R_IDX:_R_IDX + TOP_K].astype(jnp.int32)
    rank = route[:, _R_RANK:_R_RANK + TOP_K].astype(jnp.int32)
    dest_flat = (pstart[idx] + rank).reshape(n_tok * TOP_K)
    n_tiles = -(-(n_tok * TOP_K) // TM_E) + N_EXPERTS
    tile_start = jnp.arange(n_tiles, dtype=jnp.int32) * TM_E
    tile_used = (tile_start < pend[-1]).astype(jnp.int32)
    last_expert = jnp.sum((pend <= pend[-1] - 1).astype(jnp.int32))
    tile_expert = jnp.minimum(jnp.sum((pend[None, :] <= tile_start[:, None]).astype(jnp.int32), axis=1), last_expert)
    tile_first = jnp.concatenate([jnp.ones((1,), jnp.int32),
                                  (tile_expert[1:] != tile_expert[:-1]).astype(jnp.int32)])

    tail_tiles = jnp.stack([pend[-1] // TM_E, jnp.int32(n_tiles)]).astype(jnp.int32)
    xs_rows = _dispatch(h2_p, h2_s, dest_flat, pstart + counts, padded - counts, tail_tiles, n_tiles * TM_E)
    b1 = p['b1']
    ne = b1.shape[0]
    y_rows = _experts(xs_rows, tile_expert, tile_first, tile_used, p['w1'],
                      b1[:, 0::2].reshape(ne, 1, -1), b1[:, 1::2].reshape(ne, 1, -1),
                      p['w2'], p['b2'].reshape(ne, 1, d))
    out_p = _combine(x1_p, route, dest_flat, mp[5], tp, y_rows, 0)
    out_s = _combine(x1_s, route, dest_flat, msm[5], ts, y_rows, rp)

    shp = lambda a, b, t: a.reshape(b, t, N_HEADS, HEAD_DIM)
    return (out_p.reshape(bp, tp, d), out_s.reshape(bs_, ts, d),
            shp(k_p, bp, tp), shp(v_p, bp, tp), shp(k_s, bs_, ts), shp(v_s, bs_, ts), shp(vn32_s, bs_, ts))


def kernel(x_prompt, x_sample, cache_k, cache_v, c_prompt, c_sample, norm1_g, norm2_g, w_ada, b_ada, w_in,
           q_norm_g, k_norm_g, v_norm_g, w_s, b_s, w_pa, w_pb, w_o, w_router, b_router, w1, b1, w2, b2):
    depth = w_in.shape[0]
    xp, xs = x_prompt, x_sample
    kp, vp, ks, vs, vbs = [], [], [], [], []
    for l in range(depth):
        p = {'norm1_g': norm1_g[l], 'norm2_g': norm2_g[l], 'w_ada': w_ada[l], 'b_ada': b_ada[l],
             'w_in': w_in[l], 'q_norm_g': q_norm_g[l], 'k_norm_g': k_norm_g[l], 'v_norm_g': v_norm_g[l],
             'w_s': w_s[l], 'b_s': b_s[l], 'w_pa': w_pa[l], 'w_pb': w_pb[l], 'w_o': w_o[l],
             'w_router': w_router[l], 'b_router': b_router[l], 'w1': w1[l], 'b1': b1[l],
             'w2': w2[l], 'b2': b2[l]}
        xp, xs, k_p, v_p, k_s, v_s, vn_s = _layer(xp, xs, cache_k[l], cache_v[l], c_prompt, c_sample, p)
        kp.append(k_p); vp.append(v_p); ks.append(k_s); vs.append(v_s); vbs.append(vn_s)
    return (xp, xs, jnp.stack(kp), jnp.stack(vp), jnp.stack(ks), jnp.stack(vs), jnp.stack(vbs))
```

```python
import functools

import jax
import jax.numpy as jnp
from jax import lax
from jax.experimental import pallas as pl
from jax.experimental.pallas import tpu as pltpu

F32 = jnp.float32
BF16 = jnp.bfloat16

LANES = 128
MXU_DIM = 256
VMEM_LIMIT = 56 * 1024 * 1024

HEAD_DIM = 64
N_HEADS = 8
D_HEADS = N_HEADS * HEAD_DIM
PAIRS = D_HEADS // LANES
CHUNK_B = 128
N_EXPERTS = 32
TOP_K = 4
RMS_EPS = 1e-6
SWIGLU_ALPHA = 1.702
SWIGLU_LIMIT = 7.0

TM_IN = 512
TQ = 256
TK = 256
TM_POST = 256
TT = 256
TM_E = 512
RUN_ROWS = 8
N_SLOTS = TT * TOP_K + N_EXPERTS * RUN_ROWS


def _params(sem, **kw):
    return pltpu.CompilerParams(dimension_semantics=sem, vmem_limit_bytes=VMEM_LIMIT, **kw)


def _dot(a, b):
    return jnp.dot(a, b, preferred_element_type=F32)


def _dot_nt(a, b):
    return lax.dot_general(a, b, (((1,), (1,)), ((), ())), preferred_element_type=F32)


def _gelu(x):
    return jax.nn.gelu(x)


def _sigmoid(x):
    return 1.0 / (1.0 + jnp.exp(-x))


def _ada_kernel(c_ref, w_ref, b_ref, o_ref):
    c = c_ref[...]
    s = c * _sigmoid(c)
    o_ref[...] = jnp.dot(s, w_ref[...], preferred_element_type=F32,
                         precision=lax.Precision.HIGHEST) + b_ref[...]


def _ada(c, w_ada, b_ada):
    nb, d = c.shape
    n = w_ada.shape[1]
    tn = 1024
    return pl.pallas_call(
        _ada_kernel,
        out_shape=jax.ShapeDtypeStruct((nb, n), F32),
        grid=(n // tn,),
        in_specs=[pl.BlockSpec((nb, d), lambda j: (0, 0)),
                  pl.BlockSpec((d, tn), lambda j: (0, j)),
                  pl.BlockSpec((1, tn), lambda j: (0, j))],
        out_specs=pl.BlockSpec((nb, tn), lambda j: (0, j)),
        compiler_params=_params(("parallel",)),
        name="ada",
    )(c, w_ada, b_ada.reshape(1, n))


def _head_mean_sq(p, bd):
    sq = (p * p).astype(BF16)
    return jnp.concatenate([_dot(sq[:, :MXU_DIM], bd), _dot(sq[:, MXU_DIM:], bd)], axis=-1)


def _inproj_kernel(x_ref, sh_ref, sc_ref, g_ref, w_ref, qg_ref, kg_ref, vg_ref, bd_ref,
                   q_ref, k32_ref, kb_ref, v32_ref, vb_ref, u_ref, vnb_ref, sga_ref, sgb_ref, *vn32_ref,
                   kv_transposed):
    kv_out = (lambda a: a.T) if kv_transposed else (lambda a: a)
    x = x_ref[...]
    ms = jnp.mean(x * x, axis=-1, keepdims=True)
    xn = x * lax.rsqrt(ms + RMS_EPS) * g_ref[...]
    h = (xn * (1.0 + sc_ref[...]) + sh_ref[...]).astype(BF16)
    bd = bd_ref[...]
    d = x.shape[1]

    def sec(a, n):
        return _dot(h, w_ref[:, a:a + n])

    o = 0
    pq = sec(o, D_HEADS); o += D_HEADS
    q = pq * lax.rsqrt(_head_mean_sq(pq, bd) + RMS_EPS) * qg_ref[...]
    q_ref[...] = (q * (HEAD_DIM ** -0.5)).astype(BF16)
    pk = sec(o, D_HEADS); o += D_HEADS
    k = pk * lax.rsqrt(_head_mean_sq(pk, bd) + RMS_EPS) * kg_ref[...]
    k32_ref[...] = kv_out(k)
    kb_ref[...] = k.astype(BF16)
    pv = sec(o, D_HEADS); o += D_HEADS
    v32_ref[...] = kv_out(pv)
    vb_ref[...] = pv.astype(BF16)
    pu = sec(o, D_HEADS); o += D_HEADS
    u_ref[...] = _gelu(pu).astype(BF16)
    pvb = _gelu(sec(o, D_HEADS)); o += D_HEADS
    vn = pvb * lax.rsqrt(_head_mean_sq(pvb, bd) + RMS_EPS) * vg_ref[...]
    vnb_ref[...] = vn.astype(BF16)
    if vn32_ref:
        vn32_ref[0][...] = vn
    sga_ref[...] = _sigmoid(sec(o, d)).astype(BF16); o += d
    sgb_ref[...] = _sigmoid(sec(o, d)).astype(BF16)


def _mod_spec(tm, arr, rows_per_mod):
    if rows_per_mod is None:
        return pl.BlockSpec((tm, arr.shape[-1]), lambda i: (i, 0))
    return pl.BlockSpec((None, 1, arr.shape[-1]), lambda i: ((i * tm) // rows_per_mod, 0, 0))


def _mod_operand(m, seq, tm):
    if seq % tm == 0:
        return m[:, None, :], seq
    return jnp.repeat(m, seq, axis=0), None


def _inproj(x, sh, sc, seq, g, w_in_b, qg, kg, vg, bd, want_vn32, kv_transposed):
    r, d = x.shape
    tm = min(TM_IN, r)
    assert r % tm == 0 and (not kv_transposed or seq % tm == 0)
    sh_o, rpm = _mod_operand(sh, seq, tm)
    sc_o, _ = _mod_operand(sc, seq, tm)
    ncol = w_in_b.shape[1]
    const = lambda i: (0, 0)
    row = lambda n: pl.BlockSpec((tm, n), lambda i: (i, 0))
    outs = [jax.ShapeDtypeStruct((r, D_HEADS), BF16),
            jax.ShapeDtypeStruct((r, D_HEADS), F32),
            jax.ShapeDtypeStruct((r, D_HEADS), BF16),
            jax.ShapeDtypeStruct((r, D_HEADS), F32),
            jax.ShapeDtypeStruct((r, D_HEADS), BF16),
            jax.ShapeDtypeStruct((r, D_HEADS), BF16),
            jax.ShapeDtypeStruct((r, D_HEADS), BF16),
            jax.ShapeDtypeStruct((r, d), BF16),
            jax.ShapeDtypeStruct((r, d), BF16)]
    out_specs = [row(D_HEADS)] * 7 + [row(d), row(d)]
    if kv_transposed:
        per_seq = seq // tm
        for j in (1, 3):
            outs[j] = jax.ShapeDtypeStruct((r // seq, D_HEADS, seq), F32)
            out_specs[j] = pl.BlockSpec((None, D_HEADS, tm), lambda i: (i // per_seq, 0, i % per_seq))
    if want_vn32:
        outs.append(jax.ShapeDtypeStruct((r, D_HEADS), F32))
        out_specs.append(row(D_HEADS))
    return pl.pallas_call(
        functools.partial(_inproj_kernel, kv_transposed=kv_transposed),
        out_shape=outs,
        grid=(r // tm,),
        in_specs=[row(d), _mod_spec(tm, sh_o, rpm), _mod_spec(tm, sc_o, rpm),
                  pl.BlockSpec((1, d), const),
                  pl.BlockSpec((d, ncol), const, pipeline_mode=pl.Buffered(1)),
                  pl.BlockSpec((1, D_HEADS), const), pl.BlockSpec((1, D_HEADS), const),
                  pl.BlockSpec((1, D_HEADS), const), pl.BlockSpec((MXU_DIM, MXU_DIM), const)],
        out_specs=out_specs,
        compiler_params=_params(("parallel",)),
        name="inproj",
    )(x, sh_o, sc_o, g, w_in_b, qg, kg, vg, bd)


LOG_WEIGHT_CUTOFF = -104.0


def _sb_weights(z, u, c, mask):
    lom = -(jnp.maximum(z, 0.0) + jnp.log(1.0 + jnp.exp(-jnp.abs(z))))
    if mask is not None:
        lom = jnp.where(mask, lom, 0.0)
    hi = lom.astype(BF16)
    lo = (lom - hi.astype(F32)).astype(BF16)
    later = _dot(hi, u) + _dot(lo, u)
    w = jnp.exp(z + lom + later + c)
    if mask is not None:
        w = jnp.where(mask, w, 0.0)
    return w, c + jnp.sum(lom, axis=-1, keepdims=True)


def _pair_select(shape):
    return lax.broadcasted_iota(jnp.int32, shape, 1) < HEAD_DIM


def _causal_mask(m, tq, tk):
    row = lax.broadcasted_iota(jnp.int32, (m, tk), 0) & (tq - 1)
    return lax.broadcasted_iota(jnp.int32, (m, tk), 1) < row


def _attn_prompt_kernel(q_ref, k_ref, v_ref, u_ref, o_ref, acc_ref, c_ref):
    i = pl.program_id(2)
    tq = q_ref.shape[0]
    q = q_ref[...]
    first = _pair_select((tq, LANES))
    zero = jnp.zeros_like(q)
    q2 = jnp.concatenate([jnp.where(first, q, zero), jnp.where(first, zero, q)], axis=0)
    u = u_ref[...]

    def block(j, c, mask):
        start = pl.multiple_of(j * TK, TK)
        ks = k_ref[pl.ds(start, TK), :]
        vs = v_ref[pl.ds(start, TK), :]
        w, c = _sb_weights(_dot_nt(q2, ks), u, c, mask)
        wb = w.astype(BF16)
        return jnp.where(first, _dot(wb[:tq], vs), _dot(wb[tq:], vs)), c

    pv, c = block(i, jnp.zeros((2 * tq, 1), F32), _causal_mask(2 * tq, tq, TK))
    acc_ref[...] = pv
    c_ref[...] = c

    def cond(state):
        j, cmax = state
        return jnp.logical_and(j >= 0, cmax > LOG_WEIGHT_CUTOFF)

    def body(state):
        j, _ = state
        pv, c = block(j, c_ref[...], None)
        acc_ref[...] += pv
        c_ref[...] = c
        return j - 1, jnp.max(c)

    lax.while_loop(cond, body, (i - 1, jnp.max(c)))
    o_ref[...] = acc_ref[...].astype(o_ref.dtype)


def _upper_ones(n):
    return (lax.broadcasted_iota(jnp.int32, (n, n), 0) > lax.broadcasted_iota(jnp.int32, (n, n), 1)).astype(BF16)


def _attn_prompt(q, kb, vb, batch, seq):
    assert TQ == TK and seq % TQ == 0
    nq = seq // TQ
    return pl.pallas_call(
        _attn_prompt_kernel,
        out_shape=jax.ShapeDtypeStruct(q.shape, BF16),
        grid=(batch, PAIRS, nq),
        in_specs=[pl.BlockSpec((TQ, LANES), lambda b, p, i: (b * nq + i, p)),
                  pl.BlockSpec((seq, LANES), lambda b, p, i: (b, p)),
                  pl.BlockSpec((seq, LANES), lambda b, p, i: (b, p)),
                  pl.BlockSpec((TK, TK), lambda b, p, i: (0, 0))],
        out_specs=pl.BlockSpec((TQ, LANES), lambda b, p, i: (b * nq + i, p)),
        scratch_shapes=[pltpu.VMEM((TQ, LANES), F32), pltpu.VMEM((2 * TQ, 1), F32)],
        compiler_params=_params(("parallel", "parallel", "arbitrary")),
        name="attn_prompt",
    )(q, kb, vb, _upper_ones(TK))


def _attn_sample_kernel(q_ref, kn_ref, vn_ref, kc_ref, vc_ref, un_ref, u_ref, o_ref,
                        kbuf_ref, vbuf_ref, sem, acc_ref, c_ref):
    b = pl.program_id(0)
    tq = q_ref.shape[0]
    nblk = kc_ref.shape[3] // TK
    head = lambda a, h: a[:, h * HEAD_DIM:(h + 1) * HEAD_DIM]
    q = q_ref[...]
    qh = [head(q, h) for h in range(N_HEADS)]

    def copies(j, slot):
        rows = pl.ds(pl.multiple_of(j * TK, TK), TK)
        return (pltpu.make_async_copy(kc_ref.at[b, :, :, rows], kbuf_ref.at[slot], sem.at[0, slot]),
                pltpu.make_async_copy(vc_ref.at[b, :, :, rows], vbuf_ref.at[slot], sem.at[1, slot]))

    def start(j, slot):
        for cp in copies(j, slot):
            cp.start()

    def wait(j, slot):
        for cp in copies(j, slot):
            cp.wait()

    def sweep(score, apply, u, c, mask):
        z = jnp.concatenate([score(qh[h], h) for h in range(N_HEADS)], axis=0)
        w, c = _sb_weights(z, u, c, mask)
        wb = w.astype(BF16)
        pv = jnp.concatenate([apply(wb[h * tq:(h + 1) * tq], h) for h in range(N_HEADS)], axis=-1)
        return pv, c

    start(nblk - 1, 0)
    kn = kn_ref[...]
    vn = vn_ref[...]
    pv, c = sweep(lambda qq, h: _dot_nt(qq, head(kn, h)), lambda ww, h: _dot(ww, head(vn, h)), un_ref[...],
                  jnp.zeros((N_HEADS * tq, 1), F32), _causal_mask(N_HEADS * tq, tq, tq))
    acc_ref[...] = pv
    c_ref[...] = c
    u = u_ref[...]

    def cond(state):
        j, _, cmax = state
        return jnp.logical_and(j >= 0, cmax > LOG_WEIGHT_CUTOFF)

    def body(state):
        j, slot, _ = state
        wait(j, slot)

        @pl.when(j > 0)
        def _():
            start(j - 1, 1 - slot)

        pv, c = sweep(lambda qq, h: _dot(qq, kbuf_ref[slot, h].astype(BF16)),
                      lambda ww, h: _dot_nt(ww, vbuf_ref[slot, h].astype(BF16)), u, c_ref[...], None)
        acc_ref[...] += pv
        c_ref[...] = c
        return j - 1, 1 - slot, jnp.max(c)

    j, slot, _ = lax.while_loop(cond, body, (nblk - 1, 0, jnp.max(c)))

    @pl.when(j >= 0)
    def _():
        wait(j, slot)

    o_ref[...] = acc_ref[...].astype(o_ref.dtype)


def _attn_sample(q, kb, vb, cache_k, cache_v, batch, seq):
    past = cache_k.shape[1]
    assert past % TK == 0 and past >= TK
    cache_k = jnp.transpose(cache_k, (0, 2, 3, 1))
    cache_v = jnp.transpose(cache_v, (0, 2, 3, 1))
    row = pl.BlockSpec((seq, D_HEADS), lambda b: (b, 0))
    return pl.pallas_call(
        _attn_sample_kernel,
        out_shape=jax.ShapeDtypeStruct(q.shape, BF16),
        grid=(batch,),
        in_specs=[row, row, row,
                  pl.BlockSpec(memory_space=pl.ANY), pl.BlockSpec(memory_space=pl.ANY),
                  pl.BlockSpec((seq, seq), lambda b: (0, 0)),
                  pl.BlockSpec((TK, TK), lambda b: (0, 0))],
        out_specs=row,
        scratch_shapes=[pltpu.VMEM((2, N_HEADS, HEAD_DIM, TK), F32), pltpu.VMEM((2, N_HEADS, HEAD_DIM, TK), F32),
                        pltpu.SemaphoreType.DMA((2, 2)),
                        pltpu.VMEM((seq, D_HEADS), F32), pltpu.VMEM((N_HEADS * seq, 1), F32)],
        compiler_params=_params(("arbitrary",)),
        name="attn_sample",
    )(q, kb, vb, cache_k, cache_v, _upper_ones(seq), _upper_ones(TK))


def _pack_bf16_pairs(x):
    half = x.shape[1] // 2
    a = pltpu.bitcast(x[:, :half], jnp.uint32)
    b = pltpu.bitcast(x[:, half:], jnp.uint32)
    return a | (b >> 16)


def _unpack_bf16_pairs(p):
    a = pltpu.bitcast(p & jnp.uint32(0xFFFF0000), F32).astype(BF16)
    b = pltpu.bitcast(p << 16, F32).astype(BF16)
    return jnp.concatenate([a, b], axis=1)


def _pack_hi_lo(y):
    hi = y.astype(BF16).astype(F32)
    lo = (y - hi).astype(BF16).astype(F32)
    return pltpu.bitcast(hi, jnp.uint32) | (pltpu.bitcast(lo, jnp.uint32) >> 16)


def _unpack_hi_lo(p):
    hi = pltpu.bitcast(p & jnp.uint32(0xFFFF0000), F32).astype(BF16)
    lo = pltpu.bitcast(p << 16, F32).astype(BF16)
    return hi, lo


def _post_kernel(x_ref, ya_ref, u_ref, vn_ref, sga_ref, sgb_ref, g1_ref, sh2_ref, sc2_ref,
                 ws_ref, bs_ref, wpa_ref, wpb_ref, wo_ref, n2g_ref, wr_ref, br_ref,
                 x1_ref, h2_ref, lg_ref):
    tm = x_ref.shape[0]
    first = _pair_select((CHUNK_B, LANES))
    chunks = []
    for c in range(tm // CHUNK_B):
        rows = slice(c * CHUNK_B, (c + 1) * CHUNK_B)
        parts = []
        for p in range(PAIRS):
            vnp = vn_ref[rows, p * LANES:(p + 1) * LANES]
            parts.append(jnp.where(first, _dot(ws_ref[0, 2 * p], vnp), _dot(ws_ref[0, 2 * p + 1], vnp)))
        s = jnp.concatenate(parts, axis=-1) + bs_ref[0]
        chunks.append((u_ref[rows, :].astype(F32) * s).astype(BF16))
    yb = jnp.concatenate(chunks, axis=0)
    merged = (sga_ref[...].astype(F32) * _dot(ya_ref[...], wpa_ref[...])
              + sgb_ref[...].astype(F32) * _dot(yb, wpb_ref[...]))
    x1 = x_ref[...] + g1_ref[...] * _dot(merged.astype(BF16), wo_ref[...])
    x1_ref[...] = x1
    ms = jnp.mean(x1 * x1, axis=-1, keepdims=True)
    h2 = (x1 * lax.rsqrt(ms + RMS_EPS) * n2g_ref[...]) * (1.0 + sc2_ref[...]) + sh2_ref[...]
    hi = h2.astype(BF16)
    hf = hi.astype(F32)
    lo = (h2 - hf).astype(BF16)
    h2_ref[...] = _pack_bf16_pairs(hf)
    l2 = _dot(hi, wr_ref[...]) + _dot(lo, wr_ref[...])
    lg_ref[...] = l2[:, :N_EXPERTS] + l2[:, N_EXPERTS:] + br_ref[...]


def _post(x, ya, u, vn, sga, sgb, g1, sh2, sc2, seq, ws_all, bs_all, ws_idx, wpa, wpb, wo, n2g, wr, br):
    r, d = x.shape
    tm = min(TM_POST, r)
    assert r % tm == 0 and tm % CHUNK_B == 0
    g1_o, rpm = _mod_operand(g1, seq, tm)
    sh2_o, _ = _mod_operand(sh2, seq, tm)
    sc2_o, _ = _mod_operand(sc2, seq, tm)
    const = lambda i: (0, 0)
    row = lambda n: pl.BlockSpec((tm, n), lambda i: (i, 0))
    return pl.pallas_call(
        _post_kernel,
        out_shape=[jax.ShapeDtypeStruct((r, d), F32), jax.ShapeDtypeStruct((r, d // 2), jnp.uint32),
                   jax.ShapeDtypeStruct((r, N_EXPERTS), F32)],
        grid=(r // tm,),
        in_specs=[row(d), row(D_HEADS), row(D_HEADS), row(D_HEADS), row(d), row(d),
                  _mod_spec(tm, g1_o, rpm), _mod_spec(tm, sh2_o, rpm), _mod_spec(tm, sc2_o, rpm),
                  pl.BlockSpec((1, N_HEADS, CHUNK_B, CHUNK_B), lambda i: (ws_idx, 0, 0, 0)),
                  pl.BlockSpec((1, CHUNK_B, D_HEADS), lambda i: (ws_idx, 0, 0)),
                  pl.BlockSpec(wpa.shape, const), pl.BlockSpec(wpb.shape, const), pl.BlockSpec(wo.shape, const),
                  pl.BlockSpec((1, d), const), pl.BlockSpec(wr.shape, const), pl.BlockSpec((1, N_EXPERTS), const)],
        out_specs=[row(d), row(d // 2), row(N_EXPERTS)],
        compiler_params=_params(("parallel",)),
        name="post",
    )(x, ya, u, vn, sga, sgb, g1_o, sh2_o, sc2_o, ws_all, bs_all, wpa, wpb, wo, n2g, wr, br)


_R_IDX, _R_GATE, _R_SLOT = 0, TOP_K, 2 * TOP_K
_T_COUNT, _T_SLOT, _T_BASE = 0, 1, 2


def _route_kernel(lp_ref, ls_ref, tri_ref, scan_ref, o_ref, ot_ref, tbl_ref, cnt_ref, carry_ref, *, n_prompt_tiles):
    i = pl.program_id(0)

    @pl.when(i == 0)
    def _():
        carry_ref[...] = jnp.zeros_like(carry_ref)

    l = jnp.where(i < n_prompt_tiles, lp_ref[...], ls_ref[...])
    tt = l.shape[0]
    lane = lax.broadcasted_iota(jnp.int32, l.shape, 1).astype(F32)
    cur = l
    vals, idxs = [], []
    for _ in range(TOP_K):
        m = jnp.max(cur, axis=-1, keepdims=True)
        idx = jnp.min(jnp.where(cur == m, lane, float(N_EXPERTS)), axis=-1, keepdims=True)
        vals.append(m)
        idxs.append(idx)
        cur = jnp.where(lane == idx, -jnp.inf, cur)
    es = [jnp.exp(v - vals[0]) for v in vals]
    den = es[0] + es[1] + es[2] + es[3]
    onehot = jnp.zeros(l.shape, F32)
    for idx in idxs:
        onehot = onehot + (lane == idx).astype(F32)
    within = _dot(tri_ref[...], onehot.astype(BF16))
    count = jnp.sum(onehot, axis=0, keepdims=True)
    runs = jnp.ceil(count * (1.0 / RUN_ROWS))
    first_slot = RUN_ROWS * _dot(jnp.broadcast_to(runs, (8, N_EXPERTS)).astype(BF16), scan_ref[...])[0:1]
    slot_of = within + first_slot
    wide = lax.broadcasted_iota(jnp.int32, (tt, LANES), 1)
    out = jnp.zeros((tt, LANES), F32)
    for k in range(TOP_K):
        slot = jnp.sum(jnp.where(lane == idxs[k], slot_of, 0.0), axis=-1, keepdims=True)
        out = jnp.where(wide == _R_IDX + k, idxs[k], out)
        out = jnp.where(wide == _R_GATE + k, es[k] / den, out)
        out = jnp.where(wide == _R_SLOT + k, slot, out)
    o_ref[...] = out
    ot_ref[...] = out.T
    row = lax.broadcasted_iota(jnp.int32, (8, N_EXPERTS), 0)
    tbl = jnp.where(row == _T_COUNT, count, jnp.where(row == _T_SLOT, first_slot, carry_ref[...]))
    tbl_ref[...] = jnp.concatenate([tbl, jnp.zeros((8, LANES - N_EXPERTS), F32)], axis=1)
    carry_ref[...] += RUN_ROWS * runs
    cnt_ref[...] = carry_ref[...]


def _route(lg_p, lg_s):
    rp, rs = lg_p.shape[0], lg_s.shape[0]
    assert rp % TT == 0 and rs % TT == 0
    npt, nst = rp // TT, rs // TT
    nt = npt + nst
    iota = lambda shape, ax: lax.broadcasted_iota(jnp.int32, shape, ax)
    tri = (iota((TT, TT), 1) < iota((TT, TT), 0)).astype(BF16)
    scan = (iota((N_EXPERTS, N_EXPERTS), 0) < iota((N_EXPERTS, N_EXPERTS), 1)).astype(BF16)
    return pl.pallas_call(
        functools.partial(_route_kernel, n_prompt_tiles=npt),
        out_shape=[jax.ShapeDtypeStruct((rp + rs, LANES), F32), jax.ShapeDtypeStruct((LANES, rp + rs), F32),
                   jax.ShapeDtypeStruct((nt * 8, LANES), F32), jax.ShapeDtypeStruct((1, N_EXPERTS), F32)],
        grid=(nt,),
        in_specs=[pl.BlockSpec((TT, N_EXPERTS), lambda i: (jnp.minimum(i, npt - 1), 0)),
                  pl.BlockSpec((TT, N_EXPERTS), lambda i: (jnp.maximum(i - npt, 0), 0)),
                  pl.BlockSpec((TT, TT), lambda i: (0, 0)),
                  pl.BlockSpec((N_EXPERTS, N_EXPERTS), lambda i: (0, 0))],
        out_specs=[pl.BlockSpec((TT, LANES), lambda i: (i, 0)), pl.BlockSpec((LANES, TT), lambda i: (0, i)),
                   pl.BlockSpec((8, LANES), lambda i: (i, 0)), pl.BlockSpec((1, N_EXPERTS), lambda i: (0, 0))],
        scratch_shapes=[pltpu.VMEM((1, N_EXPERTS), F32)],
        compiler_params=_params(("arbitrary",)),
        name="route",
    )(lg_p, lg_s, tri, scan)


def _for_each_run(tile, run_n_ref, run_slot_ref, run_row_ref, fn):
    def per_expert(e, _):
        j = tile * N_EXPERTS + e
        slot0 = run_slot_ref[j]
        row0 = run_row_ref[j]

        def per_run(c, _):
            fn(pl.multiple_of(slot0 + c * RUN_ROWS, RUN_ROWS), pl.multiple_of(row0 + c * RUN_ROWS, RUN_ROWS))
            return 0

        lax.fori_loop(0, run_n_ref[j], per_run, 0)
        return 0

    lax.fori_loop(0, N_EXPERTS, per_expert, 0)


def _dispatch_kernel(run_n_ref, run_slot_ref, run_row_ref, pad_start_ref, pad_runs_ref, tail_ref,
                     rt_ref, hp_ref, hs_ref, xs_ref, stage_ref, zero_ref, sem, psem, *, n_prompt_tiles):
    i = pl.program_id(0)

    def run_copy(slot, row):
        return pltpu.make_async_copy(stage_ref.at[pl.ds(slot, RUN_ROWS), :], xs_ref.at[pl.ds(row, RUN_ROWS), :], sem)

    @pl.when(i == 0)
    def _():
        zero_ref[...] = jnp.zeros_like(zero_ref)

        def tail(t, _):
            rows = pl.ds(pl.multiple_of(t * TM_E, TM_E), TM_E)
            cp = pltpu.make_async_copy(zero_ref, xs_ref.at[rows, :], psem)
            cp.start()
            cp.wait()
            return 0

        lax.fori_loop(tail_ref[0], tail_ref[1], tail, 0)

        def per_expert(e, _):
            def pad_copy(r):
                rows = pl.ds(pl.multiple_of(pad_start_ref[e] + r * RUN_ROWS, RUN_ROWS), RUN_ROWS)
                return pltpu.make_async_copy(zero_ref.at[pl.ds(0, RUN_ROWS), :], xs_ref.at[rows, :], psem)

            def start(r, _):
                pad_copy(r).start()
                return 0

            def wait(r, _):
                pad_copy(r).wait()
                return 0

            lax.fori_loop(0, pad_runs_ref[e], start, 0)
            lax.fori_loop(0, pad_runs_ref[e], wait, 0)
            return 0

        lax.fori_loop(0, N_EXPERTS, per_expert, 0)

    rt = rt_ref[...]
    x = _unpack_bf16_pairs(jnp.where(i < n_prompt_tiles, hp_ref[...], hs_ref[...]))
    n_slots = stage_ref.shape[0]
    slot_id = lax.broadcasted_iota(jnp.int32, (n_slots, rt.shape[1]), 0).astype(F32)
    sel = jnp.zeros(slot_id.shape, F32)
    gsel = jnp.zeros(slot_id.shape, F32)
    for k in range(TOP_K):
        hit = slot_id == rt[_R_SLOT + k:_R_SLOT + k + 1, :]
        sel = jnp.where(hit, 1.0, sel)
        gsel = jnp.where(hit, rt[_R_GATE + k:_R_GATE + k + 1, :], gsel)
    rows = _dot(sel.astype(BF16), x)
    gate = jnp.sum(gsel, axis=-1, keepdims=True)
    packed = _pack_bf16_pairs(rows)
    stage_ref[:, :packed.shape[1]] = packed
    stage_ref[:, packed.shape[1]:] = pltpu.bitcast(jnp.broadcast_to(gate, (n_slots, LANES)), jnp.uint32)

    tables = (run_n_ref, run_slot_ref, run_row_ref)
    _for_each_run(i, *tables, lambda slot, row: run_copy(slot, row).start())
    _for_each_run(i, *tables, lambda slot, row: run_copy(slot, row).wait())


def _dispatch(h_p, h_s, route_t, run_n, run_slot, run_row, pad_start, pad_runs, tail_tiles, n_rows):
    rp, dp = h_p.shape
    rs = h_s.shape[0]
    assert rp % TT == 0 and rs % TT == 0
    npt, nst = rp // TT, rs // TT
    width = dp + LANES
    grid_spec = pltpu.PrefetchScalarGridSpec(
        num_scalar_prefetch=6,
        grid=(npt + nst,),
        in_specs=[pl.BlockSpec((2 * 8, TT), lambda i, *_: (0, i)),
                  pl.BlockSpec((TT, dp), lambda i, *_: (jnp.minimum(i, npt - 1), 0)),
                  pl.BlockSpec((TT, dp), lambda i, *_: (jnp.maximum(i - npt, 0), 0))],
        out_specs=pl.BlockSpec(memory_space=pl.ANY),
        scratch_shapes=[pltpu.VMEM((N_SLOTS, width), jnp.uint32), pltpu.VMEM((TM_E, width), jnp.uint32),
                        pltpu.SemaphoreType.DMA(()), pltpu.SemaphoreType.DMA(())],
    )
    return pl.pallas_call(
        functools.partial(_dispatch_kernel, n_prompt_tiles=npt),
        out_shape=jax.ShapeDtypeStruct((n_rows, width), jnp.uint32),
        grid_spec=grid_spec,
        compiler_params=_params(("arbitrary",), has_side_effects=True),
        name="dispatch",
    )(run_n, run_slot, run_row, pad_start, pad_runs, tail_tiles, route_t, h_p, h_s)


def _experts_kernel(te_ref, tf_ref, tu_ref, x_ref, w1_ref, b1g_ref, b1l_ref, w2_ref, b2_ref, perm_ref,
                    y_ref, w1s_ref, w2s_ref):
    i = pl.program_id(0)
    dff = w2_ref.shape[1]

    @pl.when(tf_ref[i] == 1)
    def _():
        perm = perm_ref[...]
        half = MXU_DIM // 2
        for c in range(w1_ref.shape[2] // MXU_DIM):
            blk = w1_ref[0, :, c * MXU_DIM:(c + 1) * MXU_DIM].astype(BF16)
            sp = _dot(blk, perm).astype(BF16)
            w1s_ref[:, c * half:(c + 1) * half] = sp[:, :half]
            w1s_ref[:, dff + c * half:dff + (c + 1) * half] = sp[:, half:]
        w2s_ref[...] = w2_ref[0].astype(BF16)

    @pl.when(tu_ref[i] == 1)
    def _():
        d = w2_ref.shape[2]
        x = _unpack_bf16_pairs(x_ref[:, :d // 2])
        gate = pltpu.bitcast(x_ref[:, d // 2:], F32)
        hdn = _dot(x, w1s_ref[...])
        glu = jnp.minimum(hdn[:, :dff] + b1g_ref[0], SWIGLU_LIMIT)
        lin = jnp.clip(hdn[:, dff:] + b1l_ref[0], -SWIGLU_LIMIT, SWIGLU_LIMIT)
        act = glu * _sigmoid(SWIGLU_ALPHA * glu) * (lin + 1.0)
        y = (_dot(act.astype(BF16), w2s_ref[...]) + b2_ref[0]) * jnp.concatenate([gate] * (d // LANES), axis=1)
        y_ref[...] = _pack_hi_lo(y)

    @pl.when(tu_ref[i] == 0)
    def _():
        y_ref[...] = jnp.zeros_like(y_ref)


def _deinterleave_perm():
    j = lax.broadcasted_iota(jnp.int32, (MXU_DIM, MXU_DIM), 0)
    c = lax.broadcasted_iota(jnp.int32, (MXU_DIM, MXU_DIM), 1)
    half = MXU_DIM // 2
    src = jnp.where(c < half, 2 * c, 2 * (c - half) + 1)
    return (j == src).astype(BF16)


def _experts(xs, tile_expert, tile_first, tile_used, w1, b1g, b1l, w2, b2):
    n_rows, width = xs.shape
    n_tiles = n_rows // TM_E
    ne, d, two_f = w1.shape
    dff = two_f // 2
    assert width == d // 2 + LANES
    grid_spec = pltpu.PrefetchScalarGridSpec(
        num_scalar_prefetch=3,
        grid=(n_tiles,),
        in_specs=[pl.BlockSpec((TM_E, width), lambda i, te, tf, tu: (i, 0)),
                  pl.BlockSpec((1, d, two_f), lambda i, te, tf, tu: (te[i], 0, 0)),
                  pl.BlockSpec((1, 1, dff), lambda i, te, tf, tu: (te[i], 0, 0)),
                  pl.BlockSpec((1, 1, dff), lambda i, te, tf, tu: (te[i], 0, 0)),
                  pl.BlockSpec((1, dff, d), lambda i, te, tf, tu: (te[i], 0, 0)),
                  pl.BlockSpec((1, 1, d), lambda i, te, tf, tu: (te[i], 0, 0)),
                  pl.BlockSpec((MXU_DIM, MXU_DIM), lambda i, te, tf, tu: (0, 0))],
        out_specs=pl.BlockSpec((TM_E, d), lambda i, te, tf, tu: (i, 0)),
        scratch_shapes=[pltpu.VMEM((d, two_f), BF16), pltpu.VMEM((dff, d), BF16)],
    )
    return pl.pallas_call(
        _experts_kernel,
        out_shape=jax.ShapeDtypeStruct((n_rows, d), jnp.uint32),
        grid_spec=grid_spec,
        compiler_params=_params(("arbitrary",)),
        name="experts",
    )(tile_expert, tile_first, tile_used, xs, w1, b1g, b1l, w2, b2, _deinterleave_perm())


def _combine_kernel(run_n_ref, run_slot_ref, run_row_ref, x1_ref, rt_ref, g2_ref, y_ref, o_ref, stage_ref, sem,
                    *, tile_offset):
    i = pl.program_id(0)

    @pl.when(i == 0)
    def _():
        stage_ref[...] = jnp.zeros_like(stage_ref)

    def run_copy(slot, row):
        return pltpu.make_async_copy(y_ref.at[pl.ds(row, RUN_ROWS), :], stage_ref.at[pl.ds(slot, RUN_ROWS), :], sem)

    tables = (run_n_ref, run_slot_ref, run_row_ref)
    _for_each_run(i + tile_offset, *tables, lambda slot, row: run_copy(slot, row).start())
    _for_each_run(i + tile_offset, *tables, lambda slot, row: run_copy(slot, row).wait())

    rt = rt_ref[...]
    tt = rt.shape[0]
    slot_id = lax.broadcasted_iota(jnp.int32, (tt, stage_ref.shape[0]), 1).astype(F32)
    sel = jnp.zeros(slot_id.shape, F32)
    for k in range(TOP_K):
        sel = jnp.where(slot_id == rt[:, _R_SLOT + k:_R_SLOT + k + 1], 1.0, sel)
    sel = sel.astype(BF16)
    hi, lo = _unpack_hi_lo(stage_ref[...])
    o_ref[...] = x1_ref[...] + g2_ref[...] * (_dot(sel, hi) + _dot(sel, lo))


def _combine(x1, route, run_n, run_slot, run_row, g2, seq, y_rows, row_offset):
    r, d = x1.shape
    assert r % TT == 0 and row_offset % TT == 0
    off = row_offset // TT
    g2_o, rpm = _mod_operand(g2, seq, TT)
    if rpm is None:
        g2_spec = pl.BlockSpec((TT, d), lambda i, *_: (i, 0))
    else:
        g2_spec = pl.BlockSpec((None, 1, d), lambda i, *_: ((i * TT) // rpm, 0, 0))
    grid_spec = pltpu.PrefetchScalarGridSpec(
        num_scalar_prefetch=3,
        grid=(r // TT,),
        in_specs=[pl.BlockSpec((TT, d), lambda i, *_: (i, 0)),
                  pl.BlockSpec((TT, LANES), lambda i, *_: (i + off, 0)),
                  g2_spec,
                  pl.BlockSpec(memory_space=pl.ANY)],
        out_specs=pl.BlockSpec((TT, d), lambda i, *_: (i, 0)),
        scratch_shapes=[pltpu.VMEM((N_SLOTS, d), jnp.uint32), pltpu.SemaphoreType.DMA(())],
    )
    return pl.pallas_call(
        functools.partial(_combine_kernel, tile_offset=off),
        out_shape=jax.ShapeDtypeStruct((r, d), F32),
        grid_spec=grid_spec,
        compiler_params=_params(("arbitrary",)),
        name="combine",
    )(run_n, run_slot, run_row, x1, route, g2_o, y_rows)


def _gmlp_tables(w_s, b_s, seq_s):
    g = w_s.shape[0]
    tril = jnp.tril(w_s)
    short = jnp.tril(w_s[:, :seq_s, :seq_s])
    rep = CHUNK_B // seq_s
    blockdiag = jnp.zeros_like(w_s)
    for r in range(rep):
        blockdiag = blockdiag.at[:, r * seq_s:(r + 1) * seq_s, r * seq_s:(r + 1) * seq_s].set(short)
    ws_all = jnp.stack([tril, blockdiag]).astype(BF16)
    bs_full = jnp.repeat(b_s.T, HEAD_DIM, axis=1)
    bs_short = jnp.tile(jnp.repeat(b_s[:, :seq_s].T, HEAD_DIM, axis=1), (rep, 1))
    return ws_all, jnp.stack([bs_full, bs_short])


def _layer(xp, xs, cache_k, cache_v, cp, cs, p):
    bp, tp, d = xp.shape
    bs_, ts, _ = xs.shape
    assert tp % CHUNK_B == 0 and CHUNK_B % ts == 0
    rp, rs = bp * tp, bs_ * ts
    n_tok = rp + rs

    mod = _ada(jnp.concatenate([cp, cs], axis=0), p['w_ada'], p['b_ada'])
    mods = [mod[:, j * d:(j + 1) * d] for j in range(6)]
    mp = [m[:bp] for m in mods]
    msm = [m[bp:] for m in mods]

    tile_h = lambda g: jnp.tile(g, N_HEADS).reshape(1, D_HEADS)
    bd = jnp.kron(jnp.eye(MXU_DIM // HEAD_DIM, dtype=F32),
                  jnp.full((HEAD_DIM, HEAD_DIM), 1.0 / HEAD_DIM, F32)).astype(BF16)
    w_in_b = p['w_in'].astype(BF16)
    n1g = p['norm1_g'].reshape(1, d)
    common = (n1g, w_in_b, tile_h(p['q_norm_g']), tile_h(p['k_norm_g']), tile_h(p['v_norm_g']), bd)
    q_p, k_p, kb_p, v_p, vb_p, u_p, vn_p, sga_p, sgb_p = _inproj(
        xp.reshape(rp, d), mp[0], mp[1], tp, *common, want_vn32=False, kv_transposed=True)
    q_s, k_s, kb_s, v_s, vb_s, u_s, vn_s, sga_s, sgb_s, vn32_s = _inproj(
        xs.reshape(rs, d), msm[0], msm[1], ts, *common, want_vn32=True, kv_transposed=False)

    ya_p = _attn_prompt(q_p, kb_p, vb_p, bp, tp)
    ya_s = _attn_sample(q_s, kb_s, vb_s, cache_k, cache_v, bs_, ts)

    ws_all, bs_all = _gmlp_tables(p['w_s'], p['b_s'], ts)
    wr = p['w_router']
    wr_hi = wr.astype(BF16)
    wr_lo = (wr - wr_hi.astype(F32)).astype(BF16)
    post_w = (p['w_pa'].astype(BF16), p['w_pb'].astype(BF16), p['w_o'].astype(BF16), p['norm2_g'].reshape(1, d),
              jnp.concatenate([wr_hi, wr_lo], axis=1), p['b_router'].reshape(1, N_EXPERTS))
    x1_p, h2_p, lg_p = _post(xp.reshape(rp, d), ya_p, u_p, vn_p, sga_p, sgb_p, mp[2], mp[3], mp[4], tp,
                             ws_all, bs_all, 0, *post_w)
    x1_s, h2_s, lg_s = _post(xs.reshape(rs, d), ya_s, u_s, vn_s, sga_s, sgb_s, msm[2], msm[3], msm[4], ts,
                             ws_all, bs_all, 1, *post_w)

    route, route_t, table, counts = _route(lg_p, lg_s)

    counts = counts[0].astype(jnp.int32)
    padded = (counts + TM_E - 1) // TM_E * TM_E
    pend = jnp.cumsum(padded)
    pstart = pend - padded
    table = table.reshape(-1, 8, LANES)[:, :, :N_EXPERTS].astype(jnp.int32)
    run_n = ((table[:, _T_COUNT] + RUN_ROWS - 1) // RUN_ROWS).reshape(-1)
    run_slot = table[:, _T_SLOT].reshape(-1)
    run_row = (table[:, _T_BASE] + pstart[None, :]).reshape(-1)
    n_token_tiles = n_tok // TT
    n_tiles = -(-(n_tok * TOP_K + n_token_tiles * N_EXPERTS * (RUN_ROWS - 1)) // TM_E) + N_EXPERTS
    tile_start = jnp.arange(n_tiles, dtype=jnp.int32) * TM_E
    tile_used = (tile_start < pend[-1]).astype(jnp.int32)
    last_expert = jnp.sum((pend <= pend[-1] - 1).astype(jnp.int32))
    tile_expert = jnp.minimum(jnp.sum((pend[None, :] <= tile_start[:, None]).astype(jnp.int32), axis=1), last_expert)
    tile_first = jnp.concatenate([jnp.ones((1,), jnp.int32),
                                  (tile_expert[1:] != tile_expert[:-1]).astype(jnp.int32)])

    tail_tiles = jnp.stack([pend[-1] // TM_E, jnp.int32(n_tiles)]).astype(jnp.int32)
    pad_runs = (padded - counts) // RUN_ROWS
    xs_rows = _dispatch(h2_p, h2_s, route_t, run_n, run_slot, run_row, pstart + counts, pad_runs, tail_tiles,
                        n_tiles * TM_E)
    b1 = p['b1']
    ne = b1.shape[0]
    y_rows = _experts(xs_rows, tile_expert, tile_first, tile_used, p['w1'],
                      b1[:, 0::2].reshape(ne, 1, -1), b1[:, 1::2].reshape(ne, 1, -1),
                      p['w2'], p['b2'].reshape(ne, 1, d))
    out_p = _combine(x1_p, route, run_n, run_slot, run_row, mp[5], tp, y_rows, 0)
    out_s = _combine(x1_s, route, run_n, run_slot, run_row, msm[5], ts, y_rows, rp)

    shp = lambda a, b, t: a.reshape(b, t, N_HEADS, HEAD_DIM)
    unt = lambda a: jnp.transpose(a.reshape(bp, N_HEADS, HEAD_DIM, tp), (0, 3, 1, 2))
    return (out_p.reshape(bp, tp, d), out_s.reshape(bs_, ts, d),
            unt(k_p), unt(v_p), shp(k_s, bs_, ts), shp(v_s, bs_, ts), shp(vn32_s, bs_, ts))


def kernel(x_prompt, x_sample, cache_k, cache_v, c_prompt, c_sample, norm1_g, norm2_g, w_ada, b_ada, w_in,
           q_norm_g, k_norm_g, v_norm_g, w_s, b_s, w_pa, w_pb, w_o, w_router, b_router, w1, b1, w2, b2):
    depth = w_in.shape[0]
    xp, xs = x_prompt, x_sample
    kp, vp, ks, vs, vbs = [], [], [], [], []
    for l in range(depth):
        p = {'norm1_g': norm1_g[l], 'norm2_g': norm2_g[l], 'w_ada': w_ada[l], 'b_ada': b_ada[l],
             'w_in': w_in[l], 'q_norm_g': q_norm_g[l], 'k_norm_g': k_norm_g[l], 'v_norm_g': v_norm_g[l],
             'w_s': w_s[l], 'b_s': b_s[l], 'w_pa': w_pa[l], 'w_pb': w_pb[l], 'w_o': w_o[l],
             'w_router': w_router[l], 'b_router': b_router[l], 'w1': w1[l], 'b1': b1[l],
             'w2': w2[l], 'b2': b2[l]}
        xp, xs, k_p, v_p, k_s, v_s, vn_s = _layer(xp, xs, cache_k[l], cache_v[l], c_prompt, c_sample, p)
        kp.append(k_p); vp.append(v_p); ks.append(k_s); vs.append(v_s); vbs.append(vn_s)
    return (xp, xs, jnp.stack(kp), jnp.stack(vp), jnp.stack(ks), jnp.stack(vs), jnp.stack(vbs))
```

```python
import functools

import jax
import jax.numpy as jnp
from jax import lax
from jax.experimental import pallas as pl
from jax.experimental.pallas import tpu as pltpu

F32 = jnp.float32
BF16 = jnp.bfloat16

LANES = 128
MXU_DIM = 256
VMEM_LIMIT = 56 * 1024 * 1024

HEAD_DIM = 64
N_HEADS = 8
D_HEADS = N_HEADS * HEAD_DIM
PAIRS = D_HEADS // LANES
CHUNK_B = 128
N_EXPERTS = 32
TOP_K = 4
RMS_EPS = 1e-6
SWIGLU_ALPHA = 1.702
SWIGLU_LIMIT = 7.0

TM_IN = 512
TQ = 256
TK = 256
SUB_Q = 64
LOOK_BACK = TK - SUB_Q
TM_POST = 256
TT = 256
TM_E = 512
RUN_ROWS = 8
N_SLOTS = TT * TOP_K + N_EXPERTS * RUN_ROWS


def _params(sem, **kw):
    return pltpu.CompilerParams(dimension_semantics=sem, vmem_limit_bytes=VMEM_LIMIT, **kw)


def _dot(a, b):
    return jnp.dot(a, b, preferred_element_type=F32)


def _dot_nt(a, b):
    return lax.dot_general(a, b, (((1,), (1,)), ((), ())), preferred_element_type=F32)


def _gelu(x):
    return jax.nn.gelu(x)


def _sigmoid(x):
    return 1.0 / (1.0 + jnp.exp(-x))


def _ada_kernel(c_ref, w_ref, b_ref, o_ref):
    c = c_ref[...]
    s = c * _sigmoid(c)
    o_ref[...] = jnp.dot(s, w_ref[...], preferred_element_type=F32,
                         precision=lax.Precision.HIGHEST) + b_ref[...]


def _ada(c, w_ada, b_ada):
    nb, d = c.shape
    n = w_ada.shape[1]
    tn = 1024
    return pl.pallas_call(
        _ada_kernel,
        out_shape=jax.ShapeDtypeStruct((nb, n), F32),
        grid=(n // tn,),
        in_specs=[pl.BlockSpec((nb, d), lambda j: (0, 0)),
                  pl.BlockSpec((d, tn), lambda j: (0, j)),
                  pl.BlockSpec((1, tn), lambda j: (0, j))],
        out_specs=pl.BlockSpec((nb, tn), lambda j: (0, j)),
        compiler_params=_params(("parallel",)),
        name="ada",
    )(c, w_ada, b_ada.reshape(1, n))


def _head_mean_sq(p, bd):
    sq = (p * p).astype(BF16)
    return jnp.concatenate([_dot(sq[:, :MXU_DIM], bd), _dot(sq[:, MXU_DIM:], bd)], axis=-1)


def _inproj_kernel(x_ref, sh_ref, sc_ref, g_ref, w_ref, qg_ref, kg_ref, vg_ref, bd_ref,
                   q_ref, k32_ref, kb_ref, v32_ref, vb_ref, u_ref, vnb_ref, sga_ref, sgb_ref, *vn32_ref,
                   kv_transposed):
    kv_out = (lambda a: a.T) if kv_transposed else (lambda a: a)
    x = x_ref[...]
    ms = jnp.mean(x * x, axis=-1, keepdims=True)
    xn = x * lax.rsqrt(ms + RMS_EPS) * g_ref[...]
    h = (xn * (1.0 + sc_ref[...]) + sh_ref[...]).astype(BF16)
    bd = bd_ref[...]
    d = x.shape[1]

    def sec(a, n):
        return _dot(h, w_ref[:, a:a + n])

    o = 0
    pq = sec(o, D_HEADS); o += D_HEADS
    q = pq * lax.rsqrt(_head_mean_sq(pq, bd) + RMS_EPS) * qg_ref[...]
    q_ref[...] = (q * (HEAD_DIM ** -0.5)).astype(BF16)
    pk = sec(o, D_HEADS); o += D_HEADS
    k = pk * lax.rsqrt(_head_mean_sq(pk, bd) + RMS_EPS) * kg_ref[...]
    k32_ref[...] = kv_out(k)
    kb_ref[...] = k.astype(BF16)
    pv = sec(o, D_HEADS); o += D_HEADS
    v32_ref[...] = kv_out(pv)
    vb_ref[...] = pv.astype(BF16)
    pu = sec(o, D_HEADS); o += D_HEADS
    u_ref[...] = _gelu(pu).astype(BF16)
    pvb = _gelu(sec(o, D_HEADS)); o += D_HEADS
    vn = pvb * lax.rsqrt(_head_mean_sq(pvb, bd) + RMS_EPS) * vg_ref[...]
    vnb_ref[...] = vn.astype(BF16)
    if vn32_ref:
        vn32_ref[0][...] = vn
    sga_ref[...] = _sigmoid(sec(o, d)).astype(BF16); o += d
    sgb_ref[...] = _sigmoid(sec(o, d)).astype(BF16)


def _mod_spec(tm, arr, rows_per_mod):
    if rows_per_mod is None:
        return pl.BlockSpec((tm, arr.shape[-1]), lambda i: (i, 0))
    return pl.BlockSpec((None, 1, arr.shape[-1]), lambda i: ((i * tm) // rows_per_mod, 0, 0))


def _mod_operand(m, seq, tm):
    if seq % tm == 0:
        return m[:, None, :], seq
    return jnp.repeat(m, seq, axis=0), None


def _inproj(x, sh, sc, seq, g, w_in_b, qg, kg, vg, bd, want_vn32, kv_transposed):
    r, d = x.shape
    tm = min(TM_IN, r)
    assert r % tm == 0 and (not kv_transposed or seq % tm == 0)
    sh_o, rpm = _mod_operand(sh, seq, tm)
    sc_o, _ = _mod_operand(sc, seq, tm)
    ncol = w_in_b.shape[1]
    const = lambda i: (0, 0)
    row = lambda n: pl.BlockSpec((tm, n), lambda i: (i, 0))
    outs = [jax.ShapeDtypeStruct((r, D_HEADS), BF16),
            jax.ShapeDtypeStruct((r, D_HEADS), F32),
            jax.ShapeDtypeStruct((r, D_HEADS), BF16),
            jax.ShapeDtypeStruct((r, D_HEADS), F32),
            jax.ShapeDtypeStruct((r, D_HEADS), BF16),
            jax.ShapeDtypeStruct((r, D_HEADS), BF16),
            jax.ShapeDtypeStruct((r, D_HEADS), BF16),
            jax.ShapeDtypeStruct((r, d), BF16),
            jax.ShapeDtypeStruct((r, d), BF16)]
    out_specs = [row(D_HEADS)] * 7 + [row(d), row(d)]
    if kv_transposed:
        per_seq = seq // tm
        for j in (1, 3):
            outs[j] = jax.ShapeDtypeStruct((r // seq, D_HEADS, seq), F32)
            out_specs[j] = pl.BlockSpec((None, D_HEADS, tm), lambda i: (i // per_seq, 0, i % per_seq))
    if want_vn32:
        outs.append(jax.ShapeDtypeStruct((r, D_HEADS), F32))
        out_specs.append(row(D_HEADS))
    return pl.pallas_call(
        functools.partial(_inproj_kernel, kv_transposed=kv_transposed),
        out_shape=outs,
        grid=(r // tm,),
        in_specs=[row(d), _mod_spec(tm, sh_o, rpm), _mod_spec(tm, sc_o, rpm),
                  pl.BlockSpec((1, d), const),
                  pl.BlockSpec((d, ncol), const, pipeline_mode=pl.Buffered(1)),
                  pl.BlockSpec((1, D_HEADS), const), pl.BlockSpec((1, D_HEADS), const),
                  pl.BlockSpec((1, D_HEADS), const), pl.BlockSpec((MXU_DIM, MXU_DIM), const)],
        out_specs=out_specs,
        compiler_params=_params(("parallel",)),
        name="inproj",
    )(x, sh_o, sc_o, g, w_in_b, qg, kg, vg, bd)


LOG_WEIGHT_CUTOFF = -104.0


def _sb_weights(z, u, c, mask):
    lom = -(jnp.maximum(z, 0.0) + jnp.log(1.0 + jnp.exp(-jnp.abs(z))))
    if mask is not None:
        lom = jnp.where(mask, lom, 0.0)
    hi = lom.astype(BF16)
    lo = (lom - hi.astype(F32)).astype(BF16)
    later = _dot(hi, u) + _dot(lo, u)
    w = jnp.exp(z + lom + later + c)
    if mask is not None:
        w = jnp.where(mask, w, 0.0)
    return w, c + jnp.sum(lom, axis=-1, keepdims=True)


def _pair_select(shape):
    return lax.broadcasted_iota(jnp.int32, shape, 1) < HEAD_DIM


def _causal_mask(m, tq, tk):
    row = lax.broadcasted_iota(jnp.int32, (m, tk), 0) & (tq - 1)
    return lax.broadcasted_iota(jnp.int32, (m, tk), 1) < row


def _stack_heads(q):
    first = _pair_select(q.shape)
    zero = jnp.zeros_like(q)
    return jnp.concatenate([jnp.where(first, q, zero), jnp.where(first, zero, q)], axis=0)


def _attn_prompt_kernel(q_ref, k_ref, v_ref, u_ref, o_ref, acc_ref, c_ref):
    i = pl.program_id(2)
    tq = q_ref.shape[0]
    nsub = tq // SUB_Q
    sub_rows = 2 * SUB_Q
    q = q_ref[...]
    u = u_ref[...]
    first = _pair_select((SUB_Q, LANES))

    def pv_sub(wb, s, vs):
        w0 = wb[s * sub_rows:s * sub_rows + SUB_Q]
        w1 = wb[s * sub_rows + SUB_Q:(s + 1) * sub_rows]
        return jnp.where(first, _dot(w0, vs), _dot(w1, vs))

    @pl.when(i == 0)
    def _():
        w, _ = _sb_weights(_dot_nt(_stack_heads(q), k_ref[0:TK, :]), u, jnp.zeros((2 * tq, 1), F32),
                           _causal_mask(2 * tq, tq, TK))
        wb = w.astype(BF16)
        vs = v_ref[0:TK, :]
        o_ref[...] = jnp.where(_pair_select((tq, LANES)), _dot(wb[:tq], vs), _dot(wb[tq:], vs)).astype(o_ref.dtype)

    @pl.when(i > 0)
    def _():
        qs = [_stack_heads(q[s * SUB_Q:(s + 1) * SUB_Q]) for s in range(nsub)]
        starts = [pl.multiple_of(i * tq + s * SUB_Q - LOOK_BACK, SUB_Q) for s in range(nsub)]
        z = jnp.concatenate([_dot_nt(qs[s], k_ref[pl.ds(starts[s], TK), :]) for s in range(nsub)], axis=0)
        shape = (nsub * sub_rows, TK)
        row = lax.broadcasted_iota(jnp.int32, shape, 0)
        col = lax.broadcasted_iota(jnp.int32, shape, 1)
        w, c = _sb_weights(z, u, jnp.zeros((shape[0], 1), F32), col < (row & (SUB_Q - 1)) + LOOK_BACK)
        wb = w.astype(BF16)
        acc_ref[...] = jnp.concatenate([pv_sub(wb, s, v_ref[pl.ds(starts[s], TK), :]) for s in range(nsub)], axis=0)
        c_ref[...] = c
        q_all = jnp.concatenate(qs, axis=0)
        window_start = i * tq + (row // sub_rows) * SUB_Q - LOOK_BACK

        def cond(state):
            j, cmax = state
            return jnp.logical_and(j >= 0, cmax > LOG_WEIGHT_CUTOFF)

        def body(state):
            j, _ = state
            start = pl.multiple_of(j * TK, TK)
            vs = v_ref[pl.ds(start, TK), :]
            w, c = _sb_weights(_dot_nt(q_all, k_ref[pl.ds(start, TK), :]), u, c_ref[...],
                               j * TK + col < window_start)
            wb = w.astype(BF16)
            acc_ref[...] += jnp.concatenate([pv_sub(wb, s, vs) for s in range(nsub)], axis=0)
            c_ref[...] = c
            return j - 1, jnp.max(c)

        lax.while_loop(cond, body, (i - 1, jnp.max(c)))
        o_ref[...] = acc_ref[...].astype(o_ref.dtype)


def _upper_ones(n):
    return (lax.broadcasted_iota(jnp.int32, (n, n), 0) > lax.broadcasted_iota(jnp.int32, (n, n), 1)).astype(BF16)


def _attn_prompt(q, kb, vb, batch, seq):
    assert TQ == TK and seq % TQ == 0
    nq = seq // TQ
    return pl.pallas_call(
        _attn_prompt_kernel,
        out_shape=jax.ShapeDtypeStruct(q.shape, BF16),
        grid=(batch, PAIRS, nq),
        in_specs=[pl.BlockSpec((TQ, LANES), lambda b, p, i: (b * nq + i, p)),
                  pl.BlockSpec((seq, LANES), lambda b, p, i: (b, p)),
                  pl.BlockSpec((seq, LANES), lambda b, p, i: (b, p)),
                  pl.BlockSpec((TK, TK), lambda b, p, i: (0, 0))],
        out_specs=pl.BlockSpec((TQ, LANES), lambda b, p, i: (b * nq + i, p)),
        scratch_shapes=[pltpu.VMEM((TQ, LANES), F32), pltpu.VMEM((2 * TQ, 1), F32)],
        compiler_params=_params(("parallel", "parallel", "arbitrary")),
        name="attn_prompt",
    )(q, kb, vb, _upper_ones(TK))


def _attn_sample_kernel(q_ref, kn_ref, vn_ref, kc_ref, vc_ref, un_ref, u_ref, o_ref,
                        kbuf_ref, vbuf_ref, sem, acc_ref, c_ref):
    b = pl.program_id(0)
    tq = q_ref.shape[0]
    nblk = kc_ref.shape[3] // TK
    head = lambda a, h: a[:, h * HEAD_DIM:(h + 1) * HEAD_DIM]
    q = q_ref[...]
    qh = [head(q, h) for h in range(N_HEADS)]

    def copies(j, slot):
        rows = pl.ds(pl.multiple_of(j * TK, TK), TK)
        return (pltpu.make_async_copy(kc_ref.at[b, :, :, rows], kbuf_ref.at[slot], sem.at[0, slot]),
                pltpu.make_async_copy(vc_ref.at[b, :, :, rows], vbuf_ref.at[slot], sem.at[1, slot]))

    def start(j, slot):
        for cp in copies(j, slot):
            cp.start()

    def wait(j, slot):
        for cp in copies(j, slot):
            cp.wait()

    def sweep(score, apply, u, c, mask):
        z = jnp.concatenate([score(qh[h], h) for h in range(N_HEADS)], axis=0)
        w, c = _sb_weights(z, u, c, mask)
        wb = w.astype(BF16)
        pv = jnp.concatenate([apply(wb[h * tq:(h + 1) * tq], h) for h in range(N_HEADS)], axis=-1)
        return pv, c

    start(nblk - 1, 0)
    kn = kn_ref[...]
    vn = vn_ref[...]
    pv, c = sweep(lambda qq, h: _dot_nt(qq, head(kn, h)), lambda ww, h: _dot(ww, head(vn, h)), un_ref[...],
                  jnp.zeros((N_HEADS * tq, 1), F32), _causal_mask(N_HEADS * tq, tq, tq))
    acc_ref[...] = pv
    c_ref[...] = c
    u = u_ref[...]

    def cond(state):
        j, _, cmax = state
        return jnp.logical_and(j >= 0, cmax > LOG_WEIGHT_CUTOFF)

    def body(state):
        j, slot, _ = state
        wait(j, slot)

        @pl.when(j > 0)
        def _():
            start(j - 1, 1 - slot)

        pv, c = sweep(lambda qq, h: _dot(qq, kbuf_ref[slot, h].astype(BF16)),
                      lambda ww, h: _dot_nt(ww, vbuf_ref[slot, h].astype(BF16)), u, c_ref[...], None)
        acc_ref[...] += pv
        c_ref[...] = c
        return j - 1, 1 - slot, jnp.max(c)

    j, slot, _ = lax.while_loop(cond, body, (nblk - 1, 0, jnp.max(c)))

    @pl.when(j >= 0)
    def _():
        wait(j, slot)

    o_ref[...] = acc_ref[...].astype(o_ref.dtype)


def _attn_sample(q, kb, vb, cache_k, cache_v, batch, seq):
    past = cache_k.shape[1]
    assert past % TK == 0 and past >= TK
    cache_k = jnp.transpose(cache_k, (0, 2, 3, 1))
    cache_v = jnp.transpose(cache_v, (0, 2, 3, 1))
    row = pl.BlockSpec((seq, D_HEADS), lambda b: (b, 0))
    return pl.pallas_call(
        _attn_sample_kernel,
        out_shape=jax.ShapeDtypeStruct(q.shape, BF16),
        grid=(batch,),
        in_specs=[row, row, row,
                  pl.BlockSpec(memory_space=pl.ANY), pl.BlockSpec(memory_space=pl.ANY),
                  pl.BlockSpec((seq, seq), lambda b: (0, 0)),
                  pl.BlockSpec((TK, TK), lambda b: (0, 0))],
        out_specs=row,
        scratch_shapes=[pltpu.VMEM((2, N_HEADS, HEAD_DIM, TK), F32), pltpu.VMEM((2, N_HEADS, HEAD_DIM, TK), F32),
                        pltpu.SemaphoreType.DMA((2, 2)),
                        pltpu.VMEM((seq, D_HEADS), F32), pltpu.VMEM((N_HEADS * seq, 1), F32)],
        compiler_params=_params(("arbitrary",)),
        name="attn_sample",
    )(q, kb, vb, cache_k, cache_v, _upper_ones(seq), _upper_ones(TK))


def _pack_bf16_pairs(x):
    half = x.shape[1] // 2
    a = pltpu.bitcast(x[:, :half], jnp.uint32)
    b = pltpu.bitcast(x[:, half:], jnp.uint32)
    return a | (b >> 16)


def _unpack_bf16_pairs(p):
    a = pltpu.bitcast(p & jnp.uint32(0xFFFF0000), F32).astype(BF16)
    b = pltpu.bitcast(p << 16, F32).astype(BF16)
    return jnp.concatenate([a, b], axis=1)


def _pack_hi_lo(y):
    hi = y.astype(BF16).astype(F32)
    lo = (y - hi).astype(BF16).astype(F32)
    return pltpu.bitcast(hi, jnp.uint32) | (pltpu.bitcast(lo, jnp.uint32) >> 16)


def _unpack_hi_lo(p):
    hi = pltpu.bitcast(p & jnp.uint32(0xFFFF0000), F32).astype(BF16)
    lo = pltpu.bitcast(p << 16, F32).astype(BF16)
    return hi, lo


def _post_kernel(x_ref, ya_ref, u_ref, vn_ref, sga_ref, sgb_ref, g1_ref, sh2_ref, sc2_ref,
                 ws_ref, bs_ref, wpa_ref, wpb_ref, wo_ref, n2g_ref, wr_ref, br_ref,
                 x1_ref, h2_ref, lg_ref):
    tm = x_ref.shape[0]
    first = _pair_select((CHUNK_B, LANES))
    chunks = []
    for c in range(tm // CHUNK_B):
        rows = slice(c * CHUNK_B, (c + 1) * CHUNK_B)
        parts = []
        for p in range(PAIRS):
            vnp = vn_ref[rows, p * LANES:(p + 1) * LANES]
            parts.append(jnp.where(first, _dot(ws_ref[0, 2 * p], vnp), _dot(ws_ref[0, 2 * p + 1], vnp)))
        s = jnp.concatenate(parts, axis=-1) + bs_ref[0]
        chunks.append((u_ref[rows, :].astype(F32) * s).astype(BF16))
    yb = jnp.concatenate(chunks, axis=0)
    merged = (sga_ref[...].astype(F32) * _dot(ya_ref[...], wpa_ref[...])
              + sgb_ref[...].astype(F32) * _dot(yb, wpb_ref[...]))
    x1 = x_ref[...] + g1_ref[...] * _dot(merged.astype(BF16), wo_ref[...])
    x1_ref[...] = x1
    ms = jnp.mean(x1 * x1, axis=-1, keepdims=True)
    h2 = (x1 * lax.rsqrt(ms + RMS_EPS) * n2g_ref[...]) * (1.0 + sc2_ref[...]) + sh2_ref[...]
    hi = h2.astype(BF16)
    hf = hi.astype(F32)
    lo = (h2 - hf).astype(BF16)
    h2_ref[...] = _pack_bf16_pairs(hf)
    l2 = _dot(hi, wr_ref[...]) + _dot(lo, wr_ref[...])
    lg_ref[...] = l2[:, :N_EXPERTS] + l2[:, N_EXPERTS:] + br_ref[...]


def _post(x, ya, u, vn, sga, sgb, g1, sh2, sc2, seq, ws_all, bs_all, ws_idx, wpa, wpb, wo, n2g, wr, br):
    r, d = x.shape
    tm = min(TM_POST, r)
    assert r % tm == 0 and tm % CHUNK_B == 0
    g1_o, rpm = _mod_operand(g1, seq, tm)
    sh2_o, _ = _mod_operand(sh2, seq, tm)
    sc2_o, _ = _mod_operand(sc2, seq, tm)
    const = lambda i: (0, 0)
    row = lambda n: pl.BlockSpec((tm, n), lambda i: (i, 0))
    return pl.pallas_call(
        _post_kernel,
        out_shape=[jax.ShapeDtypeStruct((r, d), F32), jax.ShapeDtypeStruct((r, d // 2), jnp.uint32),
                   jax.ShapeDtypeStruct((r, N_EXPERTS), F32)],
        grid=(r // tm,),
        in_specs=[row(d), row(D_HEADS), row(D_HEADS), row(D_HEADS), row(d), row(d),
                  _mod_spec(tm, g1_o, rpm), _mod_spec(tm, sh2_o, rpm), _mod_spec(tm, sc2_o, rpm),
                  pl.BlockSpec((1, N_HEADS, CHUNK_B, CHUNK_B), lambda i: (ws_idx, 0, 0, 0)),
                  pl.BlockSpec((1, CHUNK_B, D_HEADS), lambda i: (ws_idx, 0, 0)),
                  pl.BlockSpec(wpa.shape, const), pl.BlockSpec(wpb.shape, const), pl.BlockSpec(wo.shape, const),
                  pl.BlockSpec((1, d), const), pl.BlockSpec(wr.shape, const), pl.BlockSpec((1, N_EXPERTS), const)],
        out_specs=[row(d), row(d // 2), row(N_EXPERTS)],
        compiler_params=_params(("parallel",)),
        name="post",
    )(x, ya, u, vn, sga, sgb, g1_o, sh2_o, sc2_o, ws_all, bs_all, wpa, wpb, wo, n2g, wr, br)


_R_IDX, _R_GATE, _R_SLOT = 0, TOP_K, 2 * TOP_K
_T_COUNT, _T_SLOT, _T_BASE = 0, 1, 2


def _route_kernel(lp_ref, ls_ref, tri_ref, scan_ref, o_ref, ot_ref, tbl_ref, cnt_ref, carry_ref, *, n_prompt_tiles):
    i = pl.program_id(0)

    @pl.when(i == 0)
    def _():
        carry_ref[...] = jnp.zeros_like(carry_ref)

    l = jnp.where(i < n_prompt_tiles, lp_ref[...], ls_ref[...])
    tt = l.shape[0]
    lane = lax.broadcasted_iota(jnp.int32, l.shape, 1).astype(F32)
    cur = l
    vals, idxs = [], []
    for _ in range(TOP_K):
        m = jnp.max(cur, axis=-1, keepdims=True)
        idx = jnp.min(jnp.where(cur == m, lane, float(N_EXPERTS)), axis=-1, keepdims=True)
        vals.append(m)
        idxs.append(idx)
        cur = jnp.where(lane == idx, -jnp.inf, cur)
    es = [jnp.exp(v - vals[0]) for v in vals]
    den = es[0] + es[1] + es[2] + es[3]
    onehot = jnp.zeros(l.shape, F32)
    for idx in idxs:
        onehot = onehot + (lane == idx).astype(F32)
    within = _dot(tri_ref[...], onehot.astype(BF16))
    count = jnp.sum(onehot, axis=0, keepdims=True)
    runs = jnp.ceil(count * (1.0 / RUN_ROWS))
    first_slot = RUN_ROWS * _dot(jnp.broadcast_to(runs, (8, N_EXPERTS)).astype(BF16), scan_ref[...])[0:1]
    slot_of = within + first_slot
    wide = lax.broadcasted_iota(jnp.int32, (tt, LANES), 1)
    out = jnp.zeros((tt, LANES), F32)
    for k in range(TOP_K):
        slot = jnp.sum(jnp.where(lane == idxs[k], slot_of, 0.0), axis=-1, keepdims=True)
        out = jnp.where(wide == _R_IDX + k, idxs[k], out)
        out = jnp.where(wide == _R_GATE + k, es[k] / den, out)
        out = jnp.where(wide == _R_SLOT + k, slot, out)
    o_ref[...] = out
    ot_ref[...] = out.T
    row = lax.broadcasted_iota(jnp.int32, (8, N_EXPERTS), 0)
    tbl = jnp.where(row == _T_COUNT, count, jnp.where(row == _T_SLOT, first_slot, carry_ref[...]))
    tbl_ref[...] = jnp.concatenate([tbl, jnp.zeros((8, LANES - N_EXPERTS), F32)], axis=1)
    carry_ref[...] += RUN_ROWS * runs
    cnt_ref[...] = carry_ref[...]


def _route(lg_p, lg_s):
    rp, rs = lg_p.shape[0], lg_s.shape[0]
    assert rp % TT == 0 and rs % TT == 0
    npt, nst = rp // TT, rs // TT
    nt = npt + nst
    iota = lambda shape, ax: lax.broadcasted_iota(jnp.int32, shape, ax)
    tri = (iota((TT, TT), 1) < iota((TT, TT), 0)).astype(BF16)
    scan = (iota((N_EXPERTS, N_EXPERTS), 0) < iota((N_EXPERTS, N_EXPERTS), 1)).astype(BF16)
    return pl.pallas_call(
        functools.partial(_route_kernel, n_prompt_tiles=npt),
        out_shape=[jax.ShapeDtypeStruct((rp + rs, LANES), F32), jax.ShapeDtypeStruct((LANES, rp + rs), F32),
                   jax.ShapeDtypeStruct((nt * 8, LANES), F32), jax.ShapeDtypeStruct((1, N_EXPERTS), F32)],
        grid=(nt,),
        in_specs=[pl.BlockSpec((TT, N_EXPERTS), lambda i: (jnp.minimum(i, npt - 1), 0)),
                  pl.BlockSpec((TT, N_EXPERTS), lambda i: (jnp.maximum(i - npt, 0), 0)),
                  pl.BlockSpec((TT, TT), lambda i: (0, 0)),
                  pl.BlockSpec((N_EXPERTS, N_EXPERTS), lambda i: (0, 0))],
        out_specs=[pl.BlockSpec((TT, LANES), lambda i: (i, 0)), pl.BlockSpec((LANES, TT), lambda i: (0, i)),
                   pl.BlockSpec((8, LANES), lambda i: (i, 0)), pl.BlockSpec((1, N_EXPERTS), lambda i: (0, 0))],
        scratch_shapes=[pltpu.VMEM((1, N_EXPERTS), F32)],
        compiler_params=_params(("arbitrary",)),
        name="route",
    )(lg_p, lg_s, tri, scan)


RUNS_PER_TILE = N_SLOTS // RUN_ROWS
RUN_UNROLL = 8


def _for_each_run(tile, run_row_ref, fn):
    def group(g, _):
        for u in range(RUN_UNROLL):
            j = g * RUN_UNROLL + u
            row = run_row_ref[tile * RUNS_PER_TILE + j]
            fn(pl.multiple_of(j * RUN_ROWS, RUN_ROWS), pl.multiple_of(row, RUN_ROWS))
        return 0

    lax.fori_loop(0, RUNS_PER_TILE // RUN_UNROLL, group, 0)


def _dispatch_kernel(run_row_ref, pad_start_ref, pad_runs_ref, tail_ref,
                     rt_ref, hp_ref, hs_ref, xs_ref, stage_ref, zero_ref, sem, psem, *, n_prompt_tiles):
    i = pl.program_id(0)

    def run_copy(slot, row):
        return pltpu.make_async_copy(stage_ref.at[pl.ds(slot, RUN_ROWS), :], xs_ref.at[pl.ds(row, RUN_ROWS), :], sem)

    @pl.when(i == 0)
    def _():
        zero_ref[...] = jnp.zeros_like(zero_ref)

        def tail(t, _):
            rows = pl.ds(pl.multiple_of(t * TM_E, TM_E), TM_E)
            cp = pltpu.make_async_copy(zero_ref, xs_ref.at[rows, :], psem)
            cp.start()
            cp.wait()
            return 0

        lax.fori_loop(tail_ref[0], tail_ref[1], tail, 0)

        def per_expert(e, _):
            def pad_copy(r):
                rows = pl.ds(pl.multiple_of(pad_start_ref[e] + r * RUN_ROWS, RUN_ROWS), RUN_ROWS)
                return pltpu.make_async_copy(zero_ref.at[pl.ds(0, RUN_ROWS), :], xs_ref.at[rows, :], psem)

            def start(r, _):
                pad_copy(r).start()
                return 0

            def wait(r, _):
                pad_copy(r).wait()
                return 0

            lax.fori_loop(0, pad_runs_ref[e], start, 0)
            lax.fori_loop(0, pad_runs_ref[e], wait, 0)
            return 0

        lax.fori_loop(0, N_EXPERTS, per_expert, 0)

    rt = rt_ref[...]
    x = _unpack_bf16_pairs(jnp.where(i < n_prompt_tiles, hp_ref[...], hs_ref[...]))
    n_slots = stage_ref.shape[0]
    slot_id = lax.broadcasted_iota(jnp.int32, (n_slots, rt.shape[1]), 0).astype(F32)
    sel = jnp.zeros(slot_id.shape, F32)
    gsel = jnp.zeros(slot_id.shape, F32)
    for k in range(TOP_K):
        hit = slot_id == rt[_R_SLOT + k:_R_SLOT + k + 1, :]
        sel = jnp.where(hit, 1.0, sel)
        gsel = jnp.where(hit, rt[_R_GATE + k:_R_GATE + k + 1, :], gsel)
    rows = _dot(sel.astype(BF16), x)
    gate = jnp.sum(gsel, axis=-1, keepdims=True)
    packed = _pack_bf16_pairs(rows)
    stage_ref[:, :packed.shape[1]] = packed
    stage_ref[:, packed.shape[1]:] = pltpu.bitcast(jnp.broadcast_to(gate, (n_slots, LANES)), jnp.uint32)

    _for_each_run(i, run_row_ref, lambda slot, row: run_copy(slot, row).start())
    _for_each_run(i, run_row_ref, lambda slot, row: run_copy(slot, row).wait())


def _dispatch(h_p, h_s, route_t, run_row, pad_start, pad_runs, tail_tiles, n_rows):
    rp, dp = h_p.shape
    rs = h_s.shape[0]
    assert rp % TT == 0 and rs % TT == 0
    npt, nst = rp // TT, rs // TT
    width = dp + LANES
    grid_spec = pltpu.PrefetchScalarGridSpec(
        num_scalar_prefetch=4,
        grid=(npt + nst,),
        in_specs=[pl.BlockSpec((2 * 8, TT), lambda i, *_: (0, i)),
                  pl.BlockSpec((TT, dp), lambda i, *_: (jnp.minimum(i, npt - 1), 0)),
                  pl.BlockSpec((TT, dp), lambda i, *_: (jnp.maximum(i - npt, 0), 0))],
        out_specs=pl.BlockSpec(memory_space=pl.ANY),
        scratch_shapes=[pltpu.VMEM((N_SLOTS, width), jnp.uint32), pltpu.VMEM((TM_E, width), jnp.uint32),
                        pltpu.SemaphoreType.DMA(()), pltpu.SemaphoreType.DMA(())],
    )
    return pl.pallas_call(
        functools.partial(_dispatch_kernel, n_prompt_tiles=npt),
        out_shape=jax.ShapeDtypeStruct((n_rows, width), jnp.uint32),
        grid_spec=grid_spec,
        compiler_params=_params(("arbitrary",), has_side_effects=True),
        name="dispatch",
    )(run_row, pad_start, pad_runs, tail_tiles, route_t, h_p, h_s)


def _experts_kernel(te_ref, tf_ref, tu_ref, x_ref, w1_ref, b1g_ref, b1l_ref, w2_ref, b2_ref, perm_ref,
                    y_ref, w1s_ref, w2s_ref):
    i = pl.program_id(0)
    dff = w2_ref.shape[1]

    @pl.when(tf_ref[i] == 1)
    def _():
        perm = perm_ref[...]
        half = MXU_DIM // 2
        for c in range(w1_ref.shape[2] // MXU_DIM):
            blk = w1_ref[0, :, c * MXU_DIM:(c + 1) * MXU_DIM].astype(BF16)
            sp = _dot(blk, perm).astype(BF16)
            w1s_ref[:, c * half:(c + 1) * half] = sp[:, :half]
            w1s_ref[:, dff + c * half:dff + (c + 1) * half] = sp[:, half:]
        w2s_ref[...] = w2_ref[0].astype(BF16)

    @pl.when(tu_ref[i] == 1)
    def _():
        d = w2_ref.shape[2]
        x = _unpack_bf16_pairs(x_ref[:, :d // 2])
        gate = pltpu.bitcast(x_ref[:, d // 2:], F32)
        hdn = _dot(x, w1s_ref[...])
        glu = jnp.minimum(hdn[:, :dff] + b1g_ref[0], SWIGLU_LIMIT)
        lin = jnp.clip(hdn[:, dff:] + b1l_ref[0], -SWIGLU_LIMIT, SWIGLU_LIMIT)
        act = glu * _sigmoid(SWIGLU_ALPHA * glu) * (lin + 1.0)
        y = (_dot(act.astype(BF16), w2s_ref[...]) + b2_ref[0]) * jnp.concatenate([gate] * (d // LANES), axis=1)
        y_ref[...] = _pack_hi_lo(y)

    @pl.when(tu_ref[i] == 0)
    def _():
        y_ref[...] = jnp.zeros_like(y_ref)


def _deinterleave_perm():
    j = lax.broadcasted_iota(jnp.int32, (MXU_DIM, MXU_DIM), 0)
    c = lax.broadcasted_iota(jnp.int32, (MXU_DIM, MXU_DIM), 1)
    half = MXU_DIM // 2
    src = jnp.where(c < half, 2 * c, 2 * (c - half) + 1)
    return (j == src).astype(BF16)


def _experts(xs, tile_expert, tile_first, tile_used, w1, b1g, b1l, w2, b2):
    width = xs.shape[1]
    n_tiles = tile_expert.shape[0]
    n_rows = n_tiles * TM_E
    ne, d, two_f = w1.shape
    dff = two_f // 2
    assert width == d // 2 + LANES
    grid_spec = pltpu.PrefetchScalarGridSpec(
        num_scalar_prefetch=3,
        grid=(n_tiles,),
        in_specs=[pl.BlockSpec((TM_E, width), lambda i, te, tf, tu: (i, 0)),
                  pl.BlockSpec((1, d, two_f), lambda i, te, tf, tu: (te[i], 0, 0)),
                  pl.BlockSpec((1, 1, dff), lambda i, te, tf, tu: (te[i], 0, 0)),
                  pl.BlockSpec((1, 1, dff), lambda i, te, tf, tu: (te[i], 0, 0)),
                  pl.BlockSpec((1, dff, d), lambda i, te, tf, tu: (te[i], 0, 0)),
                  pl.BlockSpec((1, 1, d), lambda i, te, tf, tu: (te[i], 0, 0)),
                  pl.BlockSpec((MXU_DIM, MXU_DIM), lambda i, te, tf, tu: (0, 0))],
        out_specs=pl.BlockSpec((TM_E, d), lambda i, te, tf, tu: (i, 0)),
        scratch_shapes=[pltpu.VMEM((d, two_f), BF16), pltpu.VMEM((dff, d), BF16)],
    )
    return pl.pallas_call(
        _experts_kernel,
        out_shape=jax.ShapeDtypeStruct((n_rows, d), jnp.uint32),
        grid_spec=grid_spec,
        compiler_params=_params(("arbitrary",)),
        name="experts",
    )(tile_expert, tile_first, tile_used, xs, w1, b1g, b1l, w2, b2, _deinterleave_perm())


def _combine_kernel(run_row_ref, x1_ref, rt_ref, g2_ref, y_ref, o_ref, stage_ref, sem, *, tile_offset):
    i = pl.program_id(0)

    @pl.when(i == 0)
    def _():
        stage_ref[...] = jnp.zeros_like(stage_ref)

    def run_copy(slot, row):
        return pltpu.make_async_copy(y_ref.at[pl.ds(row, RUN_ROWS), :], stage_ref.at[pl.ds(slot, RUN_ROWS), :], sem)

    _for_each_run(i + tile_offset, run_row_ref, lambda slot, row: run_copy(slot, row).start())
    _for_each_run(i + tile_offset, run_row_ref, lambda slot, row: run_copy(slot, row).wait())

    rt = rt_ref[...]
    tt = rt.shape[0]
    slot_id = lax.broadcasted_iota(jnp.int32, (tt, stage_ref.shape[0]), 1).astype(F32)
    sel = jnp.zeros(slot_id.shape, F32)
    for k in range(TOP_K):
        sel = jnp.where(slot_id == rt[:, _R_SLOT + k:_R_SLOT + k + 1], 1.0, sel)
    sel = sel.astype(BF16)
    hi, lo = _unpack_hi_lo(stage_ref[...])
    o_ref[...] = x1_ref[...] + g2_ref[...] * (_dot(sel, hi) + _dot(sel, lo))


def _combine(x1, route, run_row, g2, seq, y_rows, row_offset):
    r, d = x1.shape
    assert r % TT == 0 and row_offset % TT == 0
    off = row_offset // TT
    g2_o, rpm = _mod_operand(g2, seq, TT)
    if rpm is None:
        g2_spec = pl.BlockSpec((TT, d), lambda i, *_: (i, 0))
    else:
        g2_spec = pl.BlockSpec((None, 1, d), lambda i, *_: ((i * TT) // rpm, 0, 0))
    grid_spec = pltpu.PrefetchScalarGridSpec(
        num_scalar_prefetch=1,
        grid=(r // TT,),
        in_specs=[pl.BlockSpec((TT, d), lambda i, *_: (i, 0)),
                  pl.BlockSpec((TT, LANES), lambda i, *_: (i + off, 0)),
                  g2_spec,
                  pl.BlockSpec(memory_space=pl.ANY)],
        out_specs=pl.BlockSpec((TT, d), lambda i, *_: (i, 0)),
        scratch_shapes=[pltpu.VMEM((N_SLOTS, d), jnp.uint32), pltpu.SemaphoreType.DMA(())],
    )
    return pl.pallas_call(
        functools.partial(_combine_kernel, tile_offset=off),
        out_shape=jax.ShapeDtypeStruct((r, d), F32),
        grid_spec=grid_spec,
        compiler_params=_params(("arbitrary",)),
        name="combine",
    )(run_row, x1, route, g2_o, y_rows)


def _gmlp_tables(w_s, b_s, seq_s):
    g = w_s.shape[0]
    tril = jnp.tril(w_s)
    short = jnp.tril(w_s[:, :seq_s, :seq_s])
    rep = CHUNK_B // seq_s
    blockdiag = jnp.zeros_like(w_s)
    for r in range(rep):
        blockdiag = blockdiag.at[:, r * seq_s:(r + 1) * seq_s, r * seq_s:(r + 1) * seq_s].set(short)
    ws_all = jnp.stack([tril, blockdiag]).astype(BF16)
    bs_full = jnp.repeat(b_s.T, HEAD_DIM, axis=1)
    bs_short = jnp.tile(jnp.repeat(b_s[:, :seq_s].T, HEAD_DIM, axis=1), (rep, 1))
    return ws_all, jnp.stack([bs_full, bs_short])


def _layer(xp, xs, cache_k, cache_v, cp, cs, p):
    bp, tp, d = xp.shape
    bs_, ts, _ = xs.shape
    assert tp % CHUNK_B == 0 and CHUNK_B % ts == 0
    rp, rs = bp * tp, bs_ * ts
    n_tok = rp + rs

    mod = _ada(jnp.concatenate([cp, cs], axis=0), p['w_ada'], p['b_ada'])
    mods = [mod[:, j * d:(j + 1) * d] for j in range(6)]
    mp = [m[:bp] for m in mods]
    msm = [m[bp:] for m in mods]

    tile_h = lambda g: jnp.tile(g, N_HEADS).reshape(1, D_HEADS)
    bd = jnp.kron(jnp.eye(MXU_DIM // HEAD_DIM, dtype=F32),
                  jnp.full((HEAD_DIM, HEAD_DIM), 1.0 / HEAD_DIM, F32)).astype(BF16)
    w_in_b = p['w_in'].astype(BF16)
    n1g = p['norm1_g'].reshape(1, d)
    common = (n1g, w_in_b, tile_h(p['q_norm_g']), tile_h(p['k_norm_g']), tile_h(p['v_norm_g']), bd)
    q_p, k_p, kb_p, v_p, vb_p, u_p, vn_p, sga_p, sgb_p = _inproj(
        xp.reshape(rp, d), mp[0], mp[1], tp, *common, want_vn32=False, kv_transposed=True)
    q_s, k_s, kb_s, v_s, vb_s, u_s, vn_s, sga_s, sgb_s, vn32_s = _inproj(
        xs.reshape(rs, d), msm[0], msm[1], ts, *common, want_vn32=True, kv_transposed=False)

    ya_p = _attn_prompt(q_p, kb_p, vb_p, bp, tp)
    ya_s = _attn_sample(q_s, kb_s, vb_s, cache_k, cache_v, bs_, ts)

    ws_all, bs_all = _gmlp_tables(p['w_s'], p['b_s'], ts)
    wr = p['w_router']
    wr_hi = wr.astype(BF16)
    wr_lo = (wr - wr_hi.astype(F32)).astype(BF16)
    post_w = (p['w_pa'].astype(BF16), p['w_pb'].astype(BF16), p['w_o'].astype(BF16), p['norm2_g'].reshape(1, d),
              jnp.concatenate([wr_hi, wr_lo], axis=1), p['b_router'].reshape(1, N_EXPERTS))
    x1_p, h2_p, lg_p = _post(xp.reshape(rp, d), ya_p, u_p, vn_p, sga_p, sgb_p, mp[2], mp[3], mp[4], tp,
                             ws_all, bs_all, 0, *post_w)
    x1_s, h2_s, lg_s = _post(xs.reshape(rs, d), ya_s, u_s, vn_s, sga_s, sgb_s, msm[2], msm[3], msm[4], ts,
                             ws_all, bs_all, 1, *post_w)

    route, route_t, table, counts = _route(lg_p, lg_s)

    counts = counts[0].astype(jnp.int32)
    padded = (counts + TM_E - 1) // TM_E * TM_E
    pend = jnp.cumsum(padded)
    pstart = pend - padded
    table = table.reshape(-1, 8, LANES)[:, :, :N_EXPERTS].astype(jnp.int32)
    n_token_tiles = n_tok // TT
    runs = (table[:, _T_COUNT] + RUN_ROWS - 1) // RUN_ROWS
    runs_end = jnp.cumsum(runs, axis=1)
    j = jnp.arange(RUNS_PER_TILE, dtype=jnp.int32)
    e_of = jnp.minimum(jnp.sum((runs_end[:, None, :] <= j[None, :, None]).astype(jnp.int32), axis=2), N_EXPERTS - 1)
    first_run = jnp.take_along_axis(runs_end - runs, e_of, axis=1)
    first_row = jnp.take_along_axis(table[:, _T_BASE] + pstart[None, :], e_of, axis=1)
    run_row = first_row + (j[None, :] - first_run) * RUN_ROWS
    is_run = j[None, :] < runs_end[:, -1:]
    n_tiles = -(-(n_tok * TOP_K + n_token_tiles * N_EXPERTS * (RUN_ROWS - 1)) // TM_E) + N_EXPERTS
    tile_start = jnp.arange(n_tiles, dtype=jnp.int32) * TM_E
    tile_used = (tile_start < pend[-1]).astype(jnp.int32)
    last_expert = jnp.sum((pend <= pend[-1] - 1).astype(jnp.int32))
    tile_expert = jnp.minimum(jnp.sum((pend[None, :] <= tile_start[:, None]).astype(jnp.int32), axis=1), last_expert)
    tile_first = jnp.concatenate([jnp.ones((1,), jnp.int32),
                                  (tile_expert[1:] != tile_expert[:-1]).astype(jnp.int32)])

    spare_tiles = -(-N_SLOTS // TM_E)
    spare_rows = j[None, :] * RUN_ROWS
    rows_out = jnp.where(is_run, run_row, n_tiles * TM_E + spare_rows).reshape(-1)
    rows_in = jnp.where(is_run, run_row, spare_rows).reshape(-1)
    tail_tiles = jnp.stack([pend[-1] // TM_E, jnp.int32(n_tiles + spare_tiles)]).astype(jnp.int32)
    pad_runs = (padded - counts) // RUN_ROWS
    xs_rows = _dispatch(h2_p, h2_s, route_t, rows_out, pstart + counts, pad_runs, tail_tiles,
                        (n_tiles + spare_tiles) * TM_E)
    b1 = p['b1']
    ne = b1.shape[0]
    y_rows = _experts(xs_rows, tile_expert, tile_first, tile_used, p['w1'],
                      b1[:, 0::2].reshape(ne, 1, -1), b1[:, 1::2].reshape(ne, 1, -1),
                      p['w2'], p['b2'].reshape(ne, 1, d))
    out_p = _combine(x1_p, route, rows_in, mp[5], tp, y_rows, 0)
    out_s = _combine(x1_s, route, rows_in, msm[5], ts, y_rows, rp)

    shp = lambda a, b, t: a.reshape(b, t, N_HEADS, HEAD_DIM)
    unt = lambda a: jnp.transpose(a.reshape(bp, N_HEADS, HEAD_DIM, tp), (0, 3, 1, 2))
    return (out_p.reshape(bp, tp, d), out_s.reshape(bs_, ts, d),
            unt(k_p), unt(v_p), shp(k_s, bs_, ts), shp(v_s, bs_, ts), shp(vn32_s, bs_, ts))


def kernel(x_prompt, x_sample, cache_k, cache_v, c_prompt, c_sample, norm1_g, norm2_g, w_ada, b_ada, w_in,
           q_norm_g, k_norm_g, v_norm_g, w_s, b_s, w_pa, w_pb, w_o, w_router, b_router, w1, b1, w2, b2):
    depth = w_in.shape[0]
    xp, xs = x_prompt, x_sample
    kp, vp, ks, vs, vbs = [], [], [], [], []
    for l in range(depth):
        p = {'norm1_g': norm1_g[l], 'norm2_g': norm2_g[l], 'w_ada': w_ada[l], 'b_ada': b_ada[l],
             'w_in': w_in[l], 'q_norm_g': q_norm_g[l], 'k_norm_g': k_norm_g[l], 'v_norm_g': v_norm_g[l],
             'w_s': w_s[l], 'b_s': b_s[l], 'w_pa': w_pa[l], 'w_pb': w_pb[l], 'w_o': w_o[l],
             'w_router': w_router[l], 'b_router': b_router[l], 'w1': w1[l], 'b1': b1[l],
             'w2': w2[l], 'b2': b2[l]}
        xp, xs, k_p, v_p, k_s, v_s, vn_s = _layer(xp, xs, cache_k[l], cache_v[l], c_prompt, c_sample, p)
        kp.append(k_p); vp.append(v_p); ks.append(k_s); vs.append(v_s); vbs.append(vn_s)
    return (xp, xs, jnp.stack(kp), jnp.stack(vp), jnp.stack(ks), jnp.stack(vs), jnp.stack(vbs))
```

```python
import functools

import jax
import jax.numpy as jnp
from jax import lax
from jax.experimental import pallas as pl
from jax.experimental.pallas import tpu as pltpu

F32 = jnp.float32
BF16 = jnp.bfloat16

LANES = 128
MXU_DIM = 256
VMEM_LIMIT = 56 * 1024 * 1024

HEAD_DIM = 64
N_HEADS = 8
D_HEADS = N_HEADS * HEAD_DIM
PAIRS = D_HEADS // LANES
CHUNK_B = 128
N_EXPERTS = 32
TOP_K = 4
RMS_EPS = 1e-6
SWIGLU_ALPHA = 1.702
SWIGLU_LIMIT = 7.0

TM_IN = 512
TQ = 256
TK = 256
SUB_Q = 128
LOOK_BACK = 256
WINDOW = SUB_Q + LOOK_BACK
TM_POST = 256
TT = 256
TM_E = 512
RUN_ROWS = 8
N_SLOTS = TT * TOP_K + N_EXPERTS * RUN_ROWS


def _params(sem, **kw):
    return pltpu.CompilerParams(dimension_semantics=sem, vmem_limit_bytes=VMEM_LIMIT, **kw)


def _dot(a, b):
    return jnp.dot(a, b, preferred_element_type=F32)


def _dot_nt(a, b):
    return lax.dot_general(a, b, (((1,), (1,)), ((), ())), preferred_element_type=F32)


def _gelu(x):
    return jax.nn.gelu(x)


def _sigmoid(x):
    return 1.0 / (1.0 + jnp.exp(-x))


def _ada_kernel(c_ref, w_ref, b_ref, o_ref):
    c = c_ref[...]
    s = c * _sigmoid(c)
    o_ref[...] = jnp.dot(s, w_ref[...], preferred_element_type=F32,
                         precision=lax.Precision.HIGHEST) + b_ref[...]


def _ada(c, w_ada, b_ada):
    nb, d = c.shape
    n = w_ada.shape[1]
    tn = 1024
    return pl.pallas_call(
        _ada_kernel,
        out_shape=jax.ShapeDtypeStruct((nb, n), F32),
        grid=(n // tn,),
        in_specs=[pl.BlockSpec((nb, d), lambda j: (0, 0)),
                  pl.BlockSpec((d, tn), lambda j: (0, j)),
                  pl.BlockSpec((1, tn), lambda j: (0, j))],
        out_specs=pl.BlockSpec((nb, tn), lambda j: (0, j)),
        compiler_params=_params(("parallel",)),
        name="ada",
    )(c, w_ada, b_ada.reshape(1, n))


def _head_mean_sq(p, bd):
    sq = (p * p).astype(BF16)
    return jnp.concatenate([_dot(sq[:, :MXU_DIM], bd), _dot(sq[:, MXU_DIM:], bd)], axis=-1)


def _inproj_kernel(x_ref, sh_ref, sc_ref, g_ref, w_ref, qg_ref, kg_ref, vg_ref, bd_ref,
                   q_ref, k32_ref, kb_ref, v32_ref, vb_ref, u_ref, vnb_ref, sga_ref, sgb_ref, *vn32_ref,
                   kv_transposed):
    kv_out = (lambda a: a.T) if kv_transposed else (lambda a: a)
    x = x_ref[...]
    ms = jnp.mean(x * x, axis=-1, keepdims=True)
    xn = x * lax.rsqrt(ms + RMS_EPS) * g_ref[...]
    h = (xn * (1.0 + sc_ref[...]) + sh_ref[...]).astype(BF16)
    bd = bd_ref[...]
    d = x.shape[1]

    def sec(a, n):
        return _dot(h, w_ref[:, a:a + n])

    o = 0
    pq = sec(o, D_HEADS); o += D_HEADS
    q = pq * lax.rsqrt(_head_mean_sq(pq, bd) + RMS_EPS) * qg_ref[...]
    q_ref[...] = (q * (HEAD_DIM ** -0.5)).astype(BF16)
    pk = sec(o, D_HEADS); o += D_HEADS
    k = pk * lax.rsqrt(_head_mean_sq(pk, bd) + RMS_EPS) * kg_ref[...]
    k32_ref[...] = kv_out(k)
    kb_ref[...] = k.astype(BF16)
    pv = sec(o, D_HEADS); o += D_HEADS
    v32_ref[...] = kv_out(pv)
    vb_ref[...] = pv.astype(BF16)
    pu = sec(o, D_HEADS); o += D_HEADS
    u_ref[...] = _gelu(pu).astype(BF16)
    pvb = _gelu(sec(o, D_HEADS)); o += D_HEADS
    vn = pvb * lax.rsqrt(_head_mean_sq(pvb, bd) + RMS_EPS) * vg_ref[...]
    vnb_ref[...] = vn.astype(BF16)
    if vn32_ref:
        vn32_ref[0][...] = vn
    sga_ref[...] = _sigmoid(sec(o, d)).astype(BF16); o += d
    sgb_ref[...] = _sigmoid(sec(o, d)).astype(BF16)


def _mod_spec(tm, arr, rows_per_mod):
    if rows_per_mod is None:
        return pl.BlockSpec((tm, arr.shape[-1]), lambda i: (i, 0))
    return pl.BlockSpec((None, 1, arr.shape[-1]), lambda i: ((i * tm) // rows_per_mod, 0, 0))


def _mod_operand(m, seq, tm):
    if seq % tm == 0:
        return m[:, None, :], seq
    return jnp.repeat(m, seq, axis=0), None


def _inproj(x, sh, sc, seq, g, w_in_b, qg, kg, vg, bd, want_vn32, kv_transposed):
    r, d = x.shape
    tm = min(TM_IN, r)
    assert r % tm == 0 and (not kv_transposed or seq % tm == 0)
    sh_o, rpm = _mod_operand(sh, seq, tm)
    sc_o, _ = _mod_operand(sc, seq, tm)
    ncol = w_in_b.shape[1]
    const = lambda i: (0, 0)
    row = lambda n: pl.BlockSpec((tm, n), lambda i: (i, 0))
    outs = [jax.ShapeDtypeStruct((r, D_HEADS), BF16),
            jax.ShapeDtypeStruct((r, D_HEADS), F32),
            jax.ShapeDtypeStruct((r, D_HEADS), BF16),
            jax.ShapeDtypeStruct((r, D_HEADS), F32),
            jax.ShapeDtypeStruct((r, D_HEADS), BF16),
            jax.ShapeDtypeStruct((r, D_HEADS), BF16),
            jax.ShapeDtypeStruct((r, D_HEADS), BF16),
            jax.ShapeDtypeStruct((r, d), BF16),
            jax.ShapeDtypeStruct((r, d), BF16)]
    out_specs = [row(D_HEADS)] * 7 + [row(d), row(d)]
    if kv_transposed:
        per_seq = seq // tm
        for j in (1, 3):
            outs[j] = jax.ShapeDtypeStruct((r // seq, D_HEADS, seq), F32)
            out_specs[j] = pl.BlockSpec((None, D_HEADS, tm), lambda i: (i // per_seq, 0, i % per_seq))
    if want_vn32:
        outs.append(jax.ShapeDtypeStruct((r, D_HEADS), F32))
        out_specs.append(row(D_HEADS))
    return pl.pallas_call(
        functools.partial(_inproj_kernel, kv_transposed=kv_transposed),
        out_shape=outs,
        grid=(r // tm,),
        in_specs=[row(d), _mod_spec(tm, sh_o, rpm), _mod_spec(tm, sc_o, rpm),
                  pl.BlockSpec((1, d), const),
                  pl.BlockSpec((d, ncol), const, pipeline_mode=pl.Buffered(1)),
                  pl.BlockSpec((1, D_HEADS), const), pl.BlockSpec((1, D_HEADS), const),
                  pl.BlockSpec((1, D_HEADS), const), pl.BlockSpec((MXU_DIM, MXU_DIM), const)],
        out_specs=out_specs,
        compiler_params=_params(("parallel",)),
        name="inproj",
    )(x, sh_o, sc_o, g, w_in_b, qg, kg, vg, bd)


LOG_WEIGHT_CUTOFF = -104.0


def _sb_weights(z, u, c, mask):
    lom = -(jnp.maximum(z, 0.0) + jnp.log(1.0 + jnp.exp(-jnp.abs(z))))
    if mask is not None:
        lom = jnp.where(mask, lom, 0.0)
    hi = lom.astype(BF16)
    lo = (lom - hi.astype(F32)).astype(BF16)
    later = _dot(hi, u) + _dot(lo, u)
    w = jnp.exp(z + lom + later + c)
    if mask is not None:
        w = jnp.where(mask, w, 0.0)
    return w, c + jnp.sum(lom, axis=-1, keepdims=True)


def _pair_select(shape):
    return lax.broadcasted_iota(jnp.int32, shape, 1) < HEAD_DIM


def _causal_mask(m, tq, tk):
    row = lax.broadcasted_iota(jnp.int32, (m, tk), 0) & (tq - 1)
    return lax.broadcasted_iota(jnp.int32, (m, tk), 1) < row


def _stack_heads(q):
    first = _pair_select(q.shape)
    zero = jnp.zeros_like(q)
    return jnp.concatenate([jnp.where(first, q, zero), jnp.where(first, zero, q)], axis=0)


def _attn_prompt_kernel(q_ref, k_ref, v_ref, uw_ref, u_ref, o_ref, acc_ref, c_ref):
    i = pl.program_id(2)
    tq = q_ref.shape[0]
    nsub = tq // SUB_Q
    sub_rows = 2 * SUB_Q
    q = q_ref[...]
    u = u_ref[...]
    first = _pair_select((SUB_Q, LANES))
    iotas = lambda shape: (lax.broadcasted_iota(jnp.int32, shape, 0), lax.broadcasted_iota(jnp.int32, shape, 1))

    def pv_sub(wb, s, vs):
        w0 = wb[s * sub_rows:s * sub_rows + SUB_Q]
        w1 = wb[s * sub_rows + SUB_Q:(s + 1) * sub_rows]
        return jnp.where(first, _dot(w0, vs), _dot(w1, vs))

    @pl.when(i == 0)
    def _():
        w, _ = _sb_weights(_dot_nt(_stack_heads(q), k_ref[0:TK, :]), u, jnp.zeros((2 * tq, 1), F32),
                           _causal_mask(2 * tq, tq, TK))
        wb = w.astype(BF16)
        vs = v_ref[0:TK, :]
        o_ref[...] = jnp.where(_pair_select((tq, LANES)), _dot(wb[:tq], vs), _dot(wb[tq:], vs)).astype(o_ref.dtype)

    @pl.when(i > 0)
    def _():
        qs = [_stack_heads(q[s * SUB_Q:(s + 1) * SUB_Q]) for s in range(nsub)]
        starts = [pl.multiple_of(i * tq + s * SUB_Q - LOOK_BACK, SUB_Q) for s in range(nsub)]
        z = jnp.concatenate([_dot_nt(qs[s], k_ref[pl.ds(starts[s], WINDOW), :]) for s in range(nsub)], axis=0)
        m = nsub * sub_rows
        row, col = iotas((m, WINDOW))
        w, c = _sb_weights(z, uw_ref[...], jnp.zeros((m, 1), F32), col < (row & (SUB_Q - 1)) + LOOK_BACK)
        wb = w.astype(BF16)
        acc_ref[...] = jnp.concatenate([pv_sub(wb, s, v_ref[pl.ds(starts[s], WINDOW), :]) for s in range(nsub)],
                                       axis=0)
        c_ref[...] = c
        q_all = jnp.concatenate(qs, axis=0)
        row, col = iotas((m, TK))
        window_start = i * tq + (row // sub_rows) * SUB_Q - LOOK_BACK

        def cond(state):
            j, cmax = state
            return jnp.logical_and(j >= 0, cmax > LOG_WEIGHT_CUTOFF)

        def body(state):
            j, _ = state
            start = pl.multiple_of(j * TK, TK)
            vs = v_ref[pl.ds(start, TK), :]
            w, c = _sb_weights(_dot_nt(q_all, k_ref[pl.ds(start, TK), :]), u, c_ref[...],
                               j * TK + col < window_start)
            wb = w.astype(BF16)
            acc_ref[...] += jnp.concatenate([pv_sub(wb, s, vs) for s in range(nsub)], axis=0)
            c_ref[...] = c
            return j - 1, jnp.max(c)

        lax.while_loop(cond, body, (i - 1, jnp.max(c)))
        o_ref[...] = acc_ref[...].astype(o_ref.dtype)


def _upper_ones(n):
    return (lax.broadcasted_iota(jnp.int32, (n, n), 0) > lax.broadcasted_iota(jnp.int32, (n, n), 1)).astype(BF16)


def _attn_prompt(q, kb, vb, batch, seq):
    assert TQ == TK and seq % TQ == 0 and LOOK_BACK <= TQ
    nq = seq // TQ
    return pl.pallas_call(
        _attn_prompt_kernel,
        out_shape=jax.ShapeDtypeStruct(q.shape, BF16),
        grid=(batch, PAIRS, nq),
        in_specs=[pl.BlockSpec((TQ, LANES), lambda b, p, i: (b * nq + i, p)),
                  pl.BlockSpec((seq, LANES), lambda b, p, i: (b, p)),
                  pl.BlockSpec((seq, LANES), lambda b, p, i: (b, p)),
                  pl.BlockSpec((WINDOW, WINDOW), lambda b, p, i: (0, 0)),
                  pl.BlockSpec((TK, TK), lambda b, p, i: (0, 0))],
        out_specs=pl.BlockSpec((TQ, LANES), lambda b, p, i: (b * nq + i, p)),
        scratch_shapes=[pltpu.VMEM((TQ, LANES), F32), pltpu.VMEM((2 * TQ, 1), F32)],
        compiler_params=_params(("parallel", "parallel", "arbitrary")),
        name="attn_prompt",
    )(q, kb, vb, _upper_ones(WINDOW), _upper_ones(TK))


def _attn_sample_kernel(q_ref, kn_ref, vn_ref, kc_ref, vc_ref, un_ref, u_ref, o_ref,
                        kbuf_ref, vbuf_ref, sem, acc_ref, c_ref):
    b = pl.program_id(0)
    tq = q_ref.shape[0]
    nblk = kc_ref.shape[3] // TK
    head = lambda a, h: a[:, h * HEAD_DIM:(h + 1) * HEAD_DIM]
    q = q_ref[...]
    qh = [head(q, h) for h in range(N_HEADS)]

    def copies(j, slot):
        rows = pl.ds(pl.multiple_of(j * TK, TK), TK)
        return (pltpu.make_async_copy(kc_ref.at[b, :, :, rows], kbuf_ref.at[slot], sem.at[0, slot]),
                pltpu.make_async_copy(vc_ref.at[b, :, :, rows], vbuf_ref.at[slot], sem.at[1, slot]))

    def start(j, slot):
        for cp in copies(j, slot):
            cp.start()

    def wait(j, slot):
        for cp in copies(j, slot):
            cp.wait()

    def sweep(score, apply, u, c, mask):
        z = jnp.concatenate([score(qh[h], h) for h in range(N_HEADS)], axis=0)
        w, c = _sb_weights(z, u, c, mask)
        wb = w.astype(BF16)
        pv = jnp.concatenate([apply(wb[h * tq:(h + 1) * tq], h) for h in range(N_HEADS)], axis=-1)
        return pv, c

    start(nblk - 1, 0)
    kn = kn_ref[...]
    vn = vn_ref[...]
    pv, c = sweep(lambda qq, h: _dot_nt(qq, head(kn, h)), lambda ww, h: _dot(ww, head(vn, h)), un_ref[...],
                  jnp.zeros((N_HEADS * tq, 1), F32), _causal_mask(N_HEADS * tq, tq, tq))
    acc_ref[...] = pv
    c_ref[...] = c
    u = u_ref[...]

    def cond(state):
        j, _, cmax = state
        return jnp.logical_and(j >= 0, cmax > LOG_WEIGHT_CUTOFF)

    def body(state):
        j, slot, _ = state
        wait(j, slot)

        @pl.when(j > 0)
        def _():
            start(j - 1, 1 - slot)

        pv, c = sweep(lambda qq, h: _dot(qq, kbuf_ref[slot, h].astype(BF16)),
                      lambda ww, h: _dot_nt(ww, vbuf_ref[slot, h].astype(BF16)), u, c_ref[...], None)
        acc_ref[...] += pv
        c_ref[...] = c
        return j - 1, 1 - slot, jnp.max(c)

    j, slot, _ = lax.while_loop(cond, body, (nblk - 1, 0, jnp.max(c)))

    @pl.when(j >= 0)
    def _():
        wait(j, slot)

    o_ref[...] = acc_ref[...].astype(o_ref.dtype)


def _attn_sample(q, kb, vb, cache_k, cache_v, batch, seq):
    past = cache_k.shape[1]
    assert past % TK == 0 and past >= TK
    cache_k = jnp.transpose(cache_k, (0, 2, 3, 1))
    cache_v = jnp.transpose(cache_v, (0, 2, 3, 1))
    row = pl.BlockSpec((seq, D_HEADS), lambda b: (b, 0))
    return pl.pallas_call(
        _attn_sample_kernel,
        out_shape=jax.ShapeDtypeStruct(q.shape, BF16),
        grid=(batch,),
        in_specs=[row, row, row,
                  pl.BlockSpec(memory_space=pl.ANY), pl.BlockSpec(memory_space=pl.ANY),
                  pl.BlockSpec((seq, seq), lambda b: (0, 0)),
                  pl.BlockSpec((TK, TK), lambda b: (0, 0))],
        out_specs=row,
        scratch_shapes=[pltpu.VMEM((2, N_HEADS, HEAD_DIM, TK), F32), pltpu.VMEM((2, N_HEADS, HEAD_DIM, TK), F32),
                        pltpu.SemaphoreType.DMA((2, 2)),
                        pltpu.VMEM((seq, D_HEADS), F32), pltpu.VMEM((N_HEADS * seq, 1), F32)],
        compiler_params=_params(("arbitrary",)),
        name="attn_sample",
    )(q, kb, vb, cache_k, cache_v, _upper_ones(seq), _upper_ones(TK))


def _pack_bf16_pairs(x):
    half = x.shape[1] // 2
    a = pltpu.bitcast(x[:, :half], jnp.uint32)
    b = pltpu.bitcast(x[:, half:], jnp.uint32)
    return a | (b >> 16)


def _unpack_bf16_pairs(p):
    a = pltpu.bitcast(p & jnp.uint32(0xFFFF0000), F32).astype(BF16)
    b = pltpu.bitcast(p << 16, F32).astype(BF16)
    return jnp.concatenate([a, b], axis=1)


def _pack_hi_lo(y):
    hi = y.astype(BF16).astype(F32)
    lo = (y - hi).astype(BF16).astype(F32)
    return pltpu.bitcast(hi, jnp.uint32) | (pltpu.bitcast(lo, jnp.uint32) >> 16)


def _unpack_hi_lo(p):
    hi = pltpu.bitcast(p & jnp.uint32(0xFFFF0000), F32).astype(BF16)
    lo = pltpu.bitcast(p << 16, F32).astype(BF16)
    return hi, lo


def _post_kernel(x_ref, ya_ref, u_ref, vn_ref, sga_ref, sgb_ref, g1_ref, sh2_ref, sc2_ref,
                 ws_ref, bs_ref, wpa_ref, wpb_ref, wo_ref, n2g_ref, wr_ref, br_ref,
                 x1_ref, h2_ref, lg_ref):
    tm = x_ref.shape[0]
    first = _pair_select((CHUNK_B, LANES))
    chunks = []
    for c in range(tm // CHUNK_B):
        rows = slice(c * CHUNK_B, (c + 1) * CHUNK_B)
        parts = []
        for p in range(PAIRS):
            vnp = vn_ref[rows, p * LANES:(p + 1) * LANES]
            parts.append(jnp.where(first, _dot(ws_ref[0, 2 * p], vnp), _dot(ws_ref[0, 2 * p + 1], vnp)))
        s = jnp.concatenate(parts, axis=-1) + bs_ref[0]
        chunks.append((u_ref[rows, :].astype(F32) * s).astype(BF16))
    yb = jnp.concatenate(chunks, axis=0)
    merged = (sga_ref[...].astype(F32) * _dot(ya_ref[...], wpa_ref[...])
              + sgb_ref[...].astype(F32) * _dot(yb, wpb_ref[...]))
    x1 = x_ref[...] + g1_ref[...] * _dot(merged.astype(BF16), wo_ref[...])
    x1_ref[...] = x1
    ms = jnp.mean(x1 * x1, axis=-1, keepdims=True)
    h2 = (x1 * lax.rsqrt(ms + RMS_EPS) * n2g_ref[...]) * (1.0 + sc2_ref[...]) + sh2_ref[...]
    hi = h2.astype(BF16)
    hf = hi.astype(F32)
    lo = (h2 - hf).astype(BF16)
    h2_ref[...] = _pack_bf16_pairs(hf)
    l2 = _dot(hi, wr_ref[...]) + _dot(lo, wr_ref[...])
    lg_ref[...] = l2[:, :N_EXPERTS] + l2[:, N_EXPERTS:] + br_ref[...]


def _post(x, ya, u, vn, sga, sgb, g1, sh2, sc2, seq, ws_all, bs_all, ws_idx, wpa, wpb, wo, n2g, wr, br):
    r, d = x.shape
    tm = min(TM_POST, r)
    assert r % tm == 0 and tm % CHUNK_B == 0
    g1_o, rpm = _mod_operand(g1, seq, tm)
    sh2_o, _ = _mod_operand(sh2, seq, tm)
    sc2_o, _ = _mod_operand(sc2, seq, tm)
    const = lambda i: (0, 0)
    row = lambda n: pl.BlockSpec((tm, n), lambda i: (i, 0))
    return pl.pallas_call(
        _post_kernel,
        out_shape=[jax.ShapeDtypeStruct((r, d), F32), jax.ShapeDtypeStruct((r, d // 2), jnp.uint32),
                   jax.ShapeDtypeStruct((r, N_EXPERTS), F32)],
        grid=(r // tm,),
        in_specs=[row(d), row(D_HEADS), row(D_HEADS), row(D_HEADS), row(d), row(d),
                  _mod_spec(tm, g1_o, rpm), _mod_spec(tm, sh2_o, rpm), _mod_spec(tm, sc2_o, rpm),
                  pl.BlockSpec((1, N_HEADS, CHUNK_B, CHUNK_B), lambda i: (ws_idx, 0, 0, 0)),
                  pl.BlockSpec((1, CHUNK_B, D_HEADS), lambda i: (ws_idx, 0, 0)),
                  pl.BlockSpec(wpa.shape, const), pl.BlockSpec(wpb.shape, const), pl.BlockSpec(wo.shape, const),
                  pl.BlockSpec((1, d), const), pl.BlockSpec(wr.shape, const), pl.BlockSpec((1, N_EXPERTS), const)],
        out_specs=[row(d), row(d // 2), row(N_EXPERTS)],
        compiler_params=_params(("parallel",)),
        name="post",
    )(x, ya, u, vn, sga, sgb, g1_o, sh2_o, sc2_o, ws_all, bs_all, wpa, wpb, wo, n2g, wr, br)


_R_IDX, _R_GATE, _R_SLOT = 0, TOP_K, 2 * TOP_K
_T_COUNT, _T_SLOT, _T_BASE = 0, 1, 2


def _route_kernel(lp_ref, ls_ref, tri_ref, scan_ref, o_ref, ot_ref, tbl_ref, cnt_ref, carry_ref, *, n_prompt_tiles):
    i = pl.program_id(0)

    @pl.when(i == 0)
    def _():
        carry_ref[...] = jnp.zeros_like(carry_ref)

    l = jnp.where(i < n_prompt_tiles, lp_ref[...], ls_ref[...])
    tt = l.shape[0]
    lane = lax.broadcasted_iota(jnp.int32, l.shape, 1).astype(F32)
    cur = l
    vals, idxs = [], []
    for _ in range(TOP_K):
        m = jnp.max(cur, axis=-1, keepdims=True)
        idx = jnp.min(jnp.where(cur == m, lane, float(N_EXPERTS)), axis=-1, keepdims=True)
        vals.append(m)
        idxs.append(idx)
        cur = jnp.where(lane == idx, -jnp.inf, cur)
    es = [jnp.exp(v - vals[0]) for v in vals]
    den = es[0] + es[1] + es[2] + es[3]
    onehot = jnp.zeros(l.shape, F32)
    for idx in idxs:
        onehot = onehot + (lane == idx).astype(F32)
    within = _dot(tri_ref[...], onehot.astype(BF16))
    count = jnp.sum(onehot, axis=0, keepdims=True)
    runs = jnp.ceil(count * (1.0 / RUN_ROWS))
    first_slot = RUN_ROWS * _dot(jnp.broadcast_to(runs, (8, N_EXPERTS)).astype(BF16), scan_ref[...])[0:1]
    slot_of = within + first_slot
    wide = lax.broadcasted_iota(jnp.int32, (tt, LANES), 1)
    out = jnp.zeros((tt, LANES), F32)
    for k in range(TOP_K):
        slot = jnp.sum(jnp.where(lane == idxs[k], slot_of, 0.0), axis=-1, keepdims=True)
        out = jnp.where(wide == _R_IDX + k, idxs[k], out)
        out = jnp.where(wide == _R_GATE + k, es[k] / den, out)
        out = jnp.where(wide == _R_SLOT + k, slot, out)
    o_ref[...] = out
    ot_ref[...] = out.T
    row = lax.broadcasted_iota(jnp.int32, (8, N_EXPERTS), 0)
    tbl = jnp.where(row == _T_COUNT, count, jnp.where(row == _T_SLOT, first_slot, carry_ref[...]))
    tbl_ref[...] = jnp.concatenate([tbl, jnp.zeros((8, LANES - N_EXPERTS), F32)], axis=1)
    carry_ref[...] += RUN_ROWS * runs
    cnt_ref[...] = carry_ref[...]


def _route(lg_p, lg_s):
    rp, rs = lg_p.shape[0], lg_s.shape[0]
    assert rp % TT == 0 and rs % TT == 0
    npt, nst = rp // TT, rs // TT
    nt = npt + nst
    iota = lambda shape, ax: lax.broadcasted_iota(jnp.int32, shape, ax)
    tri = (iota((TT, TT), 1) < iota((TT, TT), 0)).astype(BF16)
    scan = (iota((N_EXPERTS, N_EXPERTS), 0) < iota((N_EXPERTS, N_EXPERTS), 1)).astype(BF16)
    return pl.pallas_call(
        functools.partial(_route_kernel, n_prompt_tiles=npt),
        out_shape=[jax.ShapeDtypeStruct((rp + rs, LANES), F32), jax.ShapeDtypeStruct((LANES, rp + rs), F32),
                   jax.ShapeDtypeStruct((nt * 8, LANES), F32), jax.ShapeDtypeStruct((1, N_EXPERTS), F32)],
        grid=(nt,),
        in_specs=[pl.BlockSpec((TT, N_EXPERTS), lambda i: (jnp.minimum(i, npt - 1), 0)),
                  pl.BlockSpec((TT, N_EXPERTS), lambda i: (jnp.maximum(i - npt, 0), 0)),
                  pl.BlockSpec((TT, TT), lambda i: (0, 0)),
                  pl.BlockSpec((N_EXPERTS, N_EXPERTS), lambda i: (0, 0))],
        out_specs=[pl.BlockSpec((TT, LANES), lambda i: (i, 0)), pl.BlockSpec((LANES, TT), lambda i: (0, i)),
                   pl.BlockSpec((8, LANES), lambda i: (i, 0)), pl.BlockSpec((1, N_EXPERTS), lambda i: (0, 0))],
        scratch_shapes=[pltpu.VMEM((1, N_EXPERTS), F32)],
        compiler_params=_params(("arbitrary",)),
        name="route",
    )(lg_p, lg_s, tri, scan)


RUNS_PER_TILE = N_SLOTS // RUN_ROWS
RUN_UNROLL = 8


def _for_each_run(tile, run_row_ref, fn):
    def group(g, _):
        for u in range(RUN_UNROLL):
            j = g * RUN_UNROLL + u
            row = run_row_ref[tile * RUNS_PER_TILE + j]
            fn(pl.multiple_of(j * RUN_ROWS, RUN_ROWS), pl.multiple_of(row, RUN_ROWS))
        return 0

    lax.fori_loop(0, RUNS_PER_TILE // RUN_UNROLL, group, 0)


def _dispatch_kernel(run_row_ref, pad_start_ref, pad_runs_ref, tail_ref,
                     rt_ref, hp_ref, hs_ref, xs_ref, stage_ref, zero_ref, sem, psem, *, n_prompt_tiles):
    i = pl.program_id(0)
    last = pl.num_programs(0) - 1
    buf = i % 2

    def start_runs(tile, b):
        _for_each_run(tile, run_row_ref, lambda slot, row: pltpu.make_async_copy(
            stage_ref.at[b, pl.ds(slot, RUN_ROWS), :], xs_ref.at[pl.ds(row, RUN_ROWS), :], sem.at[b]).start())

    def wait_runs(tile, b):
        _for_each_run(tile, run_row_ref, lambda slot, row: pltpu.make_async_copy(
            stage_ref.at[b, pl.ds(slot, RUN_ROWS), :], xs_ref.at[pl.ds(row, RUN_ROWS), :], sem.at[b]).wait())

    @pl.when(i == 0)
    def _():
        zero_ref[...] = jnp.zeros_like(zero_ref)

        def tail(t, _):
            rows = pl.ds(pl.multiple_of(t * TM_E, TM_E), TM_E)
            cp = pltpu.make_async_copy(zero_ref, xs_ref.at[rows, :], psem)
            cp.start()
            cp.wait()
            return 0

        lax.fori_loop(tail_ref[0], tail_ref[1], tail, 0)

        def per_expert(e, _):
            def pad_copy(r):
                rows = pl.ds(pl.multiple_of(pad_start_ref[e] + r * RUN_ROWS, RUN_ROWS), RUN_ROWS)
                return pltpu.make_async_copy(zero_ref.at[pl.ds(0, RUN_ROWS), :], xs_ref.at[rows, :], psem)

            def start(r, _):
                pad_copy(r).start()
                return 0

            def wait(r, _):
                pad_copy(r).wait()
                return 0

            lax.fori_loop(0, pad_runs_ref[e], start, 0)
            lax.fori_loop(0, pad_runs_ref[e], wait, 0)
            return 0

        lax.fori_loop(0, N_EXPERTS, per_expert, 0)

    @pl.when(i >= 2)
    def _():
        wait_runs(i - 2, buf)

    rt = rt_ref[...]
    x = _unpack_bf16_pairs(jnp.where(i < n_prompt_tiles, hp_ref[...], hs_ref[...]))
    n_slots = stage_ref.shape[1]
    slot_id = lax.broadcasted_iota(jnp.int32, (n_slots, rt.shape[1]), 0).astype(F32)
    sel = jnp.zeros(slot_id.shape, F32)
    gsel = jnp.zeros(slot_id.shape, F32)
    for k in range(TOP_K):
        hit = slot_id == rt[_R_SLOT + k:_R_SLOT + k + 1, :]
        sel = jnp.where(hit, 1.0, sel)
        gsel = jnp.where(hit, rt[_R_GATE + k:_R_GATE + k + 1, :], gsel)
    rows = _dot(sel.astype(BF16), x)
    gate = jnp.sum(gsel, axis=-1, keepdims=True)
    packed = _pack_bf16_pairs(rows)
    stage_ref[buf, :, :packed.shape[1]] = packed
    stage_ref[buf, :, packed.shape[1]:] = pltpu.bitcast(jnp.broadcast_to(gate, (n_slots, LANES)), jnp.uint32)
    start_runs(i, buf)

    @pl.when(i == last)
    def _():
        @pl.when(i >= 1)
        def _():
            wait_runs(i - 1, 1 - buf)

        wait_runs(i, buf)


def _dispatch(h_p, h_s, route_t, run_row, pad_start, pad_runs, tail_tiles, n_rows):
    rp, dp = h_p.shape
    rs = h_s.shape[0]
    assert rp % TT == 0 and rs % TT == 0
    npt, nst = rp // TT, rs // TT
    width = dp + LANES
    grid_spec = pltpu.PrefetchScalarGridSpec(
        num_scalar_prefetch=4,
        grid=(npt + nst,),
        in_specs=[pl.BlockSpec((2 * 8, TT), lambda i, *_: (0, i)),
                  pl.BlockSpec((TT, dp), lambda i, *_: (jnp.minimum(i, npt - 1), 0)),
                  pl.BlockSpec((TT, dp), lambda i, *_: (jnp.maximum(i - npt, 0), 0))],
        out_specs=pl.BlockSpec(memory_space=pl.ANY),
        scratch_shapes=[pltpu.VMEM((2, N_SLOTS, width), jnp.uint32), pltpu.VMEM((TM_E, width), jnp.uint32),
                        pltpu.SemaphoreType.DMA((2,)), pltpu.SemaphoreType.DMA(())],
    )
    return pl.pallas_call(
        functools.partial(_dispatch_kernel, n_prompt_tiles=npt),
        out_shape=jax.ShapeDtypeStruct((n_rows, width), jnp.uint32),
        grid_spec=grid_spec,
        compiler_params=_params(("arbitrary",), has_side_effects=True),
        name="dispatch",
    )(run_row, pad_start, pad_runs, tail_tiles, route_t, h_p, h_s)


def _experts_kernel(te_ref, tf_ref, tu_ref, x_ref, w1_ref, b1g_ref, b1l_ref, w2_ref, b2_ref, perm_ref,
                    y_ref, w1s_ref, w2s_ref):
    i = pl.program_id(0)
    dff = w2_ref.shape[1]

    @pl.when(tf_ref[i] == 1)
    def _():
        perm = perm_ref[...]
        half = MXU_DIM // 2
        for c in range(w1_ref.shape[2] // MXU_DIM):
            blk = w1_ref[0, :, c * MXU_DIM:(c + 1) * MXU_DIM].astype(BF16)
            sp = _dot(blk, perm).astype(BF16)
            w1s_ref[:, c * half:(c + 1) * half] = sp[:, :half]
            w1s_ref[:, dff + c * half:dff + (c + 1) * half] = sp[:, half:]
        w2s_ref[...] = w2_ref[0].astype(BF16)

    @pl.when(tu_ref[i] == 1)
    def _():
        d = w2_ref.shape[2]
        x = _unpack_bf16_pairs(x_ref[:, :d // 2])
        gate = pltpu.bitcast(x_ref[:, d // 2:], F32)
        hdn = _dot(x, w1s_ref[...])
        glu = jnp.minimum(hdn[:, :dff] + b1g_ref[0], SWIGLU_LIMIT)
        lin = jnp.clip(hdn[:, dff:] + b1l_ref[0], -SWIGLU_LIMIT, SWIGLU_LIMIT)
        act = glu * _sigmoid(SWIGLU_ALPHA * glu) * (lin + 1.0)
        y = (_dot(act.astype(BF16), w2s_ref[...]) + b2_ref[0]) * jnp.concatenate([gate] * (d // LANES), axis=1)
        y_ref[...] = _pack_hi_lo(y)

    @pl.when(tu_ref[i] == 0)
    def _():
        y_ref[...] = jnp.zeros_like(y_ref)


def _deinterleave_perm():
    j = lax.broadcasted_iota(jnp.int32, (MXU_DIM, MXU_DIM), 0)
    c = lax.broadcasted_iota(jnp.int32, (MXU_DIM, MXU_DIM), 1)
    half = MXU_DIM // 2
    src = jnp.where(c < half, 2 * c, 2 * (c - half) + 1)
    return (j == src).astype(BF16)


def _experts(xs, tile_expert, tile_first, tile_used, w1, b1g, b1l, w2, b2):
    width = xs.shape[1]
    n_tiles = tile_expert.shape[0]
    n_rows = n_tiles * TM_E
    ne, d, two_f = w1.shape
    dff = two_f // 2
    assert width == d // 2 + LANES
    grid_spec = pltpu.PrefetchScalarGridSpec(
        num_scalar_prefetch=3,
        grid=(n_tiles,),
        in_specs=[pl.BlockSpec((TM_E, width), lambda i, te, tf, tu: (i, 0)),
                  pl.BlockSpec((1, d, two_f), lambda i, te, tf, tu: (te[i], 0, 0)),
                  pl.BlockSpec((1, 1, dff), lambda i, te, tf, tu: (te[i], 0, 0)),
                  pl.BlockSpec((1, 1, dff), lambda i, te, tf, tu: (te[i], 0, 0)),
                  pl.BlockSpec((1, dff, d), lambda i, te, tf, tu: (te[i], 0, 0)),
                  pl.BlockSpec((1, 1, d), lambda i, te, tf, tu: (te[i], 0, 0)),
                  pl.BlockSpec((MXU_DIM, MXU_DIM), lambda i, te, tf, tu: (0, 0))],
        out_specs=pl.BlockSpec((TM_E, d), lambda i, te, tf, tu: (i, 0)),
        scratch_shapes=[pltpu.VMEM((d, two_f), BF16), pltpu.VMEM((dff, d), BF16)],
    )
    return pl.pallas_call(
        _experts_kernel,
        out_shape=jax.ShapeDtypeStruct((n_rows, d), jnp.uint32),
        grid_spec=grid_spec,
        compiler_params=_params(("arbitrary",)),
        name="experts",
    )(tile_expert, tile_first, tile_used, xs, w1, b1g, b1l, w2, b2, _deinterleave_perm())


def _combine_kernel(run_row_ref, x1_ref, rt_ref, g2_ref, y_ref, o_ref, stage_ref, sem, *, tile_offset):
    i = pl.program_id(0)
    buf = i % 2

    def start_runs(step, b):
        _for_each_run(step + tile_offset, run_row_ref, lambda slot, row: pltpu.make_async_copy(
            y_ref.at[pl.ds(row, RUN_ROWS), :], stage_ref.at[b, pl.ds(slot, RUN_ROWS), :], sem.at[b]).start())

    def wait_runs(step, b):
        _for_each_run(step + tile_offset, run_row_ref, lambda slot, row: pltpu.make_async_copy(
            y_ref.at[pl.ds(row, RUN_ROWS), :], stage_ref.at[b, pl.ds(slot, RUN_ROWS), :], sem.at[b]).wait())

    @pl.when(i == 0)
    def _():
        stage_ref[...] = jnp.zeros_like(stage_ref)
        start_runs(0, 0)

    @pl.when(i + 1 < pl.num_programs(0))
    def _():
        start_runs(i + 1, 1 - buf)

    wait_runs(i, buf)

    rt = rt_ref[...]
    tt = rt.shape[0]
    slot_id = lax.broadcasted_iota(jnp.int32, (tt, stage_ref.shape[1]), 1).astype(F32)
    sel = jnp.zeros(slot_id.shape, F32)
    for k in range(TOP_K):
        sel = jnp.where(slot_id == rt[:, _R_SLOT + k:_R_SLOT + k + 1], 1.0, sel)
    sel = sel.astype(BF16)
    hi, lo = _unpack_hi_lo(stage_ref[buf])
    o_ref[...] = x1_ref[...] + g2_ref[...] * (_dot(sel, hi) + _dot(sel, lo))


def _combine(x1, route, run_row, g2, seq, y_rows, row_offset):
    r, d = x1.shape
    assert r % TT == 0 and row_offset % TT == 0
    off = row_offset // TT
    g2_o, rpm = _mod_operand(g2, seq, TT)
    if rpm is None:
        g2_spec = pl.BlockSpec((TT, d), lambda i, *_: (i, 0))
    else:
        g2_spec = pl.BlockSpec((None, 1, d), lambda i, *_: ((i * TT) // rpm, 0, 0))
    grid_spec = pltpu.PrefetchScalarGridSpec(
        num_scalar_prefetch=1,
        grid=(r // TT,),
        in_specs=[pl.BlockSpec((TT, d), lambda i, *_: (i, 0)),
                  pl.BlockSpec((TT, LANES), lambda i, *_: (i + off, 0)),
                  g2_spec,
                  pl.BlockSpec(memory_space=pl.ANY)],
        out_specs=pl.BlockSpec((TT, d), lambda i, *_: (i, 0)),
        scratch_shapes=[pltpu.VMEM((2, N_SLOTS, d), jnp.uint32), pltpu.SemaphoreType.DMA((2,))],
    )
    return pl.pallas_call(
        functools.partial(_combine_kernel, tile_offset=off),
        out_shape=jax.ShapeDtypeStruct((r, d), F32),
        grid_spec=grid_spec,
        compiler_params=_params(("arbitrary",)),
        name="combine",
    )(run_row, x1, route, g2_o, y_rows)


def _gmlp_tables(w_s, b_s, seq_s):
    g = w_s.shape[0]
    tril = jnp.tril(w_s)
    short = jnp.tril(w_s[:, :seq_s, :seq_s])
    rep = CHUNK_B // seq_s
    blockdiag = jnp.zeros_like(w_s)
    for r in range(rep):
        blockdiag = blockdiag.at[:, r * seq_s:(r + 1) * seq_s, r * seq_s:(r + 1) * seq_s].set(short)
    ws_all = jnp.stack([tril, blockdiag]).astype(BF16)
    bs_full = jnp.repeat(b_s.T, HEAD_DIM, axis=1)
    bs_short = jnp.tile(jnp.repeat(b_s[:, :seq_s].T, HEAD_DIM, axis=1), (rep, 1))
    return ws_all, jnp.stack([bs_full, bs_short])


def _layer(xp, xs, cache_k, cache_v, cp, cs, p):
    bp, tp, d = xp.shape
    bs_, ts, _ = xs.shape
    assert tp % CHUNK_B == 0 and CHUNK_B % ts == 0
    rp, rs = bp * tp, bs_ * ts
    n_tok = rp + rs

    mod = _ada(jnp.concatenate([cp, cs], axis=0), p['w_ada'], p['b_ada'])
    mods = [mod[:, j * d:(j + 1) * d] for j in range(6)]
    mp = [m[:bp] for m in mods]
    msm = [m[bp:] for m in mods]

    tile_h = lambda g: jnp.tile(g, N_HEADS).reshape(1, D_HEADS)
    bd = jnp.kron(jnp.eye(MXU_DIM // HEAD_DIM, dtype=F32),
                  jnp.full((HEAD_DIM, HEAD_DIM), 1.0 / HEAD_DIM, F32)).astype(BF16)
    w_in_b = p['w_in'].astype(BF16)
    n1g = p['norm1_g'].reshape(1, d)
    common = (n1g, w_in_b, tile_h(p['q_norm_g']), tile_h(p['k_norm_g']), tile_h(p['v_norm_g']), bd)
    q_p, k_p, kb_p, v_p, vb_p, u_p, vn_p, sga_p, sgb_p = _inproj(
        xp.reshape(rp, d), mp[0], mp[1], tp, *common, want_vn32=False, kv_transposed=True)
    q_s, k_s, kb_s, v_s, vb_s, u_s, vn_s, sga_s, sgb_s, vn32_s = _inproj(
        xs.reshape(rs, d), msm[0], msm[1], ts, *common, want_vn32=True, kv_transposed=False)

    ya_p = _attn_prompt(q_p, kb_p, vb_p, bp, tp)
    ya_s = _attn_sample(q_s, kb_s, vb_s, cache_k, cache_v, bs_, ts)

    ws_all, bs_all = _gmlp_tables(p['w_s'], p['b_s'], ts)
    wr = p['w_router']
    wr_hi = wr.astype(BF16)
    wr_lo = (wr - wr_hi.astype(F32)).astype(BF16)
    post_w = (p['w_pa'].astype(BF16), p['w_pb'].astype(BF16), p['w_o'].astype(BF16), p['norm2_g'].reshape(1, d),
              jnp.concatenate([wr_hi, wr_lo], axis=1), p['b_router'].reshape(1, N_EXPERTS))
    x1_p, h2_p, lg_p = _post(xp.reshape(rp, d), ya_p, u_p, vn_p, sga_p, sgb_p, mp[2], mp[3], mp[4], tp,
                             ws_all, bs_all, 0, *post_w)
    x1_s, h2_s, lg_s = _post(xs.reshape(rs, d), ya_s, u_s, vn_s, sga_s, sgb_s, msm[2], msm[3], msm[4], ts,
                             ws_all, bs_all, 1, *post_w)

    route, route_t, table, counts = _route(lg_p, lg_s)

    counts = counts[0].astype(jnp.int32)
    padded = (counts + TM_E - 1) // TM_E * TM_E
    pend = jnp.cumsum(padded)
    pstart = pend - padded
    table = table.reshape(-1, 8, LANES)[:, :, :N_EXPERTS].astype(jnp.int32)
    n_token_tiles = n_tok // TT
    runs = (table[:, _T_COUNT] + RUN_ROWS - 1) // RUN_ROWS
    runs_end = jnp.cumsum(runs, axis=1)
    j = jnp.arange(RUNS_PER_TILE, dtype=jnp.int32)
    e_of = jnp.minimum(jnp.sum((runs_end[:, None, :] <= j[None, :, None]).astype(jnp.int32), axis=2), N_EXPERTS - 1)
    pick = (e_of[:, :, None] == jnp.arange(N_EXPERTS, dtype=jnp.int32)[None, None, :]).astype(jnp.int32)
    first_run = jnp.sum(pick * (runs_end - runs)[:, None, :], axis=2)
    first_row = jnp.sum(pick * (table[:, _T_BASE] + pstart[None, :])[:, None, :], axis=2)
    run_row = first_row + (j[None, :] - first_run) * RUN_ROWS
    is_run = j[None, :] < runs_end[:, -1:]
    n_tiles = -(-(n_tok * TOP_K + n_token_tiles * N_EXPERTS * (RUN_ROWS - 1)) // TM_E) + N_EXPERTS
    tile_start = jnp.arange(n_tiles, dtype=jnp.int32) * TM_E
    tile_used = (tile_start < pend[-1]).astype(jnp.int32)
    last_expert = jnp.sum((pend <= pend[-1] - 1).astype(jnp.int32))
    tile_expert = jnp.minimum(jnp.sum((pend[None, :] <= tile_start[:, None]).astype(jnp.int32), axis=1), last_expert)
    tile_first = jnp.concatenate([jnp.ones((1,), jnp.int32),
                                  (tile_expert[1:] != tile_expert[:-1]).astype(jnp.int32)])

    spare_tiles = -(-2 * N_SLOTS // TM_E)
    spare_rows = j[None, :] * RUN_ROWS
    parity = (jnp.arange(n_token_tiles, dtype=jnp.int32) % 2)[:, None]
    rows_out = jnp.where(is_run, run_row, n_tiles * TM_E + parity * N_SLOTS + spare_rows).reshape(-1)
    rows_in = jnp.where(is_run, run_row, spare_rows).reshape(-1)
    tail_tiles = jnp.stack([pend[-1] // TM_E, jnp.int32(n_tiles + spare_tiles)]).astype(jnp.int32)
    pad_runs = (padded - counts) // RUN_ROWS
    xs_rows = _dispatch(h2_p, h2_s, route_t, rows_out, pstart + counts, pad_runs, tail_tiles,
                        (n_tiles + spare_tiles) * TM_E)
    b1 = p['b1']
    ne = b1.shape[0]
    y_rows = _experts(xs_rows, tile_expert, tile_first, tile_used, p['w1'],
                      b1[:, 0::2].reshape(ne, 1, -1), b1[:, 1::2].reshape(ne, 1, -1),
                      p['w2'], p['b2'].reshape(ne, 1, d))
    out_p = _combine(x1_p, route, rows_in, mp[5], tp, y_rows, 0)
    out_s = _combine(x1_s, route, rows_in, msm[5], ts, y_rows, rp)

    shp = lambda a, b, t: a.reshape(b, t, N_HEADS, HEAD_DIM)
    unt = lambda a: jnp.transpose(a.reshape(bp, N_HEADS, HEAD_DIM, tp), (0, 3, 1, 2))
    return (out_p.reshape(bp, tp, d), out_s.reshape(bs_, ts, d),
            unt(k_p), unt(v_p), shp(k_s, bs_, ts), shp(v_s, bs_, ts), shp(vn32_s, bs_, ts))


def kernel(x_prompt, x_sample, cache_k, cache_v, c_prompt, c_sample, norm1_g, norm2_g, w_ada, b_ada, w_in,
           q_norm_g, k_norm_g, v_norm_g, w_s, b_s, w_pa, w_pb, w_o, w_router, b_router, w1, b1, w2, b2):
    depth = w_in.shape[0]
    xp, xs = x_prompt, x_sample
    kp, vp, ks, vs, vbs = [], [], [], [], []
    for l in range(depth):
        p = {'norm1_g': norm1_g[l], 'norm2_g': norm2_g[l], 'w_ada': w_ada[l], 'b_ada': b_ada[l],
             'w_in': w_in[l], 'q_norm_g': q_norm_g[l], 'k_norm_g': k_norm_g[l], 'v_norm_g': v_norm_g[l],
             'w_s': w_s[l], 'b_s': b_s[l], 'w_pa': w_pa[l], 'w_pb': w_pb[l], 'w_o': w_o[l],
             'w_router': w_router[l], 'b_router': b_router[l], 'w1': w1[l], 'b1': b1[l],
             'w2': w2[l], 'b2': b2[l]}
        xp, xs, k_p, v_p, k_s, v_s, vn_s = _layer(xp, xs, cache_k[l], cache_v[l], c_prompt, c_sample, p)
        kp.append(k_p); vp.append(v_p); ks.append(k_s); vs.append(v_s); vbs.append(vn_s)
    return (xp, xs, jnp.stack(kp), jnp.stack(vp), jnp.stack(ks), jnp.stack(vs), jnp.stack(vbs))
```

```python
import functools

import jax
import jax.numpy as jnp
from jax import lax
from jax.experimental import pallas as pl
from jax.experimental.pallas import tpu as pltpu

F32 = jnp.float32
BF16 = jnp.bfloat16

LANES = 128
MXU_DIM = 256
VMEM_LIMIT = 56 * 1024 * 1024

HEAD_DIM = 64
N_HEADS = 8
D_HEADS = N_HEADS * HEAD_DIM
PAIRS = D_HEADS // LANES
CHUNK_B = 128
N_EXPERTS = 32
TOP_K = 4
RMS_EPS = 1e-6
SWIGLU_ALPHA = 1.702
SWIGLU_LIMIT = 7.0

TM_IN = 512
TQ = 512
TK = 256
SUB_Q = 128
LOOK_BACK = 256
WINDOW = SUB_Q + LOOK_BACK
TM_POST = 256
TT = 256
TM_E = 512
RUN_ROWS = 8
N_SLOTS = TT * TOP_K + N_EXPERTS * RUN_ROWS


def _params(sem, **kw):
    return pltpu.CompilerParams(dimension_semantics=sem, vmem_limit_bytes=VMEM_LIMIT, **kw)


def _dot(a, b):
    return jnp.dot(a, b, preferred_element_type=F32)


def _dot_nt(a, b):
    return lax.dot_general(a, b, (((1,), (1,)), ((), ())), preferred_element_type=F32)


def _gelu(x):
    return jax.nn.gelu(x)


def _sigmoid(x):
    return 1.0 / (1.0 + jnp.exp(-x))


def _ada_kernel(c_ref, w_ref, b_ref, o_ref):
    c = c_ref[...]
    s = c * _sigmoid(c)
    o_ref[...] = jnp.dot(s, w_ref[...], preferred_element_type=F32,
                         precision=lax.Precision.HIGHEST) + b_ref[...]


def _ada(c, w_ada, b_ada):
    nb, d = c.shape
    n = w_ada.shape[1]
    tn = 1024
    return pl.pallas_call(
        _ada_kernel,
        out_shape=jax.ShapeDtypeStruct((nb, n), F32),
        grid=(n // tn,),
        in_specs=[pl.BlockSpec((nb, d), lambda j: (0, 0)),
                  pl.BlockSpec((d, tn), lambda j: (0, j)),
                  pl.BlockSpec((1, tn), lambda j: (0, j))],
        out_specs=pl.BlockSpec((nb, tn), lambda j: (0, j)),
        compiler_params=_params(("parallel",)),
        name="ada",
    )(c, w_ada, b_ada.reshape(1, n))


def _head_mean_sq(p, bd):
    sq = (p * p).astype(BF16)
    return jnp.concatenate([_dot(sq[:, :MXU_DIM], bd), _dot(sq[:, MXU_DIM:], bd)], axis=-1)


def _inproj_kernel(x_ref, sh_ref, sc_ref, g_ref, w_ref, qg_ref, kg_ref, vg_ref, bd_ref,
                   q_ref, k32_ref, kb_ref, v32_ref, vb_ref, u_ref, vnb_ref, sga_ref, sgb_ref, *vn32_ref,
                   kv_transposed):
    kv_out = (lambda a: a.T) if kv_transposed else (lambda a: a)
    x = x_ref[...]
    ms = jnp.mean(x * x, axis=-1, keepdims=True)
    xn = x * lax.rsqrt(ms + RMS_EPS) * g_ref[...]
    h = (xn * (1.0 + sc_ref[...]) + sh_ref[...]).astype(BF16)
    bd = bd_ref[...]
    d = x.shape[1]

    def sec(a, n):
        return _dot(h, w_ref[:, a:a + n])

    o = 0
    pq = sec(o, D_HEADS); o += D_HEADS
    q = pq * lax.rsqrt(_head_mean_sq(pq, bd) + RMS_EPS) * qg_ref[...]
    q_ref[...] = (q * (HEAD_DIM ** -0.5)).astype(BF16)
    pk = sec(o, D_HEADS); o += D_HEADS
    k = pk * lax.rsqrt(_head_mean_sq(pk, bd) + RMS_EPS) * kg_ref[...]
    k32_ref[...] = kv_out(k)
    kb_ref[...] = k.astype(BF16)
    pv = sec(o, D_HEADS); o += D_HEADS
    v32_ref[...] = kv_out(pv)
    vb_ref[...] = pv.astype(BF16)
    pu = sec(o, D_HEADS); o += D_HEADS
    u_ref[...] = _gelu(pu).astype(BF16)
    pvb = _gelu(sec(o, D_HEADS)); o += D_HEADS
    vn = pvb * lax.rsqrt(_head_mean_sq(pvb, bd) + RMS_EPS) * vg_ref[...]
    vnb_ref[...] = vn.astype(BF16)
    if vn32_ref:
        vn32_ref[0][...] = vn
    sga_ref[...] = _sigmoid(sec(o, d)).astype(BF16); o += d
    sgb_ref[...] = _sigmoid(sec(o, d)).astype(BF16)


def _mod_spec(tm, arr, rows_per_mod):
    if rows_per_mod is None:
        return pl.BlockSpec((tm, arr.shape[-1]), lambda i: (i, 0))
    return pl.BlockSpec((None, 1, arr.shape[-1]), lambda i: ((i * tm) // rows_per_mod, 0, 0))


def _mod_operand(m, seq, tm):
    if seq % tm == 0:
        return m[:, None, :], seq
    return jnp.repeat(m, seq, axis=0), None


def _inproj(x, sh, sc, seq, g, w_in_b, qg, kg, vg, bd, want_vn32, kv_transposed):
    r, d = x.shape
    tm = min(TM_IN, r)
    assert r % tm == 0 and (not kv_transposed or seq % tm == 0)
    sh_o, rpm = _mod_operand(sh, seq, tm)
    sc_o, _ = _mod_operand(sc, seq, tm)
    ncol = w_in_b.shape[1]
    const = lambda i: (0, 0)
    row = lambda n: pl.BlockSpec((tm, n), lambda i: (i, 0))
    outs = [jax.ShapeDtypeStruct((r, D_HEADS), BF16),
            jax.ShapeDtypeStruct((r, D_HEADS), F32),
            jax.ShapeDtypeStruct((r, D_HEADS), BF16),
            jax.ShapeDtypeStruct((r, D_HEADS), F32),
            jax.ShapeDtypeStruct((r, D_HEADS), BF16),
            jax.ShapeDtypeStruct((r, D_HEADS), BF16),
            jax.ShapeDtypeStruct((r, D_HEADS), BF16),
            jax.ShapeDtypeStruct((r, d), BF16),
            jax.ShapeDtypeStruct((r, d), BF16)]
    out_specs = [row(D_HEADS)] * 7 + [row(d), row(d)]
    if kv_transposed:
        per_seq = seq // tm
        for j in (1, 3):
            outs[j] = jax.ShapeDtypeStruct((r // seq, D_HEADS, seq), F32)
            out_specs[j] = pl.BlockSpec((None, D_HEADS, tm), lambda i: (i // per_seq, 0, i % per_seq))
    if want_vn32:
        outs.append(jax.ShapeDtypeStruct((r, D_HEADS), F32))
        out_specs.append(row(D_HEADS))
    return pl.pallas_call(
        functools.partial(_inproj_kernel, kv_transposed=kv_transposed),
        out_shape=outs,
        grid=(r // tm,),
        in_specs=[row(d), _mod_spec(tm, sh_o, rpm), _mod_spec(tm, sc_o, rpm),
                  pl.BlockSpec((1, d), const),
                  pl.BlockSpec((d, ncol), const, pipeline_mode=pl.Buffered(1)),
                  pl.BlockSpec((1, D_HEADS), const), pl.BlockSpec((1, D_HEADS), const),
                  pl.BlockSpec((1, D_HEADS), const), pl.BlockSpec((MXU_DIM, MXU_DIM), const)],
        out_specs=out_specs,
        compiler_params=_params(("parallel",)),
        name="inproj",
    )(x, sh_o, sc_o, g, w_in_b, qg, kg, vg, bd)


LOG_WEIGHT_CUTOFF = -104.0


def _sb_weights(z, u, c, mask):
    lom = -(jnp.maximum(z, 0.0) + jnp.log(1.0 + jnp.exp(-jnp.abs(z))))
    if mask is not None:
        lom = jnp.where(mask, lom, 0.0)
    hi = lom.astype(BF16)
    lo = (lom - hi.astype(F32)).astype(BF16)
    later = _dot(hi, u) + _dot(lo, u)
    w = jnp.exp(z + lom + later + c)
    if mask is not None:
        w = jnp.where(mask, w, 0.0)
    return w, c + jnp.sum(lom, axis=-1, keepdims=True)


def _pair_select(shape):
    return lax.broadcasted_iota(jnp.int32, shape, 1) < HEAD_DIM


def _causal_mask(m, tq, tk):
    row = lax.broadcasted_iota(jnp.int32, (m, tk), 0) & (tq - 1)
    return lax.broadcasted_iota(jnp.int32, (m, tk), 1) < row


def _stack_heads(q):
    first = _pair_select(q.shape)
    zero = jnp.zeros_like(q)
    return jnp.concatenate([jnp.where(first, q, zero), jnp.where(first, zero, q)], axis=0)


def _attn_prompt_kernel(q_ref, k_ref, v_ref, uw_ref, u_ref, o_ref, acc_ref, c_ref):
    i = pl.program_id(2)
    tq = q_ref.shape[0]
    nsub = tq // SUB_Q
    sub_rows = 2 * SUB_Q
    q = q_ref[...]
    u = u_ref[...]
    first = _pair_select((SUB_Q, LANES))
    m = nsub * sub_rows

    def pv_sub(wb, s, vs):
        w0 = wb[s * sub_rows:s * sub_rows + SUB_Q]
        w1 = wb[s * sub_rows + SUB_Q:(s + 1) * sub_rows]
        return jnp.where(first, _dot(w0, vs), _dot(w1, vs))

    qs = [_stack_heads(q[s * SUB_Q:(s + 1) * SUB_Q]) for s in range(nsub)]
    starts = [pl.multiple_of(jnp.maximum(i * tq + s * SUB_Q - LOOK_BACK, 0), SUB_Q) for s in range(nsub)]
    row = lax.broadcasted_iota(jnp.int32, (m, 1), 0)
    window_start = jnp.zeros((m, 1), jnp.int32) + starts[0]
    for s in range(1, nsub):
        window_start = jnp.where(row >= s * sub_rows, starts[s], window_start)
    q_pos = i * tq + (row // sub_rows) * SUB_Q + (row & (SUB_Q - 1))

    z = jnp.concatenate([_dot_nt(qs[s], k_ref[pl.ds(starts[s], WINDOW), :]) for s in range(nsub)], axis=0)
    col = lax.broadcasted_iota(jnp.int32, (m, WINDOW), 1)
    w, c = _sb_weights(z, uw_ref[...], jnp.zeros((m, 1), F32), window_start + col < q_pos)
    wb = w.astype(BF16)
    acc_ref[...] = jnp.concatenate([pv_sub(wb, s, v_ref[pl.ds(starts[s], WINDOW), :]) for s in range(nsub)], axis=0)
    c_ref[...] = c
    q_all = jnp.concatenate(qs, axis=0)
    col = lax.broadcasted_iota(jnp.int32, (m, TK), 1)

    def cond(state):
        j, cmax = state
        return jnp.logical_and(j >= 0, cmax > LOG_WEIGHT_CUTOFF)

    def body(state):
        j, _ = state
        start = pl.multiple_of(j * TK, TK)
        vs = v_ref[pl.ds(start, TK), :]
        w, c = _sb_weights(_dot_nt(q_all, k_ref[pl.ds(start, TK), :]), u, c_ref[...],
                           j * TK + col < window_start)
        wb = w.astype(BF16)
        acc_ref[...] += jnp.concatenate([pv_sub(wb, s, vs) for s in range(nsub)], axis=0)
        c_ref[...] = c
        return j - 1, jnp.max(c)

    j0 = lax.div(starts[nsub - 1] + (TK - 1), TK) - 1
    lax.while_loop(cond, body, (j0, jnp.max(c)))
    o_ref[...] = acc_ref[...].astype(o_ref.dtype)


def _upper_ones(n):
    return (lax.broadcasted_iota(jnp.int32, (n, n), 0) > lax.broadcasted_iota(jnp.int32, (n, n), 1)).astype(BF16)


def _attn_prompt(q, kb, vb, batch, seq):
    assert seq % TQ == 0 and TQ % TK == 0 and TQ % SUB_Q == 0 and seq >= WINDOW
    nq = seq // TQ
    return pl.pallas_call(
        _attn_prompt_kernel,
        out_shape=jax.ShapeDtypeStruct(q.shape, BF16),
        grid=(batch, PAIRS, nq),
        in_specs=[pl.BlockSpec((TQ, LANES), lambda b, p, i: (b * nq + i, p)),
                  pl.BlockSpec((seq, LANES), lambda b, p, i: (b, p)),
                  pl.BlockSpec((seq, LANES), lambda b, p, i: (b, p)),
                  pl.BlockSpec((WINDOW, WINDOW), lambda b, p, i: (0, 0)),
                  pl.BlockSpec((TK, TK), lambda b, p, i: (0, 0))],
        out_specs=pl.BlockSpec((TQ, LANES), lambda b, p, i: (b * nq + i, p)),
        scratch_shapes=[pltpu.VMEM((TQ, LANES), F32), pltpu.VMEM((2 * TQ, 1), F32)],
        compiler_params=_params(("parallel", "parallel", "arbitrary")),
        name="attn_prompt",
    )(q, kb, vb, _upper_ones(WINDOW), _upper_ones(TK))


def _attn_sample_kernel(q_ref, kn_ref, vn_ref, kc_ref, vc_ref, un_ref, u_ref, o_ref,
                        kbuf_ref, vbuf_ref, sem, acc_ref, c_ref):
    b = pl.program_id(0)
    tq = q_ref.shape[0]
    nblk = kc_ref.shape[3] // TK
    head = lambda a, h: a[:, h * HEAD_DIM:(h + 1) * HEAD_DIM]
    q = q_ref[...]
    qh = [head(q, h) for h in range(N_HEADS)]

    def copies(j, slot):
        rows = pl.ds(pl.multiple_of(j * TK, TK), TK)
        return (pltpu.make_async_copy(kc_ref.at[b, :, :, rows], kbuf_ref.at[slot], sem.at[0, slot]),
                pltpu.make_async_copy(vc_ref.at[b, :, :, rows], vbuf_ref.at[slot], sem.at[1, slot]))

    def start(j, slot):
        for cp in copies(j, slot):
            cp.start()

    def wait(j, slot):
        for cp in copies(j, slot):
            cp.wait()

    def sweep(score, apply, u, c, mask):
        z = jnp.concatenate([score(qh[h], h) for h in range(N_HEADS)], axis=0)
        w, c = _sb_weights(z, u, c, mask)
        wb = w.astype(BF16)
        pv = jnp.concatenate([apply(wb[h * tq:(h + 1) * tq], h) for h in range(N_HEADS)], axis=-1)
        return pv, c

    start(nblk - 1, 0)
    kn = kn_ref[...]
    vn = vn_ref[...]
    pv, c = sweep(lambda qq, h: _dot_nt(qq, head(kn, h)), lambda ww, h: _dot(ww, head(vn, h)), un_ref[...],
                  jnp.zeros((N_HEADS * tq, 1), F32), _causal_mask(N_HEADS * tq, tq, tq))
    acc_ref[...] = pv
    c_ref[...] = c
    u = u_ref[...]

    def cond(state):
        j, _, cmax = state
        return jnp.logical_and(j >= 0, cmax > LOG_WEIGHT_CUTOFF)

    def body(state):
        j, slot, _ = state
        wait(j, slot)

        @pl.when(j > 0)
        def _():
            start(j - 1, 1 - slot)

        pv, c = sweep(lambda qq, h: _dot(qq, kbuf_ref[slot, h].astype(BF16)),
                      lambda ww, h: _dot_nt(ww, vbuf_ref[slot, h].astype(BF16)), u, c_ref[...], None)
        acc_ref[...] += pv
        c_ref[...] = c
        return j - 1, 1 - slot, jnp.max(c)

    j, slot, _ = lax.while_loop(cond, body, (nblk - 1, 0, jnp.max(c)))

    @pl.when(j >= 0)
    def _():
        wait(j, slot)

    o_ref[...] = acc_ref[...].astype(o_ref.dtype)


def _attn_sample(q, kb, vb, cache_k, cache_v, batch, seq):
    past = cache_k.shape[1]
    assert past % TK == 0 and past >= TK
    cache_k = jnp.transpose(cache_k, (0, 2, 3, 1))
    cache_v = jnp.transpose(cache_v, (0, 2, 3, 1))
    row = pl.BlockSpec((seq, D_HEADS), lambda b: (b, 0))
    return pl.pallas_call(
        _attn_sample_kernel,
        out_shape=jax.ShapeDtypeStruct(q.shape, BF16),
        grid=(batch,),
        in_specs=[row, row, row,
                  pl.BlockSpec(memory_space=pl.ANY), pl.BlockSpec(memory_space=pl.ANY),
                  pl.BlockSpec((seq, seq), lambda b: (0, 0)),
                  pl.BlockSpec((TK, TK), lambda b: (0, 0))],
        out_specs=row,
        scratch_shapes=[pltpu.VMEM((2, N_HEADS, HEAD_DIM, TK), F32), pltpu.VMEM((2, N_HEADS, HEAD_DIM, TK), F32),
                        pltpu.SemaphoreType.DMA((2, 2)),
                        pltpu.VMEM((seq, D_HEADS), F32), pltpu.VMEM((N_HEADS * seq, 1), F32)],
        compiler_params=_params(("arbitrary",)),
        name="attn_sample",
    )(q, kb, vb, cache_k, cache_v, _upper_ones(seq), _upper_ones(TK))


def _pack_bf16_pairs(x):
    half = x.shape[1] // 2
    a = pltpu.bitcast(x[:, :half], jnp.uint32)
    b = pltpu.bitcast(x[:, half:], jnp.uint32)
    return a | (b >> 16)


def _unpack_bf16_pairs(p):
    a = pltpu.bitcast(p & jnp.uint32(0xFFFF0000), F32).astype(BF16)
    b = pltpu.bitcast(p << 16, F32).astype(BF16)
    return jnp.concatenate([a, b], axis=1)


def _post_kernel(x_ref, ya_ref, u_ref, vn_ref, sga_ref, sgb_ref, g1_ref, sh2_ref, sc2_ref,
                 ws_ref, bs_ref, wpa_ref, wpb_ref, wo_ref, n2g_ref, wr_ref, br_ref,
                 x1_ref, h2_ref, lg_ref):
    tm = x_ref.shape[0]
    first = _pair_select((CHUNK_B, LANES))
    chunks = []
    for c in range(tm // CHUNK_B):
        rows = slice(c * CHUNK_B, (c + 1) * CHUNK_B)
        parts = []
        for p in range(PAIRS):
            vnp = vn_ref[rows, p * LANES:(p + 1) * LANES]
            parts.append(jnp.where(first, _dot(ws_ref[0, 2 * p], vnp), _dot(ws_ref[0, 2 * p + 1], vnp)))
        s = jnp.concatenate(parts, axis=-1) + bs_ref[0]
        chunks.append((u_ref[rows, :].astype(F32) * s).astype(BF16))
    yb = jnp.concatenate(chunks, axis=0)
    merged = (sga_ref[...].astype(F32) * _dot(ya_ref[...], wpa_ref[...])
              + sgb_ref[...].astype(F32) * _dot(yb, wpb_ref[...]))
    x1 = x_ref[...] + g1_ref[...] * _dot(merged.astype(BF16), wo_ref[...])
    x1_ref[...] = x1
    ms = jnp.mean(x1 * x1, axis=-1, keepdims=True)
    h2 = (x1 * lax.rsqrt(ms + RMS_EPS) * n2g_ref[...]) * (1.0 + sc2_ref[...]) + sh2_ref[...]
    hi = h2.astype(BF16)
    hf = hi.astype(F32)
    lo = (h2 - hf).astype(BF16)
    h2_ref[...] = _pack_bf16_pairs(hf)
    l2 = _dot(hi, wr_ref[...]) + _dot(lo, wr_ref[...])
    lg_ref[...] = l2[:, :N_EXPERTS] + l2[:, N_EXPERTS:] + br_ref[...]


def _post(x, ya, u, vn, sga, sgb, g1, sh2, sc2, seq, ws_all, bs_all, ws_idx, wpa, wpb, wo, n2g, wr, br):
    r, d = x.shape
    tm = min(TM_POST, r)
    assert r % tm == 0 and tm % CHUNK_B == 0
    g1_o, rpm = _mod_operand(g1, seq, tm)
    sh2_o, _ = _mod_operand(sh2, seq, tm)
    sc2_o, _ = _mod_operand(sc2, seq, tm)
    const = lambda i: (0, 0)
    row = lambda n: pl.BlockSpec((tm, n), lambda i: (i, 0))
    return pl.pallas_call(
        _post_kernel,
        out_shape=[jax.ShapeDtypeStruct((r, d), F32), jax.ShapeDtypeStruct((r, d // 2), jnp.uint32),
                   jax.ShapeDtypeStruct((r, N_EXPERTS), F32)],
        grid=(r // tm,),
        in_specs=[row(d), row(D_HEADS), row(D_HEADS), row(D_HEADS), row(d), row(d),
                  _mod_spec(tm, g1_o, rpm), _mod_spec(tm, sh2_o, rpm), _mod_spec(tm, sc2_o, rpm),
                  pl.BlockSpec((1, N_HEADS, CHUNK_B, CHUNK_B), lambda i: (ws_idx, 0, 0, 0)),
                  pl.BlockSpec((1, CHUNK_B, D_HEADS), lambda i: (ws_idx, 0, 0)),
                  pl.BlockSpec(wpa.shape, const), pl.BlockSpec(wpb.shape, const), pl.BlockSpec(wo.shape, const),
                  pl.BlockSpec((1, d), const), pl.BlockSpec(wr.shape, const), pl.BlockSpec((1, N_EXPERTS), const)],
        out_specs=[row(d), row(d // 2), row(N_EXPERTS)],
        compiler_params=_params(("parallel",)),
        name="post",
    )(x, ya, u, vn, sga, sgb, g1_o, sh2_o, sc2_o, ws_all, bs_all, wpa, wpb, wo, n2g, wr, br)


_R_IDX, _R_GATE, _R_SLOT = 0, TOP_K, 2 * TOP_K
_T_COUNT, _T_SLOT, _T_BASE = 0, 1, 2


def _route_kernel(lp_ref, ls_ref, tri_ref, scan_ref, o_ref, ot_ref, tbl_ref, cnt_ref, carry_ref, *, n_prompt_tiles):
    i = pl.program_id(0)

    @pl.when(i == 0)
    def _():
        carry_ref[...] = jnp.zeros_like(carry_ref)

    l = jnp.where(i < n_prompt_tiles, lp_ref[...], ls_ref[...])
    tt = l.shape[0]
    lane = lax.broadcasted_iota(jnp.int32, l.shape, 1).astype(F32)
    cur = l
    vals, idxs = [], []
    for _ in range(TOP_K):
        m = jnp.max(cur, axis=-1, keepdims=True)
        idx = jnp.min(jnp.where(cur == m, lane, float(N_EXPERTS)), axis=-1, keepdims=True)
        vals.append(m)
        idxs.append(idx)
        cur = jnp.where(lane == idx, -jnp.inf, cur)
    es = [jnp.exp(v - vals[0]) for v in vals]
    den = es[0] + es[1] + es[2] + es[3]
    onehot = jnp.zeros(l.shape, F32)
    for idx in idxs:
        onehot = onehot + (lane == idx).astype(F32)
    within = _dot(tri_ref[...], onehot.astype(BF16))
    count = jnp.sum(onehot, axis=0, keepdims=True)
    runs = jnp.ceil(count * (1.0 / RUN_ROWS))
    first_slot = RUN_ROWS * _dot(jnp.broadcast_to(runs, (8, N_EXPERTS)).astype(BF16), scan_ref[...])[0:1]
    slot_of = within + first_slot
    wide = lax.broadcasted_iota(jnp.int32, (tt, LANES), 1)
    out = jnp.zeros((tt, LANES), F32)
    for k in range(TOP_K):
        slot = jnp.sum(jnp.where(lane == idxs[k], slot_of, 0.0), axis=-1, keepdims=True)
        out = jnp.where(wide == _R_IDX + k, idxs[k], out)
        out = jnp.where(wide == _R_GATE + k, es[k] / den, out)
        out = jnp.where(wide == _R_SLOT + k, slot, out)
    o_ref[...] = out
    ot_ref[...] = out.T
    row = lax.broadcasted_iota(jnp.int32, (8, N_EXPERTS), 0)
    tbl = jnp.where(row == _T_COUNT, count, jnp.where(row == _T_SLOT, first_slot, carry_ref[...]))
    tbl_ref[...] = jnp.concatenate([tbl, jnp.zeros((8, LANES - N_EXPERTS), F32)], axis=1)
    carry_ref[...] += RUN_ROWS * runs
    cnt_ref[...] = carry_ref[...]


def _route(lg_p, lg_s):
    rp, rs = lg_p.shape[0], lg_s.shape[0]
    assert rp % TT == 0 and rs % TT == 0
    npt, nst = rp // TT, rs // TT
    nt = npt + nst
    iota = lambda shape, ax: lax.broadcasted_iota(jnp.int32, shape, ax)
    tri = (iota((TT, TT), 1) < iota((TT, TT), 0)).astype(BF16)
    scan = (iota((N_EXPERTS, N_EXPERTS), 0) < iota((N_EXPERTS, N_EXPERTS), 1)).astype(BF16)
    return pl.pallas_call(
        functools.partial(_route_kernel, n_prompt_tiles=npt),
        out_shape=[jax.ShapeDtypeStruct((rp + rs, LANES), F32), jax.ShapeDtypeStruct((LANES, rp + rs), F32),
                   jax.ShapeDtypeStruct((nt * 8, LANES), F32), jax.ShapeDtypeStruct((1, N_EXPERTS), F32)],
        grid=(nt,),
        in_specs=[pl.BlockSpec((TT, N_EXPERTS), lambda i: (jnp.minimum(i, npt - 1), 0)),
                  pl.BlockSpec((TT, N_EXPERTS), lambda i: (jnp.maximum(i - npt, 0), 0)),
                  pl.BlockSpec((TT, TT), lambda i: (0, 0)),
                  pl.BlockSpec((N_EXPERTS, N_EXPERTS), lambda i: (0, 0))],
        out_specs=[pl.BlockSpec((TT, LANES), lambda i: (i, 0)), pl.BlockSpec((LANES, TT), lambda i: (0, i)),
                   pl.BlockSpec((8, LANES), lambda i: (i, 0)), pl.BlockSpec((1, N_EXPERTS), lambda i: (0, 0))],
        scratch_shapes=[pltpu.VMEM((1, N_EXPERTS), F32)],
        compiler_params=_params(("arbitrary",)),
        name="route",
    )(lg_p, lg_s, tri, scan)


RUNS_PER_TILE = N_SLOTS // RUN_ROWS
RUN_UNROLL = 8


def _for_each_run(tile, run_row_ref, fn):
    def group(g, _):
        for u in range(RUN_UNROLL):
            j = g * RUN_UNROLL + u
            row = run_row_ref[tile * RUNS_PER_TILE + j]
            fn(pl.multiple_of(j * RUN_ROWS, RUN_ROWS), pl.multiple_of(row, RUN_ROWS))
        return 0

    lax.fori_loop(0, RUNS_PER_TILE // RUN_UNROLL, group, 0)


def _dispatch_kernel(run_row_ref, pad_start_ref, pad_runs_ref, tail_ref,
                     rt_ref, hp_ref, hs_ref, xs_ref, stage_ref, zero_ref, sem, psem, *, n_prompt_tiles):
    i = pl.program_id(0)
    last = pl.num_programs(0) - 1
    buf = i % 2

    def start_runs(tile, b):
        _for_each_run(tile, run_row_ref, lambda slot, row: pltpu.make_async_copy(
            stage_ref.at[b, pl.ds(slot, RUN_ROWS), :], xs_ref.at[pl.ds(row, RUN_ROWS), :], sem.at[b]).start())

    def wait_runs(tile, b):
        _for_each_run(tile, run_row_ref, lambda slot, row: pltpu.make_async_copy(
            stage_ref.at[b, pl.ds(slot, RUN_ROWS), :], xs_ref.at[pl.ds(row, RUN_ROWS), :], sem.at[b]).wait())

    @pl.when(i == 0)
    def _():
        zero_ref[...] = jnp.zeros_like(zero_ref)

        def tail(t, _):
            rows = pl.ds(pl.multiple_of(t * TM_E, TM_E), TM_E)
            cp = pltpu.make_async_copy(zero_ref, xs_ref.at[rows, :], psem)
            cp.start()
            cp.wait()
            return 0

        lax.fori_loop(tail_ref[0], tail_ref[1], tail, 0)

        def per_expert(e, _):
            def pad_copy(r):
                rows = pl.ds(pl.multiple_of(pad_start_ref[e] + r * RUN_ROWS, RUN_ROWS), RUN_ROWS)
                return pltpu.make_async_copy(zero_ref.at[pl.ds(0, RUN_ROWS), :], xs_ref.at[rows, :], psem)

            def start(r, _):
                pad_copy(r).start()
                return 0

            def wait(r, _):
                pad_copy(r).wait()
                return 0

            lax.fori_loop(0, pad_runs_ref[e], start, 0)
            lax.fori_loop(0, pad_runs_ref[e], wait, 0)
            return 0

        lax.fori_loop(0, N_EXPERTS, per_expert, 0)

    @pl.when(i >= 2)
    def _():
        wait_runs(i - 2, buf)

    rt = rt_ref[...]
    x = _unpack_bf16_pairs(jnp.where(i < n_prompt_tiles, hp_ref[...], hs_ref[...]))
    n_slots = stage_ref.shape[1]
    slot_id = lax.broadcasted_iota(jnp.int32, (n_slots, rt.shape[1]), 0).astype(F32)
    sel = jnp.zeros(slot_id.shape, F32)
    gsel = jnp.zeros(slot_id.shape, F32)
    for k in range(TOP_K):
        hit = slot_id == rt[_R_SLOT + k:_R_SLOT + k + 1, :]
        sel = jnp.where(hit, 1.0, sel)
        gsel = jnp.where(hit, rt[_R_GATE + k:_R_GATE + k + 1, :], gsel)
    rows = _dot(sel.astype(BF16), x)
    gate = jnp.sum(gsel, axis=-1, keepdims=True)
    packed = _pack_bf16_pairs(rows)
    stage_ref[buf, :, :packed.shape[1]] = packed
    stage_ref[buf, :, packed.shape[1]:] = pltpu.bitcast(jnp.broadcast_to(gate, (n_slots, LANES)), jnp.uint32)
    start_runs(i, buf)

    @pl.when(i == last)
    def _():
        @pl.when(i >= 1)
        def _():
            wait_runs(i - 1, 1 - buf)

        wait_runs(i, buf)


def _dispatch(h_p, h_s, route_t, run_row, pad_start, pad_runs, tail_tiles, n_rows):
    rp, dp = h_p.shape
    rs = h_s.shape[0]
    assert rp % TT == 0 and rs % TT == 0
    npt, nst = rp // TT, rs // TT
    width = dp + LANES
    grid_spec = pltpu.PrefetchScalarGridSpec(
        num_scalar_prefetch=4,
        grid=(npt + nst,),
        in_specs=[pl.BlockSpec((2 * 8, TT), lambda i, *_: (0, i)),
                  pl.BlockSpec((TT, dp), lambda i, *_: (jnp.minimum(i, npt - 1), 0)),
                  pl.BlockSpec((TT, dp), lambda i, *_: (jnp.maximum(i - npt, 0), 0))],
        out_specs=pl.BlockSpec(memory_space=pl.ANY),
        scratch_shapes=[pltpu.VMEM((2, N_SLOTS, width), jnp.uint32), pltpu.VMEM((TM_E, width), jnp.uint32),
                        pltpu.SemaphoreType.DMA((2,)), pltpu.SemaphoreType.DMA(())],
    )
    return pl.pallas_call(
        functools.partial(_dispatch_kernel, n_prompt_tiles=npt),
        out_shape=jax.ShapeDtypeStruct((n_rows, width), jnp.uint32),
        grid_spec=grid_spec,
        compiler_params=_params(("arbitrary",), has_side_effects=True),
        name="dispatch",
    )(run_row, pad_start, pad_runs, tail_tiles, route_t, h_p, h_s)


def _experts_kernel(te_ref, tf_ref, tu_ref, x_ref, w1_ref, b1g_ref, b1l_ref, w2_ref, b2_ref, perm_ref,
                    y_ref, w1s_ref, w2s_ref):
    i = pl.program_id(0)
    dff = w2_ref.shape[1]

    @pl.when(tf_ref[i] == 1)
    def _():
        perm = perm_ref[...]
        half = MXU_DIM // 2
        for c in range(w1_ref.shape[2] // MXU_DIM):
            blk = w1_ref[0, :, c * MXU_DIM:(c + 1) * MXU_DIM].astype(BF16)
            sp = _dot(blk, perm).astype(BF16)
            w1s_ref[:, c * half:(c + 1) * half] = sp[:, :half]
            w1s_ref[:, dff + c * half:dff + (c + 1) * half] = sp[:, half:]
        w2s_ref[...] = w2_ref[0].astype(BF16)

    @pl.when(tu_ref[i] == 1)
    def _():
        d = w2_ref.shape[2]
        x = _unpack_bf16_pairs(x_ref[:, :d // 2])
        gate = pltpu.bitcast(x_ref[:, d // 2:], F32)
        hdn = _dot(x, w1s_ref[...])
        glu = jnp.minimum(hdn[:, :dff] + b1g_ref[0], SWIGLU_LIMIT)
        lin = jnp.clip(hdn[:, dff:] + b1l_ref[0], -SWIGLU_LIMIT, SWIGLU_LIMIT)
        act = glu * _sigmoid(SWIGLU_ALPHA * glu) * (lin + 1.0)
        y = (_dot(act.astype(BF16), w2s_ref[...]) + b2_ref[0]) * jnp.concatenate([gate] * (d // LANES), axis=1)
        y_ref[...] = _pack_bf16_pairs(y.astype(BF16).astype(F32))

    @pl.when(tu_ref[i] == 0)
    def _():
        y_ref[...] = jnp.zeros_like(y_ref)


def _deinterleave_perm():
    j = lax.broadcasted_iota(jnp.int32, (MXU_DIM, MXU_DIM), 0)
    c = lax.broadcasted_iota(jnp.int32, (MXU_DIM, MXU_DIM), 1)
    half = MXU_DIM // 2
    src = jnp.where(c < half, 2 * c, 2 * (c - half) + 1)
    return (j == src).astype(BF16)


def _experts(xs, tile_expert, tile_first, tile_used, w1, b1g, b1l, w2, b2):
    width = xs.shape[1]
    n_tiles = tile_expert.shape[0]
    n_rows = n_tiles * TM_E
    ne, d, two_f = w1.shape
    dff = two_f // 2
    assert width == d // 2 + LANES
    grid_spec = pltpu.PrefetchScalarGridSpec(
        num_scalar_prefetch=3,
        grid=(n_tiles,),
        in_specs=[pl.BlockSpec((TM_E, width), lambda i, te, tf, tu: (i * tu[i], 0)),
                  pl.BlockSpec((1, d, two_f), lambda i, te, tf, tu: (te[i], 0, 0)),
                  pl.BlockSpec((1, 1, dff), lambda i, te, tf, tu: (te[i], 0, 0)),
                  pl.BlockSpec((1, 1, dff), lambda i, te, tf, tu: (te[i], 0, 0)),
                  pl.BlockSpec((1, dff, d), lambda i, te, tf, tu: (te[i], 0, 0)),
                  pl.BlockSpec((1, 1, d), lambda i, te, tf, tu: (te[i], 0, 0)),
                  pl.BlockSpec((MXU_DIM, MXU_DIM), lambda i, te, tf, tu: (0, 0))],
        out_specs=pl.BlockSpec((TM_E, d // 2), lambda i, te, tf, tu: (i, 0)),
        scratch_shapes=[pltpu.VMEM((d, two_f), BF16), pltpu.VMEM((dff, d), BF16)],
    )
    return pl.pallas_call(
        _experts_kernel,
        out_shape=jax.ShapeDtypeStruct((n_rows, d // 2), jnp.uint32),
        grid_spec=grid_spec,
        compiler_params=_params(("arbitrary",)),
        name="experts",
    )(tile_expert, tile_first, tile_used, xs, w1, b1g, b1l, w2, b2, _deinterleave_perm())


def _combine_kernel(run_row_ref, x1_ref, rt_ref, g2_ref, y_ref, o_ref, stage_ref, sem, *, tile_offset):
    i = pl.program_id(0)
    buf = i % 2

    def start_runs(step, b):
        _for_each_run(step + tile_offset, run_row_ref, lambda slot, row: pltpu.make_async_copy(
            y_ref.at[pl.ds(row, RUN_ROWS), :], stage_ref.at[b, pl.ds(slot, RUN_ROWS), :], sem.at[b]).start())

    def wait_runs(step, b):
        _for_each_run(step + tile_offset, run_row_ref, lambda slot, row: pltpu.make_async_copy(
            y_ref.at[pl.ds(row, RUN_ROWS), :], stage_ref.at[b, pl.ds(slot, RUN_ROWS), :], sem.at[b]).wait())

    @pl.when(i == 0)
    def _():
        stage_ref[...] = jnp.zeros_like(stage_ref)
        start_runs(0, 0)

    @pl.when(i + 1 < pl.num_programs(0))
    def _():
        start_runs(i + 1, 1 - buf)

    wait_runs(i, buf)

    rt = rt_ref[...]
    tt = rt.shape[0]
    slot_id = lax.broadcasted_iota(jnp.int32, (tt, stage_ref.shape[1]), 1).astype(F32)
    sel = jnp.zeros(slot_id.shape, F32)
    for k in range(TOP_K):
        sel = jnp.where(slot_id == rt[:, _R_SLOT + k:_R_SLOT + k + 1], 1.0, sel)
    sel = sel.astype(BF16)
    o_ref[...] = x1_ref[...] + g2_ref[...] * _dot(sel, _unpack_bf16_pairs(stage_ref[buf]))


def _combine(x1, route, run_row, g2, seq, y_rows, row_offset):
    r, d = x1.shape
    assert r % TT == 0 and row_offset % TT == 0
    off = row_offset // TT
    g2_o, rpm = _mod_operand(g2, seq, TT)
    if rpm is None:
        g2_spec = pl.BlockSpec((TT, d), lambda i, *_: (i, 0))
    else:
        g2_spec = pl.BlockSpec((None, 1, d), lambda i, *_: ((i * TT) // rpm, 0, 0))
    grid_spec = pltpu.PrefetchScalarGridSpec(
        num_scalar_prefetch=1,
        grid=(r // TT,),
        in_specs=[pl.BlockSpec((TT, d), lambda i, *_: (i, 0)),
                  pl.BlockSpec((TT, LANES), lambda i, *_: (i + off, 0)),
                  g2_spec,
                  pl.BlockSpec(memory_space=pl.ANY)],
        out_specs=pl.BlockSpec((TT, d), lambda i, *_: (i, 0)),
        scratch_shapes=[pltpu.VMEM((2, N_SLOTS, d // 2), jnp.uint32), pltpu.SemaphoreType.DMA((2,))],
    )
    return pl.pallas_call(
        functools.partial(_combine_kernel, tile_offset=off),
        out_shape=jax.ShapeDtypeStruct((r, d), F32),
        grid_spec=grid_spec,
        compiler_params=_params(("arbitrary",)),
        name="combine",
    )(run_row, x1, route, g2_o, y_rows)


def _gmlp_tables(w_s, b_s, seq_s):
    g = w_s.shape[0]
    tril = jnp.tril(w_s)
    short = jnp.tril(w_s[:, :seq_s, :seq_s])
    rep = CHUNK_B // seq_s
    blockdiag = jnp.zeros_like(w_s)
    for r in range(rep):
        blockdiag = blockdiag.at[:, r * seq_s:(r + 1) * seq_s, r * seq_s:(r + 1) * seq_s].set(short)
    ws_all = jnp.stack([tril, blockdiag]).astype(BF16)
    bs_full = jnp.repeat(b_s.T, HEAD_DIM, axis=1)
    bs_short = jnp.tile(jnp.repeat(b_s[:, :seq_s].T, HEAD_DIM, axis=1), (rep, 1))
    return ws_all, jnp.stack([bs_full, bs_short])


def _layer(xp, xs, cache_k, cache_v, cp, cs, p):
    bp, tp, d = xp.shape
    bs_, ts, _ = xs.shape
    assert tp % CHUNK_B == 0 and CHUNK_B % ts == 0
    rp, rs = bp * tp, bs_ * ts
    n_tok = rp + rs

    mod = _ada(jnp.concatenate([cp, cs], axis=0), p['w_ada'], p['b_ada'])
    mods = [mod[:, j * d:(j + 1) * d] for j in range(6)]
    mp = [m[:bp] for m in mods]
    msm = [m[bp:] for m in mods]

    tile_h = lambda g: jnp.tile(g, N_HEADS).reshape(1, D_HEADS)
    bd = jnp.kron(jnp.eye(MXU_DIM // HEAD_DIM, dtype=F32),
                  jnp.full((HEAD_DIM, HEAD_DIM), 1.0 / HEAD_DIM, F32)).astype(BF16)
    w_in_b = p['w_in'].astype(BF16)
    n1g = p['norm1_g'].reshape(1, d)
    common = (n1g, w_in_b, tile_h(p['q_norm_g']), tile_h(p['k_norm_g']), tile_h(p['v_norm_g']), bd)
    q_p, k_p, kb_p, v_p, vb_p, u_p, vn_p, sga_p, sgb_p = _inproj(
        xp.reshape(rp, d), mp[0], mp[1], tp, *common, want_vn32=False, kv_transposed=True)
    q_s, k_s, kb_s, v_s, vb_s, u_s, vn_s, sga_s, sgb_s, vn32_s = _inproj(
        xs.reshape(rs, d), msm[0], msm[1], ts, *common, want_vn32=True, kv_transposed=False)

    ya_p = _attn_prompt(q_p, kb_p, vb_p, bp, tp)
    ya_s = _attn_sample(q_s, kb_s, vb_s, cache_k, cache_v, bs_, ts)

    ws_all, bs_all = _gmlp_tables(p['w_s'], p['b_s'], ts)
    wr = p['w_router']
    wr_hi = wr.astype(BF16)
    wr_lo = (wr - wr_hi.astype(F32)).astype(BF16)
    post_w = (p['w_pa'].astype(BF16), p['w_pb'].astype(BF16), p['w_o'].astype(BF16), p['norm2_g'].reshape(1, d),
              jnp.concatenate([wr_hi, wr_lo], axis=1), p['b_router'].reshape(1, N_EXPERTS))
    x1_p, h2_p, lg_p = _post(xp.reshape(rp, d), ya_p, u_p, vn_p, sga_p, sgb_p, mp[2], mp[3], mp[4], tp,
                             ws_all, bs_all, 0, *post_w)
    x1_s, h2_s, lg_s = _post(xs.reshape(rs, d), ya_s, u_s, vn_s, sga_s, sgb_s, msm[2], msm[3], msm[4], ts,
                             ws_all, bs_all, 1, *post_w)

    route, route_t, table, counts = _route(lg_p, lg_s)

    counts = counts[0].astype(jnp.int32)
    padded = (counts + TM_E - 1) // TM_E * TM_E
    pend = jnp.cumsum(padded)
    pstart = pend - padded
    table = table.reshape(-1, 8, LANES)[:, :, :N_EXPERTS].astype(jnp.int32)
    n_token_tiles = n_tok // TT
    runs = (table[:, _T_COUNT] + RUN_ROWS - 1) // RUN_ROWS
    runs_end = jnp.cumsum(runs, axis=1)
    j = jnp.arange(RUNS_PER_TILE, dtype=jnp.int32)
    e_of = jnp.minimum(jnp.sum((runs_end[:, None, :] <= j[None, :, None]).astype(jnp.int32), axis=2), N_EXPERTS - 1)
    pick = (e_of[:, :, None] == jnp.arange(N_EXPERTS, dtype=jnp.int32)[None, None, :]).astype(jnp.int32)
    first_run = jnp.sum(pick * (runs_end - runs)[:, None, :], axis=2)
    first_row = jnp.sum(pick * (table[:, _T_BASE] + pstart[None, :])[:, None, :], axis=2)
    run_row = first_row + (j[None, :] - first_run) * RUN_ROWS
    is_run = j[None, :] < runs_end[:, -1:]
    n_tiles = -(-(n_tok * TOP_K + n_token_tiles * N_EXPERTS * (RUN_ROWS - 1)) // TM_E) + N_EXPERTS
    tile_start = jnp.arange(n_tiles, dtype=jnp.int32) * TM_E
    tile_used = (tile_start < pend[-1]).astype(jnp.int32)
    last_expert = jnp.sum((pend <= pend[-1] - 1).astype(jnp.int32))
    tile_expert = jnp.minimum(jnp.sum((pend[None, :] <= tile_start[:, None]).astype(jnp.int32), axis=1), last_expert)
    tile_first = jnp.concatenate([jnp.ones((1,), jnp.int32),
                                  (tile_expert[1:] != tile_expert[:-1]).astype(jnp.int32)])

    spare_tiles = -(-2 * N_SLOTS // TM_E)
    spare_rows = j[None, :] * RUN_ROWS
    parity = (jnp.arange(n_token_tiles, dtype=jnp.int32) % 2)[:, None]
    rows_out = jnp.where(is_run, run_row, n_tiles * TM_E + parity * N_SLOTS + spare_rows).reshape(-1)
    rows_in = jnp.where(is_run, run_row, spare_rows).reshape(-1)
    tail_tiles = jnp.stack([pend[-1] // TM_E, jnp.int32(n_tiles + spare_tiles)]).astype(jnp.int32)
    pad_runs = (padded - counts) // RUN_ROWS
    xs_rows = _dispatch(h2_p, h2_s, route_t, rows_out, pstart + counts, pad_runs, tail_tiles,
                        (n_tiles + spare_tiles) * TM_E)
    b1 = p['b1']
    ne = b1.shape[0]
    y_rows = _experts(xs_rows, tile_expert, tile_first, tile_used, p['w1'],
                      b1[:, 0::2].reshape(ne, 1, -1), b1[:, 1::2].reshape(ne, 1, -1),
                      p['w2'], p['b2'].reshape(ne, 1, d))
    out_p = _combine(x1_p, route, rows_in, mp[5], tp, y_rows, 0)
    out_s = _combine(x1_s, route, rows_in, msm[5], ts, y_rows, rp)

    shp = lambda a, b, t: a.reshape(b, t, N_HEADS, HEAD_DIM)
    unt = lambda a: jnp.transpose(a.reshape(bp, N_HEADS, HEAD_DIM, tp), (0, 3, 1, 2))
    return (out_p.reshape(bp, tp, d), out_s.reshape(bs_, ts, d),
            unt(k_p), unt(v_p), shp(k_s, bs_, ts), shp(v_s, bs_, ts), shp(vn32_s, bs_, ts))


def kernel(x_prompt, x_sample, cache_k, cache_v, c_prompt, c_sample, norm1_g, norm2_g, w_ada, b_ada, w_in,
           q_norm_g, k_norm_g, v_norm_g, w_s, b_s, w_pa, w_pb, w_o, w_router, b_router, w1, b1, w2, b2):
    depth = w_in.shape[0]
    xp, xs = x_prompt, x_sample
    kp, vp, ks, vs, vbs = [], [], [], [], []
    for l in range(depth):
        p = {'norm1_g': norm1_g[l], 'norm2_g': norm2_g[l], 'w_ada': w_ada[l], 'b_ada': b_ada[l],
             'w_in': w_in[l], 'q_norm_g': q_norm_g[l], 'k_norm_g': k_norm_g[l], 'v_norm_g': v_norm_g[l],
             'w_s': w_s[l], 'b_s': b_s[l], 'w_pa': w_pa[l], 'w_pb': w_pb[l], 'w_o': w_o[l],
             'w_router': w_router[l], 'b_router': b_router[l], 'w1': w1[l], 'b1': b1[l],
             'w2': w2[l], 'b2': b2[l]}
        xp, xs, k_p, v_p, k_s, v_s, vn_s = _layer(xp, xs, cache_k[l], cache_v[l], c_prompt, c_sample, p)
        kp.append(k_p); vp.append(v_p); ks.append(k_s); vs.append(v_s); vbs.append(vn_s)
    return (xp, xs, jnp.stack(kp), jnp.stack(vp), jnp.stack(ks), jnp.stack(vs), jnp.stack(vbs))
```

```python
import functools

import jax
import jax.numpy as jnp
from jax import lax
from jax.experimental import pallas as pl
from jax.experimental.pallas import tpu as pltpu

F32 = jnp.float32
BF16 = jnp.bfloat16

LANES = 128
MXU_DIM = 256
VMEM_LIMIT = 56 * 1024 * 1024

HEAD_DIM = 64
N_HEADS = 8
D_HEADS = N_HEADS * HEAD_DIM
PAIRS = D_HEADS // LANES
CHUNK_B = 128
N_EXPERTS = 32
TOP_K = 4
RMS_EPS = 1e-6
SWIGLU_ALPHA = 1.702
SWIGLU_LIMIT = 7.0

TM_IN = 512
TQ = 512
TK = 256
SUB_Q = 128
LOOK_BACK = 256
WINDOW = SUB_Q + LOOK_BACK
TM_POST = 256
TT = 256
TM_E = 512
RUN_ROWS = 8
N_SLOTS = TT * TOP_K + N_EXPERTS * RUN_ROWS


def _params(sem, **kw):
    return pltpu.CompilerParams(dimension_semantics=sem, vmem_limit_bytes=VMEM_LIMIT, **kw)


def _dot(a, b):
    return jnp.dot(a, b, preferred_element_type=F32)


def _dot_nt(a, b):
    return lax.dot_general(a, b, (((1,), (1,)), ((), ())), preferred_element_type=F32)


def _gelu(x):
    return jax.nn.gelu(x)


def _sigmoid(x):
    return 1.0 / (1.0 + jnp.exp(-x))


def _ada_kernel(c_ref, w_ref, b_ref, o_ref):
    c = c_ref[...]
    s = c * _sigmoid(c)
    o_ref[...] = jnp.dot(s, w_ref[...], preferred_element_type=F32,
                         precision=lax.Precision.HIGHEST) + b_ref[...]


def _ada(c, w_ada, b_ada):
    nb, d = c.shape
    n = w_ada.shape[1]
    tn = 1024
    return pl.pallas_call(
        _ada_kernel,
        out_shape=jax.ShapeDtypeStruct((nb, n), F32),
        grid=(n // tn,),
        in_specs=[pl.BlockSpec((nb, d), lambda j: (0, 0)),
                  pl.BlockSpec((d, tn), lambda j: (0, j)),
                  pl.BlockSpec((1, tn), lambda j: (0, j))],
        out_specs=pl.BlockSpec((nb, tn), lambda j: (0, j)),
        compiler_params=_params(("parallel",)),
        name="ada",
    )(c, w_ada, b_ada.reshape(1, n))


def _head_mean_sq(p, bd):
    sq = (p * p).astype(BF16)
    return jnp.concatenate([_dot(sq[:, :MXU_DIM], bd), _dot(sq[:, MXU_DIM:], bd)], axis=-1)


def _inproj_kernel(x_ref, sh_ref, sc_ref, g_ref, w_ref, qg_ref, kg_ref, vg_ref, bd_ref,
                   q_ref, k32_ref, kb_ref, v32_ref, vb_ref, u_ref, vnb_ref, sga_ref, sgb_ref, *vn32_ref,
                   kv_transposed):
    kv_out = (lambda a: a.T) if kv_transposed else (lambda a: a)
    x = x_ref[...]
    ms = jnp.mean(x * x, axis=-1, keepdims=True)
    xn = x * lax.rsqrt(ms + RMS_EPS) * g_ref[...]
    h = (xn * (1.0 + sc_ref[...]) + sh_ref[...]).astype(BF16)
    bd = bd_ref[...]
    d = x.shape[1]

    def sec(a, n):
        return _dot(h, w_ref[:, a:a + n])

    o = 0
    pq = sec(o, D_HEADS); o += D_HEADS
    q = pq * lax.rsqrt(_head_mean_sq(pq, bd) + RMS_EPS) * qg_ref[...]
    q_ref[...] = (q * (HEAD_DIM ** -0.5)).astype(BF16)
    pk = sec(o, D_HEADS); o += D_HEADS
    k = pk * lax.rsqrt(_head_mean_sq(pk, bd) + RMS_EPS) * kg_ref[...]
    k32_ref[...] = kv_out(k)
    kb_ref[...] = k.astype(BF16)
    pv = sec(o, D_HEADS); o += D_HEADS
    v32_ref[...] = kv_out(pv)
    vb_ref[...] = pv.astype(BF16)
    pu = sec(o, D_HEADS); o += D_HEADS
    u_ref[...] = _gelu(pu).astype(BF16)
    pvb = _gelu(sec(o, D_HEADS)); o += D_HEADS
    vn = pvb * lax.rsqrt(_head_mean_sq(pvb, bd) + RMS_EPS) * vg_ref[...]
    vnb_ref[...] = vn.astype(BF16)
    if vn32_ref:
        vn32_ref[0][...] = vn
    sga_ref[...] = _sigmoid(sec(o, d)).astype(BF16); o += d
    sgb_ref[...] = _sigmoid(sec(o, d)).astype(BF16)


def _mod_spec(tm, arr, rows_per_mod):
    if rows_per_mod is None:
        return pl.BlockSpec((tm, arr.shape[-1]), lambda i: (i, 0))
    return pl.BlockSpec((None, 1, arr.shape[-1]), lambda i: ((i * tm) // rows_per_mod, 0, 0))


def _mod_operand(m, seq, tm):
    if seq % tm == 0:
        return m[:, None, :], seq
    return jnp.repeat(m, seq, axis=0), None


def _inproj(x, sh, sc, seq, g, w_in_b, qg, kg, vg, bd, want_vn32, kv_transposed):
    r, d = x.shape
    tm = min(TM_IN, r)
    assert r % tm == 0 and (not kv_transposed or seq % tm == 0)
    sh_o, rpm = _mod_operand(sh, seq, tm)
    sc_o, _ = _mod_operand(sc, seq, tm)
    ncol = w_in_b.shape[1]
    const = lambda i: (0, 0)
    row = lambda n: pl.BlockSpec((tm, n), lambda i: (i, 0))
    outs = [jax.ShapeDtypeStruct((r, D_HEADS), BF16),
            jax.ShapeDtypeStruct((r, D_HEADS), F32),
            jax.ShapeDtypeStruct((r, D_HEADS), BF16),
            jax.ShapeDtypeStruct((r, D_HEADS), F32),
            jax.ShapeDtypeStruct((r, D_HEADS), BF16),
            jax.ShapeDtypeStruct((r, D_HEADS), BF16),
            jax.ShapeDtypeStruct((r, D_HEADS), BF16),
            jax.ShapeDtypeStruct((r, d), BF16),
            jax.ShapeDtypeStruct((r, d), BF16)]
    out_specs = [row(D_HEADS)] * 7 + [row(d), row(d)]
    if kv_transposed:
        per_seq = seq // tm
        for j in (1, 3):
            outs[j] = jax.ShapeDtypeStruct((r // seq, D_HEADS, seq), F32)
            out_specs[j] = pl.BlockSpec((None, D_HEADS, tm), lambda i: (i // per_seq, 0, i % per_seq))
    if want_vn32:
        outs.append(jax.ShapeDtypeStruct((r, D_HEADS), F32))
        out_specs.append(row(D_HEADS))
    return pl.pallas_call(
        functools.partial(_inproj_kernel, kv_transposed=kv_transposed),
        out_shape=outs,
        grid=(r // tm,),
        in_specs=[row(d), _mod_spec(tm, sh_o, rpm), _mod_spec(tm, sc_o, rpm),
                  pl.BlockSpec((1, d), const),
                  pl.BlockSpec((d, ncol), const, pipeline_mode=pl.Buffered(1)),
                  pl.BlockSpec((1, D_HEADS), const), pl.BlockSpec((1, D_HEADS), const),
                  pl.BlockSpec((1, D_HEADS), const), pl.BlockSpec((MXU_DIM, MXU_DIM), const)],
        out_specs=out_specs,
        compiler_params=_params(("parallel",)),
        name="inproj",
    )(x, sh_o, sc_o, g, w_in_b, qg, kg, vg, bd)


LOG_WEIGHT_CUTOFF = -104.0
MASKED_SCORE = -1e30


def _sb_weights(z, u, c, mask):
    if mask is not None:
        z = jnp.where(mask, z, MASKED_SCORE)
    sp = jnp.maximum(z, 0.0) + jnp.log(1.0 + jnp.exp(-jnp.abs(z)))
    hi = sp.astype(BF16)
    lo = (sp - hi.astype(F32)).astype(BF16)
    later = _dot(hi, u) + _dot(lo, u)
    w = jnp.exp((z + c) - (sp + later))
    return w, c - jnp.sum(sp, axis=-1, keepdims=True)


def _pair_select(shape):
    return lax.broadcasted_iota(jnp.int32, shape, 1) < HEAD_DIM


def _causal_mask(m, tq, tk):
    row = lax.broadcasted_iota(jnp.int32, (m, tk), 0) & (tq - 1)
    return lax.broadcasted_iota(jnp.int32, (m, tk), 1) < row


def _stack_heads(q):
    first = _pair_select(q.shape)
    zero = jnp.zeros_like(q)
    return jnp.concatenate([jnp.where(first, q, zero), jnp.where(first, zero, q)], axis=0)


def _attn_prompt_kernel(q_ref, k_ref, v_ref, uw_ref, u_ref, o_ref, acc_ref, c_ref):
    i = pl.program_id(2)
    tq = q_ref.shape[0]
    nsub = tq // SUB_Q
    sub_rows = 2 * SUB_Q
    q = q_ref[...]
    u = u_ref[...]
    first = _pair_select((SUB_Q, LANES))
    m = nsub * sub_rows

    def pv_sub(wb, s, vs):
        w0 = wb[s * sub_rows:s * sub_rows + SUB_Q]
        w1 = wb[s * sub_rows + SUB_Q:(s + 1) * sub_rows]
        return jnp.where(first, _dot(w0, vs), _dot(w1, vs))

    qs = [_stack_heads(q[s * SUB_Q:(s + 1) * SUB_Q]) for s in range(nsub)]
    starts = [pl.multiple_of(jnp.maximum(i * tq + s * SUB_Q - LOOK_BACK, 0), SUB_Q) for s in range(nsub)]
    row = lax.broadcasted_iota(jnp.int32, (m, 1), 0)
    window_start = jnp.zeros((m, 1), jnp.int32) + starts[0]
    for s in range(1, nsub):
        window_start = jnp.where(row >= s * sub_rows, starts[s], window_start)
    q_pos = i * tq + (row // sub_rows) * SUB_Q + (row & (SUB_Q - 1))

    z = jnp.concatenate([_dot_nt(qs[s], k_ref[pl.ds(starts[s], WINDOW), :]) for s in range(nsub)], axis=0)
    col = lax.broadcasted_iota(jnp.int32, (m, WINDOW), 1)
    w, c = _sb_weights(z, uw_ref[...], jnp.zeros((m, 1), F32), window_start + col < q_pos)
    wb = w.astype(BF16)
    acc_ref[...] = jnp.concatenate([pv_sub(wb, s, v_ref[pl.ds(starts[s], WINDOW), :]) for s in range(nsub)], axis=0)
    c_ref[...] = c
    q_all = jnp.concatenate(qs, axis=0)
    col = lax.broadcasted_iota(jnp.int32, (m, TK), 1)

    def cond(state):
        j, cmax = state
        return jnp.logical_and(j >= 0, cmax > LOG_WEIGHT_CUTOFF)

    def body(state):
        j, _ = state
        start = pl.multiple_of(j * TK, TK)
        vs = v_ref[pl.ds(start, TK), :]
        w, c = _sb_weights(_dot_nt(q_all, k_ref[pl.ds(start, TK), :]), u, c_ref[...],
                           j * TK + col < window_start)
        wb = w.astype(BF16)
        acc_ref[...] += jnp.concatenate([pv_sub(wb, s, vs) for s in range(nsub)], axis=0)
        c_ref[...] = c
        return j - 1, jnp.max(c)

    j0 = lax.div(starts[nsub - 1] + (TK - 1), TK) - 1
    lax.while_loop(cond, body, (j0, jnp.max(c)))
    o_ref[...] = acc_ref[...].astype(o_ref.dtype)


def _upper_ones(n):
    return (lax.broadcasted_iota(jnp.int32, (n, n), 0) > lax.broadcasted_iota(jnp.int32, (n, n), 1)).astype(BF16)


def _attn_prompt(q, kb, vb, batch, seq):
    assert seq % TQ == 0 and TQ % TK == 0 and TQ % SUB_Q == 0 and seq >= WINDOW
    nq = seq // TQ
    return pl.pallas_call(
        _attn_prompt_kernel,
        out_shape=jax.ShapeDtypeStruct(q.shape, BF16),
        grid=(batch, PAIRS, nq),
        in_specs=[pl.BlockSpec((TQ, LANES), lambda b, p, i: (b * nq + i, p)),
                  pl.BlockSpec((seq, LANES), lambda b, p, i: (b, p)),
                  pl.BlockSpec((seq, LANES), lambda b, p, i: (b, p)),
                  pl.BlockSpec((WINDOW, WINDOW), lambda b, p, i: (0, 0)),
                  pl.BlockSpec((TK, TK), lambda b, p, i: (0, 0))],
        out_specs=pl.BlockSpec((TQ, LANES), lambda b, p, i: (b * nq + i, p)),
        scratch_shapes=[pltpu.VMEM((TQ, LANES), F32), pltpu.VMEM((2 * TQ, 1), F32)],
        compiler_params=_params(("parallel", "parallel", "arbitrary")),
        name="attn_prompt",
    )(q, kb, vb, _upper_ones(WINDOW), _upper_ones(TK))


def _attn_sample_kernel(q_ref, kn_ref, vn_ref, kc_ref, vc_ref, un_ref, u_ref, o_ref,
                        kbuf_ref, vbuf_ref, sem, acc_ref, c_ref):
    b = pl.program_id(0)
    tq = q_ref.shape[0]
    nblk = kc_ref.shape[3] // TK
    head = lambda a, h: a[:, h * HEAD_DIM:(h + 1) * HEAD_DIM]
    q = q_ref[...]
    qh = [head(q, h) for h in range(N_HEADS)]

    def copies(j, slot):
        rows = pl.ds(pl.multiple_of(j * TK, TK), TK)
        return (pltpu.make_async_copy(kc_ref.at[b, :, :, rows], kbuf_ref.at[slot], sem.at[0, slot]),
                pltpu.make_async_copy(vc_ref.at[b, :, :, rows], vbuf_ref.at[slot], sem.at[1, slot]))

    def start(j, slot):
        for cp in copies(j, slot):
            cp.start()

    def wait(j, slot):
        for cp in copies(j, slot):
            cp.wait()

    def sweep(score, apply, u, c, mask):
        z = jnp.concatenate([score(qh[h], h) for h in range(N_HEADS)], axis=0)
        w, c = _sb_weights(z, u, c, mask)
        wb = w.astype(BF16)
        pv = jnp.concatenate([apply(wb[h * tq:(h + 1) * tq], h) for h in range(N_HEADS)], axis=-1)
        return pv, c

    start(nblk - 1, 0)
    kn = kn_ref[...]
    vn = vn_ref[...]
    pv, c = sweep(lambda qq, h: _dot_nt(qq, head(kn, h)), lambda ww, h: _dot(ww, head(vn, h)), un_ref[...],
                  jnp.zeros((N_HEADS * tq, 1), F32), _causal_mask(N_HEADS * tq, tq, tq))
    acc_ref[...] = pv
    c_ref[...] = c
    u = u_ref[...]

    def cond(state):
        j, _, cmax = state
        return jnp.logical_and(j >= 0, cmax > LOG_WEIGHT_CUTOFF)

    def body(state):
        j, slot, _ = state
        wait(j, slot)

        @pl.when(j > 0)
        def _():
            start(j - 1, 1 - slot)

        pv, c = sweep(lambda qq, h: _dot(qq, kbuf_ref[slot, h].astype(BF16)),
                      lambda ww, h: _dot_nt(ww, vbuf_ref[slot, h].astype(BF16)), u, c_ref[...], None)
        acc_ref[...] += pv
        c_ref[...] = c
        return j - 1, 1 - slot, jnp.max(c)

    j, slot, _ = lax.while_loop(cond, body, (nblk - 1, 0, jnp.max(c)))

    @pl.when(j >= 0)
    def _():
        wait(j, slot)

    o_ref[...] = acc_ref[...].astype(o_ref.dtype)


def _attn_sample(q, kb, vb, cache_k, cache_v, batch, seq):
    past = cache_k.shape[1]
    assert past % TK == 0 and past >= TK
    cache_k = jnp.transpose(cache_k, (0, 2, 3, 1))
    cache_v = jnp.transpose(cache_v, (0, 2, 3, 1))
    row = pl.BlockSpec((seq, D_HEADS), lambda b: (b, 0))
    return pl.pallas_call(
        _attn_sample_kernel,
        out_shape=jax.ShapeDtypeStruct(q.shape, BF16),
        grid=(batch,),
        in_specs=[row, row, row,
                  pl.BlockSpec(memory_space=pl.ANY), pl.BlockSpec(memory_space=pl.ANY),
                  pl.BlockSpec((seq, seq), lambda b: (0, 0)),
                  pl.BlockSpec((TK, TK), lambda b: (0, 0))],
        out_specs=row,
        scratch_shapes=[pltpu.VMEM((2, N_HEADS, HEAD_DIM, TK), F32), pltpu.VMEM((2, N_HEADS, HEAD_DIM, TK), F32),
                        pltpu.SemaphoreType.DMA((2, 2)),
                        pltpu.VMEM((seq, D_HEADS), F32), pltpu.VMEM((N_HEADS * seq, 1), F32)],
        compiler_params=_params(("arbitrary",)),
        name="attn_sample",
    )(q, kb, vb, cache_k, cache_v, _upper_ones(seq), _upper_ones(TK))


def _pack_bf16_pairs(x):
    half = x.shape[1] // 2
    a = pltpu.bitcast(x[:, :half], jnp.uint32)
    b = pltpu.bitcast(x[:, half:], jnp.uint32)
    return a | (b >> 16)


def _unpack_bf16_pairs(p):
    a = pltpu.bitcast(p & jnp.uint32(0xFFFF0000), F32).astype(BF16)
    b = pltpu.bitcast(p << 16, F32).astype(BF16)
    return jnp.concatenate([a, b], axis=1)


def _post_kernel(x_ref, ya_ref, u_ref, vn_ref, sga_ref, sgb_ref, g1_ref, sh2_ref, sc2_ref,
                 ws_ref, bs_ref, wpa_ref, wpb_ref, wo_ref, n2g_ref, wr_ref, br_ref,
                 x1_ref, h2_ref, lg_ref):
    tm = x_ref.shape[0]
    nch = tm // CHUNK_B
    first = (lax.broadcasted_iota(jnp.int32, (CHUNK_B, nch * LANES), 1) & (LANES - 1)) < HEAD_DIM
    mixed = []
    for p in range(PAIRS):
        vnp = jnp.concatenate([vn_ref[c * CHUNK_B:(c + 1) * CHUNK_B, p * LANES:(p + 1) * LANES] for c in range(nch)],
                              axis=1)
        mixed.append(jnp.where(first, _dot(ws_ref[0, 2 * p], vnp), _dot(ws_ref[0, 2 * p + 1], vnp)))
    chunks = []
    for c in range(nch):
        s = jnp.concatenate([mp[:, c * LANES:(c + 1) * LANES] for mp in mixed], axis=-1) + bs_ref[0]
        chunks.append((u_ref[c * CHUNK_B:(c + 1) * CHUNK_B, :].astype(F32) * s).astype(BF16))
    yb = jnp.concatenate(chunks, axis=0)
    merged = (sga_ref[...].astype(F32) * _dot(ya_ref[...], wpa_ref[...])
              + sgb_ref[...].astype(F32) * _dot(yb, wpb_ref[...]))
    x1 = x_ref[...] + g1_ref[...] * _dot(merged.astype(BF16), wo_ref[...])
    x1_ref[...] = x1
    ms = jnp.mean(x1 * x1, axis=-1, keepdims=True)
    h2 = (x1 * lax.rsqrt(ms + RMS_EPS) * n2g_ref[...]) * (1.0 + sc2_ref[...]) + sh2_ref[...]
    hi = h2.astype(BF16)
    hf = hi.astype(F32)
    lo = (h2 - hf).astype(BF16)
    h2_ref[...] = _pack_bf16_pairs(hf)
    l2 = _dot(hi, wr_ref[...]) + _dot(lo, wr_ref[...])
    l2t = jnp.concatenate([l2, jnp.zeros((l2.shape[0], LANES - l2.shape[1]), F32)], axis=1).T
    lg_ref[...] = l2t[:N_EXPERTS] + l2t[N_EXPERTS:2 * N_EXPERTS] + br_ref[...]


def _post(x, ya, u, vn, sga, sgb, g1, sh2, sc2, seq, ws_all, bs_all, ws_idx, wpa, wpb, wo, n2g, wr, br):
    r, d = x.shape
    tm = min(TM_POST, r)
    assert r % tm == 0 and tm % CHUNK_B == 0
    g1_o, rpm = _mod_operand(g1, seq, tm)
    sh2_o, _ = _mod_operand(sh2, seq, tm)
    sc2_o, _ = _mod_operand(sc2, seq, tm)
    const = lambda i: (0, 0)
    row = lambda n: pl.BlockSpec((tm, n), lambda i: (i, 0))
    return pl.pallas_call(
        _post_kernel,
        out_shape=[jax.ShapeDtypeStruct((r, d), F32), jax.ShapeDtypeStruct((r, d // 2), jnp.uint32),
                   jax.ShapeDtypeStruct((N_EXPERTS, r), F32)],
        grid=(r // tm,),
        in_specs=[row(d), row(D_HEADS), row(D_HEADS), row(D_HEADS), row(d), row(d),
                  _mod_spec(tm, g1_o, rpm), _mod_spec(tm, sh2_o, rpm), _mod_spec(tm, sc2_o, rpm),
                  pl.BlockSpec((1, N_HEADS, CHUNK_B, CHUNK_B), lambda i: (ws_idx, 0, 0, 0)),
                  pl.BlockSpec((1, CHUNK_B, D_HEADS), lambda i: (ws_idx, 0, 0)),
                  pl.BlockSpec(wpa.shape, const), pl.BlockSpec(wpb.shape, const), pl.BlockSpec(wo.shape, const),
                  pl.BlockSpec((1, d), const), pl.BlockSpec(wr.shape, const), pl.BlockSpec((N_EXPERTS, 1), const)],
        out_specs=[row(d), row(d // 2), pl.BlockSpec((N_EXPERTS, tm), lambda i: (0, i))],
        compiler_params=_params(("parallel",)),
        name="post",
    )(x, ya, u, vn, sga, sgb, g1_o, sh2_o, sc2_o, ws_all, bs_all, wpa, wpb, wo, n2g, wr, br)


_R_IDX, _R_GATE, _R_SLOT = 0, TOP_K, 2 * TOP_K
_R_ROWS = 16
_T_COUNT, _T_SLOT, _T_BASE = 0, 1, 2


def _route_kernel(lp_ref, ls_ref, tri_ref, scan_ref, o_ref, ot_ref, tbl_ref, cnt_ref, carry_ref, *, n_prompt_tiles):
    i = pl.program_id(0)

    @pl.when(i == 0)
    def _():
        carry_ref[...] = jnp.zeros_like(carry_ref)

    l = jnp.where(i < n_prompt_tiles, lp_ref[...], ls_ref[...])
    ne, tt = l.shape
    eid = lax.broadcasted_iota(jnp.int32, l.shape, 0).astype(F32)
    cur = l
    vals, idxs = [], []
    for _ in range(TOP_K):
        m = jnp.max(cur, axis=0, keepdims=True)
        idx = jnp.min(jnp.where(cur == m, eid, float(ne)), axis=0, keepdims=True)
        vals.append(m)
        idxs.append(idx)
        cur = jnp.where(eid == idx, -jnp.inf, cur)
    es = [jnp.exp(v - vals[0]) for v in vals]
    den = es[0] + es[1] + es[2] + es[3]
    onehot = jnp.zeros(l.shape, F32)
    for idx in idxs:
        onehot = onehot + (eid == idx).astype(F32)
    within = _dot(onehot.astype(BF16), tri_ref[...])
    count = jnp.sum(onehot, axis=1, keepdims=True)
    runs = jnp.ceil(count * (1.0 / RUN_ROWS))
    first_slot = RUN_ROWS * _dot(scan_ref[...], jnp.broadcast_to(runs, (ne, LANES)).astype(BF16))[:, 0:1]
    slot_of = within + first_slot
    row = lax.broadcasted_iota(jnp.int32, (_R_ROWS, tt), 0)
    out = jnp.zeros((_R_ROWS, tt), F32)
    for k in range(TOP_K):
        slot = jnp.sum(jnp.where(eid == idxs[k], slot_of, 0.0), axis=0, keepdims=True)
        out = jnp.where(row == _R_IDX + k, idxs[k], out)
        out = jnp.where(row == _R_GATE + k, es[k] / den, out)
        out = jnp.where(row == _R_SLOT + k, slot, out)
    ot_ref[...] = out
    o_ref[...] = jnp.concatenate([out, jnp.zeros((LANES - _R_ROWS, tt), F32)], axis=0).T
    lane = lax.broadcasted_iota(jnp.int32, (ne, LANES), 1)
    tbl_ref[...] = jnp.where(lane == _T_COUNT, count, jnp.where(lane == _T_SLOT, first_slot, carry_ref[...]))
    carry_ref[...] += RUN_ROWS * runs
    cnt_ref[...] = jnp.broadcast_to(carry_ref[...], cnt_ref.shape)


def _route(lg_p, lg_s):
    rp, rs = lg_p.shape[1], lg_s.shape[1]
    assert rp % TT == 0 and rs % TT == 0
    npt, nst = rp // TT, rs // TT
    nt = npt + nst
    iota = lambda shape, ax: lax.broadcasted_iota(jnp.int32, shape, ax)
    tri = (iota((TT, TT), 0) < iota((TT, TT), 1)).astype(BF16)
    scan = (iota((N_EXPERTS, N_EXPERTS), 1) < iota((N_EXPERTS, N_EXPERTS), 0)).astype(BF16)
    return pl.pallas_call(
        functools.partial(_route_kernel, n_prompt_tiles=npt),
        out_shape=[jax.ShapeDtypeStruct((rp + rs, LANES), F32), jax.ShapeDtypeStruct((_R_ROWS, rp + rs), F32),
                   jax.ShapeDtypeStruct((nt * N_EXPERTS, LANES), F32), jax.ShapeDtypeStruct((N_EXPERTS, LANES), F32)],
        grid=(nt,),
        in_specs=[pl.BlockSpec((N_EXPERTS, TT), lambda i: (0, jnp.minimum(i, npt - 1))),
                  pl.BlockSpec((N_EXPERTS, TT), lambda i: (0, jnp.maximum(i - npt, 0))),
                  pl.BlockSpec((TT, TT), lambda i: (0, 0)),
                  pl.BlockSpec((N_EXPERTS, N_EXPERTS), lambda i: (0, 0))],
        out_specs=[pl.BlockSpec((TT, LANES), lambda i: (i, 0)), pl.BlockSpec((_R_ROWS, TT), lambda i: (0, i)),
                   pl.BlockSpec((N_EXPERTS, LANES), lambda i: (i, 0)),
                   pl.BlockSpec((N_EXPERTS, LANES), lambda i: (0, 0))],
        scratch_shapes=[pltpu.VMEM((N_EXPERTS, 1), F32)],
        compiler_params=_params(("arbitrary",)),
        name="route",
    )(lg_p, lg_s, tri, scan)


RUNS_PER_TILE = N_SLOTS // RUN_ROWS
RUN_UNROLL = 8


def _for_each_run(tile, run_row_ref, fn):
    def group(g, _):
        for u in range(RUN_UNROLL):
            j = g * RUN_UNROLL + u
            row = run_row_ref[tile * RUNS_PER_TILE + j]
            fn(pl.multiple_of(j * RUN_ROWS, RUN_ROWS), pl.multiple_of(row, RUN_ROWS))
        return 0

    lax.fori_loop(0, RUNS_PER_TILE // RUN_UNROLL, group, 0)


def _dispatch_kernel(run_row_ref, pad_start_ref, pad_runs_ref, tail_ref,
                     rt_ref, hp_ref, hs_ref, xs_ref, stage_ref, zero_ref, sem, psem, *, n_prompt_tiles):
    i = pl.program_id(0)
    last = pl.num_programs(0) - 1
    buf = i % 2

    def start_runs(tile, b):
        _for_each_run(tile, run_row_ref, lambda slot, row: pltpu.make_async_copy(
            stage_ref.at[b, pl.ds(slot, RUN_ROWS), :], xs_ref.at[pl.ds(row, RUN_ROWS), :], sem.at[b]).start())

    def wait_runs(tile, b):
        _for_each_run(tile, run_row_ref, lambda slot, row: pltpu.make_async_copy(
            stage_ref.at[b, pl.ds(slot, RUN_ROWS), :], xs_ref.at[pl.ds(row, RUN_ROWS), :], sem.at[b]).wait())

    @pl.when(i == 0)
    def _():
        zero_ref[...] = jnp.zeros_like(zero_ref)

        def tail(t, _):
            rows = pl.ds(pl.multiple_of(t * TM_E, TM_E), TM_E)
            cp = pltpu.make_async_copy(zero_ref, xs_ref.at[rows, :], psem)
            cp.start()
            cp.wait()
            return 0

        lax.fori_loop(tail_ref[0], tail_ref[1], tail, 0)

        def per_expert(e, _):
            def pad_copy(r):
                rows = pl.ds(pl.multiple_of(pad_start_ref[e] + r * RUN_ROWS, RUN_ROWS), RUN_ROWS)
                return pltpu.make_async_copy(zero_ref.at[pl.ds(0, RUN_ROWS), :], xs_ref.at[rows, :], psem)

            def start(r, _):
                pad_copy(r).start()
                return 0

            def wait(r, _):
                pad_copy(r).wait()
                return 0

            lax.fori_loop(0, pad_runs_ref[e], start, 0)
            lax.fori_loop(0, pad_runs_ref[e], wait, 0)
            return 0

        lax.fori_loop(0, N_EXPERTS, per_expert, 0)

    @pl.when(i >= 2)
    def _():
        wait_runs(i - 2, buf)

    rt = rt_ref[...]
    x = _unpack_bf16_pairs(jnp.where(i < n_prompt_tiles, hp_ref[...], hs_ref[...]))
    n_slots = stage_ref.shape[1]
    slot_id = lax.broadcasted_iota(jnp.int32, (n_slots, rt.shape[1]), 0).astype(F32)
    sel = jnp.zeros(slot_id.shape, F32)
    gsel = jnp.zeros(slot_id.shape, F32)
    for k in range(TOP_K):
        hit = slot_id == rt[_R_SLOT + k:_R_SLOT + k + 1, :]
        sel = jnp.where(hit, 1.0, sel)
        gsel = jnp.where(hit, rt[_R_GATE + k:_R_GATE + k + 1, :], gsel)
    rows = _dot(sel.astype(BF16), x)
    gate = jnp.sum(gsel, axis=-1, keepdims=True)
    packed = _pack_bf16_pairs(rows)
    stage_ref[buf, :, :packed.shape[1]] = packed
    stage_ref[buf, :, packed.shape[1]:] = pltpu.bitcast(jnp.broadcast_to(gate, (n_slots, LANES)), jnp.uint32)
    start_runs(i, buf)

    @pl.when(i == last)
    def _():
        @pl.when(i >= 1)
        def _():
            wait_runs(i - 1, 1 - buf)

        wait_runs(i, buf)


def _dispatch(h_p, h_s, route_t, run_row, pad_start, pad_runs, tail_tiles, n_rows):
    rp, dp = h_p.shape
    rs = h_s.shape[0]
    assert rp % TT == 0 and rs % TT == 0
    npt, nst = rp // TT, rs // TT
    width = dp + LANES
    grid_spec = pltpu.PrefetchScalarGridSpec(
        num_scalar_prefetch=4,
        grid=(npt + nst,),
        in_specs=[pl.BlockSpec((_R_ROWS, TT), lambda i, *_: (0, i)),
                  pl.BlockSpec((TT, dp), lambda i, *_: (jnp.minimum(i, npt - 1), 0)),
                  pl.BlockSpec((TT, dp), lambda i, *_: (jnp.maximum(i - npt, 0), 0))],
        out_specs=pl.BlockSpec(memory_space=pl.ANY),
        scratch_shapes=[pltpu.VMEM((2, N_SLOTS, width), jnp.uint32), pltpu.VMEM((TM_E, width), jnp.uint32),
                        pltpu.SemaphoreType.DMA((2,)), pltpu.SemaphoreType.DMA(())],
    )
    return pl.pallas_call(
        functools.partial(_dispatch_kernel, n_prompt_tiles=npt),
        out_shape=jax.ShapeDtypeStruct((n_rows, width), jnp.uint32),
        grid_spec=grid_spec,
        compiler_params=_params(("arbitrary",), has_side_effects=True),
        name="dispatch",
    )(run_row, pad_start, pad_runs, tail_tiles, route_t, h_p, h_s)


def _experts_kernel(te_ref, tf_ref, tu_ref, x_ref, w1_ref, b1g_ref, b1l_ref, w2_ref, b2_ref, perm_ref,
                    y_ref, w1s_ref, w2s_ref):
    i = pl.program_id(0)
    dff = w2_ref.shape[1]

    @pl.when(tf_ref[i] == 1)
    def _():
        perm = perm_ref[...]
        half = MXU_DIM // 2
        for c in range(w1_ref.shape[2] // MXU_DIM):
            blk = w1_ref[0, :, c * MXU_DIM:(c + 1) * MXU_DIM].astype(BF16)
            sp = _dot(blk, perm).astype(BF16)
            w1s_ref[:, c * half:(c + 1) * half] = sp[:, :half]
            w1s_ref[:, dff + c * half:dff + (c + 1) * half] = sp[:, half:]
        w2s_ref[...] = w2_ref[0].astype(BF16)

    @pl.when(tu_ref[i] == 1)
    def _():
        d = w2_ref.shape[2]
        x = _unpack_bf16_pairs(x_ref[:, :d // 2])
        gate = pltpu.bitcast(x_ref[:, d // 2:], F32)
        hdn = _dot(x, w1s_ref[...])
        glu = jnp.minimum(hdn[:, :dff] + b1g_ref[0], SWIGLU_LIMIT)
        lin = jnp.clip(hdn[:, dff:] + b1l_ref[0], -SWIGLU_LIMIT, SWIGLU_LIMIT)
        act = glu * _sigmoid(SWIGLU_ALPHA * glu) * (lin + 1.0)
        y = (_dot(act.astype(BF16), w2s_ref[...]) + b2_ref[0]) * jnp.concatenate([gate] * (d // LANES), axis=1)
        y_ref[...] = _pack_bf16_pairs(y.astype(BF16).astype(F32))

    @pl.when(tu_ref[i] == 0)
    def _():
        y_ref[...] = jnp.zeros_like(y_ref)


def _deinterleave_perm():
    j = lax.broadcasted_iota(jnp.int32, (MXU_DIM, MXU_DIM), 0)
    c = lax.broadcasted_iota(jnp.int32, (MXU_DIM, MXU_DIM), 1)
    half = MXU_DIM // 2
    src = jnp.where(c < half, 2 * c, 2 * (c - half) + 1)
    return (j == src).astype(BF16)


def _experts(xs, tile_expert, tile_first, tile_used, w1, b1g, b1l, w2, b2):
    width = xs.shape[1]
    n_tiles = tile_expert.shape[0]
    n_rows = n_tiles * TM_E
    ne, d, two_f = w1.shape
    dff = two_f // 2
    assert width == d // 2 + LANES
    grid_spec = pltpu.PrefetchScalarGridSpec(
        num_scalar_prefetch=3,
        grid=(n_tiles,),
        in_specs=[pl.BlockSpec((TM_E, width), lambda i, te, tf, tu: (i * tu[i], 0)),
                  pl.BlockSpec((1, d, two_f), lambda i, te, tf, tu: (te[i], 0, 0)),
                  pl.BlockSpec((1, 1, dff), lambda i, te, tf, tu: (te[i], 0, 0)),
                  pl.BlockSpec((1, 1, dff), lambda i, te, tf, tu: (te[i], 0, 0)),
                  pl.BlockSpec((1, dff, d), lambda i, te, tf, tu: (te[i], 0, 0)),
                  pl.BlockSpec((1, 1, d), lambda i, te, tf, tu: (te[i], 0, 0)),
                  pl.BlockSpec((MXU_DIM, MXU_DIM), lambda i, te, tf, tu: (0, 0))],
        out_specs=pl.BlockSpec((TM_E, d // 2), lambda i, te, tf, tu: (i, 0)),
        scratch_shapes=[pltpu.VMEM((d, two_f), BF16), pltpu.VMEM((dff, d), BF16)],
    )
    return pl.pallas_call(
        _experts_kernel,
        out_shape=jax.ShapeDtypeStruct((n_rows, d // 2), jnp.uint32),
        grid_spec=grid_spec,
        compiler_params=_params(("arbitrary",)),
        name="experts",
    )(tile_expert, tile_first, tile_used, xs, w1, b1g, b1l, w2, b2, _deinterleave_perm())


def _combine_kernel(run_row_ref, x1_ref, rt_ref, g2_ref, y_ref, o_ref, stage_ref, sem, *, tile_offset):
    i = pl.program_id(0)
    buf = i % 2

    def start_runs(step, b):
        _for_each_run(step + tile_offset, run_row_ref, lambda slot, row: pltpu.make_async_copy(
            y_ref.at[pl.ds(row, RUN_ROWS), :], stage_ref.at[b, pl.ds(slot, RUN_ROWS), :], sem.at[b]).start())

    def wait_runs(step, b):
        _for_each_run(step + tile_offset, run_row_ref, lambda slot, row: pltpu.make_async_copy(
            y_ref.at[pl.ds(row, RUN_ROWS), :], stage_ref.at[b, pl.ds(slot, RUN_ROWS), :], sem.at[b]).wait())

    @pl.when(i == 0)
    def _():
        stage_ref[...] = jnp.zeros_like(stage_ref)
        start_runs(0, 0)

    @pl.when(i + 1 < pl.num_programs(0))
    def _():
        start_runs(i + 1, 1 - buf)

    wait_runs(i, buf)

    rt = rt_ref[...]
    tt = rt.shape[0]
    slot_id = lax.broadcasted_iota(jnp.int32, (tt, stage_ref.shape[1]), 1).astype(F32)
    sel = jnp.zeros(slot_id.shape, F32)
    for k in range(TOP_K):
        sel = jnp.where(slot_id == rt[:, _R_SLOT + k:_R_SLOT + k + 1], 1.0, sel)
    sel = sel.astype(BF16)
    o_ref[...] = x1_ref[...] + g2_ref[...] * _dot(sel, _unpack_bf16_pairs(stage_ref[buf]))


def _combine(x1, route, run_row, g2, seq, y_rows, row_offset):
    r, d = x1.shape
    assert r % TT == 0 and row_offset % TT == 0
    off = row_offset // TT
    g2_o, rpm = _mod_operand(g2, seq, TT)
    if rpm is None:
        g2_spec = pl.BlockSpec((TT, d), lambda i, *_: (i, 0))
    else:
        g2_spec = pl.BlockSpec((None, 1, d), lambda i, *_: ((i * TT) // rpm, 0, 0))
    grid_spec = pltpu.PrefetchScalarGridSpec(
        num_scalar_prefetch=1,
        grid=(r // TT,),
        in_specs=[pl.BlockSpec((TT, d), lambda i, *_: (i, 0)),
                  pl.BlockSpec((TT, LANES), lambda i, *_: (i + off, 0)),
                  g2_spec,
                  pl.BlockSpec(memory_space=pl.ANY)],
        out_specs=pl.BlockSpec((TT, d), lambda i, *_: (i, 0)),
        scratch_shapes=[pltpu.VMEM((2, N_SLOTS, d // 2), jnp.uint32), pltpu.SemaphoreType.DMA((2,))],
    )
    return pl.pallas_call(
        functools.partial(_combine_kernel, tile_offset=off),
        out_shape=jax.ShapeDtypeStruct((r, d), F32),
        grid_spec=grid_spec,
        compiler_params=_params(("arbitrary",)),
        name="combine",
    )(run_row, x1, route, g2_o, y_rows)


def _gmlp_tables(w_s, b_s, seq_s):
    g = w_s.shape[0]
    tril = jnp.tril(w_s)
    short = jnp.tril(w_s[:, :seq_s, :seq_s])
    rep = CHUNK_B // seq_s
    blockdiag = jnp.zeros_like(w_s)
    for r in range(rep):
        blockdiag = blockdiag.at[:, r * seq_s:(r + 1) * seq_s, r * seq_s:(r + 1) * seq_s].set(short)
    ws_all = jnp.stack([tril, blockdiag]).astype(BF16)
    bs_full = jnp.repeat(b_s.T, HEAD_DIM, axis=1)
    bs_short = jnp.tile(jnp.repeat(b_s[:, :seq_s].T, HEAD_DIM, axis=1), (rep, 1))
    return ws_all, jnp.stack([bs_full, bs_short])


def _layer(xp, xs, cache_k, cache_v, cp, cs, p):
    bp, tp, d = xp.shape
    bs_, ts, _ = xs.shape
    assert tp % CHUNK_B == 0 and CHUNK_B % ts == 0
    rp, rs = bp * tp, bs_ * ts
    n_tok = rp + rs

    mod = _ada(jnp.concatenate([cp, cs], axis=0), p['w_ada'], p['b_ada'])
    mods = [mod[:, j * d:(j + 1) * d] for j in range(6)]
    mp = [m[:bp] for m in mods]
    msm = [m[bp:] for m in mods]

    tile_h = lambda g: jnp.tile(g, N_HEADS).reshape(1, D_HEADS)
    bd = jnp.kron(jnp.eye(MXU_DIM // HEAD_DIM, dtype=F32),
                  jnp.full((HEAD_DIM, HEAD_DIM), 1.0 / HEAD_DIM, F32)).astype(BF16)
    w_in_b = p['w_in'].astype(BF16)
    n1g = p['norm1_g'].reshape(1, d)
    common = (n1g, w_in_b, tile_h(p['q_norm_g']), tile_h(p['k_norm_g']), tile_h(p['v_norm_g']), bd)
    q_p, k_p, kb_p, v_p, vb_p, u_p, vn_p, sga_p, sgb_p = _inproj(
        xp.reshape(rp, d), mp[0], mp[1], tp, *common, want_vn32=False, kv_transposed=True)
    q_s, k_s, kb_s, v_s, vb_s, u_s, vn_s, sga_s, sgb_s, vn32_s = _inproj(
        xs.reshape(rs, d), msm[0], msm[1], ts, *common, want_vn32=True, kv_transposed=False)

    ya_p = _attn_prompt(q_p, kb_p, vb_p, bp, tp)
    ya_s = _attn_sample(q_s, kb_s, vb_s, cache_k, cache_v, bs_, ts)

    ws_all, bs_all = _gmlp_tables(p['w_s'], p['b_s'], ts)
    wr = p['w_router']
    wr_hi = wr.astype(BF16)
    wr_lo = (wr - wr_hi.astype(F32)).astype(BF16)
    post_w = (p['w_pa'].astype(BF16), p['w_pb'].astype(BF16), p['w_o'].astype(BF16), p['norm2_g'].reshape(1, d),
              jnp.concatenate([wr_hi, wr_lo], axis=1), p['b_router'].reshape(N_EXPERTS, 1))
    x1_p, h2_p, lg_p = _post(xp.reshape(rp, d), ya_p, u_p, vn_p, sga_p, sgb_p, mp[2], mp[3], mp[4], tp,
                             ws_all, bs_all, 0, *post_w)
    x1_s, h2_s, lg_s = _post(xs.reshape(rs, d), ya_s, u_s, vn_s, sga_s, sgb_s, msm[2], msm[3], msm[4], ts,
                             ws_all, bs_all, 1, *post_w)

    route, route_t, table, counts = _route(lg_p, lg_s)

    counts = counts[:, 0].astype(jnp.int32)
    padded = (counts + TM_E - 1) // TM_E * TM_E
    pend = jnp.cumsum(padded)
    pstart = pend - padded
    table = jnp.swapaxes(table.reshape(-1, N_EXPERTS, LANES)[:, :, :3], 1, 2).astype(jnp.int32)
    n_token_tiles = n_tok // TT
    runs = (table[:, _T_COUNT] + RUN_ROWS - 1) // RUN_ROWS
    runs_end = jnp.cumsum(runs, axis=1)
    j = jnp.arange(RUNS_PER_TILE, dtype=jnp.int32)
    e_of = jnp.minimum(jnp.sum((runs_end[:, None, :] <= j[None, :, None]).astype(jnp.int32), axis=2), N_EXPERTS - 1)
    pick = (e_of[:, :, None] == jnp.arange(N_EXPERTS, dtype=jnp.int32)[None, None, :]).astype(jnp.int32)
    first_run = jnp.sum(pick * (runs_end - runs)[:, None, :], axis=2)
    first_row = jnp.sum(pick * (table[:, _T_BASE] + pstart[None, :])[:, None, :], axis=2)
    run_row = first_row + (j[None, :] - first_run) * RUN_ROWS
    is_run = j[None, :] < runs_end[:, -1:]
    n_tiles = -(-(n_tok * TOP_K + n_token_tiles * N_EXPERTS * (RUN_ROWS - 1)) // TM_E) + N_EXPERTS
    tile_start = jnp.arange(n_tiles, dtype=jnp.int32) * TM_E
    tile_used = (tile_start < pend[-1]).astype(jnp.int32)
    last_expert = jnp.sum((pend <= pend[-1] - 1).astype(jnp.int32))
    tile_expert = jnp.minimum(jnp.sum((pend[None, :] <= tile_start[:, None]).astype(jnp.int32), axis=1), last_expert)
    tile_first = jnp.concatenate([jnp.ones((1,), jnp.int32),
                                  (tile_expert[1:] != tile_expert[:-1]).astype(jnp.int32)])

    spare_tiles = -(-2 * N_SLOTS // TM_E)
    spare_rows = j[None, :] * RUN_ROWS
    parity = (jnp.arange(n_token_tiles, dtype=jnp.int32) % 2)[:, None]
    rows_out = jnp.where(is_run, run_row, n_tiles * TM_E + parity * N_SLOTS + spare_rows).reshape(-1)
    rows_in = jnp.where(is_run, run_row, spare_rows).reshape(-1)
    tail_tiles = jnp.stack([pend[-1] // TM_E, jnp.int32(n_tiles + spare_tiles)]).astype(jnp.int32)
    pad_runs = (padded - counts) // RUN_ROWS
    xs_rows = _dispatch(h2_p, h2_s, route_t, rows_out, pstart + counts, pad_runs, tail_tiles,
                        (n_tiles + spare_tiles) * TM_E)
    b1 = p['b1']
    ne = b1.shape[0]
    y_rows = _experts(xs_rows, tile_expert, tile_first, tile_used, p['w1'],
                      b1[:, 0::2].reshape(ne, 1, -1), b1[:, 1::2].reshape(ne, 1, -1),
                      p['w2'], p['b2'].reshape(ne, 1, d))
    out_p = _combine(x1_p, route, rows_in, mp[5], tp, y_rows, 0)
    out_s = _combine(x1_s, route, rows_in, msm[5], ts, y_rows, rp)

    shp = lambda a, b, t: a.reshape(b, t, N_HEADS, HEAD_DIM)
    unt = lambda a: jnp.transpose(a.reshape(bp, N_HEADS, HEAD_DIM, tp), (0, 3, 1, 2))
    return (out_p.reshape(bp, tp, d), out_s.reshape(bs_, ts, d),
            unt(k_p), unt(v_p), shp(k_s, bs_, ts), shp(v_s, bs_, ts), shp(vn32_s, bs_, ts))


def kernel(x_prompt, x_sample, cache_k, cache_v, c_prompt, c_sample, norm1_g, norm2_g, w_ada, b_ada, w_in,
           q_norm_g, k_norm_g, v_norm_g, w_s, b_s, w_pa, w_pb, w_o, w_router, b_router, w1, b1, w2, b2):
    depth = w_in.shape[0]
    xp, xs = x_prompt, x_sample
    kp, vp, ks, vs, vbs = [], [], [], [], []
    for l in range(depth):
        p = {'norm1_g': norm1_g[l], 'norm2_g': norm2_g[l], 'w_ada': w_ada[l], 'b_ada': b_ada[l],
             'w_in': w_in[l], 'q_norm_g': q_norm_g[l], 'k_norm_g': k_norm_g[l], 'v_norm_g': v_norm_g[l],
             'w_s': w_s[l], 'b_s': b_s[l], 'w_pa': w_pa[l], 'w_pb': w_pb[l], 'w_o': w_o[l],
             'w_router': w_router[l], 'b_router': b_router[l], 'w1': w1[l], 'b1': b1[l],
             'w2': w2[l], 'b2': b2[l]}
        xp, xs, k_p, v_p, k_s, v_s, vn_s = _layer(xp, xs, cache_k[l], cache_v[l], c_prompt, c_sample, p)
        kp.append(k_p); vp.append(v_p); ks.append(k_s); vs.append(v_s); vbs.append(vn_s)
    return (xp, xs, jnp.stack(kp), jnp.stack(vp), jnp.stack(ks), jnp.stack(vs), jnp.stack(vbs))
```

```python
import functools

import jax
import jax.numpy as jnp
from jax import lax
from jax.experimental import pallas as pl
from jax.experimental.pallas import tpu as pltpu

F32 = jnp.float32
BF16 = jnp.bfloat16

LANES = 128
MXU_DIM = 256
VMEM_LIMIT = 56 * 1024 * 1024

HEAD_DIM = 64
N_HEADS = 8
D_HEADS = N_HEADS * HEAD_DIM
PAIRS = D_HEADS // LANES
CHUNK_B = 128
N_EXPERTS = 32
TOP_K = 4
RMS_EPS = 1e-6
SWIGLU_ALPHA = 1.702
SWIGLU_LIMIT = 7.0

TM_IN = 512
TQ = 512
TK = 256
SUB_Q = 128
LOOK_BACK = 256
WINDOW = SUB_Q + LOOK_BACK
TM_POST = 256
TT = 256
TM_E = 512
RUN_ROWS = 8
N_SLOTS = TT * TOP_K + N_EXPERTS * RUN_ROWS


def _params(sem, **kw):
    return pltpu.CompilerParams(dimension_semantics=sem, vmem_limit_bytes=VMEM_LIMIT, **kw)


def _dot(a, b):
    return jnp.dot(a, b, preferred_element_type=F32)


def _dot_nt(a, b):
    return lax.dot_general(a, b, (((1,), (1,)), ((), ())), preferred_element_type=F32)


def _gelu(x):
    return jax.nn.gelu(x)


def _sigmoid(x):
    return 1.0 / (1.0 + jnp.exp(-x))


def _ada_kernel(c_ref, w_ref, b_ref, o_ref):
    c = c_ref[...]
    s = c * _sigmoid(c)
    o_ref[...] = jnp.dot(s, w_ref[...], preferred_element_type=F32,
                         precision=lax.Precision.HIGHEST) + b_ref[...]


def _ada(c, w_ada, b_ada):
    nb, d = c.shape
    n = w_ada.shape[1]
    tn = 1024
    return pl.pallas_call(
        _ada_kernel,
        out_shape=jax.ShapeDtypeStruct((nb, n), F32),
        grid=(n // tn,),
        in_specs=[pl.BlockSpec((nb, d), lambda j: (0, 0)),
                  pl.BlockSpec((d, tn), lambda j: (0, j)),
                  pl.BlockSpec((1, tn), lambda j: (0, j))],
        out_specs=pl.BlockSpec((nb, tn), lambda j: (0, j)),
        compiler_params=_params(("parallel",)),
        name="ada",
    )(c, w_ada, b_ada.reshape(1, n))


def _head_mean_sq(p, bd):
    sq = (p * p).astype(BF16)
    return jnp.concatenate([_dot(sq[:, :MXU_DIM], bd), _dot(sq[:, MXU_DIM:], bd)], axis=-1)


def _inproj_kernel(x_ref, sh_ref, sc_ref, g_ref, w_ref, qg_ref, kg_ref, vg_ref, bd_ref,
                   q_ref, k32_ref, kb_ref, v32_ref, vb_ref, u_ref, vnb_ref, sga_ref, sgb_ref, *vn32_ref,
                   kv_transposed):
    kv_out = (lambda a: a.T) if kv_transposed else (lambda a: a)
    x = x_ref[...]
    ms = jnp.mean(x * x, axis=-1, keepdims=True)
    xn = x * lax.rsqrt(ms + RMS_EPS) * g_ref[...]
    h = (xn * (1.0 + sc_ref[...]) + sh_ref[...]).astype(BF16)
    bd = bd_ref[...]
    d = x.shape[1]

    def sec(a, n):
        return _dot(h, w_ref[:, a:a + n])

    o = 0
    pq = sec(o, D_HEADS); o += D_HEADS
    q = pq * lax.rsqrt(_head_mean_sq(pq, bd) + RMS_EPS) * qg_ref[...]
    q_ref[...] = (q * (HEAD_DIM ** -0.5)).astype(BF16)
    pk = sec(o, D_HEADS); o += D_HEADS
    k = pk * lax.rsqrt(_head_mean_sq(pk, bd) + RMS_EPS) * kg_ref[...]
    k32_ref[...] = kv_out(k)
    kb_ref[...] = k.astype(BF16)
    pv = sec(o, D_HEADS); o += D_HEADS
    v32_ref[...] = kv_out(pv)
    vb_ref[...] = pv.astype(BF16)
    pu = sec(o, D_HEADS); o += D_HEADS
    u_ref[...] = _gelu(pu).astype(BF16)
    pvb = _gelu(sec(o, D_HEADS)); o += D_HEADS
    vn = pvb * lax.rsqrt(_head_mean_sq(pvb, bd) + RMS_EPS) * vg_ref[...]
    vnb_ref[...] = vn.astype(BF16)
    if vn32_ref:
        vn32_ref[0][...] = vn
    sga_ref[...] = _sigmoid(sec(o, d)).astype(BF16); o += d
    sgb_ref[...] = _sigmoid(sec(o, d)).astype(BF16)


def _mod_spec(tm, arr, rows_per_mod):
    if rows_per_mod is None:
        return pl.BlockSpec((tm, arr.shape[-1]), lambda i: (i, 0))
    return pl.BlockSpec((None, 1, arr.shape[-1]), lambda i: ((i * tm) // rows_per_mod, 0, 0))


def _mod_operand(m, seq, tm):
    if seq % tm == 0:
        return m[:, None, :], seq
    return jnp.repeat(m, seq, axis=0), None


def _inproj(x, sh, sc, seq, g, w_in_b, qg, kg, vg, bd, want_vn32, kv_transposed):
    r, d = x.shape
    tm = min(TM_IN, r)
    assert r % tm == 0 and (not kv_transposed or seq % tm == 0)
    sh_o, rpm = _mod_operand(sh, seq, tm)
    sc_o, _ = _mod_operand(sc, seq, tm)
    ncol = w_in_b.shape[1]
    const = lambda i: (0, 0)
    row = lambda n: pl.BlockSpec((tm, n), lambda i: (i, 0))
    outs = [jax.ShapeDtypeStruct((r, D_HEADS), BF16),
            jax.ShapeDtypeStruct((r, D_HEADS), F32),
            jax.ShapeDtypeStruct((r, D_HEADS), BF16),
            jax.ShapeDtypeStruct((r, D_HEADS), F32),
            jax.ShapeDtypeStruct((r, D_HEADS), BF16),
            jax.ShapeDtypeStruct((r, D_HEADS), BF16),
            jax.ShapeDtypeStruct((r, D_HEADS), BF16),
            jax.ShapeDtypeStruct((r, d), BF16),
            jax.ShapeDtypeStruct((r, d), BF16)]
    out_specs = [row(D_HEADS)] * 7 + [row(d), row(d)]
    if kv_transposed:
        per_seq = seq // tm
        for j in (1, 3):
            outs[j] = jax.ShapeDtypeStruct((r // seq, D_HEADS, seq), F32)
            out_specs[j] = pl.BlockSpec((None, D_HEADS, tm), lambda i: (i // per_seq, 0, i % per_seq))
    if want_vn32:
        outs.append(jax.ShapeDtypeStruct((r, D_HEADS), F32))
        out_specs.append(row(D_HEADS))
    return pl.pallas_call(
        functools.partial(_inproj_kernel, kv_transposed=kv_transposed),
        out_shape=outs,
        grid=(r // tm,),
        in_specs=[row(d), _mod_spec(tm, sh_o, rpm), _mod_spec(tm, sc_o, rpm),
                  pl.BlockSpec((1, d), const),
                  pl.BlockSpec((d, ncol), const, pipeline_mode=pl.Buffered(1)),
                  pl.BlockSpec((1, D_HEADS), const), pl.BlockSpec((1, D_HEADS), const),
                  pl.BlockSpec((1, D_HEADS), const), pl.BlockSpec((MXU_DIM, MXU_DIM), const)],
        out_specs=out_specs,
        compiler_params=_params(("parallel",)),
        name="inproj",
    )(x, sh_o, sc_o, g, w_in_b, qg, kg, vg, bd)


LOG_WEIGHT_CUTOFF = -104.0
MASKED_SCORE = -1e30


def _sb_weights(z, u, c, mask):
    if mask is not None:
        z = jnp.where(mask, z, MASKED_SCORE)
    sp = jnp.maximum(z, 0.0) + jnp.log(1.0 + jnp.exp(-jnp.abs(z)))
    hi = sp.astype(BF16)
    lo = (sp - hi.astype(F32)).astype(BF16)
    tk, tu = z.shape[1], u.shape[0]
    old = tk - tu
    later = _dot(hi[:, old:], u) + _dot(lo[:, old:], u)
    total = jnp.sum(sp[:, old:], axis=-1, keepdims=True)
    if old:
        u_old = u[:old, :old]
        later = jnp.concatenate([_dot(hi[:, :old], u_old) + _dot(lo[:, :old], u_old) + total, later], axis=1)
        total = total + jnp.sum(sp[:, :old], axis=-1, keepdims=True)
    w = jnp.exp((z + c) - (sp + later))
    return w, c - total


def _pair_select(shape):
    return lax.broadcasted_iota(jnp.int32, shape, 1) < HEAD_DIM


def _causal_mask(m, tq, tk):
    row = lax.broadcasted_iota(jnp.int32, (m, tk), 0) & (tq - 1)
    return lax.broadcasted_iota(jnp.int32, (m, tk), 1) < row


def _stack_heads(q):
    first = _pair_select(q.shape)
    zero = jnp.zeros_like(q)
    return jnp.concatenate([jnp.where(first, q, zero), jnp.where(first, zero, q)], axis=0)


def _attn_prompt_kernel(q_ref, k_ref, v_ref, u_ref, o_ref, acc_ref, c_ref):
    i = pl.program_id(2)
    tq = q_ref.shape[0]
    nsub = tq // SUB_Q
    sub_rows = 2 * SUB_Q
    q = q_ref[...]
    u = u_ref[...]
    first = _pair_select((SUB_Q, LANES))
    m = nsub * sub_rows

    def pv_sub(wb, s, vs):
        w0 = wb[s * sub_rows:s * sub_rows + SUB_Q]
        w1 = wb[s * sub_rows + SUB_Q:(s + 1) * sub_rows]
        return jnp.where(first, _dot(w0, vs), _dot(w1, vs))

    qs = [_stack_heads(q[s * SUB_Q:(s + 1) * SUB_Q]) for s in range(nsub)]
    starts = [pl.multiple_of(jnp.maximum(i * tq + s * SUB_Q - LOOK_BACK, 0), SUB_Q) for s in range(nsub)]
    row = lax.broadcasted_iota(jnp.int32, (m, 1), 0)
    window_start = jnp.zeros((m, 1), jnp.int32) + starts[0]
    for s in range(1, nsub):
        window_start = jnp.where(row >= s * sub_rows, starts[s], window_start)
    q_pos = i * tq + (row // sub_rows) * SUB_Q + (row & (SUB_Q - 1))

    z = jnp.concatenate([_dot_nt(qs[s], k_ref[pl.ds(starts[s], WINDOW), :]) for s in range(nsub)], axis=0)
    col = lax.broadcasted_iota(jnp.int32, (m, WINDOW), 1)
    w, c = _sb_weights(z, u, jnp.zeros((m, 1), F32), window_start + col < q_pos)
    wb = w.astype(BF16)
    acc_ref[...] = jnp.concatenate([pv_sub(wb, s, v_ref[pl.ds(starts[s], WINDOW), :]) for s in range(nsub)], axis=0)
    c_ref[...] = c
    q_all = jnp.concatenate(qs, axis=0)
    col = lax.broadcasted_iota(jnp.int32, (m, TK), 1)

    def cond(state):
        j, cmax = state
        return jnp.logical_and(j >= 0, cmax > LOG_WEIGHT_CUTOFF)

    def body(state):
        j, _ = state
        start = pl.multiple_of(j * TK, TK)
        vs = v_ref[pl.ds(start, TK), :]
        w, c = _sb_weights(_dot_nt(q_all, k_ref[pl.ds(start, TK), :]), u, c_ref[...],
                           j * TK + col < window_start)
        wb = w.astype(BF16)
        acc_ref[...] += jnp.concatenate([pv_sub(wb, s, vs) for s in range(nsub)], axis=0)
        c_ref[...] = c
        return j - 1, jnp.max(c)

    j0 = lax.div(starts[nsub - 1] + (TK - 1), TK) - 1
    lax.while_loop(cond, body, (j0, jnp.max(c)))
    o_ref[...] = acc_ref[...].astype(o_ref.dtype)


def _upper_ones(n):
    return (lax.broadcasted_iota(jnp.int32, (n, n), 0) > lax.broadcasted_iota(jnp.int32, (n, n), 1)).astype(BF16)


def _attn_prompt(q, kb, vb, batch, seq):
    assert seq % TQ == 0 and TQ % TK == 0 and TQ % SUB_Q == 0 and seq >= WINDOW
    nq = seq // TQ
    return pl.pallas_call(
        _attn_prompt_kernel,
        out_shape=jax.ShapeDtypeStruct(q.shape, BF16),
        grid=(batch, PAIRS, nq),
        in_specs=[pl.BlockSpec((TQ, LANES), lambda b, p, i: (b * nq + i, p)),
                  pl.BlockSpec((seq, LANES), lambda b, p, i: (b, p)),
                  pl.BlockSpec((seq, LANES), lambda b, p, i: (b, p)),
                  pl.BlockSpec((TK, TK), lambda b, p, i: (0, 0))],
        out_specs=pl.BlockSpec((TQ, LANES), lambda b, p, i: (b * nq + i, p)),
        scratch_shapes=[pltpu.VMEM((TQ, LANES), F32), pltpu.VMEM((2 * TQ, 1), F32)],
        compiler_params=_params(("parallel", "parallel", "arbitrary")),
        name="attn_prompt",
    )(q, kb, vb, _upper_ones(TK))


def _attn_sample_kernel(q_ref, kn_ref, vn_ref, kc_ref, vc_ref, un_ref, u_ref, o_ref,
                        kbuf_ref, vbuf_ref, sem, acc_ref, c_ref):
    b = pl.program_id(0)
    tq = q_ref.shape[0]
    nblk = kc_ref.shape[3] // TK
    head = lambda a, h: a[:, h * HEAD_DIM:(h + 1) * HEAD_DIM]
    q = q_ref[...]
    qh = [head(q, h) for h in range(N_HEADS)]

    def copies(j, slot):
        rows = pl.ds(pl.multiple_of(j * TK, TK), TK)
        return (pltpu.make_async_copy(kc_ref.at[b, :, :, rows], kbuf_ref.at[slot], sem.at[0, slot]),
                pltpu.make_async_copy(vc_ref.at[b, :, :, rows], vbuf_ref.at[slot], sem.at[1, slot]))

    def start(j, slot):
        for cp in copies(j, slot):
            cp.start()

    def wait(j, slot):
        for cp in copies(j, slot):
            cp.wait()

    def sweep(score, apply, u, c, mask):
        z = jnp.concatenate([score(qh[h], h) for h in range(N_HEADS)], axis=0)
        w, c = _sb_weights(z, u, c, mask)
        wb = w.astype(BF16)
        pv = jnp.concatenate([apply(wb[h * tq:(h + 1) * tq], h) for h in range(N_HEADS)], axis=-1)
        return pv, c

    start(nblk - 1, 0)
    kn = kn_ref[...]
    vn = vn_ref[...]
    pv, c = sweep(lambda qq, h: _dot_nt(qq, head(kn, h)), lambda ww, h: _dot(ww, head(vn, h)), un_ref[...],
                  jnp.zeros((N_HEADS * tq, 1), F32), _causal_mask(N_HEADS * tq, tq, tq))
    acc_ref[...] = pv
    c_ref[...] = c
    u = u_ref[...]

    def cond(state):
        j, _, cmax = state
        return jnp.logical_and(j >= 0, cmax > LOG_WEIGHT_CUTOFF)

    def body(state):
        j, slot, _ = state
        wait(j, slot)

        @pl.when(j > 0)
        def _():
            start(j - 1, 1 - slot)

        pv, c = sweep(lambda qq, h: _dot(qq, kbuf_ref[slot, h].astype(BF16)),
                      lambda ww, h: _dot_nt(ww, vbuf_ref[slot, h].astype(BF16)), u, c_ref[...], None)
        acc_ref[...] += pv
        c_ref[...] = c
        return j - 1, 1 - slot, jnp.max(c)

    j, slot, _ = lax.while_loop(cond, body, (nblk - 1, 0, jnp.max(c)))

    @pl.when(j >= 0)
    def _():
        wait(j, slot)

    o_ref[...] = acc_ref[...].astype(o_ref.dtype)


def _attn_sample(q, kb, vb, cache_k, cache_v, batch, seq):
    past = cache_k.shape[1]
    assert past % TK == 0 and past >= TK
    cache_k = jnp.transpose(cache_k, (0, 2, 3, 1))
    cache_v = jnp.transpose(cache_v, (0, 2, 3, 1))
    row = pl.BlockSpec((seq, D_HEADS), lambda b: (b, 0))
    return pl.pallas_call(
        _attn_sample_kernel,
        out_shape=jax.ShapeDtypeStruct(q.shape, BF16),
        grid=(batch,),
        in_specs=[row, row, row,
                  pl.BlockSpec(memory_space=pl.ANY), pl.BlockSpec(memory_space=pl.ANY),
                  pl.BlockSpec((seq, seq), lambda b: (0, 0)),
                  pl.BlockSpec((TK, TK), lambda b: (0, 0))],
        out_specs=row,
        scratch_shapes=[pltpu.VMEM((2, N_HEADS, HEAD_DIM, TK), F32), pltpu.VMEM((2, N_HEADS, HEAD_DIM, TK), F32),
                        pltpu.SemaphoreType.DMA((2, 2)),
                        pltpu.VMEM((seq, D_HEADS), F32), pltpu.VMEM((N_HEADS * seq, 1), F32)],
        compiler_params=_params(("arbitrary",)),
        name="attn_sample",
    )(q, kb, vb, cache_k, cache_v, _upper_ones(seq), _upper_ones(TK))


def _pack_bf16_pairs(x):
    half = x.shape[1] // 2
    a = pltpu.bitcast(x[:, :half], jnp.uint32)
    b = pltpu.bitcast(x[:, half:], jnp.uint32)
    return a | (b >> 16)


def _unpack_bf16_pairs(p):
    a = pltpu.bitcast(p & jnp.uint32(0xFFFF0000), F32).astype(BF16)
    b = pltpu.bitcast(p << 16, F32).astype(BF16)
    return jnp.concatenate([a, b], axis=1)


def _post_kernel(x_ref, ya_ref, u_ref, vn_ref, sga_ref, sgb_ref, g1_ref, sh2_ref, sc2_ref,
                 ws_ref, bs_ref, wpa_ref, wpb_ref, wo_ref, n2g_ref, wr_ref, br_ref,
                 x1_ref, h2_ref, lg_ref):
    tm = x_ref.shape[0]
    nch = tm // CHUNK_B
    first = (lax.broadcasted_iota(jnp.int32, (CHUNK_B, nch * LANES), 1) & (LANES - 1)) < HEAD_DIM
    mixed = []
    for p in range(PAIRS):
        vnp = jnp.concatenate([vn_ref[c * CHUNK_B:(c + 1) * CHUNK_B, p * LANES:(p + 1) * LANES] for c in range(nch)],
                              axis=1)
        mixed.append(jnp.where(first, _dot(ws_ref[0, 2 * p], vnp), _dot(ws_ref[0, 2 * p + 1], vnp)))
    chunks = []
    for c in range(nch):
        s = jnp.concatenate([mp[:, c * LANES:(c + 1) * LANES] for mp in mixed], axis=-1) + bs_ref[0]
        chunks.append((u_ref[c * CHUNK_B:(c + 1) * CHUNK_B, :].astype(F32) * s).astype(BF16))
    yb = jnp.concatenate(chunks, axis=0)
    merged = (sga_ref[...].astype(F32) * _dot(ya_ref[...], wpa_ref[...])
              + sgb_ref[...].astype(F32) * _dot(yb, wpb_ref[...]))
    x1 = x_ref[...] + g1_ref[...] * _dot(merged.astype(BF16), wo_ref[...])
    x1_ref[...] = x1
    ms = jnp.mean(x1 * x1, axis=-1, keepdims=True)
    h2 = (x1 * lax.rsqrt(ms + RMS_EPS) * n2g_ref[...]) * (1.0 + sc2_ref[...]) + sh2_ref[...]
    hi = h2.astype(BF16)
    hf = hi.astype(F32)
    lo = (h2 - hf).astype(BF16)
    h2_ref[...] = _pack_bf16_pairs(hf)
    l2 = _dot(hi, wr_ref[...]) + _dot(lo, wr_ref[...])
    l2t = jnp.concatenate([l2, jnp.zeros((l2.shape[0], LANES - l2.shape[1]), F32)], axis=1).T
    lg_ref[...] = l2t[:N_EXPERTS] + l2t[N_EXPERTS:2 * N_EXPERTS] + br_ref[...]


def _post(x, ya, u, vn, sga, sgb, g1, sh2, sc2, seq, ws_all, bs_all, ws_idx, wpa, wpb, wo, n2g, wr, br):
    r, d = x.shape
    tm = min(TM_POST, r)
    assert r % tm == 0 and tm % CHUNK_B == 0
    g1_o, rpm = _mod_operand(g1, seq, tm)
    sh2_o, _ = _mod_operand(sh2, seq, tm)
    sc2_o, _ = _mod_operand(sc2, seq, tm)
    const = lambda i: (0, 0)
    row = lambda n: pl.BlockSpec((tm, n), lambda i: (i, 0))
    return pl.pallas_call(
        _post_kernel,
        out_shape=[jax.ShapeDtypeStruct((r, d), F32), jax.ShapeDtypeStruct((r, d // 2), jnp.uint32),
                   jax.ShapeDtypeStruct((N_EXPERTS, r), F32)],
        grid=(r // tm,),
        in_specs=[row(d), row(D_HEADS), row(D_HEADS), row(D_HEADS), row(d), row(d),
                  _mod_spec(tm, g1_o, rpm), _mod_spec(tm, sh2_o, rpm), _mod_spec(tm, sc2_o, rpm),
                  pl.BlockSpec((1, N_HEADS, CHUNK_B, CHUNK_B), lambda i: (ws_idx, 0, 0, 0)),
                  pl.BlockSpec((1, CHUNK_B, D_HEADS), lambda i: (ws_idx, 0, 0)),
                  pl.BlockSpec(wpa.shape, const), pl.BlockSpec(wpb.shape, const), pl.BlockSpec(wo.shape, const),
                  pl.BlockSpec((1, d), const), pl.BlockSpec(wr.shape, const), pl.BlockSpec((N_EXPERTS, 1), const)],
        out_specs=[row(d), row(d // 2), pl.BlockSpec((N_EXPERTS, tm), lambda i: (0, i))],
        compiler_params=_params(("parallel",)),
        name="post",
    )(x, ya, u, vn, sga, sgb, g1_o, sh2_o, sc2_o, ws_all, bs_all, wpa, wpb, wo, n2g, wr, br)


_R_IDX, _R_GATE, _R_SLOT = 0, TOP_K, 2 * TOP_K
_R_ROWS = 16
_T_COUNT, _T_SLOT, _T_BASE = 0, 1, 2


def _route_kernel(lp_ref, ls_ref, tri_ref, scan_ref, o_ref, ot_ref, tbl_ref, cnt_ref, carry_ref, *, n_prompt_tiles):
    i = pl.program_id(0)

    @pl.when(i == 0)
    def _():
        carry_ref[...] = jnp.zeros_like(carry_ref)

    l = jnp.where(i < n_prompt_tiles, lp_ref[...], ls_ref[...])
    ne, tt = l.shape
    eid = lax.broadcasted_iota(jnp.int32, l.shape, 0).astype(F32)
    cur = l
    vals, idxs = [], []
    for _ in range(TOP_K):
        m = jnp.max(cur, axis=0, keepdims=True)
        idx = jnp.min(jnp.where(cur == m, eid, float(ne)), axis=0, keepdims=True)
        vals.append(m)
        idxs.append(idx)
        cur = jnp.where(eid == idx, -jnp.inf, cur)
    es = [jnp.exp(v - vals[0]) for v in vals]
    den = es[0] + es[1] + es[2] + es[3]
    onehot = jnp.zeros(l.shape, F32)
    for idx in idxs:
        onehot = onehot + (eid == idx).astype(F32)
    within = _dot(onehot.astype(BF16), tri_ref[...])
    count = jnp.sum(onehot, axis=1, keepdims=True)
    runs = jnp.ceil(count * (1.0 / RUN_ROWS))
    first_slot = RUN_ROWS * _dot(scan_ref[...], jnp.broadcast_to(runs, (ne, LANES)).astype(BF16))[:, 0:1]
    slot_of = within + first_slot
    row = lax.broadcasted_iota(jnp.int32, (_R_ROWS, tt), 0)
    out = jnp.zeros((_R_ROWS, tt), F32)
    for k in range(TOP_K):
        slot = jnp.sum(jnp.where(eid == idxs[k], slot_of, 0.0), axis=0, keepdims=True)
        out = jnp.where(row == _R_IDX + k, idxs[k], out)
        out = jnp.where(row == _R_GATE + k, es[k] / den, out)
        out = jnp.where(row == _R_SLOT + k, slot, out)
    ot_ref[...] = out
    o_ref[...] = jnp.concatenate([out, jnp.zeros((LANES - _R_ROWS, tt), F32)], axis=0).T
    lane = lax.broadcasted_iota(jnp.int32, (ne, LANES), 1)
    tbl_ref[...] = jnp.where(lane == _T_COUNT, count, jnp.where(lane == _T_SLOT, first_slot, carry_ref[...]))
    carry_ref[...] += RUN_ROWS * runs
    cnt_ref[...] = jnp.broadcast_to(carry_ref[...], cnt_ref.shape)


def _route(lg_p, lg_s):
    rp, rs = lg_p.shape[1], lg_s.shape[1]
    assert rp % TT == 0 and rs % TT == 0
    npt, nst = rp // TT, rs // TT
    nt = npt + nst
    iota = lambda shape, ax: lax.broadcasted_iota(jnp.int32, shape, ax)
    tri = (iota((TT, TT), 0) < iota((TT, TT), 1)).astype(BF16)
    scan = (iota((N_EXPERTS, N_EXPERTS), 1) < iota((N_EXPERTS, N_EXPERTS), 0)).astype(BF16)
    return pl.pallas_call(
        functools.partial(_route_kernel, n_prompt_tiles=npt),
        out_shape=[jax.ShapeDtypeStruct((rp + rs, LANES), F32), jax.ShapeDtypeStruct((_R_ROWS, rp + rs), F32),
                   jax.ShapeDtypeStruct((nt * N_EXPERTS, LANES), F32), jax.ShapeDtypeStruct((N_EXPERTS, LANES), F32)],
        grid=(nt,),
        in_specs=[pl.BlockSpec((N_EXPERTS, TT), lambda i: (0, jnp.minimum(i, npt - 1))),
                  pl.BlockSpec((N_EXPERTS, TT), lambda i: (0, jnp.maximum(i - npt, 0))),
                  pl.BlockSpec((TT, TT), lambda i: (0, 0)),
                  pl.BlockSpec((N_EXPERTS, N_EXPERTS), lambda i: (0, 0))],
        out_specs=[pl.BlockSpec((TT, LANES), lambda i: (i, 0)), pl.BlockSpec((_R_ROWS, TT), lambda i: (0, i)),
                   pl.BlockSpec((N_EXPERTS, LANES), lambda i: (i, 0)),
                   pl.BlockSpec((N_EXPERTS, LANES), lambda i: (0, 0))],
        scratch_shapes=[pltpu.VMEM((N_EXPERTS, 1), F32)],
        compiler_params=_params(("arbitrary",)),
        name="route",
    )(lg_p, lg_s, tri, scan)


RUNS_PER_TILE = N_SLOTS // RUN_ROWS
RUN_UNROLL = 8


def _for_each_run(tile, run_row_ref, fn):
    def group(g, _):
        for u in range(RUN_UNROLL):
            j = g * RUN_UNROLL + u
            row = run_row_ref[tile * RUNS_PER_TILE + j]
            fn(pl.multiple_of(j * RUN_ROWS, RUN_ROWS), pl.multiple_of(row, RUN_ROWS))
        return 0

    lax.fori_loop(0, RUNS_PER_TILE // RUN_UNROLL, group, 0)


def _dispatch_kernel(run_row_ref, pad_start_ref, pad_runs_ref, tail_ref,
                     rt_ref, hp_ref, hs_ref, xs_ref, stage_ref, zero_ref, sem, psem, *, n_prompt_tiles):
    i = pl.program_id(0)
    last = pl.num_programs(0) - 1
    buf = i % 2

    def start_runs(tile, b):
        _for_each_run(tile, run_row_ref, lambda slot, row: pltpu.make_async_copy(
            stage_ref.at[b, pl.ds(slot, RUN_ROWS), :], xs_ref.at[pl.ds(row, RUN_ROWS), :], sem.at[b]).start())

    def wait_runs(tile, b):
        _for_each_run(tile, run_row_ref, lambda slot, row: pltpu.make_async_copy(
            stage_ref.at[b, pl.ds(slot, RUN_ROWS), :], xs_ref.at[pl.ds(row, RUN_ROWS), :], sem.at[b]).wait())

    @pl.when(i == 0)
    def _():
        zero_ref[...] = jnp.zeros_like(zero_ref)

        def tail(t, _):
            rows = pl.ds(pl.multiple_of(t * TM_E, TM_E), TM_E)
            cp = pltpu.make_async_copy(zero_ref, xs_ref.at[rows, :], psem)
            cp.start()
            cp.wait()
            return 0

        lax.fori_loop(tail_ref[0], tail_ref[1], tail, 0)

        def per_expert(e, _):
            def pad_copy(r):
                rows = pl.ds(pl.multiple_of(pad_start_ref[e] + r * RUN_ROWS, RUN_ROWS), RUN_ROWS)
                return pltpu.make_async_copy(zero_ref.at[pl.ds(0, RUN_ROWS), :], xs_ref.at[rows, :], psem)

            def start(r, _):
                pad_copy(r).start()
                return 0

            def wait(r, _):
                pad_copy(r).wait()
                return 0

            lax.fori_loop(0, pad_runs_ref[e], start, 0)
            lax.fori_loop(0, pad_runs_ref[e], wait, 0)
            return 0

        lax.fori_loop(0, N_EXPERTS, per_expert, 0)

    @pl.when(i >= 2)
    def _():
        wait_runs(i - 2, buf)

    rt = rt_ref[...]
    x = _unpack_bf16_pairs(jnp.where(i < n_prompt_tiles, hp_ref[...], hs_ref[...]))
    n_slots = stage_ref.shape[1]
    slot_id = lax.broadcasted_iota(jnp.int32, (n_slots, rt.shape[1]), 0).astype(F32)
    sel = jnp.zeros(slot_id.shape, F32)
    gsel = jnp.zeros(slot_id.shape, F32)
    for k in range(TOP_K):
        hit = slot_id == rt[_R_SLOT + k:_R_SLOT + k + 1, :]
        sel = jnp.where(hit, 1.0, sel)
        gsel = jnp.where(hit, rt[_R_GATE + k:_R_GATE + k + 1, :], gsel)
    rows = _dot(sel.astype(BF16), x)
    gate = jnp.sum(gsel, axis=-1, keepdims=True)
    packed = _pack_bf16_pairs(rows)
    stage_ref[buf, :, :packed.shape[1]] = packed
    stage_ref[buf, :, packed.shape[1]:] = pltpu.bitcast(jnp.broadcast_to(gate, (n_slots, LANES)), jnp.uint32)
    start_runs(i, buf)

    @pl.when(i == last)
    def _():
        @pl.when(i >= 1)
        def _():
            wait_runs(i - 1, 1 - buf)

        wait_runs(i, buf)


def _dispatch(h_p, h_s, route_t, run_row, pad_start, pad_runs, tail_tiles, n_rows):
    rp, dp = h_p.shape
    rs = h_s.shape[0]
    assert rp % TT == 0 and rs % TT == 0
    npt, nst = rp // TT, rs // TT
    width = dp + LANES
    grid_spec = pltpu.PrefetchScalarGridSpec(
        num_scalar_prefetch=4,
        grid=(npt + nst,),
        in_specs=[pl.BlockSpec((_R_ROWS, TT), lambda i, *_: (0, i)),
                  pl.BlockSpec((TT, dp), lambda i, *_: (jnp.minimum(i, npt - 1), 0)),
                  pl.BlockSpec((TT, dp), lambda i, *_: (jnp.maximum(i - npt, 0), 0))],
        out_specs=pl.BlockSpec(memory_space=pl.ANY),
        scratch_shapes=[pltpu.VMEM((2, N_SLOTS, width), jnp.uint32), pltpu.VMEM((TM_E, width), jnp.uint32),
                        pltpu.SemaphoreType.DMA((2,)), pltpu.SemaphoreType.DMA(())],
    )
    return pl.pallas_call(
        functools.partial(_dispatch_kernel, n_prompt_tiles=npt),
        out_shape=jax.ShapeDtypeStruct((n_rows, width), jnp.uint32),
        grid_spec=grid_spec,
        compiler_params=_params(("arbitrary",), has_side_effects=True),
        name="dispatch",
    )(run_row, pad_start, pad_runs, tail_tiles, route_t, h_p, h_s)


def _experts_kernel(te_ref, tf_ref, tu_ref, x_ref, w1_ref, b1g_ref, b1l_ref, w2_ref, b2_ref, perm_ref,
                    y_ref, w1s_ref, w2s_ref):
    i = pl.program_id(0)
    dff = w2_ref.shape[1]

    @pl.when(tf_ref[i] == 1)
    def _():
        perm = perm_ref[...]
        half = MXU_DIM // 2
        for c in range(w1_ref.shape[2] // MXU_DIM):
            blk = w1_ref[0, :, c * MXU_DIM:(c + 1) * MXU_DIM].astype(BF16)
            sp = _dot(blk, perm).astype(BF16)
            w1s_ref[:, c * half:(c + 1) * half] = sp[:, :half]
            w1s_ref[:, dff + c * half:dff + (c + 1) * half] = sp[:, half:]
        w2s_ref[...] = w2_ref[0].astype(BF16)

    @pl.when(tu_ref[i] == 1)
    def _():
        d = w2_ref.shape[2]
        x = _unpack_bf16_pairs(x_ref[:, :d // 2])
        gate = pltpu.bitcast(x_ref[:, d // 2:], F32)
        hdn = _dot(x, w1s_ref[...])
        glu = jnp.minimum(hdn[:, :dff] + b1g_ref[0], SWIGLU_LIMIT)
        lin = jnp.clip(hdn[:, dff:] + b1l_ref[0], -SWIGLU_LIMIT, SWIGLU_LIMIT)
        act = glu * _sigmoid(SWIGLU_ALPHA * glu) * (lin + 1.0)
        y = (_dot(act.astype(BF16), w2s_ref[...]) + b2_ref[0]) * jnp.concatenate([gate] * (d // LANES), axis=1)
        y_ref[...] = _pack_bf16_pairs(y.astype(BF16).astype(F32))

    @pl.when(tu_ref[i] == 0)
    def _():
        y_ref[...] = jnp.zeros_like(y_ref)


def _deinterleave_perm():
    j = lax.broadcasted_iota(jnp.int32, (MXU_DIM, MXU_DIM), 0)
    c = lax.broadcasted_iota(jnp.int32, (MXU_DIM, MXU_DIM), 1)
    half = MXU_DIM // 2
    src = jnp.where(c < half, 2 * c, 2 * (c - half) + 1)
    return (j == src).astype(BF16)


def _experts(xs, tile_expert, tile_first, tile_used, w1, b1g, b1l, w2, b2):
    width = xs.shape[1]
    n_tiles = tile_expert.shape[0]
    n_rows = n_tiles * TM_E
    ne, d, two_f = w1.shape
    dff = two_f // 2
    assert width == d // 2 + LANES
    grid_spec = pltpu.PrefetchScalarGridSpec(
        num_scalar_prefetch=3,
        grid=(n_tiles,),
        in_specs=[pl.BlockSpec((TM_E, width), lambda i, te, tf, tu: (i * tu[i], 0)),
                  pl.BlockSpec((1, d, two_f), lambda i, te, tf, tu: (te[i], 0, 0)),
                  pl.BlockSpec((1, 1, dff), lambda i, te, tf, tu: (te[i], 0, 0)),
                  pl.BlockSpec((1, 1, dff), lambda i, te, tf, tu: (te[i], 0, 0)),
                  pl.BlockSpec((1, dff, d), lambda i, te, tf, tu: (te[i], 0, 0)),
                  pl.BlockSpec((1, 1, d), lambda i, te, tf, tu: (te[i], 0, 0)),
                  pl.BlockSpec((MXU_DIM, MXU_DIM), lambda i, te, tf, tu: (0, 0))],
        out_specs=pl.BlockSpec((TM_E, d // 2), lambda i, te, tf, tu: (i, 0)),
        scratch_shapes=[pltpu.VMEM((d, two_f), BF16), pltpu.VMEM((dff, d), BF16)],
    )
    return pl.pallas_call(
        _experts_kernel,
        out_shape=jax.ShapeDtypeStruct((n_rows, d // 2), jnp.uint32),
        grid_spec=grid_spec,
        compiler_params=_params(("arbitrary",)),
        name="experts",
    )(tile_expert, tile_first, tile_used, xs, w1, b1g, b1l, w2, b2, _deinterleave_perm())


def _combine_kernel(run_row_ref, x1_ref, rt_ref, g2_ref, y_ref, o_ref, stage_ref, sem, *, tile_offset):
    i = pl.program_id(0)
    buf = i % 2

    def start_runs(step, b):
        _for_each_run(step + tile_offset, run_row_ref, lambda slot, row: pltpu.make_async_copy(
            y_ref.at[pl.ds(row, RUN_ROWS), :], stage_ref.at[b, pl.ds(slot, RUN_ROWS), :], sem.at[b]).start())

    def wait_runs(step, b):
        _for_each_run(step + tile_offset, run_row_ref, lambda slot, row: pltpu.make_async_copy(
            y_ref.at[pl.ds(row, RUN_ROWS), :], stage_ref.at[b, pl.ds(slot, RUN_ROWS), :], sem.at[b]).wait())

    @pl.when(i == 0)
    def _():
        stage_ref[...] = jnp.zeros_like(stage_ref)
        start_runs(0, 0)

    @pl.when(i + 1 < pl.num_programs(0))
    def _():
        start_runs(i + 1, 1 - buf)

    wait_runs(i, buf)

    rt = rt_ref[...]
    tt = rt.shape[0]
    slot_id = lax.broadcasted_iota(jnp.int32, (tt, stage_ref.shape[1]), 1).astype(F32)
    sel = jnp.zeros(slot_id.shape, F32)
    for k in range(TOP_K):
        sel = jnp.where(slot_id == rt[:, _R_SLOT + k:_R_SLOT + k + 1], 1.0, sel)
    sel = sel.astype(BF16)
    o_ref[...] = x1_ref[...] + g2_ref[...] * _dot(sel, _unpack_bf16_pairs(stage_ref[buf]))


def _combine(x1, route, run_row, g2, seq, y_rows, row_offset):
    r, d = x1.shape
    assert r % TT == 0 and row_offset % TT == 0
    off = row_offset // TT
    g2_o, rpm = _mod_operand(g2, seq, TT)
    if rpm is None:
        g2_spec = pl.BlockSpec((TT, d), lambda i, *_: (i, 0))
    else:
        g2_spec = pl.BlockSpec((None, 1, d), lambda i, *_: ((i * TT) // rpm, 0, 0))
    grid_spec = pltpu.PrefetchScalarGridSpec(
        num_scalar_prefetch=1,
        grid=(r // TT,),
        in_specs=[pl.BlockSpec((TT, d), lambda i, *_: (i, 0)),
                  pl.BlockSpec((TT, LANES), lambda i, *_: (i + off, 0)),
                  g2_spec,
                  pl.BlockSpec(memory_space=pl.ANY)],
        out_specs=pl.BlockSpec((TT, d), lambda i, *_: (i, 0)),
        scratch_shapes=[pltpu.VMEM((2, N_SLOTS, d // 2), jnp.uint32), pltpu.SemaphoreType.DMA((2,))],
    )
    return pl.pallas_call(
        functools.partial(_combine_kernel, tile_offset=off),
        out_shape=jax.ShapeDtypeStruct((r, d), F32),
        grid_spec=grid_spec,
        compiler_params=_params(("arbitrary",)),
        name="combine",
    )(run_row, x1, route, g2_o, y_rows)


def _gmlp_tables(w_s, b_s, seq_s):
    g = w_s.shape[0]
    tril = jnp.tril(w_s)
    short = jnp.tril(w_s[:, :seq_s, :seq_s])
    rep = CHUNK_B // seq_s
    blockdiag = jnp.zeros_like(w_s)
    for r in range(rep):
        blockdiag = blockdiag.at[:, r * seq_s:(r + 1) * seq_s, r * seq_s:(r + 1) * seq_s].set(short)
    ws_all = jnp.stack([tril, blockdiag]).astype(BF16)
    bs_full = jnp.repeat(b_s.T, HEAD_DIM, axis=1)
    bs_short = jnp.tile(jnp.repeat(b_s[:, :seq_s].T, HEAD_DIM, axis=1), (rep, 1))
    return ws_all, jnp.stack([bs_full, bs_short])


def _layer(xp, xs, cache_k, cache_v, cp, cs, p):
    bp, tp, d = xp.shape
    bs_, ts, _ = xs.shape
    assert tp % CHUNK_B == 0 and CHUNK_B % ts == 0
    rp, rs = bp * tp, bs_ * ts
    n_tok = rp + rs

    mod = _ada(jnp.concatenate([cp, cs], axis=0), p['w_ada'], p['b_ada'])
    mods = [mod[:, j * d:(j + 1) * d] for j in range(6)]
    mp = [m[:bp] for m in mods]
    msm = [m[bp:] for m in mods]

    tile_h = lambda g: jnp.tile(g, N_HEADS).reshape(1, D_HEADS)
    bd = jnp.kron(jnp.eye(MXU_DIM // HEAD_DIM, dtype=F32),
                  jnp.full((HEAD_DIM, HEAD_DIM), 1.0 / HEAD_DIM, F32)).astype(BF16)
    w_in_b = p['w_in'].astype(BF16)
    n1g = p['norm1_g'].reshape(1, d)
    common = (n1g, w_in_b, tile_h(p['q_norm_g']), tile_h(p['k_norm_g']), tile_h(p['v_norm_g']), bd)
    q_p, k_p, kb_p, v_p, vb_p, u_p, vn_p, sga_p, sgb_p = _inproj(
        xp.reshape(rp, d), mp[0], mp[1], tp, *common, want_vn32=False, kv_transposed=True)
    q_s, k_s, kb_s, v_s, vb_s, u_s, vn_s, sga_s, sgb_s, vn32_s = _inproj(
        xs.reshape(rs, d), msm[0], msm[1], ts, *common, want_vn32=True, kv_transposed=False)

    ya_p = _attn_prompt(q_p, kb_p, vb_p, bp, tp)
    ya_s = _attn_sample(q_s, kb_s, vb_s, cache_k, cache_v, bs_, ts)

    ws_all, bs_all = _gmlp_tables(p['w_s'], p['b_s'], ts)
    wr = p['w_router']
    wr_hi = wr.astype(BF16)
    wr_lo = (wr - wr_hi.astype(F32)).astype(BF16)
    post_w = (p['w_pa'].astype(BF16), p['w_pb'].astype(BF16), p['w_o'].astype(BF16), p['norm2_g'].reshape(1, d),
              jnp.concatenate([wr_hi, wr_lo], axis=1), p['b_router'].reshape(N_EXPERTS, 1))
    x1_p, h2_p, lg_p = _post(xp.reshape(rp, d), ya_p, u_p, vn_p, sga_p, sgb_p, mp[2], mp[3], mp[4], tp,
                             ws_all, bs_all, 0, *post_w)
    x1_s, h2_s, lg_s = _post(xs.reshape(rs, d), ya_s, u_s, vn_s, sga_s, sgb_s, msm[2], msm[3], msm[4], ts,
                             ws_all, bs_all, 1, *post_w)

    route, route_t, table, counts = _route(lg_p, lg_s)

    counts = counts[:, 0].astype(jnp.int32)
    padded = (counts + TM_E - 1) // TM_E * TM_E
    pend = jnp.cumsum(padded)
    pstart = pend - padded
    table = jnp.swapaxes(table.reshape(-1, N_EXPERTS, LANES)[:, :, :3], 1, 2).astype(jnp.int32)
    n_token_tiles = n_tok // TT
    runs = (table[:, _T_COUNT] + RUN_ROWS - 1) // RUN_ROWS
    runs_end = jnp.cumsum(runs, axis=1)
    j = jnp.arange(RUNS_PER_TILE, dtype=jnp.int32)
    e_of = jnp.minimum(jnp.sum((runs_end[:, None, :] <= j[None, :, None]).astype(jnp.int32), axis=2), N_EXPERTS - 1)
    pick = (e_of[:, :, None] == jnp.arange(N_EXPERTS, dtype=jnp.int32)[None, None, :]).astype(jnp.int32)
    first_run = jnp.sum(pick * (runs_end - runs)[:, None, :], axis=2)
    first_row = jnp.sum(pick * (table[:, _T_BASE] + pstart[None, :])[:, None, :], axis=2)
    run_row = first_row + (j[None, :] - first_run) * RUN_ROWS
    is_run = j[None, :] < runs_end[:, -1:]
    n_tiles = -(-(n_tok * TOP_K + n_token_tiles * N_EXPERTS * (RUN_ROWS - 1)) // TM_E) + N_EXPERTS
    tile_start = jnp.arange(n_tiles, dtype=jnp.int32) * TM_E
    tile_used = (tile_start < pend[-1]).astype(jnp.int32)
    last_expert = jnp.sum((pend <= pend[-1] - 1).astype(jnp.int32))
    tile_expert = jnp.minimum(jnp.sum((pend[None, :] <= tile_start[:, None]).astype(jnp.int32), axis=1), last_expert)
    tile_first = jnp.concatenate([jnp.ones((1,), jnp.int32),
                                  (tile_expert[1:] != tile_expert[:-1]).astype(jnp.int32)])

    spare_tiles = -(-2 * N_SLOTS // TM_E)
    spare_rows = j[None, :] * RUN_ROWS
    parity = (jnp.arange(n_token_tiles, dtype=jnp.int32) % 2)[:, None]
    rows_out = jnp.where(is_run, run_row, n_tiles * TM_E + parity * N_SLOTS + spare_rows).reshape(-1)
    rows_in = jnp.where(is_run, run_row, spare_rows).reshape(-1)
    tail_tiles = jnp.stack([pend[-1] // TM_E, jnp.int32(n_tiles + spare_tiles)]).astype(jnp.int32)
    pad_runs = (padded - counts) // RUN_ROWS
    xs_rows = _dispatch(h2_p, h2_s, route_t, rows_out, pstart + counts, pad_runs, tail_tiles,
                        (n_tiles + spare_tiles) * TM_E)
    b1 = p['b1']
    ne = b1.shape[0]
    y_rows = _experts(xs_rows, tile_expert, tile_first, tile_used, p['w1'],
                      b1[:, 0::2].reshape(ne, 1, -1), b1[:, 1::2].reshape(ne, 1, -1),
                      p['w2'], p['b2'].reshape(ne, 1, d))
    out_p = _combine(x1_p, route, rows_in, mp[5], tp, y_rows, 0)
    out_s = _combine(x1_s, route, rows_in, msm[5], ts, y_rows, rp)

    shp = lambda a, b, t: a.reshape(b, t, N_HEADS, HEAD_DIM)
    unt = lambda a: jnp.transpose(a.reshape(bp, N_HEADS, HEAD_DIM, tp), (0, 3, 1, 2))
    return (out_p.reshape(bp, tp, d), out_s.reshape(bs_, ts, d),
            unt(k_p), unt(v_p), shp(k_s, bs_, ts), shp(v_s, bs_, ts), shp(vn32_s, bs_, ts))


def kernel(x_prompt, x_sample, cache_k, cache_v, c_prompt, c_sample, norm1_g, norm2_g, w_ada, b_ada, w_in,
           q_norm_g, k_norm_g, v_norm_g, w_s, b_s, w_pa, w_pb, w_o, w_router, b_router, w1, b1, w2, b2):
    depth = w_in.shape[0]
    xp, xs = x_prompt, x_sample
    kp, vp, ks, vs, vbs = [], [], [], [], []
    for l in range(depth):
        p = {'norm1_g': norm1_g[l], 'norm2_g': norm2_g[l], 'w_ada': w_ada[l], 'b_ada': b_ada[l],
             'w_in': w_in[l], 'q_norm_g': q_norm_g[l], 'k_norm_g': k_norm_g[l], 'v_norm_g': v_norm_g[l],
             'w_s': w_s[l], 'b_s': b_s[l], 'w_pa': w_pa[l], 'w_pb': w_pb[l], 'w_o': w_o[l],
             'w_router': w_router[l], 'b_router': b_router[l], 'w1': w1[l], 'b1': b1[l],
             'w2': w2[l], 'b2': b2[l]}
        xp, xs, k_p, v_p, k_s, v_s, vn_s = _layer(xp, xs, cache_k[l], cache_v[l], c_prompt, c_sample, p)
        kp.append(k_p); vp.append(v_p); ks.append(k_s); vs.append(v_s); vbs.append(vn_s)
    return (xp, xs, jnp.stack(kp), jnp.stack(vp), jnp.stack(ks), jnp.stack(vs), jnp.stack(vbs))
```

```python
import functools

import jax
import jax.numpy as jnp
from jax import lax
from jax.experimental import pallas as pl
from jax.experimental.pallas import tpu as pltpu

F32 = jnp.float32
BF16 = jnp.bfloat16

LANES = 128
MXU_DIM = 256
VMEM_LIMIT = 56 * 1024 * 1024

HEAD_DIM = 64
N_HEADS = 8
D_HEADS = N_HEADS * HEAD_DIM
PAIRS = D_HEADS // LANES
CHUNK_B = 128
N_EXPERTS = 32
TOP_K = 4
RMS_EPS = 1e-6
SWIGLU_ALPHA = 1.702
SWIGLU_LIMIT = 7.0

TM_IN = 512
TQ = 512
TK = 256
SUB_Q = 128
LOOK_BACK = 256
WINDOW = SUB_Q + LOOK_BACK
TM_POST = 256
TT = 256
TM_E = 512
RUN_ROWS = 8
N_SLOTS = TT * TOP_K + N_EXPERTS * RUN_ROWS


def _params(sem, **kw):
    return pltpu.CompilerParams(dimension_semantics=sem, vmem_limit_bytes=VMEM_LIMIT, **kw)


def _dot(a, b):
    return jnp.dot(a, b, preferred_element_type=F32)


def _dot_nt(a, b):
    return lax.dot_general(a, b, (((1,), (1,)), ((), ())), preferred_element_type=F32)


def _gelu(x):
    return jax.nn.gelu(x)


def _sigmoid(x):
    return 1.0 / (1.0 + jnp.exp(-x))


def _ada_kernel(c_ref, w_ref, b_ref, o_ref):
    c = c_ref[...]
    s = c * _sigmoid(c)
    o_ref[...] = jnp.dot(s, w_ref[...], preferred_element_type=F32,
                         precision=lax.Precision.HIGHEST) + b_ref[...]


def _ada(c, w_ada, b_ada):
    nb, d = c.shape
    n = w_ada.shape[1]
    tn = 1024
    return pl.pallas_call(
        _ada_kernel,
        out_shape=jax.ShapeDtypeStruct((nb, n), F32),
        grid=(n // tn,),
        in_specs=[pl.BlockSpec((nb, d), lambda j: (0, 0)),
                  pl.BlockSpec((d, tn), lambda j: (0, j)),
                  pl.BlockSpec((1, tn), lambda j: (0, j))],
        out_specs=pl.BlockSpec((nb, tn), lambda j: (0, j)),
        compiler_params=_params(("parallel",)),
        name="ada",
    )(c, w_ada, b_ada.reshape(1, n))


def _head_mean_sq(p, bd):
    sq = (p * p).astype(BF16)
    return jnp.concatenate([_dot(sq[:, :MXU_DIM], bd), _dot(sq[:, MXU_DIM:], bd)], axis=-1)


def _inproj_kernel(x_ref, sh_ref, sc_ref, g_ref, w_ref, qg_ref, kg_ref, vg_ref, bd_ref,
                   q_ref, k32_ref, kb_ref, v32_ref, vb_ref, u_ref, vnb_ref, sga_ref, sgb_ref, *vn32_ref,
                   kv_transposed):
    kv_out = (lambda a: a.T) if kv_transposed else (lambda a: a)
    x = x_ref[...]
    ms = jnp.mean(x * x, axis=-1, keepdims=True)
    xn = x * lax.rsqrt(ms + RMS_EPS) * g_ref[...]
    h = (xn * (1.0 + sc_ref[...]) + sh_ref[...]).astype(BF16)
    bd = bd_ref[...]
    d = x.shape[1]

    def sec(a, n):
        return _dot(h, w_ref[:, a:a + n])

    o = 0
    pq = sec(o, D_HEADS); o += D_HEADS
    q = pq * lax.rsqrt(_head_mean_sq(pq, bd) + RMS_EPS) * qg_ref[...]
    q_ref[...] = (q * (HEAD_DIM ** -0.5)).astype(BF16)
    pk = sec(o, D_HEADS); o += D_HEADS
    k = pk * lax.rsqrt(_head_mean_sq(pk, bd) + RMS_EPS) * kg_ref[...]
    k32_ref[...] = kv_out(k)
    kb_ref[...] = k.astype(BF16)
    pv = sec(o, D_HEADS); o += D_HEADS
    v32_ref[...] = kv_out(pv)
    vb_ref[...] = pv.astype(BF16)
    pu = sec(o, D_HEADS); o += D_HEADS
    u_ref[...] = _gelu(pu).astype(BF16)
    pvb = _gelu(sec(o, D_HEADS)); o += D_HEADS
    vn = pvb * lax.rsqrt(_head_mean_sq(pvb, bd) + RMS_EPS) * vg_ref[...]
    vnb_ref[...] = vn.astype(BF16)
    if vn32_ref:
        vn32_ref[0][...] = vn
    sga_ref[...] = _sigmoid(sec(o, d)).astype(BF16); o += d
    sgb_ref[...] = _sigmoid(sec(o, d)).astype(BF16)


def _mod_spec(tm, arr, rows_per_mod):
    if rows_per_mod is None:
        return pl.BlockSpec((tm, arr.shape[-1]), lambda i: (i, 0))
    return pl.BlockSpec((None, 1, arr.shape[-1]), lambda i: ((i * tm) // rows_per_mod, 0, 0))


def _mod_operand(m, seq, tm):
    if seq % tm == 0:
        return m[:, None, :], seq
    return jnp.repeat(m, seq, axis=0), None


def _inproj(x, sh, sc, seq, g, w_in_b, qg, kg, vg, bd, want_vn32, kv_transposed):
    r, d = x.shape
    tm = min(TM_IN, r)
    assert r % tm == 0 and (not kv_transposed or seq % tm == 0)
    sh_o, rpm = _mod_operand(sh, seq, tm)
    sc_o, _ = _mod_operand(sc, seq, tm)
    ncol = w_in_b.shape[1]
    const = lambda i: (0, 0)
    row = lambda n: pl.BlockSpec((tm, n), lambda i: (i, 0))
    outs = [jax.ShapeDtypeStruct((r, D_HEADS), BF16),
            jax.ShapeDtypeStruct((r, D_HEADS), F32),
            jax.ShapeDtypeStruct((r, D_HEADS), BF16),
            jax.ShapeDtypeStruct((r, D_HEADS), F32),
            jax.ShapeDtypeStruct((r, D_HEADS), BF16),
            jax.ShapeDtypeStruct((r, D_HEADS), BF16),
            jax.ShapeDtypeStruct((r, D_HEADS), BF16),
            jax.ShapeDtypeStruct((r, d), BF16),
            jax.ShapeDtypeStruct((r, d), BF16)]
    out_specs = [row(D_HEADS)] * 7 + [row(d), row(d)]
    if kv_transposed:
        per_seq = seq // tm
        for j in (1, 3):
            outs[j] = jax.ShapeDtypeStruct((r // seq, D_HEADS, seq), F32)
            out_specs[j] = pl.BlockSpec((None, D_HEADS, tm), lambda i: (i // per_seq, 0, i % per_seq))
    if want_vn32:
        outs.append(jax.ShapeDtypeStruct((r, D_HEADS), F32))
        out_specs.append(row(D_HEADS))
    return pl.pallas_call(
        functools.partial(_inproj_kernel, kv_transposed=kv_transposed),
        out_shape=outs,
        grid=(r // tm,),
        in_specs=[row(d), _mod_spec(tm, sh_o, rpm), _mod_spec(tm, sc_o, rpm),
                  pl.BlockSpec((1, d), const),
                  pl.BlockSpec((d, ncol), const, pipeline_mode=pl.Buffered(1)),
                  pl.BlockSpec((1, D_HEADS), const), pl.BlockSpec((1, D_HEADS), const),
                  pl.BlockSpec((1, D_HEADS), const), pl.BlockSpec((MXU_DIM, MXU_DIM), const)],
        out_specs=out_specs,
        compiler_params=_params(("parallel",)),
        name="inproj",
    )(x, sh_o, sc_o, g, w_in_b, qg, kg, vg, bd)


LOG_WEIGHT_CUTOFF = -104.0
MASKED_SCORE = -1e30


def _sb_weights(z, u, c, mask):
    if mask is not None:
        z = jnp.where(mask, z, MASKED_SCORE)
    sp = jnp.maximum(z, 0.0) + jnp.log(1.0 + jnp.exp(-jnp.abs(z)))
    hi = sp.astype(BF16)
    lo = (sp - hi.astype(F32)).astype(BF16)
    tk, tu = z.shape[1], u.shape[0]
    old = tk - tu
    later = _dot(hi[:, old:], u) + _dot(lo[:, old:], u)
    total = jnp.sum(sp[:, old:], axis=-1, keepdims=True)
    if old:
        u_old = u[:old, :old]
        later = jnp.concatenate([_dot(hi[:, :old], u_old) + _dot(lo[:, :old], u_old) + total, later], axis=1)
        total = total + jnp.sum(sp[:, :old], axis=-1, keepdims=True)
    w = jnp.exp((z + c) - (sp + later))
    return w, c - total


def _pair_select(shape):
    return lax.broadcasted_iota(jnp.int32, shape, 1) < HEAD_DIM


def _causal_mask(m, tq, tk):
    row = lax.broadcasted_iota(jnp.int32, (m, tk), 0) & (tq - 1)
    return lax.broadcasted_iota(jnp.int32, (m, tk), 1) < row


def _stack_heads(q):
    first = _pair_select(q.shape)
    zero = jnp.zeros_like(q)
    return jnp.concatenate([jnp.where(first, q, zero), jnp.where(first, zero, q)], axis=0)


def _attn_prompt_kernel(q_ref, k_ref, v_ref, u_ref, o_ref, acc_ref, c_ref):
    i = pl.program_id(2)
    tq = q_ref.shape[0]
    nsub = tq // SUB_Q
    sub_rows = 2 * SUB_Q
    q = q_ref[...]
    u = u_ref[...]
    first = _pair_select((SUB_Q, LANES))
    m = nsub * sub_rows

    def pv_sub(wb, s, vs):
        w0 = wb[s * sub_rows:s * sub_rows + SUB_Q]
        w1 = wb[s * sub_rows + SUB_Q:(s + 1) * sub_rows]
        return jnp.where(first, _dot(w0, vs), _dot(w1, vs))

    qs = [_stack_heads(q[s * SUB_Q:(s + 1) * SUB_Q]) for s in range(nsub)]
    starts = [pl.multiple_of(jnp.maximum(i * tq + s * SUB_Q - LOOK_BACK, 0), SUB_Q) for s in range(nsub)]
    row = lax.broadcasted_iota(jnp.int32, (m, 1), 0)
    window_start = jnp.zeros((m, 1), jnp.int32) + starts[0]
    for s in range(1, nsub):
        window_start = jnp.where(row >= s * sub_rows, starts[s], window_start)
    q_pos = i * tq + (row // sub_rows) * SUB_Q + (row & (SUB_Q - 1))

    z = jnp.concatenate([_dot_nt(qs[s], k_ref[pl.ds(starts[s], WINDOW), :]) for s in range(nsub)], axis=0)
    col = lax.broadcasted_iota(jnp.int32, (m, WINDOW), 1)
    w, c = _sb_weights(z, u, jnp.zeros((m, 1), F32), window_start + col < q_pos)
    wb = w.astype(BF16)
    acc_ref[...] = jnp.concatenate([pv_sub(wb, s, v_ref[pl.ds(starts[s], WINDOW), :]) for s in range(nsub)], axis=0)
    c_ref[...] = c
    q_all = jnp.concatenate(qs, axis=0)
    col = lax.broadcasted_iota(jnp.int32, (m, TK), 1)

    def cond(state):
        j, cmax = state
        return jnp.logical_and(j >= 0, cmax > LOG_WEIGHT_CUTOFF)

    def body(state):
        j, _ = state
        start = pl.multiple_of(j * TK, TK)
        vs = v_ref[pl.ds(start, TK), :]
        w, c = _sb_weights(_dot_nt(q_all, k_ref[pl.ds(start, TK), :]), u, c_ref[...],
                           j * TK + col < window_start)
        wb = w.astype(BF16)
        acc_ref[...] += jnp.concatenate([pv_sub(wb, s, vs) for s in range(nsub)], axis=0)
        c_ref[...] = c
        return j - 1, jnp.max(c)

    j0 = lax.div(starts[nsub - 1] + (TK - 1), TK) - 1
    lax.while_loop(cond, body, (j0, jnp.max(c)))
    o_ref[...] = acc_ref[...].astype(o_ref.dtype)


def _upper_ones(n):
    return (lax.broadcasted_iota(jnp.int32, (n, n), 0) > lax.broadcasted_iota(jnp.int32, (n, n), 1)).astype(BF16)


def _attn_prompt(q, kb, vb, batch, seq):
    assert seq % TQ == 0 and TQ % TK == 0 and TQ % SUB_Q == 0 and seq >= WINDOW
    nq = seq // TQ
    return pl.pallas_call(
        _attn_prompt_kernel,
        out_shape=jax.ShapeDtypeStruct(q.shape, BF16),
        grid=(batch, PAIRS, nq),
        in_specs=[pl.BlockSpec((TQ, LANES), lambda b, p, i: (b * nq + i, p)),
                  pl.BlockSpec((seq, LANES), lambda b, p, i: (b, p)),
                  pl.BlockSpec((seq, LANES), lambda b, p, i: (b, p)),
                  pl.BlockSpec((TK, TK), lambda b, p, i: (0, 0))],
        out_specs=pl.BlockSpec((TQ, LANES), lambda b, p, i: (b * nq + i, p)),
        scratch_shapes=[pltpu.VMEM((TQ, LANES), F32), pltpu.VMEM((2 * TQ, 1), F32)],
        compiler_params=_params(("parallel", "parallel", "arbitrary")),
        name="attn_prompt",
    )(q, kb, vb, _upper_ones(TK))


def _attn_sample_kernel(q_ref, kn_ref, vn_ref, kc_ref, vc_ref, un_ref, u_ref, o_ref,
                        kbuf_ref, vbuf_ref, sem, acc_ref, c_ref):
    b = pl.program_id(0)
    tq = q_ref.shape[0]
    nblk = kc_ref.shape[3] // TK
    head = lambda a, h: a[:, h * HEAD_DIM:(h + 1) * HEAD_DIM]
    q = q_ref[...]
    qh = [head(q, h) for h in range(N_HEADS)]

    def copies(j, slot):
        rows = pl.ds(pl.multiple_of(j * TK, TK), TK)
        return (pltpu.make_async_copy(kc_ref.at[b, :, :, rows], kbuf_ref.at[slot], sem.at[0, slot]),
                pltpu.make_async_copy(vc_ref.at[b, :, :, rows], vbuf_ref.at[slot], sem.at[1, slot]))

    def start(j, slot):
        for cp in copies(j, slot):
            cp.start()

    def wait(j, slot):
        for cp in copies(j, slot):
            cp.wait()

    def sweep(score, apply, u, c, mask):
        z = jnp.concatenate([score(qh[h], h) for h in range(N_HEADS)], axis=0)
        w, c = _sb_weights(z, u, c, mask)
        wb = w.astype(BF16)
        pv = jnp.concatenate([apply(wb[h * tq:(h + 1) * tq], h) for h in range(N_HEADS)], axis=-1)
        return pv, c

    start(nblk - 1, 0)
    kn = kn_ref[...]
    vn = vn_ref[...]
    pv, c = sweep(lambda qq, h: _dot_nt(qq, head(kn, h)), lambda ww, h: _dot(ww, head(vn, h)), un_ref[...],
                  jnp.zeros((N_HEADS * tq, 1), F32), _causal_mask(N_HEADS * tq, tq, tq))
    acc_ref[...] = pv
    c_ref[...] = c
    u = u_ref[...]

    def cond(state):
        j, _, cmax = state
        return jnp.logical_and(j >= 0, cmax > LOG_WEIGHT_CUTOFF)

    def body(state):
        j, slot, _ = state
        wait(j, slot)

        @pl.when(j > 0)
        def _():
            start(j - 1, 1 - slot)

        pv, c = sweep(lambda qq, h: _dot(qq, kbuf_ref[slot, h].astype(BF16)),
                      lambda ww, h: _dot_nt(ww, vbuf_ref[slot, h].astype(BF16)), u, c_ref[...], None)
        acc_ref[...] += pv
        c_ref[...] = c
        return j - 1, 1 - slot, jnp.max(c)

    j, slot, _ = lax.while_loop(cond, body, (nblk - 1, 0, jnp.max(c)))

    @pl.when(j >= 0)
    def _():
        wait(j, slot)

    o_ref[...] = acc_ref[...].astype(o_ref.dtype)


def _attn_sample(q, kb, vb, cache_k, cache_v, batch, seq):
    past = cache_k.shape[1]
    assert past % TK == 0 and past >= TK
    cache_k = jnp.transpose(cache_k, (0, 2, 3, 1))
    cache_v = jnp.transpose(cache_v, (0, 2, 3, 1))
    row = pl.BlockSpec((seq, D_HEADS), lambda b: (b, 0))
    return pl.pallas_call(
        _attn_sample_kernel,
        out_shape=jax.ShapeDtypeStruct(q.shape, BF16),
        grid=(batch,),
        in_specs=[row, row, row,
                  pl.BlockSpec(memory_space=pl.ANY), pl.BlockSpec(memory_space=pl.ANY),
                  pl.BlockSpec((seq, seq), lambda b: (0, 0)),
                  pl.BlockSpec((TK, TK), lambda b: (0, 0))],
        out_specs=row,
        scratch_shapes=[pltpu.VMEM((2, N_HEADS, HEAD_DIM, TK), F32), pltpu.VMEM((2, N_HEADS, HEAD_DIM, TK), F32),
                        pltpu.SemaphoreType.DMA((2, 2)),
                        pltpu.VMEM((seq, D_HEADS), F32), pltpu.VMEM((N_HEADS * seq, 1), F32)],
        compiler_params=_params(("arbitrary",)),
        name="attn_sample",
    )(q, kb, vb, cache_k, cache_v, _upper_ones(seq), _upper_ones(TK))


def _pack_bf16_pairs(x):
    half = x.shape[1] // 2
    a = pltpu.bitcast(x[:, :half], jnp.uint32)
    b = pltpu.bitcast(x[:, half:], jnp.uint32)
    return a | (b >> 16)


def _unpack_bf16_pairs(p):
    a = pltpu.bitcast(p & jnp.uint32(0xFFFF0000), F32).astype(BF16)
    b = pltpu.bitcast(p << 16, F32).astype(BF16)
    return jnp.concatenate([a, b], axis=1)


def _post_kernel(x_ref, ya_ref, u_ref, vn_ref, sga_ref, sgb_ref, g1_ref, sh2_ref, sc2_ref,
                 ws_ref, bs_ref, wpa_ref, wpb_ref, wo_ref, n2g_ref, wr_ref, br_ref,
                 x1_ref, h2_ref, lg_ref):
    tm = x_ref.shape[0]
    nch = tm // CHUNK_B
    first = (lax.broadcasted_iota(jnp.int32, (CHUNK_B, nch * LANES), 1) & (LANES - 1)) < HEAD_DIM
    mixed = []
    for p in range(PAIRS):
        vnp = jnp.concatenate([vn_ref[c * CHUNK_B:(c + 1) * CHUNK_B, p * LANES:(p + 1) * LANES] for c in range(nch)],
                              axis=1)
        mixed.append(jnp.where(first, _dot(ws_ref[0, 2 * p], vnp), _dot(ws_ref[0, 2 * p + 1], vnp)))
    chunks = []
    for c in range(nch):
        s = jnp.concatenate([mp[:, c * LANES:(c + 1) * LANES] for mp in mixed], axis=-1) + bs_ref[0]
        chunks.append((u_ref[c * CHUNK_B:(c + 1) * CHUNK_B, :].astype(F32) * s).astype(BF16))
    yb = jnp.concatenate(chunks, axis=0)
    merged = (sga_ref[...].astype(F32) * _dot(ya_ref[...], wpa_ref[...])
              + sgb_ref[...].astype(F32) * _dot(yb, wpb_ref[...]))
    x1 = x_ref[...] + g1_ref[...] * _dot(merged.astype(BF16), wo_ref[...])
    x1_ref[...] = x1
    ms = jnp.mean(x1 * x1, axis=-1, keepdims=True)
    h2 = (x1 * lax.rsqrt(ms + RMS_EPS) * n2g_ref[...]) * (1.0 + sc2_ref[...]) + sh2_ref[...]
    hi = h2.astype(BF16)
    hf = hi.astype(F32)
    lo = (h2 - hf).astype(BF16)
    h2_ref[...] = _pack_bf16_pairs(hf)
    l2 = _dot(hi, wr_ref[...]) + _dot(lo, wr_ref[...])
    l2t = jnp.concatenate([l2, jnp.zeros((l2.shape[0], LANES - l2.shape[1]), F32)], axis=1).T
    lg_ref[...] = l2t[:N_EXPERTS] + l2t[N_EXPERTS:2 * N_EXPERTS] + br_ref[...]


def _post(x, ya, u, vn, sga, sgb, g1, sh2, sc2, seq, ws_all, bs_all, ws_idx, wpa, wpb, wo, n2g, wr, br):
    r, d = x.shape
    tm = min(TM_POST, r)
    assert r % tm == 0 and tm % CHUNK_B == 0
    g1_o, rpm = _mod_operand(g1, seq, tm)
    sh2_o, _ = _mod_operand(sh2, seq, tm)
    sc2_o, _ = _mod_operand(sc2, seq, tm)
    const = lambda i: (0, 0)
    row = lambda n: pl.BlockSpec((tm, n), lambda i: (i, 0))
    return pl.pallas_call(
        _post_kernel,
        out_shape=[jax.ShapeDtypeStruct((r, d), F32), jax.ShapeDtypeStruct((r, d // 2), jnp.uint32),
                   jax.ShapeDtypeStruct((N_EXPERTS, r), F32)],
        grid=(r // tm,),
        in_specs=[row(d), row(D_HEADS), row(D_HEADS), row(D_HEADS), row(d), row(d),
                  _mod_spec(tm, g1_o, rpm), _mod_spec(tm, sh2_o, rpm), _mod_spec(tm, sc2_o, rpm),
                  pl.BlockSpec((1, N_HEADS, CHUNK_B, CHUNK_B), lambda i: (ws_idx, 0, 0, 0)),
                  pl.BlockSpec((1, CHUNK_B, D_HEADS), lambda i: (ws_idx, 0, 0)),
                  pl.BlockSpec(wpa.shape, const), pl.BlockSpec(wpb.shape, const), pl.BlockSpec(wo.shape, const),
                  pl.BlockSpec((1, d), const), pl.BlockSpec(wr.shape, const), pl.BlockSpec((N_EXPERTS, 1), const)],
        out_specs=[row(d), row(d // 2), pl.BlockSpec((N_EXPERTS, tm), lambda i: (0, i))],
        compiler_params=_params(("parallel",)),
        name="post",
    )(x, ya, u, vn, sga, sgb, g1_o, sh2_o, sc2_o, ws_all, bs_all, wpa, wpb, wo, n2g, wr, br)


_R_IDX, _R_GATE, _R_SLOT = 0, TOP_K, 2 * TOP_K
_R_ROWS = 16
_T_COUNT, _T_SLOT, _T_BASE = 0, 1, 2


def _route_kernel(lp_ref, ls_ref, tri_ref, scan_ref, o_ref, ot_ref, tbl_ref, cnt_ref, carry_ref, *, n_prompt_tiles):
    i = pl.program_id(0)

    @pl.when(i == 0)
    def _():
        carry_ref[...] = jnp.zeros_like(carry_ref)

    l = jnp.where(i < n_prompt_tiles, lp_ref[...], ls_ref[...])
    ne, tt = l.shape
    eid = lax.broadcasted_iota(jnp.int32, l.shape, 0).astype(F32)
    cur = l
    vals, idxs = [], []
    for _ in range(TOP_K):
        m = jnp.max(cur, axis=0, keepdims=True)
        idx = jnp.min(jnp.where(cur == m, eid, float(ne)), axis=0, keepdims=True)
        vals.append(m)
        idxs.append(idx)
        cur = jnp.where(eid == idx, -jnp.inf, cur)
    es = [jnp.exp(v - vals[0]) for v in vals]
    den = es[0] + es[1] + es[2] + es[3]
    onehot = jnp.zeros(l.shape, F32)
    for idx in idxs:
        onehot = onehot + (eid == idx).astype(F32)
    within = _dot(onehot.astype(BF16), tri_ref[...])
    count = jnp.sum(onehot, axis=1, keepdims=True)
    runs = jnp.ceil(count * (1.0 / RUN_ROWS))
    first_slot = RUN_ROWS * _dot(scan_ref[...], jnp.broadcast_to(runs, (ne, LANES)).astype(BF16))[:, 0:1]
    slot_of = within + first_slot
    row = lax.broadcasted_iota(jnp.int32, (_R_ROWS, tt), 0)
    out = jnp.zeros((_R_ROWS, tt), F32)
    for k in range(TOP_K):
        slot = jnp.sum(jnp.where(eid == idxs[k], slot_of, 0.0), axis=0, keepdims=True)
        out = jnp.where(row == _R_IDX + k, idxs[k], out)
        out = jnp.where(row == _R_GATE + k, es[k] / den, out)
        out = jnp.where(row == _R_SLOT + k, slot, out)
    ot_ref[...] = out
    o_ref[...] = jnp.concatenate([out, jnp.zeros((LANES - _R_ROWS, tt), F32)], axis=0).T
    lane = lax.broadcasted_iota(jnp.int32, (ne, LANES), 1)
    tbl_ref[...] = jnp.where(lane == _T_COUNT, count, jnp.where(lane == _T_SLOT, first_slot, carry_ref[...]))
    carry_ref[...] += RUN_ROWS * runs
    cnt_ref[...] = jnp.broadcast_to(carry_ref[...], cnt_ref.shape)


def _route(lg_p, lg_s):
    rp, rs = lg_p.shape[1], lg_s.shape[1]
    assert rp % TT == 0 and rs % TT == 0
    npt, nst = rp // TT, rs // TT
    nt = npt + nst
    iota = lambda shape, ax: lax.broadcasted_iota(jnp.int32, shape, ax)
    tri = (iota((TT, TT), 0) < iota((TT, TT), 1)).astype(BF16)
    scan = (iota((N_EXPERTS, N_EXPERTS), 1) < iota((N_EXPERTS, N_EXPERTS), 0)).astype(BF16)
    return pl.pallas_call(
        functools.partial(_route_kernel, n_prompt_tiles=npt),
        out_shape=[jax.ShapeDtypeStruct((rp + rs, LANES), F32), jax.ShapeDtypeStruct((_R_ROWS, rp + rs), F32),
                   jax.ShapeDtypeStruct((nt * N_EXPERTS, LANES), F32), jax.ShapeDtypeStruct((N_EXPERTS, LANES), F32)],
        grid=(nt,),
        in_specs=[pl.BlockSpec((N_EXPERTS, TT), lambda i: (0, jnp.minimum(i, npt - 1))),
                  pl.BlockSpec((N_EXPERTS, TT), lambda i: (0, jnp.maximum(i - npt, 0))),
                  pl.BlockSpec((TT, TT), lambda i: (0, 0)),
                  pl.BlockSpec((N_EXPERTS, N_EXPERTS), lambda i: (0, 0))],
        out_specs=[pl.BlockSpec((TT, LANES), lambda i: (i, 0)), pl.BlockSpec((_R_ROWS, TT), lambda i: (0, i)),
                   pl.BlockSpec((N_EXPERTS, LANES), lambda i: (i, 0)),
                   pl.BlockSpec((N_EXPERTS, LANES), lambda i: (0, 0))],
        scratch_shapes=[pltpu.VMEM((N_EXPERTS, 1), F32)],
        compiler_params=_params(("arbitrary",)),
        name="route",
    )(lg_p, lg_s, tri, scan)


RUNS_PER_TILE = N_SLOTS // RUN_ROWS
RUN_UNROLL = 8


def _for_each_run(tile, run_row_ref, fn):
    def group(g, _):
        for u in range(RUN_UNROLL):
            j = g * RUN_UNROLL + u
            row = run_row_ref[tile * RUNS_PER_TILE + j]
            fn(pl.multiple_of(j * RUN_ROWS, RUN_ROWS), pl.multiple_of(row, RUN_ROWS))
        return 0

    lax.fori_loop(0, RUNS_PER_TILE // RUN_UNROLL, group, 0)


def _dispatch_kernel(run_row_ref, pad_start_ref, pad_runs_ref, tail_ref,
                     rt_ref, hp_ref, hs_ref, xs_ref, stage_ref, zero_ref, sem, psem, *, n_prompt_tiles):
    i = pl.program_id(0)
    last = pl.num_programs(0) - 1
    buf = i % 2

    def start_runs(tile, b):
        _for_each_run(tile, run_row_ref, lambda slot, row: pltpu.make_async_copy(
            stage_ref.at[b, pl.ds(slot, RUN_ROWS), :], xs_ref.at[pl.ds(row, RUN_ROWS), :], sem.at[b]).start())

    def wait_runs(tile, b):
        _for_each_run(tile, run_row_ref, lambda slot, row: pltpu.make_async_copy(
            stage_ref.at[b, pl.ds(slot, RUN_ROWS), :], xs_ref.at[pl.ds(row, RUN_ROWS), :], sem.at[b]).wait())

    @pl.when(i == 0)
    def _():
        zero_ref[...] = jnp.zeros_like(zero_ref)

        def tail(t, _):
            rows = pl.ds(pl.multiple_of(t * TM_E, TM_E), TM_E)
            cp = pltpu.make_async_copy(zero_ref, xs_ref.at[rows, :], psem)
            cp.start()
            cp.wait()
            return 0

        lax.fori_loop(tail_ref[0], tail_ref[1], tail, 0)

        def per_expert(e, _):
            def pad_copy(r):
                rows = pl.ds(pl.multiple_of(pad_start_ref[e] + r * RUN_ROWS, RUN_ROWS), RUN_ROWS)
                return pltpu.make_async_copy(zero_ref.at[pl.ds(0, RUN_ROWS), :], xs_ref.at[rows, :], psem)

            def start(r, _):
                pad_copy(r).start()
                return 0

            def wait(r, _):
                pad_copy(r).wait()
                return 0

            lax.fori_loop(0, pad_runs_ref[e], start, 0)
            lax.fori_loop(0, pad_runs_ref[e], wait, 0)
            return 0

        lax.fori_loop(0, N_EXPERTS, per_expert, 0)

    @pl.when(i >= 2)
    def _():
        wait_runs(i - 2, buf)

    rt = rt_ref[...]
    x = _unpack_bf16_pairs(jnp.where(i < n_prompt_tiles, hp_ref[...], hs_ref[...]))
    n_slots = stage_ref.shape[1]
    slot_id = lax.broadcasted_iota(jnp.int32, (n_slots, rt.shape[1]), 0).astype(F32)
    sel = jnp.zeros(slot_id.shape, F32)
    gsel = jnp.zeros(slot_id.shape, F32)
    for k in range(TOP_K):
        hit = slot_id == rt[_R_SLOT + k:_R_SLOT + k + 1, :]
        sel = jnp.where(hit, 1.0, sel)
        gsel = jnp.where(hit, rt[_R_GATE + k:_R_GATE + k + 1, :], gsel)
    rows = _dot(sel.astype(BF16), x)
    gate = jnp.sum(gsel, axis=-1, keepdims=True)
    packed = _pack_bf16_pairs(rows)
    stage_ref[buf, :, :packed.shape[1]] = packed
    stage_ref[buf, :, packed.shape[1]:] = pltpu.bitcast(jnp.broadcast_to(gate, (n_slots, LANES)), jnp.uint32)
    start_runs(i, buf)

    @pl.when(i == last)
    def _():
        @pl.when(i >= 1)
        def _():
            wait_runs(i - 1, 1 - buf)

        wait_runs(i, buf)


def _dispatch(h_p, h_s, route_t, run_row, pad_start, pad_runs, tail_tiles, n_rows):
    rp, dp = h_p.shape
    rs = h_s.shape[0]
    assert rp % TT == 0 and rs % TT == 0
    npt, nst = rp // TT, rs // TT
    width = dp + LANES
    grid_spec = pltpu.PrefetchScalarGridSpec(
        num_scalar_prefetch=4,
        grid=(npt + nst,),
        in_specs=[pl.BlockSpec((_R_ROWS, TT), lambda i, *_: (0, i)),
                  pl.BlockSpec((TT, dp), lambda i, *_: (jnp.minimum(i, npt - 1), 0)),
                  pl.BlockSpec((TT, dp), lambda i, *_: (jnp.maximum(i - npt, 0), 0))],
        out_specs=pl.BlockSpec(memory_space=pl.ANY),
        scratch_shapes=[pltpu.VMEM((2, N_SLOTS, width), jnp.uint32), pltpu.VMEM((TM_E, width), jnp.uint32),
                        pltpu.SemaphoreType.DMA((2,)), pltpu.SemaphoreType.DMA(())],
    )
    return pl.pallas_call(
        functools.partial(_dispatch_kernel, n_prompt_tiles=npt),
        out_shape=jax.ShapeDtypeStruct((n_rows, width), jnp.uint32),
        grid_spec=grid_spec,
        compiler_params=_params(("arbitrary",), has_side_effects=True),
        name="dispatch",
    )(run_row, pad_start, pad_runs, tail_tiles, route_t, h_p, h_s)


def _experts_kernel(te_ref, tf_ref, tu_ref, tn_ref, ts_ref, x_ref, w1_ref, b1g_ref, b1l_ref, w2_ref, b2_ref, perm_ref,
                    y_ref, w1buf_ref, w2buf_ref, wsem, w1s_ref, w2s_ref):
    i = pl.program_id(0)
    dff = w2_ref.shape[1]

    def weight_copies(e, slot):
        return (pltpu.make_async_copy(w1_ref.at[e], w1buf_ref.at[slot], wsem.at[0, slot]),
                pltpu.make_async_copy(w2_ref.at[e], w2buf_ref.at[slot], wsem.at[1, slot]))

    @pl.when(tf_ref[i] == 1)
    def _():
        slot = ts_ref[i]

        @pl.when(i == 0)
        def _():
            for cp in weight_copies(te_ref[i], slot):
                cp.start()

        for cp in weight_copies(te_ref[i], slot):
            cp.wait()

        @pl.when(tn_ref[i] >= 0)
        def _():
            for cp in weight_copies(tn_ref[i], 1 - slot):
                cp.start()

        perm = perm_ref[...]
        half = MXU_DIM // 2
        for c in range(w1_ref.shape[2] // MXU_DIM):
            blk = w1buf_ref[slot, :, c * MXU_DIM:(c + 1) * MXU_DIM].astype(BF16)
            sp = _dot(blk, perm).astype(BF16)
            w1s_ref[:, c * half:(c + 1) * half] = sp[:, :half]
            w1s_ref[:, dff + c * half:dff + (c + 1) * half] = sp[:, half:]
        w2s_ref[...] = w2buf_ref[slot].astype(BF16)

    @pl.when(tu_ref[i] == 1)
    def _():
        d = w2_ref.shape[2]
        x = _unpack_bf16_pairs(x_ref[:, :d // 2])
        gate = pltpu.bitcast(x_ref[:, d // 2:], F32)
        hdn = _dot(x, w1s_ref[...])
        glu = jnp.minimum(hdn[:, :dff] + b1g_ref[0], SWIGLU_LIMIT)
        lin = jnp.clip(hdn[:, dff:] + b1l_ref[0], -SWIGLU_LIMIT, SWIGLU_LIMIT)
        act = glu * _sigmoid(SWIGLU_ALPHA * glu) * (lin + 1.0)
        y = (_dot(act.astype(BF16), w2s_ref[...]) + b2_ref[0]) * jnp.concatenate([gate] * (d // LANES), axis=1)
        y_ref[...] = _pack_bf16_pairs(y.astype(BF16).astype(F32))

    @pl.when(tu_ref[i] == 0)
    def _():
        y_ref[...] = jnp.zeros_like(y_ref)


def _deinterleave_perm():
    j = lax.broadcasted_iota(jnp.int32, (MXU_DIM, MXU_DIM), 0)
    c = lax.broadcasted_iota(jnp.int32, (MXU_DIM, MXU_DIM), 1)
    half = MXU_DIM // 2
    src = jnp.where(c < half, 2 * c, 2 * (c - half) + 1)
    return (j == src).astype(BF16)


def _experts(xs, tile_expert, tile_first, tile_used, tile_next, w1, b1g, b1l, w2, b2):
    width = xs.shape[1]
    n_tiles = tile_expert.shape[0]
    n_rows = n_tiles * TM_E
    ne, d, two_f = w1.shape
    dff = two_f // 2
    assert width == d // 2 + LANES
    tile_slot = (jnp.cumsum(tile_first) + 1) % 2
    by_expert = lambda i, te, *_: (te[i], 0, 0)
    grid_spec = pltpu.PrefetchScalarGridSpec(
        num_scalar_prefetch=5,
        grid=(n_tiles,),
        in_specs=[pl.BlockSpec((TM_E, width), lambda i, te, tf, tu, *_: (i * tu[i], 0)),
                  pl.BlockSpec(memory_space=pl.ANY),
                  pl.BlockSpec((1, 1, dff), by_expert),
                  pl.BlockSpec((1, 1, dff), by_expert),
                  pl.BlockSpec(memory_space=pl.ANY),
                  pl.BlockSpec((1, 1, d), by_expert),
                  pl.BlockSpec((MXU_DIM, MXU_DIM), lambda i, *_: (0, 0))],
        out_specs=pl.BlockSpec((TM_E, d // 2), lambda i, *_: (i, 0)),
        scratch_shapes=[pltpu.VMEM((2, d, two_f), F32), pltpu.VMEM((2, dff, d), F32), pltpu.SemaphoreType.DMA((2, 2)),
                        pltpu.VMEM((d, two_f), BF16), pltpu.VMEM((dff, d), BF16)],
    )
    return pl.pallas_call(
        _experts_kernel,
        out_shape=jax.ShapeDtypeStruct((n_rows, d // 2), jnp.uint32),
        grid_spec=grid_spec,
        compiler_params=_params(("arbitrary",)),
        name="experts",
    )(tile_expert, tile_first, tile_used, tile_next, tile_slot.astype(jnp.int32),
      xs, w1, b1g, b1l, w2, b2, _deinterleave_perm())


def _combine_kernel(run_row_ref, x1_ref, rt_ref, g2_ref, y_ref, o_ref, stage_ref, sem, *, tile_offset):
    i = pl.program_id(0)
    buf = i % 2

    def start_runs(step, b):
        _for_each_run(step + tile_offset, run_row_ref, lambda slot, row: pltpu.make_async_copy(
            y_ref.at[pl.ds(row, RUN_ROWS), :], stage_ref.at[b, pl.ds(slot, RUN_ROWS), :], sem.at[b]).start())

    def wait_runs(step, b):
        _for_each_run(step + tile_offset, run_row_ref, lambda slot, row: pltpu.make_async_copy(
            y_ref.at[pl.ds(row, RUN_ROWS), :], stage_ref.at[b, pl.ds(slot, RUN_ROWS), :], sem.at[b]).wait())

    @pl.when(i == 0)
    def _():
        stage_ref[...] = jnp.zeros_like(stage_ref)
        start_runs(0, 0)

    @pl.when(i + 1 < pl.num_programs(0))
    def _():
        start_runs(i + 1, 1 - buf)

    wait_runs(i, buf)

    rt = rt_ref[...]
    tt = rt.shape[0]
    slot_id = lax.broadcasted_iota(jnp.int32, (tt, stage_ref.shape[1]), 1).astype(F32)
    sel = jnp.zeros(slot_id.shape, F32)
    for k in range(TOP_K):
        sel = jnp.where(slot_id == rt[:, _R_SLOT + k:_R_SLOT + k + 1], 1.0, sel)
    sel = sel.astype(BF16)
    o_ref[...] = x1_ref[...] + g2_ref[...] * _dot(sel, _unpack_bf16_pairs(stage_ref[buf]))


def _combine(x1, route, run_row, g2, seq, y_rows, row_offset):
    r, d = x1.shape
    assert r % TT == 0 and row_offset % TT == 0
    off = row_offset // TT
    g2_o, rpm = _mod_operand(g2, seq, TT)
    if rpm is None:
        g2_spec = pl.BlockSpec((TT, d), lambda i, *_: (i, 0))
    else:
        g2_spec = pl.BlockSpec((None, 1, d), lambda i, *_: ((i * TT) // rpm, 0, 0))
    grid_spec = pltpu.PrefetchScalarGridSpec(
        num_scalar_prefetch=1,
        grid=(r // TT,),
        in_specs=[pl.BlockSpec((TT, d), lambda i, *_: (i, 0)),
                  pl.BlockSpec((TT, LANES), lambda i, *_: (i + off, 0)),
                  g2_spec,
                  pl.BlockSpec(memory_space=pl.ANY)],
        out_specs=pl.BlockSpec((TT, d), lambda i, *_: (i, 0)),
        scratch_shapes=[pltpu.VMEM((2, N_SLOTS, d // 2), jnp.uint32), pltpu.SemaphoreType.DMA((2,))],
    )
    return pl.pallas_call(
        functools.partial(_combine_kernel, tile_offset=off),
        out_shape=jax.ShapeDtypeStruct((r, d), F32),
        grid_spec=grid_spec,
        compiler_params=_params(("arbitrary",)),
        name="combine",
    )(run_row, x1, route, g2_o, y_rows)


def _gmlp_tables(w_s, b_s, seq_s):
    g = w_s.shape[0]
    tril = jnp.tril(w_s)
    short = jnp.tril(w_s[:, :seq_s, :seq_s])
    rep = CHUNK_B // seq_s
    blockdiag = jnp.zeros_like(w_s)
    for r in range(rep):
        blockdiag = blockdiag.at[:, r * seq_s:(r + 1) * seq_s, r * seq_s:(r + 1) * seq_s].set(short)
    ws_all = jnp.stack([tril, blockdiag]).astype(BF16)
    bs_full = jnp.repeat(b_s.T, HEAD_DIM, axis=1)
    bs_short = jnp.tile(jnp.repeat(b_s[:, :seq_s].T, HEAD_DIM, axis=1), (rep, 1))
    return ws_all, jnp.stack([bs_full, bs_short])


def _layer(xp, xs, cache_k, cache_v, cp, cs, p):
    bp, tp, d = xp.shape
    bs_, ts, _ = xs.shape
    assert tp % CHUNK_B == 0 and CHUNK_B % ts == 0
    rp, rs = bp * tp, bs_ * ts
    n_tok = rp + rs

    mod = _ada(jnp.concatenate([cp, cs], axis=0), p['w_ada'], p['b_ada'])
    mods = [mod[:, j * d:(j + 1) * d] for j in range(6)]
    mp = [m[:bp] for m in mods]
    msm = [m[bp:] for m in mods]

    tile_h = lambda g: jnp.tile(g, N_HEADS).reshape(1, D_HEADS)
    bd = jnp.kron(jnp.eye(MXU_DIM // HEAD_DIM, dtype=F32),
                  jnp.full((HEAD_DIM, HEAD_DIM), 1.0 / HEAD_DIM, F32)).astype(BF16)
    w_in_b = p['w_in'].astype(BF16)
    n1g = p['norm1_g'].reshape(1, d)
    common = (n1g, w_in_b, tile_h(p['q_norm_g']), tile_h(p['k_norm_g']), tile_h(p['v_norm_g']), bd)
    q_p, k_p, kb_p, v_p, vb_p, u_p, vn_p, sga_p, sgb_p = _inproj(
        xp.reshape(rp, d), mp[0], mp[1], tp, *common, want_vn32=False, kv_transposed=True)
    q_s, k_s, kb_s, v_s, vb_s, u_s, vn_s, sga_s, sgb_s, vn32_s = _inproj(
        xs.reshape(rs, d), msm[0], msm[1], ts, *common, want_vn32=True, kv_transposed=False)

    ya_p = _attn_prompt(q_p, kb_p, vb_p, bp, tp)
    ya_s = _attn_sample(q_s, kb_s, vb_s, cache_k, cache_v, bs_, ts)

    ws_all, bs_all = _gmlp_tables(p['w_s'], p['b_s'], ts)
    wr = p['w_router']
    wr_hi = wr.astype(BF16)
    wr_lo = (wr - wr_hi.astype(F32)).astype(BF16)
    post_w = (p['w_pa'].astype(BF16), p['w_pb'].astype(BF16), p['w_o'].astype(BF16), p['norm2_g'].reshape(1, d),
              jnp.concatenate([wr_hi, wr_lo], axis=1), p['b_router'].reshape(N_EXPERTS, 1))
    x1_p, h2_p, lg_p = _post(xp.reshape(rp, d), ya_p, u_p, vn_p, sga_p, sgb_p, mp[2], mp[3], mp[4], tp,
                             ws_all, bs_all, 0, *post_w)
    x1_s, h2_s, lg_s = _post(xs.reshape(rs, d), ya_s, u_s, vn_s, sga_s, sgb_s, msm[2], msm[3], msm[4], ts,
                             ws_all, bs_all, 1, *post_w)

    route, route_t, table, counts = _route(lg_p, lg_s)

    counts = counts[:, 0].astype(jnp.int32)
    padded = (counts + TM_E - 1) // TM_E * TM_E
    pend = jnp.cumsum(padded)
    pstart = pend - padded
    table = jnp.swapaxes(table.reshape(-1, N_EXPERTS, LANES)[:, :, :3], 1, 2).astype(jnp.int32)
    n_token_tiles = n_tok // TT
    runs = (table[:, _T_COUNT] + RUN_ROWS - 1) // RUN_ROWS
    runs_end = jnp.cumsum(runs, axis=1)
    j = jnp.arange(RUNS_PER_TILE, dtype=jnp.int32)
    e_of = jnp.minimum(jnp.sum((runs_end[:, None, :] <= j[None, :, None]).astype(jnp.int32), axis=2), N_EXPERTS - 1)
    pick = (e_of[:, :, None] == jnp.arange(N_EXPERTS, dtype=jnp.int32)[None, None, :]).astype(jnp.int32)
    first_run = jnp.sum(pick * (runs_end - runs)[:, None, :], axis=2)
    first_row = jnp.sum(pick * (table[:, _T_BASE] + pstart[None, :])[:, None, :], axis=2)
    run_row = first_row + (j[None, :] - first_run) * RUN_ROWS
    is_run = j[None, :] < runs_end[:, -1:]
    n_tiles = -(-(n_tok * TOP_K + n_token_tiles * N_EXPERTS * (RUN_ROWS - 1)) // TM_E) + N_EXPERTS
    tile_start = jnp.arange(n_tiles, dtype=jnp.int32) * TM_E
    tile_used = (tile_start < pend[-1]).astype(jnp.int32)
    last_expert = jnp.sum((pend <= pend[-1] - 1).astype(jnp.int32))
    tile_expert = jnp.minimum(jnp.sum((pend[None, :] <= tile_start[:, None]).astype(jnp.int32), axis=1), last_expert)
    tile_first = jnp.concatenate([jnp.ones((1,), jnp.int32),
                                  (tile_expert[1:] != tile_expert[:-1]).astype(jnp.int32)])
    e_ids = jnp.arange(N_EXPERTS, dtype=jnp.int32)
    runs_later = jnp.logical_and(e_ids[None, :] > e_ids[:, None], (padded > 0)[None, :])
    next_expert = jnp.min(jnp.where(runs_later, e_ids[None, :], N_EXPERTS), axis=1)
    next_expert = jnp.where(next_expert == N_EXPERTS, -1, next_expert)
    tile_next = jnp.sum((tile_expert[:, None] == e_ids[None, :]).astype(jnp.int32) * next_expert[None, :], axis=1)

    spare_tiles = -(-2 * N_SLOTS // TM_E)
    spare_rows = j[None, :] * RUN_ROWS
    parity = (jnp.arange(n_token_tiles, dtype=jnp.int32) % 2)[:, None]
    rows_out = jnp.where(is_run, run_row, n_tiles * TM_E + parity * N_SLOTS + spare_rows).reshape(-1)
    rows_in = jnp.where(is_run, run_row, spare_rows).reshape(-1)
    tail_tiles = jnp.stack([pend[-1] // TM_E, jnp.int32(n_tiles + spare_tiles)]).astype(jnp.int32)
    pad_runs = (padded - counts) // RUN_ROWS
    xs_rows = _dispatch(h2_p, h2_s, route_t, rows_out, pstart + counts, pad_runs, tail_tiles,
                        (n_tiles + spare_tiles) * TM_E)
    b1 = p['b1']
    ne = b1.shape[0]
    y_rows = _experts(xs_rows, tile_expert, tile_first, tile_used, tile_next, p['w1'],
                      b1[:, 0::2].reshape(ne, 1, -1), b1[:, 1::2].reshape(ne, 1, -1),
                      p['w2'], p['b2'].reshape(ne, 1, d))
    out_p = _combine(x1_p, route, rows_in, mp[5], tp, y_rows, 0)
    out_s = _combine(x1_s, route, rows_in, msm[5], ts, y_rows, rp)

    shp = lambda a, b, t: a.reshape(b, t, N_HEADS, HEAD_DIM)
    unt = lambda a: jnp.transpose(a.reshape(bp, N_HEADS, HEAD_DIM, tp), (0, 3, 1, 2))
    return (out_p.reshape(bp, tp, d), out_s.reshape(bs_, ts, d),
            unt(k_p), unt(v_p), shp(k_s, bs_, ts), shp(v_s, bs_, ts), shp(vn32_s, bs_, ts))


def kernel(x_prompt, x_sample, cache_k, cache_v, c_prompt, c_sample, norm1_g, norm2_g, w_ada, b_ada, w_in,
           q_norm_g, k_norm_g, v_norm_g, w_s, b_s, w_pa, w_pb, w_o, w_router, b_router, w1, b1, w2, b2):
    depth = w_in.shape[0]
    xp, xs = x_prompt, x_sample
    kp, vp, ks, vs, vbs = [], [], [], [], []
    for l in range(depth):
        p = {'norm1_g': norm1_g[l], 'norm2_g': norm2_g[l], 'w_ada': w_ada[l], 'b_ada': b_ada[l],
             'w_in': w_in[l], 'q_norm_g': q_norm_g[l], 'k_norm_g': k_norm_g[l], 'v_norm_g': v_norm_g[l],
             'w_s': w_s[l], 'b_s': b_s[l], 'w_pa': w_pa[l], 'w_pb': w_pb[l], 'w_o': w_o[l],
             'w_router': w_router[l], 'b_router': b_router[l], 'w1': w1[l], 'b1': b1[l],
             'w2': w2[l], 'b2': b2[l]}
        xp, xs, k_p, v_p, k_s, v_s, vn_s = _layer(xp, xs, cache_k[l], cache_v[l], c_prompt, c_sample, p)
        kp.append(k_p); vp.append(v_p); ks.append(k_s); vs.append(v_s); vbs.append(vn_s)
    return (xp, xs, jnp.stack(kp), jnp.stack(vp), jnp.stack(ks), jnp.stack(vs), jnp.stack(vbs))
```

```python
import functools

import jax
import jax.numpy as jnp
from jax import lax
from jax.experimental import pallas as pl
from jax.experimental.pallas import tpu as pltpu

F32 = jnp.float32
BF16 = jnp.bfloat16

LANES = 128
MXU_DIM = 256
VMEM_LIMIT = 56 * 1024 * 1024

HEAD_DIM = 64
N_HEADS = 8
D_HEADS = N_HEADS * HEAD_DIM
PAIRS = D_HEADS // LANES
CHUNK_B = 128
N_EXPERTS = 32
TOP_K = 4
RMS_EPS = 1e-6
SWIGLU_ALPHA = 1.702
SWIGLU_LIMIT = 7.0

TM_IN = 512
TQ = 1024
TK = 256
SUB_Q = 128
LOOK_BACK = 256
WINDOW = SUB_Q + LOOK_BACK
TM_POST = 256
TT = 256
TM_E = 512
RUN_ROWS = 8
N_SLOTS = TT * TOP_K + N_EXPERTS * RUN_ROWS


def _params(sem, **kw):
    return pltpu.CompilerParams(dimension_semantics=sem, vmem_limit_bytes=VMEM_LIMIT, **kw)


def _dot(a, b):
    return jnp.dot(a, b, preferred_element_type=F32)


def _dot_nt(a, b):
    return lax.dot_general(a, b, (((1,), (1,)), ((), ())), preferred_element_type=F32)


def _gelu(x):
    return jax.nn.gelu(x)


def _sigmoid(x):
    return 1.0 / (1.0 + jnp.exp(-x))


def _ada_kernel(c_ref, w_ref, b_ref, o_ref):
    c = c_ref[...]
    s = c * _sigmoid(c)
    o_ref[...] = jnp.dot(s, w_ref[...], preferred_element_type=F32,
                         precision=lax.Precision.HIGHEST) + b_ref[...]


def _ada(c, w_ada, b_ada):
    nb, d = c.shape
    n = w_ada.shape[1]
    tn = 1024
    return pl.pallas_call(
        _ada_kernel,
        out_shape=jax.ShapeDtypeStruct((nb, n), F32),
        grid=(n // tn,),
        in_specs=[pl.BlockSpec((nb, d), lambda j: (0, 0)),
                  pl.BlockSpec((d, tn), lambda j: (0, j)),
                  pl.BlockSpec((1, tn), lambda j: (0, j))],
        out_specs=pl.BlockSpec((nb, tn), lambda j: (0, j)),
        compiler_params=_params(("parallel",)),
        name="ada",
    )(c, w_ada, b_ada.reshape(1, n))


def _head_mean_sq(p, bd):
    sq = (p * p).astype(BF16)
    return jnp.concatenate([_dot(sq[:, :MXU_DIM], bd), _dot(sq[:, MXU_DIM:], bd)], axis=-1)


def _inproj_kernel(x_ref, sh_ref, sc_ref, g_ref, w_ref, qg_ref, kg_ref, vg_ref, bd_ref,
                   q_ref, k32_ref, kb_ref, v32_ref, vb_ref, u_ref, vnb_ref, sga_ref, sgb_ref, *vn32_ref,
                   kv_transposed):
    kv_out = (lambda a: a.T) if kv_transposed else (lambda a: a)
    x = x_ref[...]
    ms = jnp.mean(x * x, axis=-1, keepdims=True)
    xn = x * lax.rsqrt(ms + RMS_EPS) * g_ref[...]
    h = (xn * (1.0 + sc_ref[...]) + sh_ref[...]).astype(BF16)
    bd = bd_ref[...]
    d = x.shape[1]

    def sec(a, n):
        return _dot(h, w_ref[:, a:a + n])

    o = 0
    pq = sec(o, D_HEADS); o += D_HEADS
    q = pq * lax.rsqrt(_head_mean_sq(pq, bd) + RMS_EPS) * qg_ref[...]
    q_ref[...] = (q * (HEAD_DIM ** -0.5)).astype(BF16)
    pk = sec(o, D_HEADS); o += D_HEADS
    k = pk * lax.rsqrt(_head_mean_sq(pk, bd) + RMS_EPS) * kg_ref[...]
    k32_ref[...] = kv_out(k)
    kb_ref[...] = k.astype(BF16)
    pv = sec(o, D_HEADS); o += D_HEADS
    v32_ref[...] = kv_out(pv)
    vb_ref[...] = pv.astype(BF16)
    pu = sec(o, D_HEADS); o += D_HEADS
    u_ref[...] = _gelu(pu).astype(BF16)
    pvb = _gelu(sec(o, D_HEADS)); o += D_HEADS
    vn = pvb * lax.rsqrt(_head_mean_sq(pvb, bd) + RMS_EPS) * vg_ref[...]
    vnb_ref[...] = vn.astype(BF16)
    if vn32_ref:
        vn32_ref[0][...] = vn
    sga_ref[...] = _sigmoid(sec(o, d)).astype(BF16); o += d
    sgb_ref[...] = _sigmoid(sec(o, d)).astype(BF16)


def _mod_spec(tm, arr, rows_per_mod):
    if rows_per_mod is None:
        return pl.BlockSpec((tm, arr.shape[-1]), lambda i: (i, 0))
    return pl.BlockSpec((None, 1, arr.shape[-1]), lambda i: ((i * tm) // rows_per_mod, 0, 0))


def _mod_operand(m, seq, tm):
    if seq % tm == 0:
        return m[:, None, :], seq
    return jnp.repeat(m, seq, axis=0), None


def _inproj(x, sh, sc, seq, g, w_in_b, qg, kg, vg, bd, want_vn32, kv_transposed):
    r, d = x.shape
    tm = min(TM_IN, r)
    assert r % tm == 0 and (not kv_transposed or seq % tm == 0)
    sh_o, rpm = _mod_operand(sh, seq, tm)
    sc_o, _ = _mod_operand(sc, seq, tm)
    ncol = w_in_b.shape[1]
    const = lambda i: (0, 0)
    row = lambda n: pl.BlockSpec((tm, n), lambda i: (i, 0))
    outs = [jax.ShapeDtypeStruct((r, D_HEADS), BF16),
            jax.ShapeDtypeStruct((r, D_HEADS), F32),
            jax.ShapeDtypeStruct((r, D_HEADS), BF16),
            jax.ShapeDtypeStruct((r, D_HEADS), F32),
            jax.ShapeDtypeStruct((r, D_HEADS), BF16),
            jax.ShapeDtypeStruct((r, D_HEADS), BF16),
            jax.ShapeDtypeStruct((r, D_HEADS), BF16),
            jax.ShapeDtypeStruct((r, d), BF16),
            jax.ShapeDtypeStruct((r, d), BF16)]
    out_specs = [row(D_HEADS)] * 7 + [row(d), row(d)]
    if kv_transposed:
        per_seq = seq // tm
        for j in (1, 3):
            outs[j] = jax.ShapeDtypeStruct((r // seq, D_HEADS, seq), F32)
            out_specs[j] = pl.BlockSpec((None, D_HEADS, tm), lambda i: (i // per_seq, 0, i % per_seq))
    if want_vn32:
        outs.append(jax.ShapeDtypeStruct((r, D_HEADS), F32))
        out_specs.append(row(D_HEADS))
    return pl.pallas_call(
        functools.partial(_inproj_kernel, kv_transposed=kv_transposed),
        out_shape=outs,
        grid=(r // tm,),
        in_specs=[row(d), _mod_spec(tm, sh_o, rpm), _mod_spec(tm, sc_o, rpm),
                  pl.BlockSpec((1, d), const),
                  pl.BlockSpec((d, ncol), const, pipeline_mode=pl.Buffered(1)),
                  pl.BlockSpec((1, D_HEADS), const), pl.BlockSpec((1, D_HEADS), const),
                  pl.BlockSpec((1, D_HEADS), const), pl.BlockSpec((MXU_DIM, MXU_DIM), const)],
        out_specs=out_specs,
        compiler_params=_params(("parallel",)),
        name="inproj",
    )(x, sh_o, sc_o, g, w_in_b, qg, kg, vg, bd)


LOG_WEIGHT_CUTOFF = -104.0
MASKED_SCORE = -1e30


def _sb_weights(z, u, c, mask):
    if mask is not None:
        z = jnp.where(mask, z, MASKED_SCORE)
    sp = jnp.maximum(z, 0.0) + jnp.log(1.0 + jnp.exp(-jnp.abs(z)))
    hi = sp.astype(BF16)
    lo = (sp - hi.astype(F32)).astype(BF16)
    tk, tu = z.shape[1], u.shape[0]
    old = tk - tu
    later = _dot(hi[:, old:], u) + _dot(lo[:, old:], u)
    total = jnp.sum(sp[:, old:], axis=-1, keepdims=True)
    if old:
        u_old = u[:old, :old]
        later = jnp.concatenate([_dot(hi[:, :old], u_old) + _dot(lo[:, :old], u_old) + total, later], axis=1)
        total = total + jnp.sum(sp[:, :old], axis=-1, keepdims=True)
    w = jnp.exp((z + c) - (sp + later))
    return w, c - total


def _pair_select(shape):
    return lax.broadcasted_iota(jnp.int32, shape, 1) < HEAD_DIM


def _causal_mask(m, tq, tk):
    row = lax.broadcasted_iota(jnp.int32, (m, tk), 0) & (tq - 1)
    return lax.broadcasted_iota(jnp.int32, (m, tk), 1) < row


def _stack_heads(q):
    first = _pair_select(q.shape)
    zero = jnp.zeros_like(q)
    return jnp.concatenate([jnp.where(first, q, zero), jnp.where(first, zero, q)], axis=0)


def _attn_prompt_kernel(q_ref, k_ref, v_ref, u_ref, o_ref, acc_ref, c_ref):
    i = pl.program_id(2)
    tq = q_ref.shape[0]
    nsub = tq // SUB_Q
    sub_rows = 2 * SUB_Q
    q = q_ref[...]
    u = u_ref[...]
    first = _pair_select((SUB_Q, LANES))
    m = nsub * sub_rows

    def pv_sub(wb, s, vs):
        w0 = wb[s * sub_rows:s * sub_rows + SUB_Q]
        w1 = wb[s * sub_rows + SUB_Q:(s + 1) * sub_rows]
        return jnp.where(first, _dot(w0, vs), _dot(w1, vs))

    qs = [_stack_heads(q[s * SUB_Q:(s + 1) * SUB_Q]) for s in range(nsub)]
    starts = [pl.multiple_of(jnp.maximum(i * tq + s * SUB_Q - LOOK_BACK, 0), SUB_Q) for s in range(nsub)]
    row = lax.broadcasted_iota(jnp.int32, (m, 1), 0)
    window_start = jnp.zeros((m, 1), jnp.int32) + starts[0]
    for s in range(1, nsub):
        window_start = jnp.where(row >= s * sub_rows, starts[s], window_start)
    q_pos = i * tq + (row // sub_rows) * SUB_Q + (row & (SUB_Q - 1))

    z = jnp.concatenate([_dot_nt(qs[s], k_ref[pl.ds(starts[s], WINDOW), :]) for s in range(nsub)], axis=0)
    col = lax.broadcasted_iota(jnp.int32, (m, WINDOW), 1)
    w, c = _sb_weights(z, u, jnp.zeros((m, 1), F32), window_start + col < q_pos)
    wb = w.astype(BF16)
    acc_ref[...] = jnp.concatenate([pv_sub(wb, s, v_ref[pl.ds(starts[s], WINDOW), :]) for s in range(nsub)], axis=0)
    c_ref[...] = c
    q_all = jnp.concatenate(qs, axis=0)
    col = lax.broadcasted_iota(jnp.int32, (m, TK), 1)

    def cond(state):
        j, cmax = state
        return jnp.logical_and(j >= 0, cmax > LOG_WEIGHT_CUTOFF)

    def body(state):
        j, _ = state
        start = pl.multiple_of(j * TK, TK)
        vs = v_ref[pl.ds(start, TK), :]
        w, c = _sb_weights(_dot_nt(q_all, k_ref[pl.ds(start, TK), :]), u, c_ref[...],
                           j * TK + col < window_start)
        wb = w.astype(BF16)
        acc_ref[...] += jnp.concatenate([pv_sub(wb, s, vs) for s in range(nsub)], axis=0)
        c_ref[...] = c
        return j - 1, jnp.max(c)

    j0 = lax.div(starts[nsub - 1] + (TK - 1), TK) - 1
    lax.while_loop(cond, body, (j0, jnp.max(c)))
    o_ref[...] = acc_ref[...].astype(o_ref.dtype)


def _upper_ones(n):
    return (lax.broadcasted_iota(jnp.int32, (n, n), 0) > lax.broadcasted_iota(jnp.int32, (n, n), 1)).astype(BF16)


def _attn_prompt(q, kb, vb, batch, seq):
    assert seq % TQ == 0 and TQ % TK == 0 and TQ % SUB_Q == 0 and seq >= WINDOW
    nq = seq // TQ
    return pl.pallas_call(
        _attn_prompt_kernel,
        out_shape=jax.ShapeDtypeStruct(q.shape, BF16),
        grid=(batch, PAIRS, nq),
        in_specs=[pl.BlockSpec((TQ, LANES), lambda b, p, i: (b * nq + i, p)),
                  pl.BlockSpec((seq, LANES), lambda b, p, i: (b, p)),
                  pl.BlockSpec((seq, LANES), lambda b, p, i: (b, p)),
                  pl.BlockSpec((TK, TK), lambda b, p, i: (0, 0))],
        out_specs=pl.BlockSpec((TQ, LANES), lambda b, p, i: (b * nq + i, p)),
        scratch_shapes=[pltpu.VMEM((TQ, LANES), F32), pltpu.VMEM((2 * TQ, 1), F32)],
        compiler_params=_params(("parallel", "parallel", "arbitrary")),
        name="attn_prompt",
    )(q, kb, vb, _upper_ones(TK))


def _attn_sample_kernel(q_ref, kn_ref, vn_ref, kc_ref, vc_ref, un_ref, u_ref, o_ref,
                        kbuf_ref, vbuf_ref, sem, acc_ref, c_ref):
    b = pl.program_id(0)
    tq = q_ref.shape[0]
    nblk = kc_ref.shape[3] // TK
    head = lambda a, h: a[:, h * HEAD_DIM:(h + 1) * HEAD_DIM]
    q = q_ref[...]
    qh = [head(q, h) for h in range(N_HEADS)]

    def copies(j, slot):
        rows = pl.ds(pl.multiple_of(j * TK, TK), TK)
        return (pltpu.make_async_copy(kc_ref.at[b, :, :, rows], kbuf_ref.at[slot], sem.at[0, slot]),
                pltpu.make_async_copy(vc_ref.at[b, :, :, rows], vbuf_ref.at[slot], sem.at[1, slot]))

    def start(j, slot):
        for cp in copies(j, slot):
            cp.start()

    def wait(j, slot):
        for cp in copies(j, slot):
            cp.wait()

    def sweep(score, apply, u, c, mask):
        z = jnp.concatenate([score(qh[h], h) for h in range(N_HEADS)], axis=0)
        w, c = _sb_weights(z, u, c, mask)
        wb = w.astype(BF16)
        pv = jnp.concatenate([apply(wb[h * tq:(h + 1) * tq], h) for h in range(N_HEADS)], axis=-1)
        return pv, c

    start(nblk - 1, 0)
    kn = kn_ref[...]
    vn = vn_ref[...]
    pv, c = sweep(lambda qq, h: _dot_nt(qq, head(kn, h)), lambda ww, h: _dot(ww, head(vn, h)), un_ref[...],
                  jnp.zeros((N_HEADS * tq, 1), F32), _causal_mask(N_HEADS * tq, tq, tq))
    acc_ref[...] = pv
    c_ref[...] = c
    u = u_ref[...]

    def cond(state):
        j, _, cmax = state
        return jnp.logical_and(j >= 0, cmax > LOG_WEIGHT_CUTOFF)

    def body(state):
        j, slot, _ = state
        wait(j, slot)

        @pl.when(j > 0)
        def _():
            start(j - 1, 1 - slot)

        pv, c = sweep(lambda qq, h: _dot(qq, kbuf_ref[slot, h].astype(BF16)),
                      lambda ww, h: _dot_nt(ww, vbuf_ref[slot, h].astype(BF16)), u, c_ref[...], None)
        acc_ref[...] += pv
        c_ref[...] = c
        return j - 1, 1 - slot, jnp.max(c)

    j, slot, _ = lax.while_loop(cond, body, (nblk - 1, 0, jnp.max(c)))

    @pl.when(j >= 0)
    def _():
        wait(j, slot)

    o_ref[...] = acc_ref[...].astype(o_ref.dtype)


def _attn_sample(q, kb, vb, cache_k, cache_v, batch, seq):
    past = cache_k.shape[1]
    assert past % TK == 0 and past >= TK
    cache_k = jnp.transpose(cache_k, (0, 2, 3, 1))
    cache_v = jnp.transpose(cache_v, (0, 2, 3, 1))
    row = pl.BlockSpec((seq, D_HEADS), lambda b: (b, 0))
    return pl.pallas_call(
        _attn_sample_kernel,
        out_shape=jax.ShapeDtypeStruct(q.shape, BF16),
        grid=(batch,),
        in_specs=[row, row, row,
                  pl.BlockSpec(memory_space=pl.ANY), pl.BlockSpec(memory_space=pl.ANY),
                  pl.BlockSpec((seq, seq), lambda b: (0, 0)),
                  pl.BlockSpec((TK, TK), lambda b: (0, 0))],
        out_specs=row,
        scratch_shapes=[pltpu.VMEM((2, N_HEADS, HEAD_DIM, TK), F32), pltpu.VMEM((2, N_HEADS, HEAD_DIM, TK), F32),
                        pltpu.SemaphoreType.DMA((2, 2)),
                        pltpu.VMEM((seq, D_HEADS), F32), pltpu.VMEM((N_HEADS * seq, 1), F32)],
        compiler_params=_params(("arbitrary",)),
        name="attn_sample",
    )(q, kb, vb, cache_k, cache_v, _upper_ones(seq), _upper_ones(TK))


def _pack_bf16_pairs(x):
    half = x.shape[1] // 2
    a = pltpu.bitcast(x[:, :half], jnp.uint32)
    b = pltpu.bitcast(x[:, half:], jnp.uint32)
    return a | (b >> 16)


def _unpack_bf16_pairs(p):
    a = pltpu.bitcast(p & jnp.uint32(0xFFFF0000), F32).astype(BF16)
    b = pltpu.bitcast(p << 16, F32).astype(BF16)
    return jnp.concatenate([a, b], axis=1)


def _post_kernel(x_ref, ya_ref, u_ref, vn_ref, sga_ref, sgb_ref, g1_ref, sh2_ref, sc2_ref,
                 ws_ref, bs_ref, wpa_ref, wpb_ref, wo_ref, n2g_ref, wr_ref, br_ref,
                 x1_ref, h2_ref, lg_ref):
    tm = x_ref.shape[0]
    nch = tm // CHUNK_B
    first = (lax.broadcasted_iota(jnp.int32, (CHUNK_B, nch * LANES), 1) & (LANES - 1)) < HEAD_DIM
    mixed = []
    for p in range(PAIRS):
        vnp = jnp.concatenate([vn_ref[c * CHUNK_B:(c + 1) * CHUNK_B, p * LANES:(p + 1) * LANES] for c in range(nch)],
                              axis=1)
        mixed.append(jnp.where(first, _dot(ws_ref[0, 2 * p], vnp), _dot(ws_ref[0, 2 * p + 1], vnp)))
    chunks = []
    for c in range(nch):
        s = jnp.concatenate([mp[:, c * LANES:(c + 1) * LANES] for mp in mixed], axis=-1) + bs_ref[0]
        chunks.append((u_ref[c * CHUNK_B:(c + 1) * CHUNK_B, :].astype(F32) * s).astype(BF16))
    yb = jnp.concatenate(chunks, axis=0)
    merged = (sga_ref[...].astype(F32) * _dot(ya_ref[...], wpa_ref[...])
              + sgb_ref[...].astype(F32) * _dot(yb, wpb_ref[...]))
    x1 = x_ref[...] + g1_ref[...] * _dot(merged.astype(BF16), wo_ref[...])
    x1_ref[...] = x1
    ms = jnp.mean(x1 * x1, axis=-1, keepdims=True)
    h2 = (x1 * lax.rsqrt(ms + RMS_EPS) * n2g_ref[...]) * (1.0 + sc2_ref[...]) + sh2_ref[...]
    hi = h2.astype(BF16)
    hf = hi.astype(F32)
    lo = (h2 - hf).astype(BF16)
    h2_ref[...] = _pack_bf16_pairs(hf)
    l2 = _dot(hi, wr_ref[...]) + _dot(lo, wr_ref[...])
    l2t = jnp.concatenate([l2, jnp.zeros((l2.shape[0], LANES - l2.shape[1]), F32)], axis=1).T
    lg_ref[...] = l2t[:N_EXPERTS] + l2t[N_EXPERTS:2 * N_EXPERTS] + br_ref[...]


def _post(x, ya, u, vn, sga, sgb, g1, sh2, sc2, seq, ws_all, bs_all, ws_idx, wpa, wpb, wo, n2g, wr, br):
    r, d = x.shape
    tm = min(TM_POST, r)
    assert r % tm == 0 and tm % CHUNK_B == 0
    g1_o, rpm = _mod_operand(g1, seq, tm)
    sh2_o, _ = _mod_operand(sh2, seq, tm)
    sc2_o, _ = _mod_operand(sc2, seq, tm)
    const = lambda i: (0, 0)
    row = lambda n: pl.BlockSpec((tm, n), lambda i: (i, 0))
    return pl.pallas_call(
        _post_kernel,
        out_shape=[jax.ShapeDtypeStruct((r, d), F32), jax.ShapeDtypeStruct((r, d // 2), jnp.uint32),
                   jax.ShapeDtypeStruct((N_EXPERTS, r), F32)],
        grid=(r // tm,),
        in_specs=[row(d), row(D_HEADS), row(D_HEADS), row(D_HEADS), row(d), row(d),
                  _mod_spec(tm, g1_o, rpm), _mod_spec(tm, sh2_o, rpm), _mod_spec(tm, sc2_o, rpm),
                  pl.BlockSpec((1, N_HEADS, CHUNK_B, CHUNK_B), lambda i: (ws_idx, 0, 0, 0)),
                  pl.BlockSpec((1, CHUNK_B, D_HEADS), lambda i: (ws_idx, 0, 0)),
                  pl.BlockSpec(wpa.shape, const), pl.BlockSpec(wpb.shape, const), pl.BlockSpec(wo.shape, const),
                  pl.BlockSpec((1, d), const), pl.BlockSpec(wr.shape, const), pl.BlockSpec((N_EXPERTS, 1), const)],
        out_specs=[row(d), row(d // 2), pl.BlockSpec((N_EXPERTS, tm), lambda i: (0, i))],
        compiler_params=_params(("parallel",)),
        name="post",
    )(x, ya, u, vn, sga, sgb, g1_o, sh2_o, sc2_o, ws_all, bs_all, wpa, wpb, wo, n2g, wr, br)


_R_IDX, _R_GATE, _R_SLOT = 0, TOP_K, 2 * TOP_K
_R_ROWS = 16
_T_COUNT, _T_SLOT, _T_BASE = 0, 1, 2


def _route_kernel(lp_ref, ls_ref, tri_ref, scan_ref, o_ref, ot_ref, tbl_ref, cnt_ref, carry_ref, *, n_prompt_tiles):
    i = pl.program_id(0)

    @pl.when(i == 0)
    def _():
        carry_ref[...] = jnp.zeros_like(carry_ref)

    l = jnp.where(i < n_prompt_tiles, lp_ref[...], ls_ref[...])
    ne, tt = l.shape
    eid = lax.broadcasted_iota(jnp.int32, l.shape, 0).astype(F32)
    cur = l
    vals, idxs = [], []
    for _ in range(TOP_K):
        m = jnp.max(cur, axis=0, keepdims=True)
        idx = jnp.min(jnp.where(cur == m, eid, float(ne)), axis=0, keepdims=True)
        vals.append(m)
        idxs.append(idx)
        cur = jnp.where(eid == idx, -jnp.inf, cur)
    es = [jnp.exp(v - vals[0]) for v in vals]
    den = es[0] + es[1] + es[2] + es[3]
    onehot = jnp.zeros(l.shape, F32)
    for idx in idxs:
        onehot = onehot + (eid == idx).astype(F32)
    within = _dot(onehot.astype(BF16), tri_ref[...])
    count = jnp.sum(onehot, axis=1, keepdims=True)
    runs = jnp.ceil(count * (1.0 / RUN_ROWS))
    first_slot = RUN_ROWS * _dot(scan_ref[...], jnp.broadcast_to(runs, (ne, LANES)).astype(BF16))[:, 0:1]
    slot_of = within + first_slot
    row = lax.broadcasted_iota(jnp.int32, (_R_ROWS, tt), 0)
    out = jnp.zeros((_R_ROWS, tt), F32)
    for k in range(TOP_K):
        slot = jnp.sum(jnp.where(eid == idxs[k], slot_of, 0.0), axis=0, keepdims=True)
        out = jnp.where(row == _R_IDX + k, idxs[k], out)
        out = jnp.where(row == _R_GATE + k, es[k] / den, out)
        out = jnp.where(row == _R_SLOT + k, slot, out)
    ot_ref[...] = out
    o_ref[...] = jnp.concatenate([out, jnp.zeros((LANES - _R_ROWS, tt), F32)], axis=0).T
    lane = lax.broadcasted_iota(jnp.int32, (ne, LANES), 1)
    tbl_ref[...] = jnp.where(lane == _T_COUNT, count, jnp.where(lane == _T_SLOT, first_slot, carry_ref[...]))
    carry_ref[...] += RUN_ROWS * runs
    cnt_ref[...] = jnp.broadcast_to(carry_ref[...], cnt_ref.shape)


def _route(lg_p, lg_s):
    rp, rs = lg_p.shape[1], lg_s.shape[1]
    assert rp % TT == 0 and rs % TT == 0
    npt, nst = rp // TT, rs // TT
    nt = npt + nst
    iota = lambda shape, ax: lax.broadcasted_iota(jnp.int32, shape, ax)
    tri = (iota((TT, TT), 0) < iota((TT, TT), 1)).astype(BF16)
    scan = (iota((N_EXPERTS, N_EXPERTS), 1) < iota((N_EXPERTS, N_EXPERTS), 0)).astype(BF16)
    return pl.pallas_call(
        functools.partial(_route_kernel, n_prompt_tiles=npt),
        out_shape=[jax.ShapeDtypeStruct((rp + rs, LANES), F32), jax.ShapeDtypeStruct((_R_ROWS, rp + rs), F32),
                   jax.ShapeDtypeStruct((nt * N_EXPERTS, LANES), F32), jax.ShapeDtypeStruct((N_EXPERTS, LANES), F32)],
        grid=(nt,),
        in_specs=[pl.BlockSpec((N_EXPERTS, TT), lambda i: (0, jnp.minimum(i, npt - 1))),
                  pl.BlockSpec((N_EXPERTS, TT), lambda i: (0, jnp.maximum(i - npt, 0))),
                  pl.BlockSpec((TT, TT), lambda i: (0, 0)),
                  pl.BlockSpec((N_EXPERTS, N_EXPERTS), lambda i: (0, 0))],
        out_specs=[pl.BlockSpec((TT, LANES), lambda i: (i, 0)), pl.BlockSpec((_R_ROWS, TT), lambda i: (0, i)),
                   pl.BlockSpec((N_EXPERTS, LANES), lambda i: (i, 0)),
                   pl.BlockSpec((N_EXPERTS, LANES), lambda i: (0, 0))],
        scratch_shapes=[pltpu.VMEM((N_EXPERTS, 1), F32)],
        compiler_params=_params(("arbitrary",)),
        name="route",
    )(lg_p, lg_s, tri, scan)


RUNS_PER_TILE = N_SLOTS // RUN_ROWS
RUN_UNROLL = 8
RUN_TABLE_STRIDE = RUNS_PER_TILE + RUN_UNROLL


def _for_each_run(tile, run_row_ref, fn):
    base = tile * RUN_TABLE_STRIDE

    def group(g, _):
        for u in range(RUN_UNROLL):
            j = g * RUN_UNROLL + u
            row = run_row_ref[base + j]
            fn(pl.multiple_of(j * RUN_ROWS, RUN_ROWS), pl.multiple_of(row, RUN_ROWS))
        return 0

    lax.fori_loop(0, run_row_ref[base + RUNS_PER_TILE], group, 0)


def _dispatch_kernel(run_row_ref, pad_start_ref, pad_runs_ref, tail_ref,
                     rt_ref, hp_ref, hs_ref, xs_ref, stage_ref, zero_ref, sem, psem, *, n_prompt_tiles):
    i = pl.program_id(0)
    last = pl.num_programs(0) - 1
    buf = i % 2

    def start_runs(tile, b):
        _for_each_run(tile, run_row_ref, lambda slot, row: pltpu.make_async_copy(
            stage_ref.at[b, pl.ds(slot, RUN_ROWS), :], xs_ref.at[pl.ds(row, RUN_ROWS), :], sem.at[b]).start())

    def wait_runs(tile, b):
        _for_each_run(tile, run_row_ref, lambda slot, row: pltpu.make_async_copy(
            stage_ref.at[b, pl.ds(slot, RUN_ROWS), :], xs_ref.at[pl.ds(row, RUN_ROWS), :], sem.at[b]).wait())

    @pl.when(i == 0)
    def _():
        zero_ref[...] = jnp.zeros_like(zero_ref)

        def tail(t, _):
            rows = pl.ds(pl.multiple_of(t * TM_E, TM_E), TM_E)
            cp = pltpu.make_async_copy(zero_ref, xs_ref.at[rows, :], psem)
            cp.start()
            cp.wait()
            return 0

        lax.fori_loop(tail_ref[0], tail_ref[1], tail, 0)

        def per_expert(e, _):
            def pad_copy(r):
                rows = pl.ds(pl.multiple_of(pad_start_ref[e] + r * RUN_ROWS, RUN_ROWS), RUN_ROWS)
                return pltpu.make_async_copy(zero_ref.at[pl.ds(0, RUN_ROWS), :], xs_ref.at[rows, :], psem)

            def start(r, _):
                pad_copy(r).start()
                return 0

            def wait(r, _):
                pad_copy(r).wait()
                return 0

            lax.fori_loop(0, pad_runs_ref[e], start, 0)
            lax.fori_loop(0, pad_runs_ref[e], wait, 0)
            return 0

        lax.fori_loop(0, N_EXPERTS, per_expert, 0)

    @pl.when(i >= 2)
    def _():
        wait_runs(i - 2, buf)

    rt = rt_ref[...]
    x = _unpack_bf16_pairs(jnp.where(i < n_prompt_tiles, hp_ref[...], hs_ref[...]))
    n_slots = stage_ref.shape[1]
    slot_id = lax.broadcasted_iota(jnp.int32, (n_slots, rt.shape[1]), 0).astype(F32)
    sel = jnp.zeros(slot_id.shape, F32)
    gsel = jnp.zeros(slot_id.shape, F32)
    for k in range(TOP_K):
        hit = slot_id == rt[_R_SLOT + k:_R_SLOT + k + 1, :]
        sel = jnp.where(hit, 1.0, sel)
        gsel = jnp.where(hit, rt[_R_GATE + k:_R_GATE + k + 1, :], gsel)
    rows = _dot(sel.astype(BF16), x)
    gate = jnp.sum(gsel, axis=-1, keepdims=True)
    packed = _pack_bf16_pairs(rows)
    stage_ref[buf, :, :packed.shape[1]] = packed
    stage_ref[buf, :, packed.shape[1]:] = pltpu.bitcast(jnp.broadcast_to(gate, (n_slots, LANES)), jnp.uint32)
    start_runs(i, buf)

    @pl.when(i == last)
    def _():
        @pl.when(i >= 1)
        def _():
            wait_runs(i - 1, 1 - buf)

        wait_runs(i, buf)


def _dispatch(h_p, h_s, route_t, run_row, pad_start, pad_runs, tail_tiles, n_rows):
    rp, dp = h_p.shape
    rs = h_s.shape[0]
    assert rp % TT == 0 and rs % TT == 0
    npt, nst = rp // TT, rs // TT
    width = dp + LANES
    grid_spec = pltpu.PrefetchScalarGridSpec(
        num_scalar_prefetch=4,
        grid=(npt + nst,),
        in_specs=[pl.BlockSpec((_R_ROWS, TT), lambda i, *_: (0, i)),
                  pl.BlockSpec((TT, dp), lambda i, *_: (jnp.minimum(i, npt - 1), 0)),
                  pl.BlockSpec((TT, dp), lambda i, *_: (jnp.maximum(i - npt, 0), 0))],
        out_specs=pl.BlockSpec(memory_space=pl.ANY),
        scratch_shapes=[pltpu.VMEM((2, N_SLOTS, width), jnp.uint32), pltpu.VMEM((TM_E, width), jnp.uint32),
                        pltpu.SemaphoreType.DMA((2,)), pltpu.SemaphoreType.DMA(())],
    )
    return pl.pallas_call(
        functools.partial(_dispatch_kernel, n_prompt_tiles=npt),
        out_shape=jax.ShapeDtypeStruct((n_rows, width), jnp.uint32),
        grid_spec=grid_spec,
        compiler_params=_params(("arbitrary",), has_side_effects=True),
        name="dispatch",
    )(run_row, pad_start, pad_runs, tail_tiles, route_t, h_p, h_s)


def _experts_kernel(te_ref, tf_ref, tu_ref, tn_ref, ts_ref, x_ref, w1_ref, b1g_ref, b1l_ref, w2_ref, b2_ref, perm_ref,
                    y_ref, w1buf_ref, w2buf_ref, wsem, w1s_ref, w2s_ref):
    i = pl.program_id(0)
    dff = w2_ref.shape[1]

    def weight_copies(e, slot):
        return (pltpu.make_async_copy(w1_ref.at[e], w1buf_ref.at[slot], wsem.at[0, slot]),
                pltpu.make_async_copy(w2_ref.at[e], w2buf_ref.at[slot], wsem.at[1, slot]))

    @pl.when(tf_ref[i] == 1)
    def _():
        slot = ts_ref[i]

        @pl.when(i == 0)
        def _():
            for cp in weight_copies(te_ref[i], slot):
                cp.start()

        for cp in weight_copies(te_ref[i], slot):
            cp.wait()

        @pl.when(tn_ref[i] >= 0)
        def _():
            for cp in weight_copies(tn_ref[i], 1 - slot):
                cp.start()

        perm = perm_ref[...]
        half = MXU_DIM // 2
        for c in range(w1_ref.shape[2] // MXU_DIM):
            blk = w1buf_ref[slot, :, c * MXU_DIM:(c + 1) * MXU_DIM].astype(BF16)
            sp = _dot(blk, perm).astype(BF16)
            w1s_ref[:, c * half:(c + 1) * half] = sp[:, :half]
            w1s_ref[:, dff + c * half:dff + (c + 1) * half] = sp[:, half:]
        w2s_ref[...] = w2buf_ref[slot].astype(BF16)

    @pl.when(tu_ref[i] == 1)
    def _():
        d = w2_ref.shape[2]
        x = _unpack_bf16_pairs(x_ref[:, :d // 2])
        gate = pltpu.bitcast(x_ref[:, d // 2:], F32)
        hdn = _dot(x, w1s_ref[...])
        glu = jnp.minimum(hdn[:, :dff] + b1g_ref[0], SWIGLU_LIMIT)
        lin = jnp.clip(hdn[:, dff:] + b1l_ref[0], -SWIGLU_LIMIT, SWIGLU_LIMIT)
        act = glu * _sigmoid(SWIGLU_ALPHA * glu) * (lin + 1.0)
        y = (_dot(act.astype(BF16), w2s_ref[...]) + b2_ref[0]) * jnp.concatenate([gate] * (d // LANES), axis=1)
        y_ref[...] = _pack_bf16_pairs(y.astype(BF16).astype(F32))

    @pl.when(tu_ref[i] == 0)
    def _():
        y_ref[...] = jnp.zeros_like(y_ref)


def _deinterleave_perm():
    j = lax.broadcasted_iota(jnp.int32, (MXU_DIM, MXU_DIM), 0)
    c = lax.broadcasted_iota(jnp.int32, (MXU_DIM, MXU_DIM), 1)
    half = MXU_DIM // 2
    src = jnp.where(c < half, 2 * c, 2 * (c - half) + 1)
    return (j == src).astype(BF16)


def _experts(xs, tile_expert, tile_first, tile_used, tile_next, w1, b1g, b1l, w2, b2):
    width = xs.shape[1]
    n_tiles = tile_expert.shape[0]
    n_rows = n_tiles * TM_E
    ne, d, two_f = w1.shape
    dff = two_f // 2
    assert width == d // 2 + LANES
    tile_slot = (jnp.cumsum(tile_first) + 1) % 2
    by_expert = lambda i, te, *_: (te[i], 0, 0)
    grid_spec = pltpu.PrefetchScalarGridSpec(
        num_scalar_prefetch=5,
        grid=(n_tiles,),
        in_specs=[pl.BlockSpec((TM_E, width), lambda i, te, tf, tu, *_: (i * tu[i], 0)),
                  pl.BlockSpec(memory_space=pl.ANY),
                  pl.BlockSpec((1, 1, dff), by_expert),
                  pl.BlockSpec((1, 1, dff), by_expert),
                  pl.BlockSpec(memory_space=pl.ANY),
                  pl.BlockSpec((1, 1, d), by_expert),
                  pl.BlockSpec((MXU_DIM, MXU_DIM), lambda i, *_: (0, 0))],
        out_specs=pl.BlockSpec((TM_E, d // 2), lambda i, *_: (i, 0)),
        scratch_shapes=[pltpu.VMEM((2, d, two_f), F32), pltpu.VMEM((2, dff, d), F32), pltpu.SemaphoreType.DMA((2, 2)),
                        pltpu.VMEM((d, two_f), BF16), pltpu.VMEM((dff, d), BF16)],
    )
    return pl.pallas_call(
        _experts_kernel,
        out_shape=jax.ShapeDtypeStruct((n_rows, d // 2), jnp.uint32),
        grid_spec=grid_spec,
        compiler_params=_params(("arbitrary",)),
        name="experts",
    )(tile_expert, tile_first, tile_used, tile_next, tile_slot.astype(jnp.int32),
      xs, w1, b1g, b1l, w2, b2, _deinterleave_perm())


def _combine_kernel(run_row_ref, x1_ref, rt_ref, g2_ref, y_ref, o_ref, stage_ref, sem, *, tile_offset):
    i = pl.program_id(0)
    buf = i % 2

    def start_runs(step, b):
        _for_each_run(step + tile_offset, run_row_ref, lambda slot, row: pltpu.make_async_copy(
            y_ref.at[pl.ds(row, RUN_ROWS), :], stage_ref.at[b, pl.ds(slot, RUN_ROWS), :], sem.at[b]).start())

    def wait_runs(step, b):
        _for_each_run(step + tile_offset, run_row_ref, lambda slot, row: pltpu.make_async_copy(
            y_ref.at[pl.ds(row, RUN_ROWS), :], stage_ref.at[b, pl.ds(slot, RUN_ROWS), :], sem.at[b]).wait())

    @pl.when(i == 0)
    def _():
        stage_ref[...] = jnp.zeros_like(stage_ref)
        start_runs(0, 0)

    @pl.when(i + 1 < pl.num_programs(0))
    def _():
        start_runs(i + 1, 1 - buf)

    wait_runs(i, buf)

    rt = rt_ref[...]
    tt = rt.shape[0]
    slot_id = lax.broadcasted_iota(jnp.int32, (tt, stage_ref.shape[1]), 1).astype(F32)
    sel = jnp.zeros(slot_id.shape, F32)
    for k in range(TOP_K):
        sel = jnp.where(slot_id == rt[:, _R_SLOT + k:_R_SLOT + k + 1], 1.0, sel)
    sel = sel.astype(BF16)
    o_ref[...] = x1_ref[...] + g2_ref[...] * _dot(sel, _unpack_bf16_pairs(stage_ref[buf]))


def _combine(x1, route, run_row, g2, seq, y_rows, row_offset):
    r, d = x1.shape
    assert r % TT == 0 and row_offset % TT == 0
    off = row_offset // TT
    g2_o, rpm = _mod_operand(g2, seq, TT)
    if rpm is None:
        g2_spec = pl.BlockSpec((TT, d), lambda i, *_: (i, 0))
    else:
        g2_spec = pl.BlockSpec((None, 1, d), lambda i, *_: ((i * TT) // rpm, 0, 0))
    grid_spec = pltpu.PrefetchScalarGridSpec(
        num_scalar_prefetch=1,
        grid=(r // TT,),
        in_specs=[pl.BlockSpec((TT, d), lambda i, *_: (i, 0)),
                  pl.BlockSpec((TT, LANES), lambda i, *_: (i + off, 0)),
                  g2_spec,
                  pl.BlockSpec(memory_space=pl.ANY)],
        out_specs=pl.BlockSpec((TT, d), lambda i, *_: (i, 0)),
        scratch_shapes=[pltpu.VMEM((2, N_SLOTS, d // 2), jnp.uint32), pltpu.SemaphoreType.DMA((2,))],
    )
    return pl.pallas_call(
        functools.partial(_combine_kernel, tile_offset=off),
        out_shape=jax.ShapeDtypeStruct((r, d), F32),
        grid_spec=grid_spec,
        compiler_params=_params(("arbitrary",)),
        name="combine",
    )(run_row, x1, route, g2_o, y_rows)


def _gmlp_tables(w_s, b_s, seq_s):
    g = w_s.shape[0]
    tril = jnp.tril(w_s)
    short = jnp.tril(w_s[:, :seq_s, :seq_s])
    rep = CHUNK_B // seq_s
    blockdiag = jnp.zeros_like(w_s)
    for r in range(rep):
        blockdiag = blockdiag.at[:, r * seq_s:(r + 1) * seq_s, r * seq_s:(r + 1) * seq_s].set(short)
    ws_all = jnp.stack([tril, blockdiag]).astype(BF16)
    bs_full = jnp.repeat(b_s.T, HEAD_DIM, axis=1)
    bs_short = jnp.tile(jnp.repeat(b_s[:, :seq_s].T, HEAD_DIM, axis=1), (rep, 1))
    return ws_all, jnp.stack([bs_full, bs_short])


def _layer(xp, xs, cache_k, cache_v, cp, cs, p):
    bp, tp, d = xp.shape
    bs_, ts, _ = xs.shape
    assert tp % CHUNK_B == 0 and CHUNK_B % ts == 0
    rp, rs = bp * tp, bs_ * ts
    n_tok = rp + rs

    mod = _ada(jnp.concatenate([cp, cs], axis=0), p['w_ada'], p['b_ada'])
    mods = [mod[:, j * d:(j + 1) * d] for j in range(6)]
    mp = [m[:bp] for m in mods]
    msm = [m[bp:] for m in mods]

    tile_h = lambda g: jnp.tile(g, N_HEADS).reshape(1, D_HEADS)
    bd = jnp.kron(jnp.eye(MXU_DIM // HEAD_DIM, dtype=F32),
                  jnp.full((HEAD_DIM, HEAD_DIM), 1.0 / HEAD_DIM, F32)).astype(BF16)
    w_in_b = p['w_in'].astype(BF16)
    n1g = p['norm1_g'].reshape(1, d)
    common = (n1g, w_in_b, tile_h(p['q_norm_g']), tile_h(p['k_norm_g']), tile_h(p['v_norm_g']), bd)
    q_p, k_p, kb_p, v_p, vb_p, u_p, vn_p, sga_p, sgb_p = _inproj(
        xp.reshape(rp, d), mp[0], mp[1], tp, *common, want_vn32=False, kv_transposed=True)
    q_s, k_s, kb_s, v_s, vb_s, u_s, vn_s, sga_s, sgb_s, vn32_s = _inproj(
        xs.reshape(rs, d), msm[0], msm[1], ts, *common, want_vn32=True, kv_transposed=False)

    ya_p = _attn_prompt(q_p, kb_p, vb_p, bp, tp)
    ya_s = _attn_sample(q_s, kb_s, vb_s, cache_k, cache_v, bs_, ts)

    ws_all, bs_all = _gmlp_tables(p['w_s'], p['b_s'], ts)
    wr = p['w_router']
    wr_hi = wr.astype(BF16)
    wr_lo = (wr - wr_hi.astype(F32)).astype(BF16)
    post_w = (p['w_pa'].astype(BF16), p['w_pb'].astype(BF16), p['w_o'].astype(BF16), p['norm2_g'].reshape(1, d),
              jnp.concatenate([wr_hi, wr_lo], axis=1), p['b_router'].reshape(N_EXPERTS, 1))
    x1_p, h2_p, lg_p = _post(xp.reshape(rp, d), ya_p, u_p, vn_p, sga_p, sgb_p, mp[2], mp[3], mp[4], tp,
                             ws_all, bs_all, 0, *post_w)
    x1_s, h2_s, lg_s = _post(xs.reshape(rs, d), ya_s, u_s, vn_s, sga_s, sgb_s, msm[2], msm[3], msm[4], ts,
                             ws_all, bs_all, 1, *post_w)

    route, route_t, table, counts = _route(lg_p, lg_s)

    counts = counts[:, 0].astype(jnp.int32)
    padded = (counts + TM_E - 1) // TM_E * TM_E
    pend = jnp.cumsum(padded)
    pstart = pend - padded
    table = jnp.swapaxes(table.reshape(-1, N_EXPERTS, LANES)[:, :, :3], 1, 2).astype(jnp.int32)
    n_token_tiles = n_tok // TT
    runs = (table[:, _T_COUNT] + RUN_ROWS - 1) // RUN_ROWS
    runs_end = jnp.cumsum(runs, axis=1)
    j = jnp.arange(RUNS_PER_TILE, dtype=jnp.int32)
    e_of = jnp.minimum(jnp.sum((runs_end[:, None, :] <= j[None, :, None]).astype(jnp.int32), axis=2), N_EXPERTS - 1)
    pick = (e_of[:, :, None] == jnp.arange(N_EXPERTS, dtype=jnp.int32)[None, None, :]).astype(jnp.int32)
    first_run = jnp.sum(pick * (runs_end - runs)[:, None, :], axis=2)
    first_row = jnp.sum(pick * (table[:, _T_BASE] + pstart[None, :])[:, None, :], axis=2)
    run_row = first_row + (j[None, :] - first_run) * RUN_ROWS
    is_run = j[None, :] < runs_end[:, -1:]
    n_tiles = -(-(n_tok * TOP_K + n_token_tiles * N_EXPERTS * (RUN_ROWS - 1)) // TM_E) + N_EXPERTS
    tile_start = jnp.arange(n_tiles, dtype=jnp.int32) * TM_E
    tile_used = (tile_start < pend[-1]).astype(jnp.int32)
    last_expert = jnp.sum((pend <= pend[-1] - 1).astype(jnp.int32))
    tile_expert = jnp.minimum(jnp.sum((pend[None, :] <= tile_start[:, None]).astype(jnp.int32), axis=1), last_expert)
    tile_first = jnp.concatenate([jnp.ones((1,), jnp.int32),
                                  (tile_expert[1:] != tile_expert[:-1]).astype(jnp.int32)])
    e_ids = jnp.arange(N_EXPERTS, dtype=jnp.int32)
    runs_later = jnp.logical_and(e_ids[None, :] > e_ids[:, None], (padded > 0)[None, :])
    next_expert = jnp.min(jnp.where(runs_later, e_ids[None, :], N_EXPERTS), axis=1)
    next_expert = jnp.where(next_expert == N_EXPERTS, -1, next_expert)
    tile_next = jnp.sum((tile_expert[:, None] == e_ids[None, :]).astype(jnp.int32) * next_expert[None, :], axis=1)

    spare_tiles = -(-2 * N_SLOTS // TM_E)
    spare_rows = j[None, :] * RUN_ROWS
    parity = (jnp.arange(n_token_tiles, dtype=jnp.int32) % 2)[:, None]
    groups = jnp.broadcast_to((runs_end[:, -1:] + RUN_UNROLL - 1) // RUN_UNROLL, (n_token_tiles, RUN_UNROLL))
    with_groups = lambda rows: jnp.concatenate([rows, groups], axis=1).reshape(-1)
    rows_out = with_groups(jnp.where(is_run, run_row, n_tiles * TM_E + parity * N_SLOTS + spare_rows))
    rows_in = with_groups(jnp.where(is_run, run_row, spare_rows))
    tail_tiles = jnp.stack([pend[-1] // TM_E, jnp.int32(n_tiles + spare_tiles)]).astype(jnp.int32)
    pad_runs = (padded - counts) // RUN_ROWS
    xs_rows = _dispatch(h2_p, h2_s, route_t, rows_out, pstart + counts, pad_runs, tail_tiles,
                        (n_tiles + spare_tiles) * TM_E)
    b1 = p['b1']
    ne = b1.shape[0]
    y_rows = _experts(xs_rows, tile_expert, tile_first, tile_used, tile_next, p['w1'],
                      b1[:, 0::2].reshape(ne, 1, -1), b1[:, 1::2].reshape(ne, 1, -1),
                      p['w2'], p['b2'].reshape(ne, 1, d))
    out_p = _combine(x1_p, route, rows_in, mp[5], tp, y_rows, 0)
    out_s = _combine(x1_s, route, rows_in, msm[5], ts, y_rows, rp)

    shp = lambda a, b, t: a.reshape(b, t, N_HEADS, HEAD_DIM)
    unt = lambda a: jnp.transpose(a.reshape(bp, N_HEADS, HEAD_DIM, tp), (0, 3, 1, 2))
    return (out_p.reshape(bp, tp, d), out_s.reshape(bs_, ts, d),
            unt(k_p), unt(v_p), shp(k_s, bs_, ts), shp(v_s, bs_, ts), shp(vn32_s, bs_, ts))


def kernel(x_prompt, x_sample, cache_k, cache_v, c_prompt, c_sample, norm1_g, norm2_g, w_ada, b_ada, w_in,
           q_norm_g, k_norm_g, v_norm_g, w_s, b_s, w_pa, w_pb, w_o, w_router, b_router, w1, b1, w2, b2):
    depth = w_in.shape[0]
    xp, xs = x_prompt, x_sample
    kp, vp, ks, vs, vbs = [], [], [], [], []
    for l in range(depth):
        p = {'norm1_g': norm1_g[l], 'norm2_g': norm2_g[l], 'w_ada': w_ada[l], 'b_ada': b_ada[l],
             'w_in': w_in[l], 'q_norm_g': q_norm_g[l], 'k_norm_g': k_norm_g[l], 'v_norm_g': v_norm_g[l],
             'w_s': w_s[l], 'b_s': b_s[l], 'w_pa': w_pa[l], 'w_pb': w_pb[l], 'w_o': w_o[l],
             'w_router': w_router[l], 'b_router': b_router[l], 'w1': w1[l], 'b1': b1[l],
             'w2': w2[l], 'b2': b2[l]}
        xp, xs, k_p, v_p, k_s, v_s, vn_s = _layer(xp, xs, cache_k[l], cache_v[l], c_prompt, c_sample, p)
        kp.append(k_p); vp.append(v_p); ks.append(k_s); vs.append(v_s); vbs.append(vn_s)
    return (xp, xs, jnp.stack(kp), jnp.stack(vp), jnp.stack(ks), jnp.stack(vs), jnp.stack(vbs))
```

```python
import functools

import jax
import jax.numpy as jnp
from jax import lax
from jax.experimental import pallas as pl
from jax.experimental.pallas import tpu as pltpu

F32 = jnp.float32
BF16 = jnp.bfloat16

LANES = 128
MXU_DIM = 256
VMEM_LIMIT = 56 * 1024 * 1024

HEAD_DIM = 64
N_HEADS = 8
D_HEADS = N_HEADS * HEAD_DIM
PAIRS = D_HEADS // LANES
CHUNK_B = 128
N_EXPERTS = 32
TOP_K = 4
RMS_EPS = 1e-6
SWIGLU_ALPHA = 1.702
SWIGLU_LIMIT = 7.0

TM_IN = 512
TQ = 512
TK = 256
SUB_Q = 128
LOOK_BACK = 256
WINDOW = SUB_Q + LOOK_BACK
TK_OLD = 128
TM_POST = 256
TT = 256
TM_E = 512
RUN_ROWS = 8
N_SLOTS = TT * TOP_K + N_EXPERTS * RUN_ROWS


def _params(sem, **kw):
    return pltpu.CompilerParams(dimension_semantics=sem, vmem_limit_bytes=VMEM_LIMIT, **kw)


def _dot(a, b):
    return jnp.dot(a, b, preferred_element_type=F32)


def _dot_nt(a, b):
    return lax.dot_general(a, b, (((1,), (1,)), ((), ())), preferred_element_type=F32)


def _gelu(x):
    return jax.nn.gelu(x)


def _sigmoid(x):
    return 1.0 / (1.0 + jnp.exp(-x))


def _ada_kernel(c_ref, w_ref, b_ref, o_ref):
    c = c_ref[...]
    s = c * _sigmoid(c)
    o_ref[...] = jnp.dot(s, w_ref[...], preferred_element_type=F32,
                         precision=lax.Precision.HIGHEST) + b_ref[...]


def _ada(c, w_ada, b_ada):
    nb, d = c.shape
    n = w_ada.shape[1]
    tn = 1024
    return pl.pallas_call(
        _ada_kernel,
        out_shape=jax.ShapeDtypeStruct((nb, n), F32),
        grid=(n // tn,),
        in_specs=[pl.BlockSpec((nb, d), lambda j: (0, 0)),
                  pl.BlockSpec((d, tn), lambda j: (0, j)),
                  pl.BlockSpec((1, tn), lambda j: (0, j))],
        out_specs=pl.BlockSpec((nb, tn), lambda j: (0, j)),
        compiler_params=_params(("parallel",)),
        name="ada",
    )(c, w_ada, b_ada.reshape(1, n))


def _head_mean_sq(p, bd):
    sq = (p * p).astype(BF16)
    return jnp.concatenate([_dot(sq[:, :MXU_DIM], bd), _dot(sq[:, MXU_DIM:], bd)], axis=-1)


def _inproj_kernel(x_ref, sh_ref, sc_ref, g_ref, w_ref, qg_ref, kg_ref, vg_ref, bd_ref,
                   q_ref, k32_ref, kb_ref, v32_ref, vb_ref, u_ref, vnb_ref, sga_ref, sgb_ref, *vn32_ref,
                   kv_transposed):
    kv_out = (lambda a: a.T) if kv_transposed else (lambda a: a)
    x = x_ref[...]
    ms = jnp.mean(x * x, axis=-1, keepdims=True)
    xn = x * lax.rsqrt(ms + RMS_EPS) * g_ref[...]
    h = (xn * (1.0 + sc_ref[...]) + sh_ref[...]).astype(BF16)
    bd = bd_ref[...]
    d = x.shape[1]

    def sec(a, n):
        return _dot(h, w_ref[:, a:a + n])

    o = 0
    pq = sec(o, D_HEADS); o += D_HEADS
    q = pq * lax.rsqrt(_head_mean_sq(pq, bd) + RMS_EPS) * qg_ref[...]
    q_ref[...] = (q * (HEAD_DIM ** -0.5)).astype(BF16)
    pk = sec(o, D_HEADS); o += D_HEADS
    k = pk * lax.rsqrt(_head_mean_sq(pk, bd) + RMS_EPS) * kg_ref[...]
    k32_ref[...] = kv_out(k)
    kb_ref[...] = k.astype(BF16)
    pv = sec(o, D_HEADS); o += D_HEADS
    v32_ref[...] = kv_out(pv)
    vb_ref[...] = pv.astype(BF16)
    pu = sec(o, D_HEADS); o += D_HEADS
    u_ref[...] = _gelu(pu).astype(BF16)
    pvb = _gelu(sec(o, D_HEADS)); o += D_HEADS
    vn = pvb * lax.rsqrt(_head_mean_sq(pvb, bd) + RMS_EPS) * vg_ref[...]
    vnb_ref[...] = vn.astype(BF16)
    if vn32_ref:
        vn32_ref[0][...] = vn
    sga_ref[...] = _sigmoid(sec(o, d)).astype(BF16); o += d
    sgb_ref[...] = _sigmoid(sec(o, d)).astype(BF16)


def _mod_spec(tm, arr, rows_per_mod):
    if rows_per_mod is None:
        return pl.BlockSpec((tm, arr.shape[-1]), lambda i: (i, 0))
    return pl.BlockSpec((None, 1, arr.shape[-1]), lambda i: ((i * tm) // rows_per_mod, 0, 0))


def _mod_operand(m, seq, tm):
    if seq % tm == 0:
        return m[:, None, :], seq
    return jnp.repeat(m, seq, axis=0), None


def _inproj(x, sh, sc, seq, g, w_in_b, qg, kg, vg, bd, want_vn32, kv_transposed):
    r, d = x.shape
    tm = min(TM_IN, r)
    assert r % tm == 0 and (not kv_transposed or seq % tm == 0)
    sh_o, rpm = _mod_operand(sh, seq, tm)
    sc_o, _ = _mod_operand(sc, seq, tm)
    ncol = w_in_b.shape[1]
    const = lambda i: (0, 0)
    row = lambda n: pl.BlockSpec((tm, n), lambda i: (i, 0))
    outs = [jax.ShapeDtypeStruct((r, D_HEADS), BF16),
            jax.ShapeDtypeStruct((r, D_HEADS), F32),
            jax.ShapeDtypeStruct((r, D_HEADS), BF16),
            jax.ShapeDtypeStruct((r, D_HEADS), F32),
            jax.ShapeDtypeStruct((r, D_HEADS), BF16),
            jax.ShapeDtypeStruct((r, D_HEADS), BF16),
            jax.ShapeDtypeStruct((r, D_HEADS), BF16),
            jax.ShapeDtypeStruct((r, d), BF16),
            jax.ShapeDtypeStruct((r, d), BF16)]
    out_specs = [row(D_HEADS)] * 7 + [row(d), row(d)]
    if kv_transposed:
        per_seq = seq // tm
        for j in (1, 3):
            outs[j] = jax.ShapeDtypeStruct((r // seq, D_HEADS, seq), F32)
            out_specs[j] = pl.BlockSpec((None, D_HEADS, tm), lambda i: (i // per_seq, 0, i % per_seq))
    if want_vn32:
        outs.append(jax.ShapeDtypeStruct((r, D_HEADS), F32))
        out_specs.append(row(D_HEADS))
    return pl.pallas_call(
        functools.partial(_inproj_kernel, kv_transposed=kv_transposed),
        out_shape=outs,
        grid=(r // tm,),
        in_specs=[row(d), _mod_spec(tm, sh_o, rpm), _mod_spec(tm, sc_o, rpm),
                  pl.BlockSpec((1, d), const),
                  pl.BlockSpec((d, ncol), const, pipeline_mode=pl.Buffered(1)),
                  pl.BlockSpec((1, D_HEADS), const), pl.BlockSpec((1, D_HEADS), const),
                  pl.BlockSpec((1, D_HEADS), const), pl.BlockSpec((MXU_DIM, MXU_DIM), const)],
        out_specs=out_specs,
        compiler_params=_params(("parallel",)),
        name="inproj",
    )(x, sh_o, sc_o, g, w_in_b, qg, kg, vg, bd)


LOG_WEIGHT_CUTOFF = -104.0
MASKED_SCORE = -1e30


def _sb_weights(z, u, c, mask):
    if mask is not None:
        z = jnp.where(mask, z, MASKED_SCORE)
    sp = jnp.maximum(z, 0.0) + jnp.log(1.0 + jnp.exp(-jnp.abs(z)))
    hi = sp.astype(BF16)
    lo = (sp - hi.astype(F32)).astype(BF16)
    tk, tu = z.shape[1], u.shape[0]
    old = tk - tu
    later = _dot(hi[:, old:], u) + _dot(lo[:, old:], u)
    total = jnp.sum(sp[:, old:], axis=-1, keepdims=True)
    if old:
        u_old = u[:old, :old]
        later = jnp.concatenate([_dot(hi[:, :old], u_old) + _dot(lo[:, :old], u_old) + total, later], axis=1)
        total = total + jnp.sum(sp[:, :old], axis=-1, keepdims=True)
    w = jnp.exp((z + c) - (sp + later))
    return w, c - total


def _pair_select(shape):
    return lax.broadcasted_iota(jnp.int32, shape, 1) < HEAD_DIM


def _causal_mask(m, tq, tk):
    row = lax.broadcasted_iota(jnp.int32, (m, tk), 0) & (tq - 1)
    return lax.broadcasted_iota(jnp.int32, (m, tk), 1) < row


def _stack_heads(q):
    first = _pair_select(q.shape)
    zero = jnp.zeros_like(q)
    return jnp.concatenate([jnp.where(first, q, zero), jnp.where(first, zero, q)], axis=0)


def _attn_prompt_kernel(q_ref, k_ref, v_ref, u_ref, o_ref, acc_ref, c_ref):
    i = pl.program_id(2)
    tq = q_ref.shape[0]
    nsub = tq // SUB_Q
    sub_rows = 2 * SUB_Q
    q = q_ref[...]
    u = u_ref[...]
    first = _pair_select((SUB_Q, LANES))
    m = nsub * sub_rows

    def pv_sub(wb, s, vs):
        w0 = wb[s * sub_rows:s * sub_rows + SUB_Q]
        w1 = wb[s * sub_rows + SUB_Q:(s + 1) * sub_rows]
        return jnp.where(first, _dot(w0, vs), _dot(w1, vs))

    qs = [_stack_heads(q[s * SUB_Q:(s + 1) * SUB_Q]) for s in range(nsub)]
    starts = [pl.multiple_of(jnp.maximum(i * tq + s * SUB_Q - LOOK_BACK, 0), SUB_Q) for s in range(nsub)]
    row = lax.broadcasted_iota(jnp.int32, (m, 1), 0)
    window_start = jnp.zeros((m, 1), jnp.int32) + starts[0]
    for s in range(1, nsub):
        window_start = jnp.where(row >= s * sub_rows, starts[s], window_start)
    q_pos = i * tq + (row // sub_rows) * SUB_Q + (row & (SUB_Q - 1))

    z = jnp.concatenate([_dot_nt(qs[s], k_ref[pl.ds(starts[s], WINDOW), :]) for s in range(nsub)], axis=0)
    col = lax.broadcasted_iota(jnp.int32, (m, WINDOW), 1)
    w, c = _sb_weights(z, u, jnp.zeros((m, 1), F32), window_start + col < q_pos)
    wb = w.astype(BF16)
    acc_ref[...] = jnp.concatenate([pv_sub(wb, s, v_ref[pl.ds(starts[s], WINDOW), :]) for s in range(nsub)], axis=0)
    c_ref[...] = c
    q_all = jnp.concatenate(qs, axis=0)
    col = lax.broadcasted_iota(jnp.int32, (m, TK_OLD), 1)
    u_old = u[:TK_OLD, :TK_OLD]

    def cond(state):
        j, cmax = state
        return jnp.logical_and(j >= 0, cmax > LOG_WEIGHT_CUTOFF)

    def body(state):
        j, _ = state
        start = pl.multiple_of(j * TK_OLD, TK_OLD)
        vs = v_ref[pl.ds(start, TK_OLD), :]
        w, c = _sb_weights(_dot_nt(q_all, k_ref[pl.ds(start, TK_OLD), :]), u_old, c_ref[...],
                           j * TK_OLD + col < window_start)
        wb = w.astype(BF16)
        acc_ref[...] += jnp.concatenate([pv_sub(wb, s, vs) for s in range(nsub)], axis=0)
        c_ref[...] = c
        return j - 1, jnp.max(c)

    j0 = lax.div(starts[nsub - 1] + (TK_OLD - 1), TK_OLD) - 1
    lax.while_loop(cond, body, (j0, jnp.max(c)))
    o_ref[...] = acc_ref[...].astype(o_ref.dtype)


def _upper_ones(n):
    return (lax.broadcasted_iota(jnp.int32, (n, n), 0) > lax.broadcasted_iota(jnp.int32, (n, n), 1)).astype(BF16)


def _attn_prompt(q, kb, vb, batch, seq):
    assert seq % TQ == 0 and TQ % TK == 0 and TQ % SUB_Q == 0 and seq >= WINDOW
    nq = seq // TQ
    return pl.pallas_call(
        _attn_prompt_kernel,
        out_shape=jax.ShapeDtypeStruct(q.shape, BF16),
        grid=(batch, PAIRS, nq),
        in_specs=[pl.BlockSpec((TQ, LANES), lambda b, p, i: (b * nq + i, p)),
                  pl.BlockSpec((seq, LANES), lambda b, p, i: (b, p)),
                  pl.BlockSpec((seq, LANES), lambda b, p, i: (b, p)),
                  pl.BlockSpec((TK, TK), lambda b, p, i: (0, 0))],
        out_specs=pl.BlockSpec((TQ, LANES), lambda b, p, i: (b * nq + i, p)),
        scratch_shapes=[pltpu.VMEM((TQ, LANES), F32), pltpu.VMEM((2 * TQ, 1), F32)],
        compiler_params=_params(("parallel", "parallel", "arbitrary")),
        name="attn_prompt",
    )(q, kb, vb, _upper_ones(TK))


def _attn_sample_kernel(q_ref, kn_ref, vn_ref, kc_ref, vc_ref, un_ref, u_ref, o_ref,
                        kbuf_ref, vbuf_ref, sem, acc_ref, c_ref):
    b = pl.program_id(0)
    tq = q_ref.shape[0]
    nblk = kc_ref.shape[3] // TK
    head = lambda a, h: a[:, h * HEAD_DIM:(h + 1) * HEAD_DIM]
    q = q_ref[...]
    qh = [head(q, h) for h in range(N_HEADS)]

    def copies(j, slot):
        rows = pl.ds(pl.multiple_of(j * TK, TK), TK)
        return (pltpu.make_async_copy(kc_ref.at[b, :, :, rows], kbuf_ref.at[slot], sem.at[0, slot]),
                pltpu.make_async_copy(vc_ref.at[b, :, :, rows], vbuf_ref.at[slot], sem.at[1, slot]))

    def start(j, slot):
        for cp in copies(j, slot):
            cp.start()

    def wait(j, slot):
        for cp in copies(j, slot):
            cp.wait()

    def sweep(score, apply, u, c, mask):
        z = jnp.concatenate([score(qh[h], h) for h in range(N_HEADS)], axis=0)
        w, c = _sb_weights(z, u, c, mask)
        wb = w.astype(BF16)
        pv = jnp.concatenate([apply(wb[h * tq:(h + 1) * tq], h) for h in range(N_HEADS)], axis=-1)
        return pv, c

    start(nblk - 1, 0)
    kn = kn_ref[...]
    vn = vn_ref[...]
    pv, c = sweep(lambda qq, h: _dot_nt(qq, head(kn, h)), lambda ww, h: _dot(ww, head(vn, h)), un_ref[...],
                  jnp.zeros((N_HEADS * tq, 1), F32), _causal_mask(N_HEADS * tq, tq, tq))
    acc_ref[...] = pv
    c_ref[...] = c
    u = u_ref[...]

    def cond(state):
        j, _, cmax = state
        return jnp.logical_and(j >= 0, cmax > LOG_WEIGHT_CUTOFF)

    def body(state):
        j, slot, _ = state
        wait(j, slot)

        @pl.when(j > 0)
        def _():
            start(j - 1, 1 - slot)

        pv, c = sweep(lambda qq, h: _dot(qq, kbuf_ref[slot, h].astype(BF16)),
                      lambda ww, h: _dot_nt(ww, vbuf_ref[slot, h].astype(BF16)), u, c_ref[...], None)
        acc_ref[...] += pv
        c_ref[...] = c
        return j - 1, 1 - slot, jnp.max(c)

    j, slot, _ = lax.while_loop(cond, body, (nblk - 1, 0, jnp.max(c)))

    @pl.when(j >= 0)
    def _():
        wait(j, slot)

    o_ref[...] = acc_ref[...].astype(o_ref.dtype)


def _attn_sample(q, kb, vb, cache_k, cache_v, batch, seq):
    past = cache_k.shape[1]
    assert past % TK == 0 and past >= TK
    cache_k = jnp.transpose(cache_k, (0, 2, 3, 1))
    cache_v = jnp.transpose(cache_v, (0, 2, 3, 1))
    row = pl.BlockSpec((seq, D_HEADS), lambda b: (b, 0))
    return pl.pallas_call(
        _attn_sample_kernel,
        out_shape=jax.ShapeDtypeStruct(q.shape, BF16),
        grid=(batch,),
        in_specs=[row, row, row,
                  pl.BlockSpec(memory_space=pl.ANY), pl.BlockSpec(memory_space=pl.ANY),
                  pl.BlockSpec((seq, seq), lambda b: (0, 0)),
                  pl.BlockSpec((TK, TK), lambda b: (0, 0))],
        out_specs=row,
        scratch_shapes=[pltpu.VMEM((2, N_HEADS, HEAD_DIM, TK), F32), pltpu.VMEM((2, N_HEADS, HEAD_DIM, TK), F32),
                        pltpu.SemaphoreType.DMA((2, 2)),
                        pltpu.VMEM((seq, D_HEADS), F32), pltpu.VMEM((N_HEADS * seq, 1), F32)],
        compiler_params=_params(("arbitrary",)),
        name="attn_sample",
    )(q, kb, vb, cache_k, cache_v, _upper_ones(seq), _upper_ones(TK))


def _pack_bf16_pairs(x):
    half = x.shape[1] // 2
    a = pltpu.bitcast(x[:, :half], jnp.uint32)
    b = pltpu.bitcast(x[:, half:], jnp.uint32)
    return a | (b >> 16)


def _unpack_bf16_pairs(p):
    a = pltpu.bitcast(p & jnp.uint32(0xFFFF0000), F32).astype(BF16)
    b = pltpu.bitcast(p << 16, F32).astype(BF16)
    return jnp.concatenate([a, b], axis=1)


def _post_kernel(x_ref, ya_ref, u_ref, vn_ref, sga_ref, sgb_ref, g1_ref, sh2_ref, sc2_ref,
                 ws_ref, bs_ref, wpa_ref, wpb_ref, wo_ref, n2g_ref, wr_ref, br_ref,
                 x1_ref, h2_ref, lg_ref):
    tm = x_ref.shape[0]
    nch = tm // CHUNK_B
    first = (lax.broadcasted_iota(jnp.int32, (CHUNK_B, nch * LANES), 1) & (LANES - 1)) < HEAD_DIM
    mixed = []
    for p in range(PAIRS):
        vnp = jnp.concatenate([vn_ref[c * CHUNK_B:(c + 1) * CHUNK_B, p * LANES:(p + 1) * LANES] for c in range(nch)],
                              axis=1)
        mixed.append(jnp.where(first, _dot(ws_ref[0, 2 * p], vnp), _dot(ws_ref[0, 2 * p + 1], vnp)))
    chunks = []
    for c in range(nch):
        s = jnp.concatenate([mp[:, c * LANES:(c + 1) * LANES] for mp in mixed], axis=-1) + bs_ref[0]
        chunks.append((u_ref[c * CHUNK_B:(c + 1) * CHUNK_B, :].astype(F32) * s).astype(BF16))
    yb = jnp.concatenate(chunks, axis=0)
    merged = (sga_ref[...].astype(F32) * _dot(ya_ref[...], wpa_ref[...])
              + sgb_ref[...].astype(F32) * _dot(yb, wpb_ref[...]))
    x1 = x_ref[...] + g1_ref[...] * _dot(merged.astype(BF16), wo_ref[...])
    x1_ref[...] = x1
    ms = jnp.mean(x1 * x1, axis=-1, keepdims=True)
    h2 = (x1 * lax.rsqrt(ms + RMS_EPS) * n2g_ref[...]) * (1.0 + sc2_ref[...]) + sh2_ref[...]
    hi = h2.astype(BF16)
    hf = hi.astype(F32)
    lo = (h2 - hf).astype(BF16)
    h2_ref[...] = _pack_bf16_pairs(hf)
    l2 = _dot(hi, wr_ref[...]) + _dot(lo, wr_ref[...])
    l2t = jnp.concatenate([l2, jnp.zeros((l2.shape[0], LANES - l2.shape[1]), F32)], axis=1).T
    lg_ref[...] = l2t[:N_EXPERTS] + l2t[N_EXPERTS:2 * N_EXPERTS] + br_ref[...]


def _post(x, ya, u, vn, sga, sgb, g1, sh2, sc2, seq, ws_all, bs_all, ws_idx, wpa, wpb, wo, n2g, wr, br):
    r, d = x.shape
    tm = min(TM_POST, r)
    assert r % tm == 0 and tm % CHUNK_B == 0
    g1_o, rpm = _mod_operand(g1, seq, tm)
    sh2_o, _ = _mod_operand(sh2, seq, tm)
    sc2_o, _ = _mod_operand(sc2, seq, tm)
    const = lambda i: (0, 0)
    row = lambda n: pl.BlockSpec((tm, n), lambda i: (i, 0))
    return pl.pallas_call(
        _post_kernel,
        out_shape=[jax.ShapeDtypeStruct((r, d), F32), jax.ShapeDtypeStruct((r, d // 2), jnp.uint32),
                   jax.ShapeDtypeStruct((N_EXPERTS, r), F32)],
        grid=(r // tm,),
        in_specs=[row(d), row(D_HEADS), row(D_HEADS), row(D_HEADS), row(d), row(d),
                  _mod_spec(tm, g1_o, rpm), _mod_spec(tm, sh2_o, rpm), _mod_spec(tm, sc2_o, rpm),
                  pl.BlockSpec((1, N_HEADS, CHUNK_B, CHUNK_B), lambda i: (ws_idx, 0, 0, 0)),
                  pl.BlockSpec((1, CHUNK_B, D_HEADS), lambda i: (ws_idx, 0, 0)),
                  pl.BlockSpec(wpa.shape, const), pl.BlockSpec(wpb.shape, const), pl.BlockSpec(wo.shape, const),
                  pl.BlockSpec((1, d), const), pl.BlockSpec(wr.shape, const), pl.BlockSpec((N_EXPERTS, 1), const)],
        out_specs=[row(d), row(d // 2), pl.BlockSpec((N_EXPERTS, tm), lambda i: (0, i))],
        compiler_params=_params(("parallel",)),
        name="post",
    )(x, ya, u, vn, sga, sgb, g1_o, sh2_o, sc2_o, ws_all, bs_all, wpa, wpb, wo, n2g, wr, br)


_R_IDX, _R_GATE, _R_SLOT = 0, TOP_K, 2 * TOP_K
_R_ROWS = 16
_T_COUNT, _T_SLOT, _T_BASE = 0, 1, 2


def _route_kernel(lp_ref, ls_ref, tri_ref, scan_ref, o_ref, ot_ref, tbl_ref, cnt_ref, carry_ref, *, n_prompt_tiles):
    i = pl.program_id(0)

    @pl.when(i == 0)
    def _():
        carry_ref[...] = jnp.zeros_like(carry_ref)

    l = jnp.where(i < n_prompt_tiles, lp_ref[...], ls_ref[...])
    ne, tt = l.shape
    eid = lax.broadcasted_iota(jnp.int32, l.shape, 0).astype(F32)
    cur = l
    vals, idxs = [], []
    for _ in range(TOP_K):
        m = jnp.max(cur, axis=0, keepdims=True)
        idx = jnp.min(jnp.where(cur == m, eid, float(ne)), axis=0, keepdims=True)
        vals.append(m)
        idxs.append(idx)
        cur = jnp.where(eid == idx, -jnp.inf, cur)
    es = [jnp.exp(v - vals[0]) for v in vals]
    den = es[0] + es[1] + es[2] + es[3]
    onehot = jnp.zeros(l.shape, F32)
    for idx in idxs:
        onehot = onehot + (eid == idx).astype(F32)
    within = _dot(onehot.astype(BF16), tri_ref[...])
    count = jnp.sum(onehot, axis=1, keepdims=True)
    runs = jnp.ceil(count * (1.0 / RUN_ROWS))
    first_slot = RUN_ROWS * _dot(scan_ref[...], jnp.broadcast_to(runs, (ne, LANES)).astype(BF16))[:, 0:1]
    slot_of = within + first_slot
    row = lax.broadcasted_iota(jnp.int32, (_R_ROWS, tt), 0)
    out = jnp.zeros((_R_ROWS, tt), F32)
    for k in range(TOP_K):
        slot = jnp.sum(jnp.where(eid == idxs[k], slot_of, 0.0), axis=0, keepdims=True)
        out = jnp.where(row == _R_IDX + k, idxs[k], out)
        out = jnp.where(row == _R_GATE + k, es[k] / den, out)
        out = jnp.where(row == _R_SLOT + k, slot, out)
    ot_ref[...] = out
    o_ref[...] = jnp.concatenate([out, jnp.zeros((LANES - _R_ROWS, tt), F32)], axis=0).T
    lane = lax.broadcasted_iota(jnp.int32, (ne, LANES), 1)
    tbl_ref[...] = jnp.where(lane == _T_COUNT, count, jnp.where(lane == _T_SLOT, first_slot, carry_ref[...]))
    carry_ref[...] += RUN_ROWS * runs
    cnt_ref[...] = jnp.broadcast_to(carry_ref[...], cnt_ref.shape)


def _route(lg_p, lg_s):
    rp, rs = lg_p.shape[1], lg_s.shape[1]
    assert rp % TT == 0 and rs % TT == 0
    npt, nst = rp // TT, rs // TT
    nt = npt + nst
    iota = lambda shape, ax: lax.broadcasted_iota(jnp.int32, shape, ax)
    tri = (iota((TT, TT), 0) < iota((TT, TT), 1)).astype(BF16)
    scan = (iota((N_EXPERTS, N_EXPERTS), 1) < iota((N_EXPERTS, N_EXPERTS), 0)).astype(BF16)
    return pl.pallas_call(
        functools.partial(_route_kernel, n_prompt_tiles=npt),
        out_shape=[jax.ShapeDtypeStruct((rp + rs, LANES), F32), jax.ShapeDtypeStruct((_R_ROWS, rp + rs), F32),
                   jax.ShapeDtypeStruct((nt * N_EXPERTS, LANES), F32), jax.ShapeDtypeStruct((N_EXPERTS, LANES), F32)],
        grid=(nt,),
        in_specs=[pl.BlockSpec((N_EXPERTS, TT), lambda i: (0, jnp.minimum(i, npt - 1))),
                  pl.BlockSpec((N_EXPERTS, TT), lambda i: (0, jnp.maximum(i - npt, 0))),
                  pl.BlockSpec((TT, TT), lambda i: (0, 0)),
                  pl.BlockSpec((N_EXPERTS, N_EXPERTS), lambda i: (0, 0))],
        out_specs=[pl.BlockSpec((TT, LANES), lambda i: (i, 0)), pl.BlockSpec((_R_ROWS, TT), lambda i: (0, i)),
                   pl.BlockSpec((N_EXPERTS, LANES), lambda i: (i, 0)),
                   pl.BlockSpec((N_EXPERTS, LANES), lambda i: (0, 0))],
        scratch_shapes=[pltpu.VMEM((N_EXPERTS, 1), F32)],
        compiler_params=_params(("arbitrary",)),
        name="route",
    )(lg_p, lg_s, tri, scan)


RUNS_PER_TILE = N_SLOTS // RUN_ROWS
RUN_UNROLL = 8
RUN_TABLE_STRIDE = RUNS_PER_TILE + RUN_UNROLL


def _for_each_run(tile, run_row_ref, fn):
    base = tile * RUN_TABLE_STRIDE

    def group(g, _):
        for u in range(RUN_UNROLL):
            j = g * RUN_UNROLL + u
            row = run_row_ref[base + j]
            fn(pl.multiple_of(j * RUN_ROWS, RUN_ROWS), pl.multiple_of(row, RUN_ROWS))
        return 0

    lax.fori_loop(0, run_row_ref[base + RUNS_PER_TILE], group, 0)


def _dispatch_kernel(run_row_ref, pad_start_ref, pad_runs_ref, tail_ref,
                     rt_ref, hp_ref, hs_ref, xs_ref, stage_ref, zero_ref, sem, psem, *, n_prompt_tiles):
    i = pl.program_id(0)
    last = pl.num_programs(0) - 1
    buf = i % 2

    def start_runs(tile, b):
        _for_each_run(tile, run_row_ref, lambda slot, row: pltpu.make_async_copy(
            stage_ref.at[b, pl.ds(slot, RUN_ROWS), :], xs_ref.at[pl.ds(row, RUN_ROWS), :], sem.at[b]).start())

    def wait_runs(tile, b):
        _for_each_run(tile, run_row_ref, lambda slot, row: pltpu.make_async_copy(
            stage_ref.at[b, pl.ds(slot, RUN_ROWS), :], xs_ref.at[pl.ds(row, RUN_ROWS), :], sem.at[b]).wait())

    @pl.when(i == 0)
    def _():
        zero_ref[...] = jnp.zeros_like(zero_ref)

        def tail(t, _):
            rows = pl.ds(pl.multiple_of(t * TM_E, TM_E), TM_E)
            cp = pltpu.make_async_copy(zero_ref, xs_ref.at[rows, :], psem)
            cp.start()
            cp.wait()
            return 0

        lax.fori_loop(tail_ref[0], tail_ref[1], tail, 0)

        def per_expert(e, _):
            def pad_copy(r):
                rows = pl.ds(pl.multiple_of(pad_start_ref[e] + r * RUN_ROWS, RUN_ROWS), RUN_ROWS)
                return pltpu.make_async_copy(zero_ref.at[pl.ds(0, RUN_ROWS), :], xs_ref.at[rows, :], psem)

            def start(r, _):
                pad_copy(r).start()
                return 0

            def wait(r, _):
                pad_copy(r).wait()
                return 0

            lax.fori_loop(0, pad_runs_ref[e], start, 0)
            lax.fori_loop(0, pad_runs_ref[e], wait, 0)
            return 0

        lax.fori_loop(0, N_EXPERTS, per_expert, 0)

    @pl.when(i >= 2)
    def _():
        wait_runs(i - 2, buf)

    rt = rt_ref[...]
    x = _unpack_bf16_pairs(jnp.where(i < n_prompt_tiles, hp_ref[...], hs_ref[...]))
    n_slots = stage_ref.shape[1]
    slot_id = lax.broadcasted_iota(jnp.int32, (n_slots, rt.shape[1]), 0).astype(F32)
    sel = jnp.zeros(slot_id.shape, F32)
    gsel = jnp.zeros(slot_id.shape, F32)
    for k in range(TOP_K):
        hit = slot_id == rt[_R_SLOT + k:_R_SLOT + k + 1, :]
        sel = jnp.where(hit, 1.0, sel)
        gsel = jnp.where(hit, rt[_R_GATE + k:_R_GATE + k + 1, :], gsel)
    rows = _dot(sel.astype(BF16), x)
    gate = jnp.sum(gsel, axis=-1, keepdims=True)
    packed = _pack_bf16_pairs(rows)
    stage_ref[buf, :, :packed.shape[1]] = packed
    stage_ref[buf, :, packed.shape[1]:] = pltpu.bitcast(jnp.broadcast_to(gate, (n_slots, LANES)), jnp.uint32)
    start_runs(i, buf)

    @pl.when(i == last)
    def _():
        @pl.when(i >= 1)
        def _():
            wait_runs(i - 1, 1 - buf)

        wait_runs(i, buf)


def _dispatch(h_p, h_s, route_t, run_row, pad_start, pad_runs, tail_tiles, n_rows):
    rp, dp = h_p.shape
    rs = h_s.shape[0]
    assert rp % TT == 0 and rs % TT == 0
    npt, nst = rp // TT, rs // TT
    width = dp + LANES
    grid_spec = pltpu.PrefetchScalarGridSpec(
        num_scalar_prefetch=4,
        grid=(npt + nst,),
        in_specs=[pl.BlockSpec((_R_ROWS, TT), lambda i, *_: (0, i)),
                  pl.BlockSpec((TT, dp), lambda i, *_: (jnp.minimum(i, npt - 1), 0)),
                  pl.BlockSpec((TT, dp), lambda i, *_: (jnp.maximum(i - npt, 0), 0))],
        out_specs=pl.BlockSpec(memory_space=pl.ANY),
        scratch_shapes=[pltpu.VMEM((2, N_SLOTS, width), jnp.uint32), pltpu.VMEM((TM_E, width), jnp.uint32),
                        pltpu.SemaphoreType.DMA((2,)), pltpu.SemaphoreType.DMA(())],
    )
    return pl.pallas_call(
        functools.partial(_dispatch_kernel, n_prompt_tiles=npt),
        out_shape=jax.ShapeDtypeStruct((n_rows, width), jnp.uint32),
        grid_spec=grid_spec,
        compiler_params=_params(("arbitrary",), has_side_effects=True),
        name="dispatch",
    )(run_row, pad_start, pad_runs, tail_tiles, route_t, h_p, h_s)


def _experts_kernel(te_ref, tf_ref, tu_ref, tn_ref, ts_ref, x_ref, w1_ref, b1g_ref, b1l_ref, w2_ref, b2_ref, perm_ref,
                    y_ref, w1buf_ref, w2buf_ref, wsem, w1s_ref, w2s_ref):
    i = pl.program_id(0)
    dff = w2_ref.shape[1]

    def weight_copies(e, slot):
        return (pltpu.make_async_copy(w1_ref.at[e], w1buf_ref.at[slot], wsem.at[0, slot]),
                pltpu.make_async_copy(w2_ref.at[e], w2buf_ref.at[slot], wsem.at[1, slot]))

    @pl.when(tf_ref[i] == 1)
    def _():
        slot = ts_ref[i]

        @pl.when(i == 0)
        def _():
            for cp in weight_copies(te_ref[i], slot):
                cp.start()

        for cp in weight_copies(te_ref[i], slot):
            cp.wait()

        @pl.when(tn_ref[i] >= 0)
        def _():
            for cp in weight_copies(tn_ref[i], 1 - slot):
                cp.start()

        perm = perm_ref[...]
        half = MXU_DIM // 2
        for c in range(w1_ref.shape[2] // MXU_DIM):
            blk = w1buf_ref[slot, :, c * MXU_DIM:(c + 1) * MXU_DIM].astype(BF16)
            sp = _dot(blk, perm).astype(BF16)
            w1s_ref[:, c * half:(c + 1) * half] = sp[:, :half]
            w1s_ref[:, dff + c * half:dff + (c + 1) * half] = sp[:, half:]
        w2s_ref[...] = w2buf_ref[slot].astype(BF16)

    @pl.when(tu_ref[i] == 1)
    def _():
        d = w2_ref.shape[2]
        x = _unpack_bf16_pairs(x_ref[:, :d // 2])
        gate = pltpu.bitcast(x_ref[:, d // 2:], F32)
        hdn = _dot(x, w1s_ref[...])
        glu = jnp.minimum(hdn[:, :dff] + b1g_ref[0], SWIGLU_LIMIT)
        lin = jnp.clip(hdn[:, dff:] + b1l_ref[0], -SWIGLU_LIMIT, SWIGLU_LIMIT)
        act = glu * _sigmoid(SWIGLU_ALPHA * glu) * (lin + 1.0)
        y = (_dot(act.astype(BF16), w2s_ref[...]) + b2_ref[0]) * jnp.concatenate([gate] * (d // LANES), axis=1)
        y_ref[...] = _pack_bf16_pairs(y.astype(BF16).astype(F32))

    @pl.when(tu_ref[i] == 0)
    def _():
        y_ref[...] = jnp.zeros_like(y_ref)


def _deinterleave_perm():
    j = lax.broadcasted_iota(jnp.int32, (MXU_DIM, MXU_DIM), 0)
    c = lax.broadcasted_iota(jnp.int32, (MXU_DIM, MXU_DIM), 1)
    half = MXU_DIM // 2
    src = jnp.where(c < half, 2 * c, 2 * (c - half) + 1)
    return (j == src).astype(BF16)


def _experts(xs, tile_expert, tile_first, tile_used, tile_next, w1, b1g, b1l, w2, b2):
    width = xs.shape[1]
    n_tiles = tile_expert.shape[0]
    n_rows = n_tiles * TM_E
    ne, d, two_f = w1.shape
    dff = two_f // 2
    assert width == d // 2 + LANES
    tile_slot = (jnp.cumsum(tile_first) + 1) % 2
    by_expert = lambda i, te, *_: (te[i], 0, 0)
    grid_spec = pltpu.PrefetchScalarGridSpec(
        num_scalar_prefetch=5,
        grid=(n_tiles,),
        in_specs=[pl.BlockSpec((TM_E, width), lambda i, te, tf, tu, *_: (i * tu[i], 0)),
                  pl.BlockSpec(memory_space=pl.ANY),
                  pl.BlockSpec((1, 1, dff), by_expert),
                  pl.BlockSpec((1, 1, dff), by_expert),
                  pl.BlockSpec(memory_space=pl.ANY),
                  pl.BlockSpec((1, 1, d), by_expert),
                  pl.BlockSpec((MXU_DIM, MXU_DIM), lambda i, *_: (0, 0))],
        out_specs=pl.BlockSpec((TM_E, d // 2), lambda i, *_: (i, 0)),
        scratch_shapes=[pltpu.VMEM((2, d, two_f), F32), pltpu.VMEM((2, dff, d), F32), pltpu.SemaphoreType.DMA((2, 2)),
                        pltpu.VMEM((d, two_f), BF16), pltpu.VMEM((dff, d), BF16)],
    )
    return pl.pallas_call(
        _experts_kernel,
        out_shape=jax.ShapeDtypeStruct((n_rows, d // 2), jnp.uint32),
        grid_spec=grid_spec,
        compiler_params=_params(("arbitrary",)),
        name="experts",
    )(tile_expert, tile_first, tile_used, tile_next, tile_slot.astype(jnp.int32),
      xs, w1, b1g, b1l, w2, b2, _deinterleave_perm())


def _combine_kernel(run_row_ref, x1_ref, rt_ref, g2_ref, y_ref, o_ref, stage_ref, sem, *, tile_offset):
    i = pl.program_id(0)
    buf = i % 2

    def start_runs(step, b):
        _for_each_run(step + tile_offset, run_row_ref, lambda slot, row: pltpu.make_async_copy(
            y_ref.at[pl.ds(row, RUN_ROWS), :], stage_ref.at[b, pl.ds(slot, RUN_ROWS), :], sem.at[b]).start())

    def wait_runs(step, b):
        _for_each_run(step + tile_offset, run_row_ref, lambda slot, row: pltpu.make_async_copy(
            y_ref.at[pl.ds(row, RUN_ROWS), :], stage_ref.at[b, pl.ds(slot, RUN_ROWS), :], sem.at[b]).wait())

    @pl.when(i == 0)
    def _():
        stage_ref[...] = jnp.zeros_like(stage_ref)
        start_runs(0, 0)

    @pl.when(i + 1 < pl.num_programs(0))
    def _():
        start_runs(i + 1, 1 - buf)

    wait_runs(i, buf)

    rt = rt_ref[...]
    tt = rt.shape[0]
    slot_id = lax.broadcasted_iota(jnp.int32, (tt, stage_ref.shape[1]), 1).astype(F32)
    sel = jnp.zeros(slot_id.shape, F32)
    for k in range(TOP_K):
        sel = jnp.where(slot_id == rt[:, _R_SLOT + k:_R_SLOT + k + 1], 1.0, sel)
    sel = sel.astype(BF16)
    o_ref[...] = x1_ref[...] + g2_ref[...] * _dot(sel, _unpack_bf16_pairs(stage_ref[buf]))


def _combine(x1, route, run_row, g2, seq, y_rows, row_offset):
    r, d = x1.shape
    assert r % TT == 0 and row_offset % TT == 0
    off = row_offset // TT
    g2_o, rpm = _mod_operand(g2, seq, TT)
    if rpm is None:
        g2_spec = pl.BlockSpec((TT, d), lambda i, *_: (i, 0))
    else:
        g2_spec = pl.BlockSpec((None, 1, d), lambda i, *_: ((i * TT) // rpm, 0, 0))
    grid_spec = pltpu.PrefetchScalarGridSpec(
        num_scalar_prefetch=1,
        grid=(r // TT,),
        in_specs=[pl.BlockSpec((TT, d), lambda i, *_: (i, 0)),
                  pl.BlockSpec((TT, LANES), lambda i, *_: (i + off, 0)),
                  g2_spec,
                  pl.BlockSpec(memory_space=pl.ANY)],
        out_specs=pl.BlockSpec((TT, d), lambda i, *_: (i, 0)),
        scratch_shapes=[pltpu.VMEM((2, N_SLOTS, d // 2), jnp.uint32), pltpu.SemaphoreType.DMA((2,))],
    )
    return pl.pallas_call(
        functools.partial(_combine_kernel, tile_offset=off),
        out_shape=jax.ShapeDtypeStruct((r, d), F32),
        grid_spec=grid_spec,
        compiler_params=_params(("arbitrary",)),
        name="combine",
    )(run_row, x1, route, g2_o, y_rows)


def _gmlp_tables(w_s, b_s, seq_s):
    g = w_s.shape[0]
    tril = jnp.tril(w_s)
    short = jnp.tril(w_s[:, :seq_s, :seq_s])
    rep = CHUNK_B // seq_s
    blockdiag = jnp.zeros_like(w_s)
    for r in range(rep):
        blockdiag = blockdiag.at[:, r * seq_s:(r + 1) * seq_s, r * seq_s:(r + 1) * seq_s].set(short)
    ws_all = jnp.stack([tril, blockdiag]).astype(BF16)
    bs_full = jnp.repeat(b_s.T, HEAD_DIM, axis=1)
    bs_short = jnp.tile(jnp.repeat(b_s[:, :seq_s].T, HEAD_DIM, axis=1), (rep, 1))
    return ws_all, jnp.stack([bs_full, bs_short])


def _layer(xp, xs, cache_k, cache_v, cp, cs, p):
    bp, tp, d = xp.shape
    bs_, ts, _ = xs.shape
    assert tp % CHUNK_B == 0 and CHUNK_B % ts == 0
    rp, rs = bp * tp, bs_ * ts
    n_tok = rp + rs

    mod = _ada(jnp.concatenate([cp, cs], axis=0), p['w_ada'], p['b_ada'])
    mods = [mod[:, j * d:(j + 1) * d] for j in range(6)]
    mp = [m[:bp] for m in mods]
    msm = [m[bp:] for m in mods]

    tile_h = lambda g: jnp.tile(g, N_HEADS).reshape(1, D_HEADS)
    bd = jnp.kron(jnp.eye(MXU_DIM // HEAD_DIM, dtype=F32),
                  jnp.full((HEAD_DIM, HEAD_DIM), 1.0 / HEAD_DIM, F32)).astype(BF16)
    w_in_b = p['w_in'].astype(BF16)
    n1g = p['norm1_g'].reshape(1, d)
    common = (n1g, w_in_b, tile_h(p['q_norm_g']), tile_h(p['k_norm_g']), tile_h(p['v_norm_g']), bd)
    q_p, k_p, kb_p, v_p, vb_p, u_p, vn_p, sga_p, sgb_p = _inproj(
        xp.reshape(rp, d), mp[0], mp[1], tp, *common, want_vn32=False, kv_transposed=True)
    q_s, k_s, kb_s, v_s, vb_s, u_s, vn_s, sga_s, sgb_s, vn32_s = _inproj(
        xs.reshape(rs, d), msm[0], msm[1], ts, *common, want_vn32=True, kv_transposed=False)

    ya_p = _attn_prompt(q_p, kb_p, vb_p, bp, tp)
    ya_s = _attn_sample(q_s, kb_s, vb_s, cache_k, cache_v, bs_, ts)

    ws_all, bs_all = _gmlp_tables(p['w_s'], p['b_s'], ts)
    wr = p['w_router']
    wr_hi = wr.astype(BF16)
    wr_lo = (wr - wr_hi.astype(F32)).astype(BF16)
    post_w = (p['w_pa'].astype(BF16), p['w_pb'].astype(BF16), p['w_o'].astype(BF16), p['norm2_g'].reshape(1, d),
              jnp.concatenate([wr_hi, wr_lo], axis=1), p['b_router'].reshape(N_EXPERTS, 1))
    x1_p, h2_p, lg_p = _post(xp.reshape(rp, d), ya_p, u_p, vn_p, sga_p, sgb_p, mp[2], mp[3], mp[4], tp,
                             ws_all, bs_all, 0, *post_w)
    x1_s, h2_s, lg_s = _post(xs.reshape(rs, d), ya_s, u_s, vn_s, sga_s, sgb_s, msm[2], msm[3], msm[4], ts,
                             ws_all, bs_all, 1, *post_w)

    route, route_t, table, counts = _route(lg_p, lg_s)

    counts = counts[:, 0].astype(jnp.int32)
    padded = (counts + TM_E - 1) // TM_E * TM_E
    pend = jnp.cumsum(padded)
    pstart = pend - padded
    table = jnp.swapaxes(table.reshape(-1, N_EXPERTS, LANES)[:, :, :3], 1, 2).astype(jnp.int32)
    n_token_tiles = n_tok // TT
    runs = (table[:, _T_COUNT] + RUN_ROWS - 1) // RUN_ROWS
    runs_end = jnp.cumsum(runs, axis=1)
    j = jnp.arange(RUNS_PER_TILE, dtype=jnp.int32)
    e_of = jnp.minimum(jnp.sum((runs_end[:, None, :] <= j[None, :, None]).astype(jnp.int32), axis=2), N_EXPERTS - 1)
    pick = (e_of[:, :, None] == jnp.arange(N_EXPERTS, dtype=jnp.int32)[None, None, :]).astype(jnp.int32)
    first_run = jnp.sum(pick * (runs_end - runs)[:, None, :], axis=2)
    first_row = jnp.sum(pick * (table[:, _T_BASE] + pstart[None, :])[:, None, :], axis=2)
    run_row = first_row + (j[None, :] - first_run) * RUN_ROWS
    is_run = j[None, :] < runs_end[:, -1:]
    n_tiles = -(-(n_tok * TOP_K + n_token_tiles * N_EXPERTS * (RUN_ROWS - 1)) // TM_E) + N_EXPERTS
    tile_start = jnp.arange(n_tiles, dtype=jnp.int32) * TM_E
    tile_used = (tile_start < pend[-1]).astype(jnp.int32)
    last_expert = jnp.sum((pend <= pend[-1] - 1).astype(jnp.int32))
    tile_expert = jnp.minimum(jnp.sum((pend[None, :] <= tile_start[:, None]).astype(jnp.int32), axis=1), last_expert)
    tile_first = jnp.concatenate([jnp.ones((1,), jnp.int32),
                                  (tile_expert[1:] != tile_expert[:-1]).astype(jnp.int32)])
    e_ids = jnp.arange(N_EXPERTS, dtype=jnp.int32)
    runs_later = jnp.logical_and(e_ids[None, :] > e_ids[:, None], (padded > 0)[None, :])
    next_expert = jnp.min(jnp.where(runs_later, e_ids[None, :], N_EXPERTS), axis=1)
    next_expert = jnp.where(next_expert == N_EXPERTS, -1, next_expert)
    tile_next = jnp.sum((tile_expert[:, None] == e_ids[None, :]).astype(jnp.int32) * next_expert[None, :], axis=1)

    spare_tiles = -(-2 * N_SLOTS // TM_E)
    spare_rows = j[None, :] * RUN_ROWS
    parity = (jnp.arange(n_token_tiles, dtype=jnp.int32) % 2)[:, None]
    groups = jnp.broadcast_to((runs_end[:, -1:] + RUN_UNROLL - 1) // RUN_UNROLL, (n_token_tiles, RUN_UNROLL))
    with_groups = lambda rows: jnp.concatenate([rows, groups], axis=1).reshape(-1)
    rows_out = with_groups(jnp.where(is_run, run_row, n_tiles * TM_E + parity * N_SLOTS + spare_rows))
    rows_in = with_groups(jnp.where(is_run, run_row, spare_rows))
    tail_tiles = jnp.stack([pend[-1] // TM_E, jnp.int32(n_tiles + spare_tiles)]).astype(jnp.int32)
    pad_runs = (padded - counts) // RUN_ROWS
    xs_rows = _dispatch(h2_p, h2_s, route_t, rows_out, pstart + counts, pad_runs, tail_tiles,
                        (n_tiles + spare_tiles) * TM_E)
    b1 = p['b1']
    ne = b1.shape[0]
    y_rows = _experts(xs_rows, tile_expert, tile_first, tile_used, tile_next, p['w1'],
                      b1[:, 0::2].reshape(ne, 1, -1), b1[:, 1::2].reshape(ne, 1, -1),
                      p['w2'], p['b2'].reshape(ne, 1, d))
    out_p = _combine(x1_p, route, rows_in, mp[5], tp, y_rows, 0)
    out_s = _combine(x1_s, route, rows_in, msm[5], ts, y_rows, rp)

    shp = lambda a, b, t: a.reshape(b, t, N_HEADS, HEAD_DIM)
    unt = lambda a: jnp.transpose(a.reshape(bp, N_HEADS, HEAD_DIM, tp), (0, 3, 1, 2))
    return (out_p.reshape(bp, tp, d), out_s.reshape(bs_, ts, d),
            unt(k_p), unt(v_p), shp(k_s, bs_, ts), shp(v_s, bs_, ts), shp(vn32_s, bs_, ts))


def kernel(x_prompt, x_sample, cache_k, cache_v, c_prompt, c_sample, norm1_g, norm2_g, w_ada, b_ada, w_in,
           q_norm_g, k_norm_g, v_norm_g, w_s, b_s, w_pa, w_pb, w_o, w_router, b_router, w1, b1, w2, b2):
    depth = w_in.shape[0]
    xp, xs = x_prompt, x_sample
    kp, vp, ks, vs, vbs = [], [], [], [], []
    for l in range(depth):
        p = {'norm1_g': norm1_g[l], 'norm2_g': norm2_g[l], 'w_ada': w_ada[l], 'b_ada': b_ada[l],
             'w_in': w_in[l], 'q_norm_g': q_norm_g[l], 'k_norm_g': k_norm_g[l], 'v_norm_g': v_norm_g[l],
             'w_s': w_s[l], 'b_s': b_s[l], 'w_pa': w_pa[l], 'w_pb': w_pb[l], 'w_o': w_o[l],
             'w_router': w_router[l], 'b_router': b_router[l], 'w1': w1[l], 'b1': b1[l],
             'w2': w2[l], 'b2': b2[l]}
        xp, xs, k_p, v_p, k_s, v_s, vn_s = _layer(xp, xs, cache_k[l], cache_v[l], c_prompt, c_sample, p)
        kp.append(k_p); vp.append(v_p); ks.append(k_s); vs.append(v_s); vbs.append(vn_s)
    return (xp, xs, jnp.stack(kp), jnp.stack(vp), jnp.stack(ks), jnp.stack(vs), jnp.stack(vbs))
```

```python
import functools

import jax
import jax.numpy as jnp
from jax import lax
from jax.experimental import pallas as pl
from jax.experimental.pallas import tpu as pltpu

F32 = jnp.float32
BF16 = jnp.bfloat16

LANES = 128
MXU_DIM = 256
VMEM_LIMIT = 56 * 1024 * 1024

HEAD_DIM = 64
N_HEADS = 8
D_HEADS = N_HEADS * HEAD_DIM
PAIRS = D_HEADS // LANES
CHUNK_B = 128
N_EXPERTS = 32
TOP_K = 4
RMS_EPS = 1e-6
SWIGLU_ALPHA = 1.702
SWIGLU_LIMIT = 7.0

TM_IN = 512
TQ = 512
TK = 256
SUB_Q = 128
LOOK_BACK = 256
WINDOW = SUB_Q + LOOK_BACK
TK_OLD = 128
TM_POST = 256
TT = 256
TM_E = 512
RUN_ROWS = 8
N_SLOTS = TT * TOP_K + N_EXPERTS * RUN_ROWS


def _params(sem, **kw):
    return pltpu.CompilerParams(dimension_semantics=sem, vmem_limit_bytes=VMEM_LIMIT, **kw)


def _dot(a, b):
    return jnp.dot(a, b, preferred_element_type=F32)


def _dot_nt(a, b):
    return lax.dot_general(a, b, (((1,), (1,)), ((), ())), preferred_element_type=F32)


def _gelu(x):
    return jax.nn.gelu(x)


def _sigmoid(x):
    return 0.5 * jnp.tanh(0.5 * x) + 0.5


def _ada_kernel(c_ref, w_ref, b_ref, o_ref):
    c = c_ref[...]
    s = c * _sigmoid(c)
    o_ref[...] = jnp.dot(s, w_ref[...], preferred_element_type=F32,
                         precision=lax.Precision.HIGHEST) + b_ref[...]


def _ada(c, w_ada, b_ada):
    nb, d = c.shape
    n = w_ada.shape[1]
    tn = 1024
    return pl.pallas_call(
        _ada_kernel,
        out_shape=jax.ShapeDtypeStruct((nb, n), F32),
        grid=(n // tn,),
        in_specs=[pl.BlockSpec((nb, d), lambda j: (0, 0)),
                  pl.BlockSpec((d, tn), lambda j: (0, j)),
                  pl.BlockSpec((1, tn), lambda j: (0, j))],
        out_specs=pl.BlockSpec((nb, tn), lambda j: (0, j)),
        compiler_params=_params(("parallel",)),
        name="ada",
    )(c, w_ada, b_ada.reshape(1, n))


def _head_mean_sq(p, bd):
    sq = (p * p).astype(BF16)
    return jnp.concatenate([_dot(sq[:, :MXU_DIM], bd), _dot(sq[:, MXU_DIM:], bd)], axis=-1)


def _inproj_kernel(x_ref, sh_ref, sc_ref, g_ref, w_ref, qg_ref, kg_ref, vg_ref, bd_ref,
                   q_ref, k32_ref, kb_ref, v32_ref, vb_ref, u_ref, vnb_ref, sga_ref, sgb_ref, *vn32_ref,
                   kv_transposed):
    kv_out = (lambda a: a.T) if kv_transposed else (lambda a: a)
    x = x_ref[...]
    ms = jnp.mean(x * x, axis=-1, keepdims=True)
    xn = x * lax.rsqrt(ms + RMS_EPS) * g_ref[...]
    h = (xn * (1.0 + sc_ref[...]) + sh_ref[...]).astype(BF16)
    bd = bd_ref[...]
    d = x.shape[1]

    def sec(a, n):
        return _dot(h, w_ref[:, a:a + n])

    o = 0
    pq = sec(o, D_HEADS); o += D_HEADS
    q = pq * lax.rsqrt(_head_mean_sq(pq, bd) + RMS_EPS) * qg_ref[...]
    q_ref[...] = (q * (HEAD_DIM ** -0.5)).astype(BF16)
    pk = sec(o, D_HEADS); o += D_HEADS
    k = pk * lax.rsqrt(_head_mean_sq(pk, bd) + RMS_EPS) * kg_ref[...]
    k32_ref[...] = kv_out(k)
    kb_ref[...] = k.astype(BF16)
    pv = sec(o, D_HEADS); o += D_HEADS
    v32_ref[...] = kv_out(pv)
    vb_ref[...] = pv.astype(BF16)
    pu = sec(o, D_HEADS); o += D_HEADS
    u_ref[...] = _gelu(pu).astype(BF16)
    pvb = _gelu(sec(o, D_HEADS)); o += D_HEADS
    vn = pvb * lax.rsqrt(_head_mean_sq(pvb, bd) + RMS_EPS) * vg_ref[...]
    vnb_ref[...] = vn.astype(BF16)
    if vn32_ref:
        vn32_ref[0][...] = vn
    sga_ref[...] = _sigmoid(sec(o, d)).astype(BF16); o += d
    sgb_ref[...] = _sigmoid(sec(o, d)).astype(BF16)


def _mod_spec(tm, arr, rows_per_mod):
    if rows_per_mod is None:
        return pl.BlockSpec((tm, arr.shape[-1]), lambda i: (i, 0))
    return pl.BlockSpec((None, 1, arr.shape[-1]), lambda i: ((i * tm) // rows_per_mod, 0, 0))


def _mod_operand(m, seq, tm):
    if seq % tm == 0:
        return m[:, None, :], seq
    return jnp.repeat(m, seq, axis=0), None


def _inproj(x, sh, sc, seq, g, w_in_b, qg, kg, vg, bd, want_vn32, kv_transposed):
    r, d = x.shape
    tm = min(TM_IN, r)
    assert r % tm == 0 and (not kv_transposed or seq % tm == 0)
    sh_o, rpm = _mod_operand(sh, seq, tm)
    sc_o, _ = _mod_operand(sc, seq, tm)
    ncol = w_in_b.shape[1]
    const = lambda i: (0, 0)
    row = lambda n: pl.BlockSpec((tm, n), lambda i: (i, 0))
    outs = [jax.ShapeDtypeStruct((r, D_HEADS), BF16),
            jax.ShapeDtypeStruct((r, D_HEADS), F32),
            jax.ShapeDtypeStruct((r, D_HEADS), BF16),
            jax.ShapeDtypeStruct((r, D_HEADS), F32),
            jax.ShapeDtypeStruct((r, D_HEADS), BF16),
            jax.ShapeDtypeStruct((r, D_HEADS), BF16),
            jax.ShapeDtypeStruct((r, D_HEADS), BF16),
            jax.ShapeDtypeStruct((r, d), BF16),
            jax.ShapeDtypeStruct((r, d), BF16)]
    out_specs = [row(D_HEADS)] * 7 + [row(d), row(d)]
    if kv_transposed:
        per_seq = seq // tm
        for j in (1, 3):
            outs[j] = jax.ShapeDtypeStruct((r // seq, D_HEADS, seq), F32)
            out_specs[j] = pl.BlockSpec((None, D_HEADS, tm), lambda i: (i // per_seq, 0, i % per_seq))
    if want_vn32:
        outs.append(jax.ShapeDtypeStruct((r, D_HEADS), F32))
        out_specs.append(row(D_HEADS))
    return pl.pallas_call(
        functools.partial(_inproj_kernel, kv_transposed=kv_transposed),
        out_shape=outs,
        grid=(r // tm,),
        in_specs=[row(d), _mod_spec(tm, sh_o, rpm), _mod_spec(tm, sc_o, rpm),
                  pl.BlockSpec((1, d), const),
                  pl.BlockSpec((d, ncol), const, pipeline_mode=pl.Buffered(1)),
                  pl.BlockSpec((1, D_HEADS), const), pl.BlockSpec((1, D_HEADS), const),
                  pl.BlockSpec((1, D_HEADS), const), pl.BlockSpec((MXU_DIM, MXU_DIM), const)],
        out_specs=out_specs,
        compiler_params=_params(("parallel",)),
        name="inproj",
    )(x, sh_o, sc_o, g, w_in_b, qg, kg, vg, bd)


LOG_WEIGHT_CUTOFF = -104.0
MASKED_SCORE = -1e30


def _sb_weights(z, u, c, mask):
    if mask is not None:
        z = jnp.where(mask, z, MASKED_SCORE)
    sp = jnp.maximum(z, 0.0) + jnp.log(1.0 + jnp.exp(-jnp.abs(z)))
    hi = sp.astype(BF16)
    lo = (sp - hi.astype(F32)).astype(BF16)
    tk, tu = z.shape[1], u.shape[0]
    old = tk - tu
    later = _dot(hi[:, old:], u) + _dot(lo[:, old:], u)
    total = jnp.sum(sp[:, old:], axis=-1, keepdims=True)
    if old:
        u_old = u[:old, :old]
        later = jnp.concatenate([_dot(hi[:, :old], u_old) + _dot(lo[:, :old], u_old) + total, later], axis=1)
        total = total + jnp.sum(sp[:, :old], axis=-1, keepdims=True)
    w = jnp.exp((z + c) - (sp + later))
    return w, c - total


def _pair_select(shape):
    return lax.broadcasted_iota(jnp.int32, shape, 1) < HEAD_DIM


def _causal_mask(m, tq, tk):
    row = lax.broadcasted_iota(jnp.int32, (m, tk), 0) & (tq - 1)
    return lax.broadcasted_iota(jnp.int32, (m, tk), 1) < row


def _stack_heads(q):
    first = _pair_select(q.shape)
    zero = jnp.zeros_like(q)
    return jnp.concatenate([jnp.where(first, q, zero), jnp.where(first, zero, q)], axis=0)


def _attn_prompt_kernel(q_ref, k_ref, v_ref, u_ref, o_ref, acc_ref, c_ref):
    i = pl.program_id(2)
    tq = q_ref.shape[0]
    nsub = tq // SUB_Q
    sub_rows = 2 * SUB_Q
    q = q_ref[...]
    u = u_ref[...]
    first = _pair_select((SUB_Q, LANES))
    m = nsub * sub_rows

    def pv_sub(wb, s, vs):
        w0 = wb[s * sub_rows:s * sub_rows + SUB_Q]
        w1 = wb[s * sub_rows + SUB_Q:(s + 1) * sub_rows]
        return jnp.where(first, _dot(w0, vs), _dot(w1, vs))

    qs = [_stack_heads(q[s * SUB_Q:(s + 1) * SUB_Q]) for s in range(nsub)]
    starts = [pl.multiple_of(jnp.maximum(i * tq + s * SUB_Q - LOOK_BACK, 0), SUB_Q) for s in range(nsub)]
    row = lax.broadcasted_iota(jnp.int32, (m, 1), 0)
    window_start = jnp.zeros((m, 1), jnp.int32) + starts[0]
    for s in range(1, nsub):
        window_start = jnp.where(row >= s * sub_rows, starts[s], window_start)
    q_pos = i * tq + (row // sub_rows) * SUB_Q + (row & (SUB_Q - 1))

    z = jnp.concatenate([_dot_nt(qs[s], k_ref[pl.ds(starts[s], WINDOW), :]) for s in range(nsub)], axis=0)
    col = lax.broadcasted_iota(jnp.int32, (m, WINDOW), 1)
    w, c = _sb_weights(z, u, jnp.zeros((m, 1), F32), window_start + col < q_pos)
    wb = w.astype(BF16)
    acc_ref[...] = jnp.concatenate([pv_sub(wb, s, v_ref[pl.ds(starts[s], WINDOW), :]) for s in range(nsub)], axis=0)
    c_ref[...] = c
    q_all = jnp.concatenate(qs, axis=0)
    col = lax.broadcasted_iota(jnp.int32, (m, TK_OLD), 1)
    u_old = u[:TK_OLD, :TK_OLD]

    def cond(state):
        j, cmax = state
        return jnp.logical_and(j >= 0, cmax > LOG_WEIGHT_CUTOFF)

    def body(state):
        j, _ = state
        start = pl.multiple_of(j * TK_OLD, TK_OLD)
        vs = v_ref[pl.ds(start, TK_OLD), :]
        w, c = _sb_weights(_dot_nt(q_all, k_ref[pl.ds(start, TK_OLD), :]), u_old, c_ref[...],
                           j * TK_OLD + col < window_start)
        wb = w.astype(BF16)
        acc_ref[...] += jnp.concatenate([pv_sub(wb, s, vs) for s in range(nsub)], axis=0)
        c_ref[...] = c
        return j - 1, jnp.max(c)

    j0 = lax.div(starts[nsub - 1] + (TK_OLD - 1), TK_OLD) - 1
    lax.while_loop(cond, body, (j0, jnp.max(c)))
    o_ref[...] = acc_ref[...].astype(o_ref.dtype)


def _upper_ones(n):
    return (lax.broadcasted_iota(jnp.int32, (n, n), 0) > lax.broadcasted_iota(jnp.int32, (n, n), 1)).astype(BF16)


def _attn_prompt(q, kb, vb, batch, seq):
    assert seq % TQ == 0 and TQ % TK == 0 and TQ % SUB_Q == 0 and seq >= WINDOW
    nq = seq // TQ
    return pl.pallas_call(
        _attn_prompt_kernel,
        out_shape=jax.ShapeDtypeStruct(q.shape, BF16),
        grid=(batch, PAIRS, nq),
        in_specs=[pl.BlockSpec((TQ, LANES), lambda b, p, i: (b * nq + i, p)),
                  pl.BlockSpec((seq, LANES), lambda b, p, i: (b, p)),
                  pl.BlockSpec((seq, LANES), lambda b, p, i: (b, p)),
                  pl.BlockSpec((TK, TK), lambda b, p, i: (0, 0))],
        out_specs=pl.BlockSpec((TQ, LANES), lambda b, p, i: (b * nq + i, p)),
        scratch_shapes=[pltpu.VMEM((TQ, LANES), F32), pltpu.VMEM((2 * TQ, 1), F32)],
        compiler_params=_params(("parallel", "parallel", "arbitrary")),
        name="attn_prompt",
    )(q, kb, vb, _upper_ones(TK))


def _attn_sample_kernel(q_ref, kn_ref, vn_ref, kc_ref, vc_ref, un_ref, u_ref, o_ref,
                        kbuf_ref, vbuf_ref, sem, acc_ref, c_ref):
    b = pl.program_id(0)
    tq = q_ref.shape[0]
    nblk = kc_ref.shape[3] // TK
    head = lambda a, h: a[:, h * HEAD_DIM:(h + 1) * HEAD_DIM]
    q = q_ref[...]
    qh = [head(q, h) for h in range(N_HEADS)]

    def copies(j, slot):
        rows = pl.ds(pl.multiple_of(j * TK, TK), TK)
        return (pltpu.make_async_copy(kc_ref.at[b, :, :, rows], kbuf_ref.at[slot], sem.at[0, slot]),
                pltpu.make_async_copy(vc_ref.at[b, :, :, rows], vbuf_ref.at[slot], sem.at[1, slot]))

    def start(j, slot):
        for cp in copies(j, slot):
            cp.start()

    def wait(j, slot):
        for cp in copies(j, slot):
            cp.wait()

    def sweep(score, apply, u, c, mask):
        z = jnp.concatenate([score(qh[h], h) for h in range(N_HEADS)], axis=0)
        w, c = _sb_weights(z, u, c, mask)
        wb = w.astype(BF16)
        pv = jnp.concatenate([apply(wb[h * tq:(h + 1) * tq], h) for h in range(N_HEADS)], axis=-1)
        return pv, c

    start(nblk - 1, 0)
    kn = kn_ref[...]
    vn = vn_ref[...]
    pv, c = sweep(lambda qq, h: _dot_nt(qq, head(kn, h)), lambda ww, h: _dot(ww, head(vn, h)), un_ref[...],
                  jnp.zeros((N_HEADS * tq, 1), F32), _causal_mask(N_HEADS * tq, tq, tq))
    acc_ref[...] = pv
    c_ref[...] = c
    u = u_ref[...]

    def cond(state):
        j, _, cmax = state
        return jnp.logical_and(j >= 0, cmax > LOG_WEIGHT_CUTOFF)

    def body(state):
        j, slot, _ = state
        wait(j, slot)

        @pl.when(j > 0)
        def _():
            start(j - 1, 1 - slot)

        pv, c = sweep(lambda qq, h: _dot(qq, kbuf_ref[slot, h].astype(BF16)),
                      lambda ww, h: _dot_nt(ww, vbuf_ref[slot, h].astype(BF16)), u, c_ref[...], None)
        acc_ref[...] += pv
        c_ref[...] = c
        return j - 1, 1 - slot, jnp.max(c)

    j, slot, _ = lax.while_loop(cond, body, (nblk - 1, 0, jnp.max(c)))

    @pl.when(j >= 0)
    def _():
        wait(j, slot)

    o_ref[...] = acc_ref[...].astype(o_ref.dtype)


def _attn_sample(q, kb, vb, cache_k, cache_v, batch, seq):
    past = cache_k.shape[1]
    assert past % TK == 0 and past >= TK
    cache_k = jnp.transpose(cache_k, (0, 2, 3, 1))
    cache_v = jnp.transpose(cache_v, (0, 2, 3, 1))
    row = pl.BlockSpec((seq, D_HEADS), lambda b: (b, 0))
    return pl.pallas_call(
        _attn_sample_kernel,
        out_shape=jax.ShapeDtypeStruct(q.shape, BF16),
        grid=(batch,),
        in_specs=[row, row, row,
                  pl.BlockSpec(memory_space=pl.ANY), pl.BlockSpec(memory_space=pl.ANY),
                  pl.BlockSpec((seq, seq), lambda b: (0, 0)),
                  pl.BlockSpec((TK, TK), lambda b: (0, 0))],
        out_specs=row,
        scratch_shapes=[pltpu.VMEM((2, N_HEADS, HEAD_DIM, TK), F32), pltpu.VMEM((2, N_HEADS, HEAD_DIM, TK), F32),
                        pltpu.SemaphoreType.DMA((2, 2)),
                        pltpu.VMEM((seq, D_HEADS), F32), pltpu.VMEM((N_HEADS * seq, 1), F32)],
        compiler_params=_params(("arbitrary",)),
        name="attn_sample",
    )(q, kb, vb, cache_k, cache_v, _upper_ones(seq), _upper_ones(TK))


def _pack_bf16_pairs(x):
    half = x.shape[1] // 2
    a = pltpu.bitcast(x[:, :half], jnp.uint32)
    b = pltpu.bitcast(x[:, half:], jnp.uint32)
    return a | (b >> 16)


def _unpack_bf16_pairs(p):
    a = pltpu.bitcast(p & jnp.uint32(0xFFFF0000), F32).astype(BF16)
    b = pltpu.bitcast(p << 16, F32).astype(BF16)
    return jnp.concatenate([a, b], axis=1)


def _post_kernel(x_ref, ya_ref, u_ref, vn_ref, sga_ref, sgb_ref, g1_ref, sh2_ref, sc2_ref,
                 ws_ref, bs_ref, wpa_ref, wpb_ref, wo_ref, n2g_ref, wr_ref, br_ref,
                 x1_ref, h2_ref, lg_ref):
    tm = x_ref.shape[0]
    nch = tm // CHUNK_B
    first = (lax.broadcasted_iota(jnp.int32, (CHUNK_B, nch * LANES), 1) & (LANES - 1)) < HEAD_DIM
    mixed = []
    for p in range(PAIRS):
        vnp = jnp.concatenate([vn_ref[c * CHUNK_B:(c + 1) * CHUNK_B, p * LANES:(p + 1) * LANES] for c in range(nch)],
                              axis=1)
        mixed.append(jnp.where(first, _dot(ws_ref[0, 2 * p], vnp), _dot(ws_ref[0, 2 * p + 1], vnp)))
    chunks = []
    for c in range(nch):
        s = jnp.concatenate([mp[:, c * LANES:(c + 1) * LANES] for mp in mixed], axis=-1) + bs_ref[0]
        chunks.append((u_ref[c * CHUNK_B:(c + 1) * CHUNK_B, :].astype(F32) * s).astype(BF16))
    yb = jnp.concatenate(chunks, axis=0)
    merged = (sga_ref[...].astype(F32) * _dot(ya_ref[...], wpa_ref[...])
              + sgb_ref[...].astype(F32) * _dot(yb, wpb_ref[...]))
    x1 = x_ref[...] + g1_ref[...] * _dot(merged.astype(BF16), wo_ref[...])
    x1_ref[...] = x1
    ms = jnp.mean(x1 * x1, axis=-1, keepdims=True)
    h2 = (x1 * lax.rsqrt(ms + RMS_EPS) * n2g_ref[...]) * (1.0 + sc2_ref[...]) + sh2_ref[...]
    hi = h2.astype(BF16)
    hf = hi.astype(F32)
    lo = (h2 - hf).astype(BF16)
    h2_ref[...] = _pack_bf16_pairs(hf)
    l2 = _dot(hi, wr_ref[...]) + _dot(lo, wr_ref[...])
    l2t = jnp.concatenate([l2, jnp.zeros((l2.shape[0], LANES - l2.shape[1]), F32)], axis=1).T
    lg_ref[...] = l2t[:N_EXPERTS] + l2t[N_EXPERTS:2 * N_EXPERTS] + br_ref[...]


def _post(x, ya, u, vn, sga, sgb, g1, sh2, sc2, seq, ws_all, bs_all, ws_idx, wpa, wpb, wo, n2g, wr, br):
    r, d = x.shape
    tm = min(TM_POST, r)
    assert r % tm == 0 and tm % CHUNK_B == 0
    g1_o, rpm = _mod_operand(g1, seq, tm)
    sh2_o, _ = _mod_operand(sh2, seq, tm)
    sc2_o, _ = _mod_operand(sc2, seq, tm)
    const = lambda i: (0, 0)
    row = lambda n: pl.BlockSpec((tm, n), lambda i: (i, 0))
    return pl.pallas_call(
        _post_kernel,
        out_shape=[jax.ShapeDtypeStruct((r, d), F32), jax.ShapeDtypeStruct((r, d // 2), jnp.uint32),
                   jax.ShapeDtypeStruct((N_EXPERTS, r), F32)],
        grid=(r // tm,),
        in_specs=[row(d), row(D_HEADS), row(D_HEADS), row(D_HEADS), row(d), row(d),
                  _mod_spec(tm, g1_o, rpm), _mod_spec(tm, sh2_o, rpm), _mod_spec(tm, sc2_o, rpm),
                  pl.BlockSpec((1, N_HEADS, CHUNK_B, CHUNK_B), lambda i: (ws_idx, 0, 0, 0)),
                  pl.BlockSpec((1, CHUNK_B, D_HEADS), lambda i: (ws_idx, 0, 0)),
                  pl.BlockSpec(wpa.shape, const), pl.BlockSpec(wpb.shape, const), pl.BlockSpec(wo.shape, const),
                  pl.BlockSpec((1, d), const), pl.BlockSpec(wr.shape, const), pl.BlockSpec((N_EXPERTS, 1), const)],
        out_specs=[row(d), row(d // 2), pl.BlockSpec((N_EXPERTS, tm), lambda i: (0, i))],
        compiler_params=_params(("parallel",)),
        name="post",
    )(x, ya, u, vn, sga, sgb, g1_o, sh2_o, sc2_o, ws_all, bs_all, wpa, wpb, wo, n2g, wr, br)


_R_IDX, _R_GATE, _R_SLOT = 0, TOP_K, 2 * TOP_K
_R_ROWS = 16
_T_COUNT, _T_SLOT, _T_BASE = 0, 1, 2


def _route_kernel(lp_ref, ls_ref, tri_ref, scan_ref, o_ref, ot_ref, tbl_ref, cnt_ref, carry_ref, *, n_prompt_tiles):
    i = pl.program_id(0)

    @pl.when(i == 0)
    def _():
        carry_ref[...] = jnp.zeros_like(carry_ref)

    l = jnp.where(i < n_prompt_tiles, lp_ref[...], ls_ref[...])
    ne, tt = l.shape
    eid = lax.broadcasted_iota(jnp.int32, l.shape, 0).astype(F32)
    cur = l
    vals, idxs = [], []
    for _ in range(TOP_K):
        m = jnp.max(cur, axis=0, keepdims=True)
        idx = jnp.min(jnp.where(cur == m, eid, float(ne)), axis=0, keepdims=True)
        vals.append(m)
        idxs.append(idx)
        cur = jnp.where(eid == idx, -jnp.inf, cur)
    es = [jnp.exp(v - vals[0]) for v in vals]
    den = es[0] + es[1] + es[2] + es[3]
    onehot = jnp.zeros(l.shape, F32)
    for idx in idxs:
        onehot = onehot + (eid == idx).astype(F32)
    within = _dot(onehot.astype(BF16), tri_ref[...])
    count = jnp.sum(onehot, axis=1, keepdims=True)
    runs = jnp.ceil(count * (1.0 / RUN_ROWS))
    first_slot = RUN_ROWS * _dot(scan_ref[...], jnp.broadcast_to(runs, (ne, LANES)).astype(BF16))[:, 0:1]
    slot_of = within + first_slot
    row = lax.broadcasted_iota(jnp.int32, (_R_ROWS, tt), 0)
    out = jnp.zeros((_R_ROWS, tt), F32)
    for k in range(TOP_K):
        slot = jnp.sum(jnp.where(eid == idxs[k], slot_of, 0.0), axis=0, keepdims=True)
        out = jnp.where(row == _R_IDX + k, idxs[k], out)
        out = jnp.where(row == _R_GATE + k, es[k] / den, out)
        out = jnp.where(row == _R_SLOT + k, slot, out)
    ot_ref[...] = out
    o_ref[...] = jnp.concatenate([out, jnp.zeros((LANES - _R_ROWS, tt), F32)], axis=0).T
    lane = lax.broadcasted_iota(jnp.int32, (ne, LANES), 1)
    tbl_ref[...] = jnp.where(lane == _T_COUNT, count, jnp.where(lane == _T_SLOT, first_slot, carry_ref[...]))
    carry_ref[...] += RUN_ROWS * runs
    cnt_ref[...] = jnp.broadcast_to(carry_ref[...], cnt_ref.shape)


def _route(lg_p, lg_s):
    rp, rs = lg_p.shape[1], lg_s.shape[1]
    assert rp % TT == 0 and rs % TT == 0
    npt, nst = rp // TT, rs // TT
    nt = npt + nst
    iota = lambda shape, ax: lax.broadcasted_iota(jnp.int32, shape, ax)
    tri = (iota((TT, TT), 0) < iota((TT, TT), 1)).astype(BF16)
    scan = (iota((N_EXPERTS, N_EXPERTS), 1) < iota((N_EXPERTS, N_EXPERTS), 0)).astype(BF16)
    return pl.pallas_call(
        functools.partial(_route_kernel, n_prompt_tiles=npt),
        out_shape=[jax.ShapeDtypeStruct((rp + rs, LANES), F32), jax.ShapeDtypeStruct((_R_ROWS, rp + rs), F32),
                   jax.ShapeDtypeStruct((nt * N_EXPERTS, LANES), F32), jax.ShapeDtypeStruct((N_EXPERTS, LANES), F32)],
        grid=(nt,),
        in_specs=[pl.BlockSpec((N_EXPERTS, TT), lambda i: (0, jnp.minimum(i, npt - 1))),
                  pl.BlockSpec((N_EXPERTS, TT), lambda i: (0, jnp.maximum(i - npt, 0))),
                  pl.BlockSpec((TT, TT), lambda i: (0, 0)),
                  pl.BlockSpec((N_EXPERTS, N_EXPERTS), lambda i: (0, 0))],
        out_specs=[pl.BlockSpec((TT, LANES), lambda i: (i, 0)), pl.BlockSpec((_R_ROWS, TT), lambda i: (0, i)),
                   pl.BlockSpec((N_EXPERTS, LANES), lambda i: (i, 0)),
                   pl.BlockSpec((N_EXPERTS, LANES), lambda i: (0, 0))],
        scratch_shapes=[pltpu.VMEM((N_EXPERTS, 1), F32)],
        compiler_params=_params(("arbitrary",)),
        name="route",
    )(lg_p, lg_s, tri, scan)


RUNS_PER_TILE = N_SLOTS // RUN_ROWS
RUN_UNROLL = 8
RUN_TABLE_STRIDE = RUNS_PER_TILE + RUN_UNROLL


def _for_each_run(tile, run_row_ref, fn):
    base = tile * RUN_TABLE_STRIDE

    def group(g, _):
        for u in range(RUN_UNROLL):
            j = g * RUN_UNROLL + u
            row = run_row_ref[base + j]
            fn(pl.multiple_of(j * RUN_ROWS, RUN_ROWS), pl.multiple_of(row, RUN_ROWS))
        return 0

    lax.fori_loop(0, run_row_ref[base + RUNS_PER_TILE], group, 0)


def _dispatch_kernel(run_row_ref, pad_start_ref, pad_runs_ref, tail_ref,
                     rt_ref, hp_ref, hs_ref, xs_ref, stage_ref, zero_ref, sem, psem, *, n_prompt_tiles):
    i = pl.program_id(0)
    last = pl.num_programs(0) - 1
    buf = i % 2

    def start_runs(tile, b):
        _for_each_run(tile, run_row_ref, lambda slot, row: pltpu.make_async_copy(
            stage_ref.at[b, pl.ds(slot, RUN_ROWS), :], xs_ref.at[pl.ds(row, RUN_ROWS), :], sem.at[b]).start())

    def wait_runs(tile, b):
        _for_each_run(tile, run_row_ref, lambda slot, row: pltpu.make_async_copy(
            stage_ref.at[b, pl.ds(slot, RUN_ROWS), :], xs_ref.at[pl.ds(row, RUN_ROWS), :], sem.at[b]).wait())

    @pl.when(i == 0)
    def _():
        zero_ref[...] = jnp.zeros_like(zero_ref)

        def tail(t, _):
            rows = pl.ds(pl.multiple_of(t * TM_E, TM_E), TM_E)
            cp = pltpu.make_async_copy(zero_ref, xs_ref.at[rows, :], psem)
            cp.start()
            cp.wait()
            return 0

        lax.fori_loop(tail_ref[0], tail_ref[1], tail, 0)

        def per_expert(e, _):
            def pad_copy(r):
                rows = pl.ds(pl.multiple_of(pad_start_ref[e] + r * RUN_ROWS, RUN_ROWS), RUN_ROWS)
                return pltpu.make_async_copy(zero_ref.at[pl.ds(0, RUN_ROWS), :], xs_ref.at[rows, :], psem)

            def start(r, _):
                pad_copy(r).start()
                return 0

            def wait(r, _):
                pad_copy(r).wait()
                return 0

            lax.fori_loop(0, pad_runs_ref[e], start, 0)
            lax.fori_loop(0, pad_runs_ref[e], wait, 0)
            return 0

        lax.fori_loop(0, N_EXPERTS, per_expert, 0)

    @pl.when(i >= 2)
    def _():
        wait_runs(i - 2, buf)

    rt = rt_ref[...]
    x = _unpack_bf16_pairs(jnp.where(i < n_prompt_tiles, hp_ref[...], hs_ref[...]))
    n_slots = stage_ref.shape[1]
    slot_id = lax.broadcasted_iota(jnp.int32, (n_slots, rt.shape[1]), 0).astype(F32)
    sel = jnp.zeros(slot_id.shape, F32)
    gsel = jnp.zeros(slot_id.shape, F32)
    for k in range(TOP_K):
        hit = slot_id == rt[_R_SLOT + k:_R_SLOT + k + 1, :]
        sel = jnp.where(hit, 1.0, sel)
        gsel = jnp.where(hit, rt[_R_GATE + k:_R_GATE + k + 1, :], gsel)
    rows = _dot(sel.astype(BF16), x)
    gate = jnp.sum(gsel, axis=-1, keepdims=True)
    packed = _pack_bf16_pairs(rows)
    stage_ref[buf, :, :packed.shape[1]] = packed
    stage_ref[buf, :, packed.shape[1]:] = pltpu.bitcast(jnp.broadcast_to(gate, (n_slots, LANES)), jnp.uint32)
    start_runs(i, buf)

    @pl.when(i == last)
    def _():
        @pl.when(i >= 1)
        def _():
            wait_runs(i - 1, 1 - buf)

        wait_runs(i, buf)


def _dispatch(h_p, h_s, route_t, run_row, pad_start, pad_runs, tail_tiles, n_rows):
    rp, dp = h_p.shape
    rs = h_s.shape[0]
    assert rp % TT == 0 and rs % TT == 0
    npt, nst = rp // TT, rs // TT
    width = dp + LANES
    grid_spec = pltpu.PrefetchScalarGridSpec(
        num_scalar_prefetch=4,
        grid=(npt + nst,),
        in_specs=[pl.BlockSpec((_R_ROWS, TT), lambda i, *_: (0, i)),
                  pl.BlockSpec((TT, dp), lambda i, *_: (jnp.minimum(i, npt - 1), 0)),
                  pl.BlockSpec((TT, dp), lambda i, *_: (jnp.maximum(i - npt, 0), 0))],
        out_specs=pl.BlockSpec(memory_space=pl.ANY),
        scratch_shapes=[pltpu.VMEM((2, N_SLOTS, width), jnp.uint32), pltpu.VMEM((TM_E, width), jnp.uint32),
                        pltpu.SemaphoreType.DMA((2,)), pltpu.SemaphoreType.DMA(())],
    )
    return pl.pallas_call(
        functools.partial(_dispatch_kernel, n_prompt_tiles=npt),
        out_shape=jax.ShapeDtypeStruct((n_rows, width), jnp.uint32),
        grid_spec=grid_spec,
        compiler_params=_params(("arbitrary",), has_side_effects=True),
        name="dispatch",
    )(run_row, pad_start, pad_runs, tail_tiles, route_t, h_p, h_s)


def _experts_kernel(te_ref, tf_ref, tu_ref, tn_ref, ts_ref, x_ref, w1_ref, b1g_ref, b1l_ref, w2_ref, b2_ref, perm_ref,
                    y_ref, w1buf_ref, w2buf_ref, wsem, w1s_ref, w2s_ref):
    i = pl.program_id(0)
    dff = w2_ref.shape[1]

    def weight_copies(e, slot):
        return (pltpu.make_async_copy(w1_ref.at[e], w1buf_ref.at[slot], wsem.at[0, slot]),
                pltpu.make_async_copy(w2_ref.at[e], w2buf_ref.at[slot], wsem.at[1, slot]))

    @pl.when(tf_ref[i] == 1)
    def _():
        slot = ts_ref[i]

        @pl.when(i == 0)
        def _():
            for cp in weight_copies(te_ref[i], slot):
                cp.start()

        for cp in weight_copies(te_ref[i], slot):
            cp.wait()

        @pl.when(tn_ref[i] >= 0)
        def _():
            for cp in weight_copies(tn_ref[i], 1 - slot):
                cp.start()

        perm = perm_ref[...]
        half = MXU_DIM // 2
        for c in range(w1_ref.shape[2] // MXU_DIM):
            blk = w1buf_ref[slot, :, c * MXU_DIM:(c + 1) * MXU_DIM].astype(BF16)
            sp = _dot(blk, perm).astype(BF16)
            w1s_ref[:, c * half:(c + 1) * half] = sp[:, :half]
            w1s_ref[:, dff + c * half:dff + (c + 1) * half] = sp[:, half:]
        w2s_ref[...] = w2buf_ref[slot].astype(BF16)

    @pl.when(tu_ref[i] == 1)
    def _():
        d = w2_ref.shape[2]
        x = _unpack_bf16_pairs(x_ref[:, :d // 2])
        gate = pltpu.bitcast(x_ref[:, d // 2:], F32)
        hdn = _dot(x, w1s_ref[...])
        glu = jnp.minimum(hdn[:, :dff] + b1g_ref[0], SWIGLU_LIMIT)
        lin = jnp.clip(hdn[:, dff:] + b1l_ref[0], -SWIGLU_LIMIT, SWIGLU_LIMIT)
        act = glu * _sigmoid(SWIGLU_ALPHA * glu) * (lin + 1.0)
        y = (_dot(act.astype(BF16), w2s_ref[...]) + b2_ref[0]) * jnp.concatenate([gate] * (d // LANES), axis=1)
        y_ref[...] = _pack_bf16_pairs(y.astype(BF16).astype(F32))

    @pl.when(tu_ref[i] == 0)
    def _():
        y_ref[...] = jnp.zeros_like(y_ref)


def _deinterleave_perm():
    j = lax.broadcasted_iota(jnp.int32, (MXU_DIM, MXU_DIM), 0)
    c = lax.broadcasted_iota(jnp.int32, (MXU_DIM, MXU_DIM), 1)
    half = MXU_DIM // 2
    src = jnp.where(c < half, 2 * c, 2 * (c - half) + 1)
    return (j == src).astype(BF16)


def _experts(xs, tile_expert, tile_first, tile_used, tile_next, w1, b1g, b1l, w2, b2):
    width = xs.shape[1]
    n_tiles = tile_expert.shape[0]
    n_rows = n_tiles * TM_E
    ne, d, two_f = w1.shape
    dff = two_f // 2
    assert width == d // 2 + LANES
    tile_slot = (jnp.cumsum(tile_first) + 1) % 2
    by_expert = lambda i, te, *_: (te[i], 0, 0)
    grid_spec = pltpu.PrefetchScalarGridSpec(
        num_scalar_prefetch=5,
        grid=(n_tiles,),
        in_specs=[pl.BlockSpec((TM_E, width), lambda i, te, tf, tu, *_: (i * tu[i], 0)),
                  pl.BlockSpec(memory_space=pl.ANY),
                  pl.BlockSpec((1, 1, dff), by_expert),
                  pl.BlockSpec((1, 1, dff), by_expert),
                  pl.BlockSpec(memory_space=pl.ANY),
                  pl.BlockSpec((1, 1, d), by_expert),
                  pl.BlockSpec((MXU_DIM, MXU_DIM), lambda i, *_: (0, 0))],
        out_specs=pl.BlockSpec((TM_E, d // 2), lambda i, *_: (i, 0)),
        scratch_shapes=[pltpu.VMEM((2, d, two_f), F32), pltpu.VMEM((2, dff, d), F32), pltpu.SemaphoreType.DMA((2, 2)),
                        pltpu.VMEM((d, two_f), BF16), pltpu.VMEM((dff, d), BF16)],
    )
    return pl.pallas_call(
        _experts_kernel,
        out_shape=jax.ShapeDtypeStruct((n_rows, d // 2), jnp.uint32),
        grid_spec=grid_spec,
        compiler_params=_params(("arbitrary",)),
        name="experts",
    )(tile_expert, tile_first, tile_used, tile_next, tile_slot.astype(jnp.int32),
      xs, w1, b1g, b1l, w2, b2, _deinterleave_perm())


def _combine_kernel(run_row_ref, x1_ref, rt_ref, g2_ref, y_ref, o_ref, stage_ref, sem, *, tile_offset):
    i = pl.program_id(0)
    buf = i % 2

    def start_runs(step, b):
        _for_each_run(step + tile_offset, run_row_ref, lambda slot, row: pltpu.make_async_copy(
            y_ref.at[pl.ds(row, RUN_ROWS), :], stage_ref.at[b, pl.ds(slot, RUN_ROWS), :], sem.at[b]).start())

    def wait_runs(step, b):
        _for_each_run(step + tile_offset, run_row_ref, lambda slot, row: pltpu.make_async_copy(
            y_ref.at[pl.ds(row, RUN_ROWS), :], stage_ref.at[b, pl.ds(slot, RUN_ROWS), :], sem.at[b]).wait())

    @pl.when(i == 0)
    def _():
        stage_ref[...] = jnp.zeros_like(stage_ref)
        start_runs(0, 0)

    @pl.when(i + 1 < pl.num_programs(0))
    def _():
        start_runs(i + 1, 1 - buf)

    wait_runs(i, buf)

    rt = rt_ref[...]
    tt = rt.shape[0]
    slot_id = lax.broadcasted_iota(jnp.int32, (tt, stage_ref.shape[1]), 1).astype(F32)
    sel = jnp.zeros(slot_id.shape, F32)
    for k in range(TOP_K):
        sel = jnp.where(slot_id == rt[:, _R_SLOT + k:_R_SLOT + k + 1], 1.0, sel)
    sel = sel.astype(BF16)
    o_ref[...] = x1_ref[...] + g2_ref[...] * _dot(sel, _unpack_bf16_pairs(stage_ref[buf]))


def _combine(x1, route, run_row, g2, seq, y_rows, row_offset):
    r, d = x1.shape
    assert r % TT == 0 and row_offset % TT == 0
    off = row_offset // TT
    g2_o, rpm = _mod_operand(g2, seq, TT)
    if rpm is None:
        g2_spec = pl.BlockSpec((TT, d), lambda i, *_: (i, 0))
    else:
        g2_spec = pl.BlockSpec((None, 1, d), lambda i, *_: ((i * TT) // rpm, 0, 0))
    grid_spec = pltpu.PrefetchScalarGridSpec(
        num_scalar_prefetch=1,
        grid=(r // TT,),
        in_specs=[pl.BlockSpec((TT, d), lambda i, *_: (i, 0)),
                  pl.BlockSpec((TT, LANES), lambda i, *_: (i + off, 0)),
                  g2_spec,
                  pl.BlockSpec(memory_space=pl.ANY)],
        out_specs=pl.BlockSpec((TT, d), lambda i, *_: (i, 0)),
        scratch_shapes=[pltpu.VMEM((2, N_SLOTS, d // 2), jnp.uint32), pltpu.SemaphoreType.DMA((2,))],
    )
    return pl.pallas_call(
        functools.partial(_combine_kernel, tile_offset=off),
        out_shape=jax.ShapeDtypeStruct((r, d), F32),
        grid_spec=grid_spec,
        compiler_params=_params(("arbitrary",)),
        name="combine",
    )(run_row, x1, route, g2_o, y_rows)


def _gmlp_tables(w_s, b_s, seq_s):
    g = w_s.shape[0]
    tril = jnp.tril(w_s)
    short = jnp.tril(w_s[:, :seq_s, :seq_s])
    rep = CHUNK_B // seq_s
    blockdiag = jnp.zeros_like(w_s)
    for r in range(rep):
        blockdiag = blockdiag.at[:, r * seq_s:(r + 1) * seq_s, r * seq_s:(r + 1) * seq_s].set(short)
    ws_all = jnp.stack([tril, blockdiag]).astype(BF16)
    bs_full = jnp.repeat(b_s.T, HEAD_DIM, axis=1)
    bs_short = jnp.tile(jnp.repeat(b_s[:, :seq_s].T, HEAD_DIM, axis=1), (rep, 1))
    return ws_all, jnp.stack([bs_full, bs_short])


def _layer(xp, xs, cache_k, cache_v, cp, cs, p):
    bp, tp, d = xp.shape
    bs_, ts, _ = xs.shape
    assert tp % CHUNK_B == 0 and CHUNK_B % ts == 0
    rp, rs = bp * tp, bs_ * ts
    n_tok = rp + rs

    mod = _ada(jnp.concatenate([cp, cs], axis=0), p['w_ada'], p['b_ada'])
    mods = [mod[:, j * d:(j + 1) * d] for j in range(6)]
    mp = [m[:bp] for m in mods]
    msm = [m[bp:] for m in mods]

    tile_h = lambda g: jnp.tile(g, N_HEADS).reshape(1, D_HEADS)
    bd = jnp.kron(jnp.eye(MXU_DIM // HEAD_DIM, dtype=F32),
                  jnp.full((HEAD_DIM, HEAD_DIM), 1.0 / HEAD_DIM, F32)).astype(BF16)
    w_in_b = p['w_in'].astype(BF16)
    n1g = p['norm1_g'].reshape(1, d)
    common = (n1g, w_in_b, tile_h(p['q_norm_g']), tile_h(p['k_norm_g']), tile_h(p['v_norm_g']), bd)
    q_p, k_p, kb_p, v_p, vb_p, u_p, vn_p, sga_p, sgb_p = _inproj(
        xp.reshape(rp, d), mp[0], mp[1], tp, *common, want_vn32=False, kv_transposed=True)
    q_s, k_s, kb_s, v_s, vb_s, u_s, vn_s, sga_s, sgb_s, vn32_s = _inproj(
        xs.reshape(rs, d), msm[0], msm[1], ts, *common, want_vn32=True, kv_transposed=False)

    ya_p = _attn_prompt(q_p, kb_p, vb_p, bp, tp)
    ya_s = _attn_sample(q_s, kb_s, vb_s, cache_k, cache_v, bs_, ts)

    ws_all, bs_all = _gmlp_tables(p['w_s'], p['b_s'], ts)
    wr = p['w_router']
    wr_hi = wr.astype(BF16)
    wr_lo = (wr - wr_hi.astype(F32)).astype(BF16)
    post_w = (p['w_pa'].astype(BF16), p['w_pb'].astype(BF16), p['w_o'].astype(BF16), p['norm2_g'].reshape(1, d),
              jnp.concatenate([wr_hi, wr_lo], axis=1), p['b_router'].reshape(N_EXPERTS, 1))
    x1_p, h2_p, lg_p = _post(xp.reshape(rp, d), ya_p, u_p, vn_p, sga_p, sgb_p, mp[2], mp[3], mp[4], tp,
                             ws_all, bs_all, 0, *post_w)
    x1_s, h2_s, lg_s = _post(xs.reshape(rs, d), ya_s, u_s, vn_s, sga_s, sgb_s, msm[2], msm[3], msm[4], ts,
                             ws_all, bs_all, 1, *post_w)

    route, route_t, table, counts = _route(lg_p, lg_s)

    counts = counts[:, 0].astype(jnp.int32)
    padded = (counts + TM_E - 1) // TM_E * TM_E
    pend = jnp.cumsum(padded)
    pstart = pend - padded
    table = jnp.swapaxes(table.reshape(-1, N_EXPERTS, LANES)[:, :, :3], 1, 2).astype(jnp.int32)
    n_token_tiles = n_tok // TT
    runs = (table[:, _T_COUNT] + RUN_ROWS - 1) // RUN_ROWS
    runs_end = jnp.cumsum(runs, axis=1)
    j = jnp.arange(RUNS_PER_TILE, dtype=jnp.int32)
    e_of = jnp.minimum(jnp.sum((runs_end[:, None, :] <= j[None, :, None]).astype(jnp.int32), axis=2), N_EXPERTS - 1)
    pick = (e_of[:, :, None] == jnp.arange(N_EXPERTS, dtype=jnp.int32)[None, None, :]).astype(jnp.int32)
    first_run = jnp.sum(pick * (runs_end - runs)[:, None, :], axis=2)
    first_row = jnp.sum(pick * (table[:, _T_BASE] + pstart[None, :])[:, None, :], axis=2)
    run_row = first_row + (j[None, :] - first_run) * RUN_ROWS
    is_run = j[None, :] < runs_end[:, -1:]
    n_tiles = -(-(n_tok * TOP_K + n_token_tiles * N_EXPERTS * (RUN_ROWS - 1)) // TM_E) + N_EXPERTS
    tile_start = jnp.arange(n_tiles, dtype=jnp.int32) * TM_E
    tile_used = (tile_start < pend[-1]).astype(jnp.int32)
    last_expert = jnp.sum((pend <= pend[-1] - 1).astype(jnp.int32))
    tile_expert = jnp.minimum(jnp.sum((pend[None, :] <= tile_start[:, None]).astype(jnp.int32), axis=1), last_expert)
    tile_first = jnp.concatenate([jnp.ones((1,), jnp.int32),
                                  (tile_expert[1:] != tile_expert[:-1]).astype(jnp.int32)])
    e_ids = jnp.arange(N_EXPERTS, dtype=jnp.int32)
    runs_later = jnp.logical_and(e_ids[None, :] > e_ids[:, None], (padded > 0)[None, :])
    next_expert = jnp.min(jnp.where(runs_later, e_ids[None, :], N_EXPERTS), axis=1)
    next_expert = jnp.where(next_expert == N_EXPERTS, -1, next_expert)
    tile_next = jnp.sum((tile_expert[:, None] == e_ids[None, :]).astype(jnp.int32) * next_expert[None, :], axis=1)

    spare_tiles = -(-2 * N_SLOTS // TM_E)
    spare_rows = j[None, :] * RUN_ROWS
    parity = (jnp.arange(n_token_tiles, dtype=jnp.int32) % 2)[:, None]
    groups = jnp.broadcast_to((runs_end[:, -1:] + RUN_UNROLL - 1) // RUN_UNROLL, (n_token_tiles, RUN_UNROLL))
    with_groups = lambda rows: jnp.concatenate([rows, groups], axis=1).reshape(-1)
    rows_out = with_groups(jnp.where(is_run, run_row, n_tiles * TM_E + parity * N_SLOTS + spare_rows))
    rows_in = with_groups(jnp.where(is_run, run_row, spare_rows))
    tail_tiles = jnp.stack([pend[-1] // TM_E, jnp.int32(n_tiles + spare_tiles)]).astype(jnp.int32)
    pad_runs = (padded - counts) // RUN_ROWS
    xs_rows = _dispatch(h2_p, h2_s, route_t, rows_out, pstart + counts, pad_runs, tail_tiles,
                        (n_tiles + spare_tiles) * TM_E)
    b1 = p['b1']
    ne = b1.shape[0]
    y_rows = _experts(xs_rows, tile_expert, tile_first, tile_used, tile_next, p['w1'],
                      b1[:, 0::2].reshape(ne, 1, -1), b1[:, 1::2].reshape(ne, 1, -1),
                      p['w2'], p['b2'].reshape(ne, 1, d))
    out_p = _combine(x1_p, route, rows_in, mp[5], tp, y_rows, 0)
    out_s = _combine(x1_s, route, rows_in, msm[5], ts, y_rows, rp)

    shp = lambda a, b, t: a.reshape(b, t, N_HEADS, HEAD_DIM)
    unt = lambda a: jnp.transpose(a.reshape(bp, N_HEADS, HEAD_DIM, tp), (0, 3, 1, 2))
    return (out_p.reshape(bp, tp, d), out_s.reshape(bs_, ts, d),
            unt(k_p), unt(v_p), shp(k_s, bs_, ts), shp(v_s, bs_, ts), shp(vn32_s, bs_, ts))


def kernel(x_prompt, x_sample, cache_k, cache_v, c_prompt, c_sample, norm1_g, norm2_g, w_ada, b_ada, w_in,
           q_norm_g, k_norm_g, v_norm_g, w_s, b_s, w_pa, w_pb, w_o, w_router, b_router, w1, b1, w2, b2):
    depth = w_in.shape[0]
    xp, xs = x_prompt, x_sample
    kp, vp, ks, vs, vbs = [], [], [], [], []
    for l in range(depth):
        p = {'norm1_g': norm1_g[l], 'norm2_g': norm2_g[l], 'w_ada': w_ada[l], 'b_ada': b_ada[l],
             'w_in': w_in[l], 'q_norm_g': q_norm_g[l], 'k_norm_g': k_norm_g[l], 'v_norm_g': v_norm_g[l],
             'w_s': w_s[l], 'b_s': b_s[l], 'w_pa': w_pa[l], 'w_pb': w_pb[l], 'w_o': w_o[l],
             'w_router': w_router[l], 'b_router': b_router[l], 'w1': w1[l], 'b1': b1[l],
             'w2': w2[l], 'b2': b2[l]}
        xp, xs, k_p, v_p, k_s, v_s, vn_s = _layer(xp, xs, cache_k[l], cache_v[l], c_prompt, c_sample, p)
        kp.append(k_p); vp.append(v_p); ks.append(k_s); vs.append(v_s); vbs.append(vn_s)
    return (xp, xs, jnp.stack(kp), jnp.stack(vp), jnp.stack(ks), jnp.stack(vs), jnp.stack(vbs))
```

```python
import functools

import jax
import jax.numpy as jnp
from jax import lax
from jax.experimental import pallas as pl
from jax.experimental.pallas import tpu as pltpu

F32 = jnp.float32
BF16 = jnp.bfloat16

LANES = 128
MXU_DIM = 256
VMEM_LIMIT = 56 * 1024 * 1024

HEAD_DIM = 64
N_HEADS = 8
D_HEADS = N_HEADS * HEAD_DIM
PAIRS = D_HEADS // LANES
CHUNK_B = 128
N_EXPERTS = 32
TOP_K = 4
RMS_EPS = 1e-6
SWIGLU_ALPHA = 1.702
SWIGLU_LIMIT = 7.0

TM_IN = 512
TQ = 512
TK = 256
SUB_Q = 128
LOOK_BACK = 256
WINDOW = SUB_Q + LOOK_BACK
TK_OLD = 128
TM_POST = 256
TT = 256
TM_E = 512
RUN_ROWS = 8
N_SLOTS = TT * TOP_K + N_EXPERTS * RUN_ROWS


def _params(sem, **kw):
    return pltpu.CompilerParams(dimension_semantics=sem, vmem_limit_bytes=VMEM_LIMIT, **kw)


def _dot(a, b):
    return jnp.dot(a, b, preferred_element_type=F32)


def _dot_nt(a, b):
    return lax.dot_general(a, b, (((1,), (1,)), ((), ())), preferred_element_type=F32)


def _gelu(x):
    return jax.nn.gelu(x)


def _sigmoid(x):
    return 0.5 * jnp.tanh(0.5 * x) + 0.5


def _ada_kernel(c_ref, w_ref, b_ref, o_ref):
    c = c_ref[...]
    s = c * _sigmoid(c)
    o_ref[...] = jnp.dot(s, w_ref[...], preferred_element_type=F32,
                         precision=lax.Precision.HIGHEST) + b_ref[...]


def _ada(c, w_ada, b_ada):
    nb, d = c.shape
    n = w_ada.shape[1]
    tn = 1024
    return pl.pallas_call(
        _ada_kernel,
        out_shape=jax.ShapeDtypeStruct((nb, n), F32),
        grid=(n // tn,),
        in_specs=[pl.BlockSpec((nb, d), lambda j: (0, 0)),
                  pl.BlockSpec((d, tn), lambda j: (0, j)),
                  pl.BlockSpec((1, tn), lambda j: (0, j))],
        out_specs=pl.BlockSpec((nb, tn), lambda j: (0, j)),
        compiler_params=_params(("parallel",)),
        name="ada",
    )(c, w_ada, b_ada.reshape(1, n))


def _head_mean_sq(p, bd):
    sq = (p * p).astype(BF16)
    return jnp.concatenate([_dot(sq[:, :MXU_DIM], bd), _dot(sq[:, MXU_DIM:], bd)], axis=-1)


def _inproj_kernel(x_ref, sh_ref, sc_ref, g_ref, w_ref, qg_ref, kg_ref, vg_ref, bd_ref,
                   q_ref, k32_ref, kb_ref, v32_ref, vb_ref, u_ref, vnb_ref, sga_ref, sgb_ref, *vn32_ref,
                   kv_transposed):
    kv_out = (lambda a: a.T) if kv_transposed else (lambda a: a)
    x = x_ref[...]
    ms = jnp.mean(x * x, axis=-1, keepdims=True)
    xn = x * lax.rsqrt(ms + RMS_EPS) * g_ref[...]
    h = (xn * (1.0 + sc_ref[...]) + sh_ref[...]).astype(BF16)
    bd = bd_ref[...]
    d = x.shape[1]

    def sec(a, n):
        return _dot(h, w_ref[:, a:a + n])

    o = 0
    pq = sec(o, D_HEADS); o += D_HEADS
    q = pq * lax.rsqrt(_head_mean_sq(pq, bd) + RMS_EPS) * qg_ref[...]
    q_ref[...] = (q * (HEAD_DIM ** -0.5)).astype(BF16)
    pk = sec(o, D_HEADS); o += D_HEADS
    k = pk * lax.rsqrt(_head_mean_sq(pk, bd) + RMS_EPS) * kg_ref[...]
    k32_ref[...] = kv_out(k)
    kb_ref[...] = k.astype(BF16)
    pv = sec(o, D_HEADS); o += D_HEADS
    v32_ref[...] = kv_out(pv)
    vb_ref[...] = pv.astype(BF16)
    pu = sec(o, D_HEADS); o += D_HEADS
    u_ref[...] = _gelu(pu).astype(BF16)
    pvb = _gelu(sec(o, D_HEADS)); o += D_HEADS
    vn = pvb * lax.rsqrt(_head_mean_sq(pvb, bd) + RMS_EPS) * vg_ref[...]
    vnb_ref[...] = vn.astype(BF16)
    if vn32_ref:
        vn32_ref[0][...] = vn
    sga_ref[...] = _sigmoid(sec(o, d)).astype(BF16); o += d
    sgb_ref[...] = _sigmoid(sec(o, d)).astype(BF16)


def _mod_spec(tm, arr, rows_per_mod):
    if rows_per_mod is None:
        return pl.BlockSpec((tm, arr.shape[-1]), lambda i: (i, 0))
    return pl.BlockSpec((None, 1, arr.shape[-1]), lambda i: ((i * tm) // rows_per_mod, 0, 0))


def _mod_operand(m, seq, tm):
    if seq % tm == 0:
        return m[:, None, :], seq
    return jnp.repeat(m, seq, axis=0), None


def _inproj(x, sh, sc, seq, g, w_in_b, qg, kg, vg, bd, want_vn32, kv_transposed):
    r, d = x.shape
    tm = min(TM_IN, r)
    assert r % tm == 0 and (not kv_transposed or seq % tm == 0)
    sh_o, rpm = _mod_operand(sh, seq, tm)
    sc_o, _ = _mod_operand(sc, seq, tm)
    ncol = w_in_b.shape[1]
    const = lambda i: (0, 0)
    row = lambda n: pl.BlockSpec((tm, n), lambda i: (i, 0))
    outs = [jax.ShapeDtypeStruct((r, D_HEADS), BF16),
            jax.ShapeDtypeStruct((r, D_HEADS), F32),
            jax.ShapeDtypeStruct((r, D_HEADS), BF16),
            jax.ShapeDtypeStruct((r, D_HEADS), F32),
            jax.ShapeDtypeStruct((r, D_HEADS), BF16),
            jax.ShapeDtypeStruct((r, D_HEADS), BF16),
            jax.ShapeDtypeStruct((r, D_HEADS), BF16),
            jax.ShapeDtypeStruct((r, d), BF16),
            jax.ShapeDtypeStruct((r, d), BF16)]
    out_specs = [row(D_HEADS)] * 7 + [row(d), row(d)]
    if kv_transposed:
        per_seq = seq // tm
        for j in (1, 3):
            outs[j] = jax.ShapeDtypeStruct((r // seq, D_HEADS, seq), F32)
            out_specs[j] = pl.BlockSpec((None, D_HEADS, tm), lambda i: (i // per_seq, 0, i % per_seq))
    if want_vn32:
        outs.append(jax.ShapeDtypeStruct((r, D_HEADS), F32))
        out_specs.append(row(D_HEADS))
    return pl.pallas_call(
        functools.partial(_inproj_kernel, kv_transposed=kv_transposed),
        out_shape=outs,
        grid=(r // tm,),
        in_specs=[row(d), _mod_spec(tm, sh_o, rpm), _mod_spec(tm, sc_o, rpm),
                  pl.BlockSpec((1, d), const),
                  pl.BlockSpec((d, ncol), const, pipeline_mode=pl.Buffered(1)),
                  pl.BlockSpec((1, D_HEADS), const), pl.BlockSpec((1, D_HEADS), const),
                  pl.BlockSpec((1, D_HEADS), const), pl.BlockSpec((MXU_DIM, MXU_DIM), const)],
        out_specs=out_specs,
        compiler_params=_params(("parallel",)),
        name="inproj",
    )(x, sh_o, sc_o, g, w_in_b, qg, kg, vg, bd)


LOG_WEIGHT_CUTOFF = -104.0
MASKED_SCORE = -1e30


def _sb_weights(z, u, c, mask):
    if mask is not None:
        z = jnp.where(mask, z, MASKED_SCORE)
    sp = jnp.maximum(z, 0.0) + jnp.log(1.0 + jnp.exp(-jnp.abs(z)))
    hi = sp.astype(BF16)
    lo = (sp - hi.astype(F32)).astype(BF16)
    tk, tu = z.shape[1], u.shape[0]
    old = tk - tu
    later = _dot(hi[:, old:], u) + _dot(lo[:, old:], u)
    total = jnp.sum(sp[:, old:], axis=-1, keepdims=True)
    if old:
        u_old = u[:old, :old]
        later = jnp.concatenate([_dot(hi[:, :old], u_old) + _dot(lo[:, :old], u_old) + total, later], axis=1)
        total = total + jnp.sum(sp[:, :old], axis=-1, keepdims=True)
    w = jnp.exp((z + c) - (sp + later))
    return w, c - total


def _pair_select(shape):
    return lax.broadcasted_iota(jnp.int32, shape, 1) < HEAD_DIM


def _causal_mask(m, tq, tk):
    row = lax.broadcasted_iota(jnp.int32, (m, tk), 0) & (tq - 1)
    return lax.broadcasted_iota(jnp.int32, (m, tk), 1) < row


def _stack_heads(q):
    first = _pair_select(q.shape)
    zero = jnp.zeros_like(q)
    return jnp.concatenate([jnp.where(first, q, zero), jnp.where(first, zero, q)], axis=0)


def _attn_prompt_kernel(q_ref, k_ref, v_ref, u_ref, o_ref, acc_ref, c_ref):
    i = pl.program_id(2)
    tq = q_ref.shape[0]
    nsub = tq // SUB_Q
    sub_rows = 2 * SUB_Q
    q = q_ref[...]
    u = u_ref[...]
    first = _pair_select((SUB_Q, LANES))
    m = nsub * sub_rows

    def pv_sub(wb, s, vs):
        w0 = wb[s * sub_rows:s * sub_rows + SUB_Q]
        w1 = wb[s * sub_rows + SUB_Q:(s + 1) * sub_rows]
        return jnp.where(first, _dot(w0, vs), _dot(w1, vs))

    qs = [_stack_heads(q[s * SUB_Q:(s + 1) * SUB_Q]) for s in range(nsub)]
    starts = [pl.multiple_of(jnp.maximum(i * tq + s * SUB_Q - LOOK_BACK, 0), SUB_Q) for s in range(nsub)]
    row = lax.broadcasted_iota(jnp.int32, (m, 1), 0)
    window_start = jnp.zeros((m, 1), jnp.int32) + starts[0]
    for s in range(1, nsub):
        window_start = jnp.where(row >= s * sub_rows, starts[s], window_start)
    q_pos = i * tq + (row // sub_rows) * SUB_Q + (row & (SUB_Q - 1))

    z = jnp.concatenate([_dot_nt(qs[s], k_ref[pl.ds(starts[s], WINDOW), :]) for s in range(nsub)], axis=0)
    col = lax.broadcasted_iota(jnp.int32, (m, WINDOW), 1)
    w, c = _sb_weights(z, u, jnp.zeros((m, 1), F32), window_start + col < q_pos)
    wb = w.astype(BF16)
    acc_ref[...] = jnp.concatenate([pv_sub(wb, s, v_ref[pl.ds(starts[s], WINDOW), :]) for s in range(nsub)], axis=0)
    c_ref[...] = c
    q_all = jnp.concatenate(qs, axis=0)
    col = lax.broadcasted_iota(jnp.int32, (m, TK_OLD), 1)
    u_old = u[:TK_OLD, :TK_OLD]

    def cond(state):
        j, cmax = state
        return jnp.logical_and(j >= 0, cmax > LOG_WEIGHT_CUTOFF)

    def body(state):
        j, _ = state
        start = pl.multiple_of(j * TK_OLD, TK_OLD)
        vs = v_ref[pl.ds(start, TK_OLD), :]
        w, c = _sb_weights(_dot_nt(q_all, k_ref[pl.ds(start, TK_OLD), :]), u_old, c_ref[...],
                           j * TK_OLD + col < window_start)
        wb = w.astype(BF16)
        acc_ref[...] += jnp.concatenate([pv_sub(wb, s, vs) for s in range(nsub)], axis=0)
        c_ref[...] = c
        return j - 1, jnp.max(c)

    j0 = lax.div(starts[nsub - 1] + (TK_OLD - 1), TK_OLD) - 1
    lax.while_loop(cond, body, (j0, jnp.max(c)))
    o_ref[...] = acc_ref[...].astype(o_ref.dtype)


def _upper_ones(n):
    return (lax.broadcasted_iota(jnp.int32, (n, n), 0) > lax.broadcasted_iota(jnp.int32, (n, n), 1)).astype(BF16)


def _attn_prompt(q, kb, vb, batch, seq):
    assert seq % TQ == 0 and TQ % TK == 0 and TQ % SUB_Q == 0 and seq >= WINDOW
    nq = seq // TQ
    return pl.pallas_call(
        _attn_prompt_kernel,
        out_shape=jax.ShapeDtypeStruct(q.shape, BF16),
        grid=(batch, PAIRS, nq),
        in_specs=[pl.BlockSpec((TQ, LANES), lambda b, p, i: (b * nq + i, p)),
                  pl.BlockSpec((seq, LANES), lambda b, p, i: (b, p)),
                  pl.BlockSpec((seq, LANES), lambda b, p, i: (b, p)),
                  pl.BlockSpec((TK, TK), lambda b, p, i: (0, 0))],
        out_specs=pl.BlockSpec((TQ, LANES), lambda b, p, i: (b * nq + i, p)),
        scratch_shapes=[pltpu.VMEM((TQ, LANES), F32), pltpu.VMEM((2 * TQ, 1), F32)],
        compiler_params=_params(("parallel", "parallel", "arbitrary")),
        name="attn_prompt",
    )(q, kb, vb, _upper_ones(TK))


def _attn_sample_kernel(q_ref, kn_ref, vn_ref, kc_ref, vc_ref, un_ref, u_ref, o_ref,
                        kbuf_ref, vbuf_ref, sem, acc_ref, c_ref):
    b = pl.program_id(0)
    tq = q_ref.shape[0]
    nblk = kc_ref.shape[3] // TK
    head = lambda a, h: a[:, h * HEAD_DIM:(h + 1) * HEAD_DIM]
    q = q_ref[...]
    qh = [head(q, h) for h in range(N_HEADS)]

    def copies(j, slot):
        rows = pl.ds(pl.multiple_of(j * TK, TK), TK)
        return (pltpu.make_async_copy(kc_ref.at[b, :, :, rows], kbuf_ref.at[slot], sem.at[0, slot]),
                pltpu.make_async_copy(vc_ref.at[b, :, :, rows], vbuf_ref.at[slot], sem.at[1, slot]))

    def start(j, slot):
        for cp in copies(j, slot):
            cp.start()

    def wait(j, slot):
        for cp in copies(j, slot):
            cp.wait()

    def sweep(score, apply, u, c, mask):
        z = jnp.concatenate([score(qh[h], h) for h in range(N_HEADS)], axis=0)
        w, c = _sb_weights(z, u, c, mask)
        wb = w.astype(BF16)
        pv = jnp.concatenate([apply(wb[h * tq:(h + 1) * tq], h) for h in range(N_HEADS)], axis=-1)
        return pv, c

    start(nblk - 1, 0)
    kn = kn_ref[...]
    vn = vn_ref[...]
    pv, c = sweep(lambda qq, h: _dot_nt(qq, head(kn, h)), lambda ww, h: _dot(ww, head(vn, h)), un_ref[...],
                  jnp.zeros((N_HEADS * tq, 1), F32), _causal_mask(N_HEADS * tq, tq, tq))
    acc_ref[...] = pv
    c_ref[...] = c
    u = u_ref[...]

    def cond(state):
        j, _, cmax = state
        return jnp.logical_and(j >= 0, cmax > LOG_WEIGHT_CUTOFF)

    def body(state):
        j, slot, _ = state
        wait(j, slot)

        @pl.when(j > 0)
        def _():
            start(j - 1, 1 - slot)

        pv, c = sweep(lambda qq, h: _dot(qq, kbuf_ref[slot, h].astype(BF16)),
                      lambda ww, h: _dot_nt(ww, vbuf_ref[slot, h].astype(BF16)), u, c_ref[...], None)
        acc_ref[...] += pv
        c_ref[...] = c
        return j - 1, 1 - slot, jnp.max(c)

    j, slot, _ = lax.while_loop(cond, body, (nblk - 1, 0, jnp.max(c)))

    @pl.when(j >= 0)
    def _():
        wait(j, slot)

    o_ref[...] = acc_ref[...].astype(o_ref.dtype)


def _attn_sample(q, kb, vb, cache_k, cache_v, batch, seq):
    past = cache_k.shape[1]
    assert past % TK == 0 and past >= TK
    cache_k = jnp.transpose(cache_k, (0, 2, 3, 1))
    cache_v = jnp.transpose(cache_v, (0, 2, 3, 1))
    row = pl.BlockSpec((seq, D_HEADS), lambda b: (b, 0))
    return pl.pallas_call(
        _attn_sample_kernel,
        out_shape=jax.ShapeDtypeStruct(q.shape, BF16),
        grid=(batch,),
        in_specs=[row, row, row,
                  pl.BlockSpec(memory_space=pl.ANY), pl.BlockSpec(memory_space=pl.ANY),
                  pl.BlockSpec((seq, seq), lambda b: (0, 0)),
                  pl.BlockSpec((TK, TK), lambda b: (0, 0))],
        out_specs=row,
        scratch_shapes=[pltpu.VMEM((2, N_HEADS, HEAD_DIM, TK), F32), pltpu.VMEM((2, N_HEADS, HEAD_DIM, TK), F32),
                        pltpu.SemaphoreType.DMA((2, 2)),
                        pltpu.VMEM((seq, D_HEADS), F32), pltpu.VMEM((N_HEADS * seq, 1), F32)],
        compiler_params=_params(("arbitrary",)),
        name="attn_sample",
    )(q, kb, vb, cache_k, cache_v, _upper_ones(seq), _upper_ones(TK))


def _pack_bf16_pairs(x):
    half = x.shape[1] // 2
    a = pltpu.bitcast(x[:, :half], jnp.uint32)
    b = pltpu.bitcast(x[:, half:], jnp.uint32)
    return a | (b >> 16)


def _unpack_bf16_pairs(p):
    a = pltpu.bitcast(p & jnp.uint32(0xFFFF0000), F32).astype(BF16)
    b = pltpu.bitcast(p << 16, F32).astype(BF16)
    return jnp.concatenate([a, b], axis=1)


def _post_kernel(x_ref, ya_ref, u_ref, vn_ref, sga_ref, sgb_ref, g1_ref, sh2_ref, sc2_ref,
                 ws_ref, bs_ref, wpa_ref, wpb_ref, wo_ref, n2g_ref, wr_ref, br_ref,
                 x1_ref, h2_ref, lg_ref):
    tm = x_ref.shape[0]
    nch = tm // CHUNK_B
    first = (lax.broadcasted_iota(jnp.int32, (CHUNK_B, nch * LANES), 1) & (LANES - 1)) < HEAD_DIM
    mixed = []
    for p in range(PAIRS):
        vnp = jnp.concatenate([vn_ref[c * CHUNK_B:(c + 1) * CHUNK_B, p * LANES:(p + 1) * LANES] for c in range(nch)],
                              axis=1)
        mixed.append(jnp.where(first, _dot(ws_ref[0, 2 * p], vnp), _dot(ws_ref[0, 2 * p + 1], vnp)))
    chunks = []
    for c in range(nch):
        s = jnp.concatenate([mp[:, c * LANES:(c + 1) * LANES] for mp in mixed], axis=-1) + bs_ref[0]
        chunks.append((u_ref[c * CHUNK_B:(c + 1) * CHUNK_B, :].astype(F32) * s).astype(BF16))
    yb = jnp.concatenate(chunks, axis=0)
    merged = (sga_ref[...].astype(F32) * _dot(ya_ref[...], wpa_ref[...])
              + sgb_ref[...].astype(F32) * _dot(yb, wpb_ref[...]))
    x1 = x_ref[...] + g1_ref[...] * _dot(merged.astype(BF16), wo_ref[...])
    x1_ref[...] = x1
    ms = jnp.mean(x1 * x1, axis=-1, keepdims=True)
    h2 = (x1 * lax.rsqrt(ms + RMS_EPS) * n2g_ref[...]) * (1.0 + sc2_ref[...]) + sh2_ref[...]
    hi = h2.astype(BF16)
    hf = hi.astype(F32)
    lo = (h2 - hf).astype(BF16)
    h2_ref[...] = _pack_bf16_pairs(hf)
    l2 = _dot(hi, wr_ref[...]) + _dot(lo, wr_ref[...])
    l2t = jnp.concatenate([l2, jnp.zeros((l2.shape[0], LANES - l2.shape[1]), F32)], axis=1).T
    lg_ref[...] = l2t[:N_EXPERTS] + l2t[N_EXPERTS:2 * N_EXPERTS] + br_ref[...]


def _post(x, ya, u, vn, sga, sgb, g1, sh2, sc2, seq, ws_all, bs_all, ws_idx, wpa, wpb, wo, n2g, wr, br):
    r, d = x.shape
    tm = min(TM_POST, r)
    assert r % tm == 0 and tm % CHUNK_B == 0
    g1_o, rpm = _mod_operand(g1, seq, tm)
    sh2_o, _ = _mod_operand(sh2, seq, tm)
    sc2_o, _ = _mod_operand(sc2, seq, tm)
    const = lambda i: (0, 0)
    row = lambda n: pl.BlockSpec((tm, n), lambda i: (i, 0))
    return pl.pallas_call(
        _post_kernel,
        out_shape=[jax.ShapeDtypeStruct((r, d), F32), jax.ShapeDtypeStruct((r, d // 2), jnp.uint32),
                   jax.ShapeDtypeStruct((N_EXPERTS, r), F32)],
        grid=(r // tm,),
        in_specs=[row(d), row(D_HEADS), row(D_HEADS), row(D_HEADS), row(d), row(d),
                  _mod_spec(tm, g1_o, rpm), _mod_spec(tm, sh2_o, rpm), _mod_spec(tm, sc2_o, rpm),
                  pl.BlockSpec((1, N_HEADS, CHUNK_B, CHUNK_B), lambda i: (ws_idx, 0, 0, 0)),
                  pl.BlockSpec((1, CHUNK_B, D_HEADS), lambda i: (ws_idx, 0, 0)),
                  pl.BlockSpec(wpa.shape, const), pl.BlockSpec(wpb.shape, const), pl.BlockSpec(wo.shape, const),
                  pl.BlockSpec((1, d), const), pl.BlockSpec(wr.shape, const), pl.BlockSpec((N_EXPERTS, 1), const)],
        out_specs=[row(d), row(d // 2), pl.BlockSpec((N_EXPERTS, tm), lambda i: (0, i))],
        compiler_params=_params(("parallel",)),
        name="post",
    )(x, ya, u, vn, sga, sgb, g1_o, sh2_o, sc2_o, ws_all, bs_all, wpa, wpb, wo, n2g, wr, br)


_R_IDX, _R_GATE, _R_SLOT = 0, TOP_K, 2 * TOP_K
_R_ROWS = 16
_T_COUNT, _T_SLOT, _T_BASE = 0, 1, 2


def _route_kernel(lp_ref, ls_ref, tri_ref, scan_ref, o_ref, ot_ref, tbl_ref, cnt_ref, carry_ref, *, n_prompt_tiles):
    i = pl.program_id(0)

    @pl.when(i == 0)
    def _():
        carry_ref[...] = jnp.zeros_like(carry_ref)

    l = jnp.where(i < n_prompt_tiles, lp_ref[...], ls_ref[...])
    ne, tt = l.shape
    eid = lax.broadcasted_iota(jnp.int32, l.shape, 0).astype(F32)
    cur = l
    vals, idxs = [], []
    for _ in range(TOP_K):
        m = jnp.max(cur, axis=0, keepdims=True)
        idx = jnp.min(jnp.where(cur == m, eid, float(ne)), axis=0, keepdims=True)
        vals.append(m)
        idxs.append(idx)
        cur = jnp.where(eid == idx, -jnp.inf, cur)
    es = [jnp.exp(v - vals[0]) for v in vals]
    den = es[0] + es[1] + es[2] + es[3]
    onehot = jnp.zeros(l.shape, F32)
    for idx in idxs:
        onehot = onehot + (eid == idx).astype(F32)
    within = _dot(onehot.astype(BF16), tri_ref[...])
    count = jnp.sum(onehot, axis=1, keepdims=True)
    runs = jnp.ceil(count * (1.0 / RUN_ROWS))
    first_slot = RUN_ROWS * _dot(scan_ref[...], jnp.broadcast_to(runs, (ne, LANES)).astype(BF16))[:, 0:1]
    slot_of = within + first_slot
    row = lax.broadcasted_iota(jnp.int32, (_R_ROWS, tt), 0)
    out = jnp.zeros((_R_ROWS, tt), F32)
    for k in range(TOP_K):
        slot = jnp.sum(jnp.where(eid == idxs[k], slot_of, 0.0), axis=0, keepdims=True)
        out = jnp.where(row == _R_IDX + k, idxs[k], out)
        out = jnp.where(row == _R_GATE + k, es[k] / den, out)
        out = jnp.where(row == _R_SLOT + k, slot, out)
    ot_ref[...] = out
    o_ref[...] = jnp.concatenate([out, jnp.zeros((LANES - _R_ROWS, tt), F32)], axis=0).T
    lane = lax.broadcasted_iota(jnp.int32, (ne, LANES), 1)
    tbl_ref[...] = jnp.where(lane == _T_COUNT, count, jnp.where(lane == _T_SLOT, first_slot, carry_ref[...]))
    carry_ref[...] += RUN_ROWS * runs
    cnt_ref[...] = jnp.broadcast_to(carry_ref[...], cnt_ref.shape)


def _route(lg_p, lg_s):
    rp, rs = lg_p.shape[1], lg_s.shape[1]
    assert rp % TT == 0 and rs % TT == 0
    npt, nst = rp // TT, rs // TT
    nt = npt + nst
    iota = lambda shape, ax: lax.broadcasted_iota(jnp.int32, shape, ax)
    tri = (iota((TT, TT), 0) < iota((TT, TT), 1)).astype(BF16)
    scan = (iota((N_EXPERTS, N_EXPERTS), 1) < iota((N_EXPERTS, N_EXPERTS), 0)).astype(BF16)
    return pl.pallas_call(
        functools.partial(_route_kernel, n_prompt_tiles=npt),
        out_shape=[jax.ShapeDtypeStruct((rp + rs, LANES), F32), jax.ShapeDtypeStruct((_R_ROWS, rp + rs), F32),
                   jax.ShapeDtypeStruct((nt * N_EXPERTS, LANES), F32), jax.ShapeDtypeStruct((N_EXPERTS, LANES), F32)],
        grid=(nt,),
        in_specs=[pl.BlockSpec((N_EXPERTS, TT), lambda i: (0, jnp.minimum(i, npt - 1))),
                  pl.BlockSpec((N_EXPERTS, TT), lambda i: (0, jnp.maximum(i - npt, 0))),
                  pl.BlockSpec((TT, TT), lambda i: (0, 0)),
                  pl.BlockSpec((N_EXPERTS, N_EXPERTS), lambda i: (0, 0))],
        out_specs=[pl.BlockSpec((TT, LANES), lambda i: (i, 0)), pl.BlockSpec((_R_ROWS, TT), lambda i: (0, i)),
                   pl.BlockSpec((N_EXPERTS, LANES), lambda i: (i, 0)),
                   pl.BlockSpec((N_EXPERTS, LANES), lambda i: (0, 0))],
        scratch_shapes=[pltpu.VMEM((N_EXPERTS, 1), F32)],
        compiler_params=_params(("arbitrary",)),
        name="route",
    )(lg_p, lg_s, tri, scan)


RUNS_PER_TILE = N_SLOTS // RUN_ROWS
RUN_UNROLL = 8
RUN_TABLE_STRIDE = RUNS_PER_TILE + RUN_UNROLL


def _for_each_run(tile, run_row_ref, fn):
    base = tile * RUN_TABLE_STRIDE

    def group(g, _):
        for u in range(RUN_UNROLL):
            j = g * RUN_UNROLL + u
            row = run_row_ref[base + j]
            fn(pl.multiple_of(j * RUN_ROWS, RUN_ROWS), pl.multiple_of(row, RUN_ROWS))
        return 0

    lax.fori_loop(0, run_row_ref[base + RUNS_PER_TILE], group, 0)


def _dispatch_kernel(run_row_ref, pad_start_ref, pad_runs_ref, tail_ref,
                     rt_ref, hp_ref, hs_ref, xs_ref, stage_ref, zero_ref, sem, psem, *, n_prompt_tiles):
    i = pl.program_id(0)
    last = pl.num_programs(0) - 1
    buf = i % 2

    def start_runs(tile, b):
        _for_each_run(tile, run_row_ref, lambda slot, row: pltpu.make_async_copy(
            stage_ref.at[b, pl.ds(slot, RUN_ROWS), :], xs_ref.at[pl.ds(row, RUN_ROWS), :], sem.at[b]).start())

    def wait_runs(tile, b):
        _for_each_run(tile, run_row_ref, lambda slot, row: pltpu.make_async_copy(
            stage_ref.at[b, pl.ds(slot, RUN_ROWS), :], xs_ref.at[pl.ds(row, RUN_ROWS), :], sem.at[b]).wait())

    @pl.when(i == 0)
    def _():
        zero_ref[...] = jnp.zeros_like(zero_ref)

        def tail(t, _):
            rows = pl.ds(pl.multiple_of(t * TM_E, TM_E), TM_E)
            cp = pltpu.make_async_copy(zero_ref, xs_ref.at[rows, :], psem)
            cp.start()
            cp.wait()
            return 0

        lax.fori_loop(tail_ref[0], tail_ref[1], tail, 0)

        def per_expert(e, _):
            def pad_copy(r):
                rows = pl.ds(pl.multiple_of(pad_start_ref[e] + r * RUN_ROWS, RUN_ROWS), RUN_ROWS)
                return pltpu.make_async_copy(zero_ref.at[pl.ds(0, RUN_ROWS), :], xs_ref.at[rows, :], psem)

            def start(r, _):
                pad_copy(r).start()
                return 0

            def wait(r, _):
                pad_copy(r).wait()
                return 0

            lax.fori_loop(0, pad_runs_ref[e], start, 0)
            lax.fori_loop(0, pad_runs_ref[e], wait, 0)
            return 0

        lax.fori_loop(0, N_EXPERTS, per_expert, 0)

    @pl.when(i >= 2)
    def _():
        wait_runs(i - 2, buf)

    rt = rt_ref[...]
    x = _unpack_bf16_pairs(jnp.where(i < n_prompt_tiles, hp_ref[...], hs_ref[...]))
    n_slots = stage_ref.shape[1]
    slot_id = lax.broadcasted_iota(jnp.int32, (n_slots, rt.shape[1]), 0).astype(F32)
    sel = jnp.zeros(slot_id.shape, F32)
    gsel = jnp.zeros(slot_id.shape, F32)
    for k in range(TOP_K):
        hit = slot_id == rt[_R_SLOT + k:_R_SLOT + k + 1, :]
        sel = jnp.where(hit, 1.0, sel)
        gsel = jnp.where(hit, rt[_R_GATE + k:_R_GATE + k + 1, :], gsel)
    rows = _dot(sel.astype(BF16), x)
    gate = jnp.sum(gsel, axis=-1, keepdims=True)
    packed = _pack_bf16_pairs(rows)
    stage_ref[buf, :, :packed.shape[1]] = packed
    stage_ref[buf, :, packed.shape[1]:] = pltpu.bitcast(jnp.broadcast_to(gate, (n_slots, LANES)), jnp.uint32)
    start_runs(i, buf)

    @pl.when(i == last)
    def _():
        @pl.when(i >= 1)
        def _():
            wait_runs(i - 1, 1 - buf)

        wait_runs(i, buf)


def _dispatch(h_p, h_s, route_t, run_row, pad_start, pad_runs, tail_tiles, n_rows):
    rp, dp = h_p.shape
    rs = h_s.shape[0]
    assert rp % TT == 0 and rs % TT == 0
    npt, nst = rp // TT, rs // TT
    width = dp + LANES
    grid_spec = pltpu.PrefetchScalarGridSpec(
        num_scalar_prefetch=4,
        grid=(npt + nst,),
        in_specs=[pl.BlockSpec((_R_ROWS, TT), lambda i, *_: (0, i)),
                  pl.BlockSpec((TT, dp), lambda i, *_: (jnp.minimum(i, npt - 1), 0)),
                  pl.BlockSpec((TT, dp), lambda i, *_: (jnp.maximum(i - npt, 0), 0))],
        out_specs=pl.BlockSpec(memory_space=pl.ANY),
        scratch_shapes=[pltpu.VMEM((2, N_SLOTS, width), jnp.uint32), pltpu.VMEM((TM_E, width), jnp.uint32),
                        pltpu.SemaphoreType.DMA((2,)), pltpu.SemaphoreType.DMA(())],
    )
    return pl.pallas_call(
        functools.partial(_dispatch_kernel, n_prompt_tiles=npt),
        out_shape=jax.ShapeDtypeStruct((n_rows, width), jnp.uint32),
        grid_spec=grid_spec,
        compiler_params=_params(("arbitrary",), has_side_effects=True),
        name="dispatch",
    )(run_row, pad_start, pad_runs, tail_tiles, route_t, h_p, h_s)


def _experts_kernel(te_ref, tf_ref, tu_ref, tn_ref, ts_ref, x_ref, w1_ref, b1g_ref, b1l_ref, w2_ref, b2_ref, perm_ref,
                    y_ref, w1buf_ref, w2buf_ref, wsem, w1s_ref, w2s_ref):
    i = pl.program_id(0)
    dff = w2_ref.shape[1]

    def weight_copies(e, slot):
        return (pltpu.make_async_copy(w1_ref.at[e], w1buf_ref.at[slot], wsem.at[0, slot]),
                pltpu.make_async_copy(w2_ref.at[e], w2buf_ref.at[slot], wsem.at[1, slot]))

    @pl.when(tf_ref[i] == 1)
    def _():
        slot = ts_ref[i]

        @pl.when(i == 0)
        def _():
            for cp in weight_copies(te_ref[i], slot):
                cp.start()

        for cp in weight_copies(te_ref[i], slot):
            cp.wait()

        @pl.when(tn_ref[i] >= 0)
        def _():
            for cp in weight_copies(tn_ref[i], 1 - slot):
                cp.start()

        perm = perm_ref[...]
        half = MXU_DIM // 2
        for c in range(w1_ref.shape[2] // MXU_DIM):
            blk = w1buf_ref[slot, :, c * MXU_DIM:(c + 1) * MXU_DIM].astype(BF16)
            sp = _dot(blk, perm).astype(BF16)
            w1s_ref[:, c * half:(c + 1) * half] = sp[:, :half]
            w1s_ref[:, dff + c * half:dff + (c + 1) * half] = sp[:, half:]
        w2s_ref[...] = w2buf_ref[slot].astype(BF16)

    @pl.when(tu_ref[i] == 1)
    def _():
        d = w2_ref.shape[2]
        x = _unpack_bf16_pairs(x_ref[:, :d // 2])
        gate = pltpu.bitcast(x_ref[:, d // 2:], F32)
        hdn = _dot(x, w1s_ref[...])
        glu = jnp.minimum(hdn[:, :dff] + b1g_ref[0], SWIGLU_LIMIT)
        lin = jnp.clip(hdn[:, dff:] + b1l_ref[0], -SWIGLU_LIMIT, SWIGLU_LIMIT)
        act = glu * _sigmoid(SWIGLU_ALPHA * glu) * (lin + 1.0)
        y = (_dot(act.astype(BF16), w2s_ref[...]) + b2_ref[0]) * jnp.concatenate([gate] * (d // LANES), axis=1)
        y_ref[...] = _pack_bf16_pairs(y.astype(BF16).astype(F32))

    @pl.when(tu_ref[i] == 0)
    def _():
        y_ref[...] = jnp.zeros_like(y_ref)


def _deinterleave_perm():
    j = lax.broadcasted_iota(jnp.int32, (MXU_DIM, MXU_DIM), 0)
    c = lax.broadcasted_iota(jnp.int32, (MXU_DIM, MXU_DIM), 1)
    half = MXU_DIM // 2
    src = jnp.where(c < half, 2 * c, 2 * (c - half) + 1)
    return (j == src).astype(BF16)


def _experts(xs, tile_expert, tile_first, tile_used, tile_next, w1, b1g, b1l, w2, b2):
    width = xs.shape[1]
    n_tiles = tile_expert.shape[0]
    n_rows = n_tiles * TM_E
    ne, d, two_f = w1.shape
    dff = two_f // 2
    assert width == d // 2 + LANES
    tile_slot = (jnp.cumsum(tile_first) + 1) % 2
    by_expert = lambda i, te, *_: (te[i], 0, 0)
    grid_spec = pltpu.PrefetchScalarGridSpec(
        num_scalar_prefetch=5,
        grid=(n_tiles,),
        in_specs=[pl.BlockSpec((TM_E, width), lambda i, te, tf, tu, *_: (i * tu[i], 0)),
                  pl.BlockSpec(memory_space=pl.ANY),
                  pl.BlockSpec((1, 1, dff), by_expert),
                  pl.BlockSpec((1, 1, dff), by_expert),
                  pl.BlockSpec(memory_space=pl.ANY),
                  pl.BlockSpec((1, 1, d), by_expert),
                  pl.BlockSpec((MXU_DIM, MXU_DIM), lambda i, *_: (0, 0))],
        out_specs=pl.BlockSpec((TM_E, d // 2), lambda i, *_: (i, 0)),
        scratch_shapes=[pltpu.VMEM((2, d, two_f), F32), pltpu.VMEM((2, dff, d), F32), pltpu.SemaphoreType.DMA((2, 2)),
                        pltpu.VMEM((d, two_f), BF16), pltpu.VMEM((dff, d), BF16)],
    )
    return pl.pallas_call(
        _experts_kernel,
        out_shape=jax.ShapeDtypeStruct((n_rows, d // 2), jnp.uint32),
        grid_spec=grid_spec,
        compiler_params=_params(("arbitrary",)),
        name="experts",
    )(tile_expert, tile_first, tile_used, tile_next, tile_slot.astype(jnp.int32),
      xs, w1, b1g, b1l, w2, b2, _deinterleave_perm())


def _combine_kernel(run_row_ref, x1_ref, rt_ref, g2_ref, y_ref, o_ref, stage_ref, sem, *, tile_offset):
    i = pl.program_id(0)
    buf = i % 2

    def start_runs(step, b):
        _for_each_run(step + tile_offset, run_row_ref, lambda slot, row: pltpu.make_async_copy(
            y_ref.at[pl.ds(row, RUN_ROWS), :], stage_ref.at[b, pl.ds(slot, RUN_ROWS), :], sem.at[b]).start())

    def wait_runs(step, b):
        _for_each_run(step + tile_offset, run_row_ref, lambda slot, row: pltpu.make_async_copy(
            y_ref.at[pl.ds(row, RUN_ROWS), :], stage_ref.at[b, pl.ds(slot, RUN_ROWS), :], sem.at[b]).wait())

    @pl.when(i == 0)
    def _():
        stage_ref[...] = jnp.zeros_like(stage_ref)
        start_runs(0, 0)

    @pl.when(i + 1 < pl.num_programs(0))
    def _():
        start_runs(i + 1, 1 - buf)

    rt = rt_ref[...]
    tt = rt.shape[0]
    slot_id = lax.broadcasted_iota(jnp.int32, (tt, stage_ref.shape[1]), 1).astype(F32)
    sel = jnp.zeros(slot_id.shape, F32)
    for k in range(TOP_K):
        sel = jnp.where(slot_id == rt[:, _R_SLOT + k:_R_SLOT + k + 1], 1.0, sel)
    sel = sel.astype(BF16)
    wait_runs(i, buf)
    o_ref[...] = x1_ref[...] + g2_ref[...] * _dot(sel, _unpack_bf16_pairs(stage_ref[buf]))


def _combine(x1, route, run_row, g2, seq, y_rows, row_offset):
    r, d = x1.shape
    assert r % TT == 0 and row_offset % TT == 0
    off = row_offset // TT
    g2_o, rpm = _mod_operand(g2, seq, TT)
    if rpm is None:
        g2_spec = pl.BlockSpec((TT, d), lambda i, *_: (i, 0))
    else:
        g2_spec = pl.BlockSpec((None, 1, d), lambda i, *_: ((i * TT) // rpm, 0, 0))
    grid_spec = pltpu.PrefetchScalarGridSpec(
        num_scalar_prefetch=1,
        grid=(r // TT,),
        in_specs=[pl.BlockSpec((TT, d), lambda i, *_: (i, 0)),
                  pl.BlockSpec((TT, LANES), lambda i, *_: (i + off, 0)),
                  g2_spec,
                  pl.BlockSpec(memory_space=pl.ANY)],
        out_specs=pl.BlockSpec((TT, d), lambda i, *_: (i, 0)),
        scratch_shapes=[pltpu.VMEM((2, N_SLOTS, d // 2), jnp.uint32), pltpu.SemaphoreType.DMA((2,))],
    )
    return pl.pallas_call(
        functools.partial(_combine_kernel, tile_offset=off),
        out_shape=jax.ShapeDtypeStruct((r, d), F32),
        grid_spec=grid_spec,
        compiler_params=_params(("arbitrary",)),
        name="combine",
    )(run_row, x1, route, g2_o, y_rows)


def _gmlp_tables(w_s, b_s, seq_s):
    g = w_s.shape[0]
    tril = jnp.tril(w_s)
    short = jnp.tril(w_s[:, :seq_s, :seq_s])
    rep = CHUNK_B // seq_s
    blockdiag = jnp.zeros_like(w_s)
    for r in range(rep):
        blockdiag = blockdiag.at[:, r * seq_s:(r + 1) * seq_s, r * seq_s:(r + 1) * seq_s].set(short)
    ws_all = jnp.stack([tril, blockdiag]).astype(BF16)
    bs_full = jnp.repeat(b_s.T, HEAD_DIM, axis=1)
    bs_short = jnp.tile(jnp.repeat(b_s[:, :seq_s].T, HEAD_DIM, axis=1), (rep, 1))
    return ws_all, jnp.stack([bs_full, bs_short])


def _layer(xp, xs, cache_k, cache_v, cp, cs, p):
    bp, tp, d = xp.shape
    bs_, ts, _ = xs.shape
    assert tp % CHUNK_B == 0 and CHUNK_B % ts == 0
    rp, rs = bp * tp, bs_ * ts
    n_tok = rp + rs

    mod = _ada(jnp.concatenate([cp, cs], axis=0), p['w_ada'], p['b_ada'])
    mods = [mod[:, j * d:(j + 1) * d] for j in range(6)]
    mp = [m[:bp] for m in mods]
    msm = [m[bp:] for m in mods]

    tile_h = lambda g: jnp.tile(g, N_HEADS).reshape(1, D_HEADS)
    bd = jnp.kron(jnp.eye(MXU_DIM // HEAD_DIM, dtype=F32),
                  jnp.full((HEAD_DIM, HEAD_DIM), 1.0 / HEAD_DIM, F32)).astype(BF16)
    w_in_b = p['w_in'].astype(BF16)
    n1g = p['norm1_g'].reshape(1, d)
    common = (n1g, w_in_b, tile_h(p['q_norm_g']), tile_h(p['k_norm_g']), tile_h(p['v_norm_g']), bd)
    q_p, k_p, kb_p, v_p, vb_p, u_p, vn_p, sga_p, sgb_p = _inproj(
        xp.reshape(rp, d), mp[0], mp[1], tp, *common, want_vn32=False, kv_transposed=True)
    q_s, k_s, kb_s, v_s, vb_s, u_s, vn_s, sga_s, sgb_s, vn32_s = _inproj(
        xs.reshape(rs, d), msm[0], msm[1], ts, *common, want_vn32=True, kv_transposed=False)

    ya_p = _attn_prompt(q_p, kb_p, vb_p, bp, tp)
    ya_s = _attn_sample(q_s, kb_s, vb_s, cache_k, cache_v, bs_, ts)

    ws_all, bs_all = _gmlp_tables(p['w_s'], p['b_s'], ts)
    wr = p['w_router']
    wr_hi = wr.astype(BF16)
    wr_lo = (wr - wr_hi.astype(F32)).astype(BF16)
    post_w = (p['w_pa'].astype(BF16), p['w_pb'].astype(BF16), p['w_o'].astype(BF16), p['norm2_g'].reshape(1, d),
              jnp.concatenate([wr_hi, wr_lo], axis=1), p['b_router'].reshape(N_EXPERTS, 1))
    x1_p, h2_p, lg_p = _post(xp.reshape(rp, d), ya_p, u_p, vn_p, sga_p, sgb_p, mp[2], mp[3], mp[4], tp,
                             ws_all, bs_all, 0, *post_w)
    x1_s, h2_s, lg_s = _post(xs.reshape(rs, d), ya_s, u_s, vn_s, sga_s, sgb_s, msm[2], msm[3], msm[4], ts,
                             ws_all, bs_all, 1, *post_w)

    route, route_t, table, counts = _route(lg_p, lg_s)

    counts = counts[:, 0].astype(jnp.int32)
    padded = (counts + TM_E - 1) // TM_E * TM_E
    pend = jnp.cumsum(padded)
    pstart = pend - padded
    table = jnp.swapaxes(table.reshape(-1, N_EXPERTS, LANES)[:, :, :3], 1, 2).astype(jnp.int32)
    n_token_tiles = n_tok // TT
    runs = (table[:, _T_COUNT] + RUN_ROWS - 1) // RUN_ROWS
    runs_end = jnp.cumsum(runs, axis=1)
    j = jnp.arange(RUNS_PER_TILE, dtype=jnp.int32)
    e_of = jnp.minimum(jnp.sum((runs_end[:, None, :] <= j[None, :, None]).astype(jnp.int32), axis=2), N_EXPERTS - 1)
    pick = (e_of[:, :, None] == jnp.arange(N_EXPERTS, dtype=jnp.int32)[None, None, :]).astype(jnp.int32)
    first_run = jnp.sum(pick * (runs_end - runs)[:, None, :], axis=2)
    first_row = jnp.sum(pick * (table[:, _T_BASE] + pstart[None, :])[:, None, :], axis=2)
    run_row = first_row + (j[None, :] - first_run) * RUN_ROWS
    is_run = j[None, :] < runs_end[:, -1:]
    n_tiles = -(-(n_tok * TOP_K + n_token_tiles * N_EXPERTS * (RUN_ROWS - 1)) // TM_E) + N_EXPERTS
    tile_start = jnp.arange(n_tiles, dtype=jnp.int32) * TM_E
    tile_used = (tile_start < pend[-1]).astype(jnp.int32)
    last_expert = jnp.sum((pend <= pend[-1] - 1).astype(jnp.int32))
    tile_expert = jnp.minimum(jnp.sum((pend[None, :] <= tile_start[:, None]).astype(jnp.int32), axis=1), last_expert)
    tile_first = jnp.concatenate([jnp.ones((1,), jnp.int32),
                                  (tile_expert[1:] != tile_expert[:-1]).astype(jnp.int32)])
    e_ids = jnp.arange(N_EXPERTS, dtype=jnp.int32)
    runs_later = jnp.logical_and(e_ids[None, :] > e_ids[:, None], (padded > 0)[None, :])
    next_expert = jnp.min(jnp.where(runs_later, e_ids[None, :], N_EXPERTS), axis=1)
    next_expert = jnp.where(next_expert == N_EXPERTS, -1, next_expert)
    tile_next = jnp.sum((tile_expert[:, None] == e_ids[None, :]).astype(jnp.int32) * next_expert[None, :], axis=1)

    spare_tiles = -(-2 * N_SLOTS // TM_E)
    spare_rows = j[None, :] * RUN_ROWS
    parity = (jnp.arange(n_token_tiles, dtype=jnp.int32) % 2)[:, None]
    groups = jnp.broadcast_to((runs_end[:, -1:] + RUN_UNROLL - 1) // RUN_UNROLL, (n_token_tiles, RUN_UNROLL))
    with_groups = lambda rows: jnp.concatenate([rows, groups], axis=1).reshape(-1)
    rows_out = with_groups(jnp.where(is_run, run_row, n_tiles * TM_E + parity * N_SLOTS + spare_rows))
    rows_in = with_groups(jnp.where(is_run, run_row, spare_rows))
    tail_tiles = jnp.stack([pend[-1] // TM_E, jnp.int32(n_tiles + spare_tiles)]).astype(jnp.int32)
    pad_runs = (padded - counts) // RUN_ROWS
    xs_rows = _dispatch(h2_p, h2_s, route_t, rows_out, pstart + counts, pad_runs, tail_tiles,
                        (n_tiles + spare_tiles) * TM_E)
    b1 = p['b1']
    ne = b1.shape[0]
    y_rows = _experts(xs_rows, tile_expert, tile_first, tile_used, tile_next, p['w1'],
                      b1[:, 0::2].reshape(ne, 1, -1), b1[:, 1::2].reshape(ne, 1, -1),
                      p['w2'], p['b2'].reshape(ne, 1, d))
    out_p = _combine(x1_p, route, rows_in, mp[5], tp, y_rows, 0)
    out_s = _combine(x1_s, route, rows_in, msm[5], ts, y_rows, rp)

    shp = lambda a, b, t: a.reshape(b, t, N_HEADS, HEAD_DIM)
    unt = lambda a: jnp.transpose(a.reshape(bp, N_HEADS, HEAD_DIM, tp), (0, 3, 1, 2))
    return (out_p.reshape(bp, tp, d), out_s.reshape(bs_, ts, d),
            unt(k_p), unt(v_p), shp(k_s, bs_, ts), shp(v_s, bs_, ts), shp(vn32_s, bs_, ts))


def kernel(x_prompt, x_sample, cache_k, cache_v, c_prompt, c_sample, norm1_g, norm2_g, w_ada, b_ada, w_in,
           q_norm_g, k_norm_g, v_norm_g, w_s, b_s, w_pa, w_pb, w_o, w_router, b_router, w1, b1, w2, b2):
    depth = w_in.shape[0]
    xp, xs = x_prompt, x_sample
    kp, vp, ks, vs, vbs = [], [], [], [], []
    for l in range(depth):
        p = {'norm1_g': norm1_g[l], 'norm2_g': norm2_g[l], 'w_ada': w_ada[l], 'b_ada': b_ada[l],
             'w_in': w_in[l], 'q_norm_g': q_norm_g[l], 'k_norm_g': k_norm_g[l], 'v_norm_g': v_norm_g[l],
             'w_s': w_s[l], 'b_s': b_s[l], 'w_pa': w_pa[l], 'w_pb': w_pb[l], 'w_o': w_o[l],
             'w_router': w_router[l], 'b_router': b_router[l], 'w1': w1[l], 'b1': b1[l],
             'w2': w2[l], 'b2': b2[l]}
        xp, xs, k_p, v_p, k_s, v_s, vn_s = _layer(xp, xs, cache_k[l], cache_v[l], c_prompt, c_sample, p)
        kp.append(k_p); vp.append(v_p); ks.append(k_s); vs.append(v_s); vbs.append(vn_s)
    return (xp, xs, jnp.stack(kp), jnp.stack(vp), jnp.stack(ks), jnp.stack(vs), jnp.stack(vbs))
```
